```python
import jax, jax.numpy as jnp
from jax import lax
import numpy as np

D_MODEL = 1024
BATCH = 16
SEQ = 2048
DEPTH = 1

CHUNK = 64
D_MIX = D_MODEL
SSD_HEADS = 8
SSD_HEAD_DIM = 64
SSD_INNER = SSD_HEADS * SSD_HEAD_DIM
SSD_GROUPS = 2
SSD_STATE = 128
CONV_WIDTH = 4
SSD_CONV_DIM = SSD_INNER + 2 * SSD_GROUPS * SSD_STATE
ATT_HEADS = 8
ATT_HEAD_DIM = 64
ATT_INNER = ATT_HEADS * ATT_HEAD_DIM
Q_BLOCK = 128
N_EXPERT_GROUPS = 4
EXPERTS_PER_GROUP = 4
N_EXPERTS = N_EXPERT_GROUPS * EXPERTS_PER_GROUP
TOP_K = 2
D_FF_EXPERT = 512
DEEPNORM_ALPHA = (2.0 * DEPTH) ** 0.25
DEEPNORM_BETA = (8.0 * DEPTH) ** -0.25
LN_EPS = 1e-5
RMS_EPS = 1e-5
IN_SIZES = (SSD_INNER, SSD_CONV_DIM, SSD_HEADS, ATT_INNER, ATT_INNER, ATT_INNER, ATT_HEADS)
D_IN_PROJ = SSD_INNER + SSD_CONV_DIM + SSD_HEADS + 3 * ATT_INNER + ATT_HEADS

kernel_name = "hybrid_ssd_fox_hiermoe_deepnorm"


def _split_cols(u, sizes):
    idx = np.cumsum(np.array(sizes))[:-1].tolist()
    return jnp.split(u, idx, axis=-1)


def _layer_norm(u, g, b):
    uf = u.astype(jnp.float32)
    mu = jnp.mean(uf, axis=-1, keepdims=True)
    var = jnp.mean(jnp.square(uf - mu), axis=-1, keepdims=True)
    return ((uf - mu) * lax.rsqrt(var + LN_EPS) * g + b).astype(u.dtype)


def _causal_dwconv(u, w, b):
    s = u.shape[1]
    up = jnp.pad(u, ((0, 0), (CONV_WIDTH - 1, 0), (0, 0)))
    out = b
    for k in range(CONV_WIDTH):
        out = out + up[:, k:k + s, :] * w[k]
    return out


def _ssd_scan(xs, dt, a, bm, cm, d_skip):
    f32 = jnp.float32
    xs, dt, bm, cm = xs.astype(f32), dt.astype(f32), bm.astype(f32), cm.astype(f32)
    b, s, h, p = xs.shape
    nc = s // CHUNK
    r = h // SSD_GROUPS
    xdt = (xs * dt[..., None]).reshape(b, nc, CHUNK, SSD_GROUPS, r, p)
    da = (dt * a.astype(f32)).reshape(b, nc, CHUNK, SSD_GROUPS, r)
    bc = bm.reshape(b, nc, CHUNK, SSD_GROUPS, SSD_STATE)
    cc = cm.reshape(b, nc, CHUNK, SSD_GROUPS, SSD_STATE)
    a_cum = jnp.cumsum(da, axis=2)
    seg = a_cum[:, :, :, None] - a_cum[:, :, None, :]
    tri = jnp.tril(jnp.ones((CHUNK, CHUNK), dtype=bool))[None, None, :, :, None, None]
    lmat = jnp.exp(jnp.where(tri, seg, -jnp.inf))
    cb = jnp.einsum('bclgn,bcsgn->bclsg', cc, bc)
    y_diag = jnp.einsum('bclsg,bclsgr,bcsgrp->bclgrp', cb, lmat, xdt)
    decay_to_end = jnp.exp(a_cum[:, :, -1:] - a_cum)
    chunk_states = jnp.einsum('bclgn,bclgr,bclgrp->bcgrpn', bc, decay_to_end, xdt)
    chunk_decay = jnp.exp(a_cum[:, :, -1])

    def step(state, inp):
        st, dec = inp
        return state * dec[..., None, None] + st, state

    init = jnp.zeros((b, SSD_GROUPS, r, p, SSD_STATE), f32)
    _, states_in = lax.scan(step, init, (jnp.moveaxis(chunk_states, 1, 0),
                                         jnp.moveaxis(chunk_decay, 1, 0)))
    states_in = jnp.moveaxis(states_in, 0, 1)
    y_off = jnp.einsum('bclgn,bcgrpn,bclgr->bclgrp', cc, states_in, jnp.exp(a_cum))
    y = (y_diag + y_off).reshape(b, s, h, p)
    return y + xs * d_skip.astype(f32)[:, None]


def _forgetting_attention(q, k, v, log_f):
    b, s, h, d = q.shape
    fcum = jnp.transpose(jnp.cumsum(log_f, axis=1), (0, 2, 1))
    scale = d ** -0.5
    outs = []
    for blk in range(s // Q_BLOCK):
        q0, q1 = blk * Q_BLOCK, (blk + 1) * Q_BLOCK
        logits = jnp.einsum('bqhd,bkhd->bhqk', q[:, q0:q1], k[:, :q1]).astype(jnp.float32) * scale
        logits = logits + fcum[:, :, q0:q1, None] - fcum[:, :, None, :q1]
        qpos = jnp.arange(q0, q1)[:, None]
        kpos = jnp.arange(q1)[None, :]
        logits = jnp.where(qpos >= kpos, logits, -jnp.inf)
        probs = jax.nn.softmax(logits, axis=-1)
        outs.append(jnp.einsum('bhqk,bkhd->bqhd', probs.astype(v.dtype), v[:, :q1]))
    return jnp.concatenate(outs, axis=1)


def _hybrid_mixer(h, w_in, b_in, conv_w, conv_b, a_log, d_skip, ssd_norm_g, w_out):
    bsz, s, _ = h.shape
    proj = jnp.einsum('bsd,de->bse', h, w_in) + b_in
    z, xbc, dt_raw, q, k, v, f_raw = _split_cols(proj, IN_SIZES)
    xbc = jax.nn.silu(_causal_dwconv(xbc, conv_w, conv_b))
    xs, bm, cm = _split_cols(xbc, (SSD_INNER, SSD_GROUPS * SSD_STATE, SSD_GROUPS * SSD_STATE))
    xs = xs.reshape(bsz, s, SSD_HEADS, SSD_HEAD_DIM)
    bm = bm.reshape(bsz, s, SSD_GROUPS, SSD_STATE)
    cm = cm.reshape(bsz, s, SSD_GROUPS, SSD_STATE)
    dt = jax.nn.softplus(dt_raw.astype(jnp.float32))
    a = -jnp.exp(a_log.astype(jnp.float32))
    y = _ssd_scan(xs, dt, a, bm, cm, d_skip).reshape(bsz, s, SSD_INNER)
    y = (y * jax.nn.silu(z.astype(jnp.float32))).reshape(bsz, s, SSD_GROUPS, SSD_INNER // SSD_GROUPS)
    y = y * lax.rsqrt(jnp.mean(jnp.square(y), axis=-1, keepdims=True) + RMS_EPS)
    y_ssd = (y.reshape(bsz, s, SSD_INNER) * ssd_norm_g).astype(h.dtype)
    q = q.reshape(bsz, s, ATT_HEADS, ATT_HEAD_DIM)
    k = k.reshape(bsz, s, ATT_HEADS, ATT_HEAD_DIM)
    v = v.reshape(bsz, s, ATT_HEADS, ATT_HEAD_DIM)
    log_f = jax.nn.log_sigmoid(f_raw.astype(jnp.float32))
    y_att = _forgetting_attention(q, k, v, log_f).reshape(bsz, s, ATT_INNER)
    merged = jnp.concatenate([y_ssd, y_att.astype(h.dtype)], axis=-1)
    return jnp.einsum('bse,ed->bsd', merged, w_out)


def _hier_moe(h, rg_w, rg_b, re_w, re_b, w_gate, w_up, w_down):
    bsz, s, d = h.shape
    hf = h.reshape(bsz * s, d)
    g_logits = (hf @ rg_w + rg_b).astype(jnp.float32)
    g_val, g_idx = lax.top_k(jax.nn.softmax(g_logits, axis=-1), 1)
    g_val, g_idx = g_val[:, 0], g_idx[:, 0]
    e_all = (jnp.einsum('td,gde->tge', hf, re_w) + re_b).astype(jnp.float32)
    e_logits = jnp.take_along_axis(e_all, g_idx[:, None, None], axis=1)[:, 0]
    e_val, e_idx = lax.top_k(jax.nn.softmax(e_logits, axis=-1), TOP_K)
    e_val = e_val / jnp.sum(e_val, axis=-1, keepdims=True)
    weights = g_val[:, None] * e_val
    expert_id = g_idx[:, None] * EXPERTS_PER_GROUP + e_idx
    combine = jnp.sum(jax.nn.one_hot(expert_id, N_EXPERTS, dtype=jnp.float32) * weights[..., None], axis=1)
    out = jnp.zeros((bsz * s, d), jnp.float32)
    for e in range(N_EXPERTS):
        act = jax.nn.silu(hf @ w_gate[e]) * (hf @ w_up[e])
        out = out + combine[:, e:e + 1] * (act @ w_down[e]).astype(jnp.float32)
    return out.astype(h.dtype).reshape(bsz, s, d)


def setup_inputs(seed: int = 0) -> dict:
    key = jax.random.key(seed)
    ks = jax.random.split(key, 20)
    f32 = jnp.float32
    x = jax.random.normal(ks[0], (BATCH, SEQ, D_MODEL), f32)
    col_scale = np.concatenate([
        np.ones(SSD_INNER + SSD_CONV_DIM), np.full(SSD_HEADS, 0.1),
        np.ones(2 * ATT_INNER), np.full(ATT_INNER, DEEPNORM_BETA), np.full(ATT_HEADS, 0.1)]).astype(np.float32)
    w_in = jax.random.normal(ks[1], (DEPTH, D_MODEL, D_IN_PROJ), f32) * (D_MODEL ** -0.5) * jnp.asarray(col_scale)
    b_in = 0.01 * jax.random.normal(ks[2], (DEPTH, D_IN_PROJ), f32)
    dt0 = jnp.exp(jax.random.uniform(ks[3], (DEPTH, SSD_HEADS), f32, np.log(1e-3), np.log(1e-1)))
    dt_bias = dt0 + jnp.log(-jnp.expm1(-dt0))
    off_dt = SSD_INNER + SSD_CONV_DIM
    b_in = b_in.at[:, off_dt:off_dt + SSD_HEADS].set(dt_bias)
    f_bias = jax.random.uniform(ks[4], (DEPTH, ATT_HEADS), f32, 1.0, 5.0)
    b_in = b_in.at[:, D_IN_PROJ - ATT_HEADS:].set(f_bias)
    conv_w = jax.random.normal(ks[5], (DEPTH, CONV_WIDTH, SSD_CONV_DIM), f32) * (CONV_WIDTH ** -0.5)
    conv_b = 0.01 * jax.random.normal(ks[6], (DEPTH, SSD_CONV_DIM), f32)
    a_log = jnp.log(jax.random.uniform(ks[7], (DEPTH, SSD_HEADS), f32, 1.0, 16.0))
    d_skip = 1.0 + 0.1 * jax.random.normal(ks[8], (DEPTH, SSD_HEADS), f32)
    ssd_norm_g = 1.0 + 0.05 * jax.random.normal(ks[9], (DEPTH, SSD_INNER), f32)
    w_out = jax.random.normal(ks[10], (DEPTH, D_MIX, D_MODEL), f32) * (D_MIX ** -0.5) * DEEPNORM_BETA
    ln1_g = 1.0 + 0.05 * jax.random.normal(ks[11], (DEPTH, D_MODEL), f32)
    ln1_b = 0.02 * jax.random.normal(ks[12], (DEPTH, D_MODEL), f32)
    router_group_w = jax.random.normal(ks[13], (DEPTH, D_MODEL, N_EXPERT_GROUPS), f32) * (D_MODEL ** -0.5)
    router_group_b = 0.01 * jax.random.normal(ks[14], (DEPTH, N_EXPERT_GROUPS), f32)
    router_expert_w = jax.random.normal(ks[15], (DEPTH, N_EXPERT_GROUPS, D_MODEL, EXPERTS_PER_GROUP), f32) * (D_MODEL ** -0.5)
    router_expert_b = 0.01 * jax.random.normal(ks[16], (DEPTH, N_EXPERT_GROUPS, EXPERTS_PER_GROUP), f32)
    kw = jax.random.split(ks[17], 3)
    w_gate = jax.random.normal(kw[0], (DEPTH, N_EXPERTS, D_MODEL, D_FF_EXPERT), f32) * (D_MODEL ** -0.5)
    w_up = jax.random.normal(kw[1], (DEPTH, N_EXPERTS, D_MODEL, D_FF_EXPERT), f32) * (D_MODEL ** -0.5)
    w_down = jax.random.normal(kw[2], (DEPTH, N_EXPERTS, D_FF_EXPERT, D_MODEL), f32) * (D_FF_EXPERT ** -0.5) * DEEPNORM_BETA
    ln2_g = 1.0 + 0.05 * jax.random.normal(ks[18], (DEPTH, D_MODEL), f32)
    ln2_b = 0.02 * jax.random.normal(ks[19], (DEPTH, D_MODEL), f32)
    return {"x": x, "w_in": w_in, "b_in": b_in, "conv_w": conv_w, "conv_b": conv_b,
            "a_log": a_log, "d_skip": d_skip, "ssd_norm_g": ssd_norm_g, "w_out": w_out,
            "ln1_g": ln1_g, "ln1_b": ln1_b, "router_group_w": router_group_w,
            "router_group_b": router_group_b, "router_expert_w": router_expert_w,
            "router_expert_b": router_expert_b, "w_gate": w_gate, "w_up": w_up,
            "w_down": w_down, "ln2_g": ln2_g, "ln2_b": ln2_b}


def reference(x, w_in, b_in, conv_w, conv_b, a_log, d_skip, ssd_norm_g, w_out,
              ln1_g, ln1_b, router_group_w, router_group_b, router_expert_w,
              router_expert_b, w_gate, w_up, w_down, ln2_g, ln2_b):
    h = x
    for l in range(DEPTH):
        mix = _hybrid_mixer(h, w_in[l], b_in[l], conv_w[l], conv_b[l], a_log[l],
                            d_skip[l], ssd_norm_g[l], w_out[l])
        h = _layer_norm(DEEPNORM_ALPHA * h + mix, ln1_g[l], ln1_b[l])
        ffn = _hier_moe(h, router_group_w[l], router_group_b[l], router_expert_w[l],
                        router_expert_b[l], w_gate[l], w_up[l], w_down[l])
        h = _layer_norm(DEEPNORM_ALPHA * h + ffn, ln2_g[l], ln2_b[l])
    return h
```

```python
import functools

import jax
import jax.numpy as jnp
from jax import lax
from jax.experimental import pallas as pl
from jax.experimental.pallas import tpu as pltpu

F32 = jnp.float32
BF16 = jnp.bfloat16

D_MODEL = 1024
SSD_HEADS = 8
SSD_HEAD_DIM = 64
SSD_INNER = SSD_HEADS * SSD_HEAD_DIM
SSD_GROUPS = 2
SSD_STATE = 128
CONV_WIDTH = 4
SSD_CONV_DIM = SSD_INNER + 2 * SSD_GROUPS * SSD_STATE
ATT_HEADS = 8
ATT_HEAD_DIM = 64
ATT_INNER = ATT_HEADS * ATT_HEAD_DIM
N_EXPERT_GROUPS = 4
EXPERTS_PER_GROUP = 4
N_EXPERTS = N_EXPERT_GROUPS * EXPERTS_PER_GROUP
D_FF_EXPERT = 512
DEPTH = 1
DEEPNORM_ALPHA = (2.0 * DEPTH) ** 0.25
LN_EPS = 1e-5
RMS_EPS = 1e-5

LANES = 128
SUBLANES = 8
VMEM_LIMIT_BYTES = 56 * 1024 * 1024

GATE_A_LANE = 0
GATE_F_LANE = SSD_HEADS
GATE_DT_LANE = 2 * SSD_HEADS
ROUTE_E_LANE = N_EXPERT_GROUPS

SSD_CHUNK = 64
NEG_BIG = -1e30


def _split3(v):
    hi = v.astype(BF16)
    r1 = v - hi.astype(F32)
    mid = r1.astype(BF16)
    lo = (r1 - mid.astype(F32)).astype(BF16)
    return hi, mid, lo


def _silu(v):
    return v * (1.0 / (1.0 + jnp.exp(-v)))


def _softplus(v):
    return jnp.maximum(v, 0.0) + jnp.log1p(jnp.exp(-jnp.abs(v)))


def _in_proj_kernel(x_ref, wm_ref, bm_ref, ws_ref, bs_ref, arow_ref, tri_ref,
                    z_ref, xbc_ref, q_ref, k_ref, v_ref, gate_ref, gate_t_ref,
                    carry_ref, *, tiles_per_seq):
    i = pl.program_id(0)
    tm = x_ref.shape[0]
    xb = x_ref[...].astype(BF16)

    col = 0
    for ref, scale in ((z_ref, None), (xbc_ref, None), (q_ref, ATT_HEAD_DIM ** -0.5),
                       (k_ref, None), (v_ref, None)):
        n = ref.shape[1]
        acc = jnp.dot(xb, wm_ref[:, col:col + n], preferred_element_type=F32) + bm_ref[:, col:col + n]
        if scale is not None:
            acc = acc * scale
        ref[...] = acc.astype(ref.dtype)
        col += n

    s = jnp.dot(xb, ws_ref[...], preferred_element_type=F32) + bs_ref[...]
    lane = lax.broadcasted_iota(jnp.int32, (1, LANES), 1)
    sp = _softplus(s)
    log_f = -_softplus(-s)
    val = jnp.where(lane < GATE_F_LANE, sp * arow_ref[...],
                    jnp.where(lane < GATE_DT_LANE, log_f,
                              jnp.where(lane < GATE_DT_LANE + SSD_HEADS, sp, 0.0)))

    @pl.when(i % tiles_per_seq == 0)
    def _():
        carry_ref[...] = jnp.zeros_like(carry_ref)

    hi, mid, lo = _split3(val)
    tri = tri_ref[...]
    cs = (jnp.dot(tri, hi, preferred_element_type=F32)
          + jnp.dot(tri, mid, preferred_element_type=F32)
          + jnp.dot(tri, lo, preferred_element_type=F32)) + carry_ref[...]
    carry_ref[...] = cs[tm - 1:tm, :]
    out = jnp.where(lane < GATE_DT_LANE, cs, val)
    gate_ref[...] = out
    gate_t_ref[...] = out.T


def _in_proj(x2d, w_main, b_main, w_small, b_small, a_row, seq):
    t = x2d.shape[0]
    tm = min(512, seq)
    tri = jnp.tri(tm, dtype=BF16)
    n_main = w_main.shape[1]
    row = lambda i: (i, 0)
    const = lambda i: (0, 0)
    outs = (
        jax.ShapeDtypeStruct((t, SSD_INNER), BF16),
        jax.ShapeDtypeStruct((t, SSD_CONV_DIM), BF16),
        jax.ShapeDtypeStruct((t, ATT_INNER), BF16),
        jax.ShapeDtypeStruct((t, ATT_INNER), BF16),
        jax.ShapeDtypeStruct((t, ATT_INNER), BF16),
        jax.ShapeDtypeStruct((t, LANES), F32),
        jax.ShapeDtypeStruct((LANES, t), F32),
    )
    return pl.pallas_call(
        functools.partial(_in_proj_kernel, tiles_per_seq=seq // tm),
        grid=(t // tm,),
        in_specs=[
            pl.BlockSpec((tm, D_MODEL), row),
            pl.BlockSpec((D_MODEL, n_main), const),
            pl.BlockSpec((1, n_main), const),
            pl.BlockSpec((D_MODEL, LANES), const),
            pl.BlockSpec((1, LANES), const),
            pl.BlockSpec((1, LANES), const),
            pl.BlockSpec((tm, tm), const),
        ],
        out_specs=[
            pl.BlockSpec((tm, SSD_INNER), row),
            pl.BlockSpec((tm, SSD_CONV_DIM), row),
            pl.BlockSpec((tm, ATT_INNER), row),
            pl.BlockSpec((tm, ATT_INNER), row),
            pl.BlockSpec((tm, ATT_INNER), row),
            pl.BlockSpec((tm, LANES), row),
            pl.BlockSpec((LANES, tm), lambda i: (0, i)),
        ],
        out_shape=outs,
        scratch_shapes=[pltpu.VMEM((1, LANES), F32)],
        compiler_params=pltpu.CompilerParams(
            dimension_semantics=("arbitrary",), vmem_limit_bytes=VMEM_LIMIT_BYTES),
        name="in_proj",
    )(x2d, w_main, b_main, w_small, b_small, a_row, tri)


def _expand_heads(gate, lane0, rows):
    lane = lax.broadcasted_iota(jnp.int32, (1, LANES), 1)
    pieces = []
    for p in range(SSD_HEADS // 2):
        a = jnp.broadcast_to(gate[:, lane0 + 2 * p:lane0 + 2 * p + 1], (rows, LANES))
        b = jnp.broadcast_to(gate[:, lane0 + 2 * p + 1:lane0 + 2 * p + 2], (rows, LANES))
        pieces.append(jnp.where(lane < SSD_HEAD_DIM, a, b))
    return jnp.concatenate(pieces, axis=1)


def _ssd_kernel(xbc_ref, z_ref, gate_ref, cw_ref, cb_ref, aexp_ref, dskip_ref, ng_ref,
                y_ref, ext_ref, state_ref):
    j = pl.program_id(1)
    rows = xbc_ref.shape[0]
    q = SSD_CHUNK
    gw = SSD_INNER // SSD_GROUPS
    hpg = SSD_HEADS // SSD_GROUPS
    pad = SUBLANES

    @pl.when(j == 0)
    def _():
        ext_ref[0:pad, :] = jnp.zeros((pad, SSD_CONV_DIM), F32)
        state_ref[...] = jnp.zeros_like(state_ref)

    ext_ref[pad:pad + rows, :] = xbc_ref[...].astype(F32)
    conv = cb_ref[...]
    for kk in range(CONV_WIDTH):
        off = pad - (CONV_WIDTH - 1) + kk
        conv = conv + ext_ref[off:off + rows, :] * cw_ref[kk:kk + 1, :]
    ext_ref[0:pad, :] = ext_ref[rows:rows + pad, :]
    u = _silu(conv)
    xs = u[:, :SSD_INNER]
    b_all = u[:, SSD_INNER:SSD_INNER + SSD_GROUPS * SSD_STATE].astype(BF16)
    c_all = u[:, SSD_INNER + SSD_GROUPS * SSD_STATE:].astype(BF16)

    gate = gate_ref[...]
    dt_e = _expand_heads(gate, GATE_DT_LANE, rows)
    a_e = _expand_heads(gate, GATE_A_LANE, rows)
    xdt = xs * dt_e
    da_e = dt_e * aexp_ref[...]

    r_i = lax.broadcasted_iota(jnp.int32, (q, gw), 0)
    c_i = lax.broadcasted_iota(jnp.int32, (q, gw), 1)
    s_i = c_i % q
    eye_t = s_i == r_i
    tril_t = s_i <= r_i
    rb = lax.broadcasted_iota(jnp.int32, (hpg * q, gw), 0) // q
    cbk = lax.broadcasted_iota(jnp.int32, (hpg * q, gw), 1) // SSD_HEAD_DIM
    blk = rb == cbk

    ys = []
    for c in range(rows // q):
        r0 = c * q
        a_c = a_e[r0:r0 + q, :]
        a_prev = a_c[0:1, :] - da_e[r0:r0 + 1, :]
        a_end = a_c[q - 1:q, :]
        y_groups = []
        for g in range(SSD_GROUPS):
            l0 = g * gw
            a_cg = a_c[:, l0:l0 + gw]
            diag = jnp.sum(jnp.where(eye_t, a_cg, 0.0), axis=0, keepdims=True)
            seg = jnp.minimum(a_cg - diag, 0.0)
            lcat = jnp.where(tril_t, jnp.exp(seg), 0.0)
            bg = b_all[r0:r0 + q, g * SSD_STATE:(g + 1) * SSD_STATE]
            cg = c_all[r0:r0 + q, g * SSD_STATE:(g + 1) * SSD_STATE]
            b_tile = jnp.concatenate([bg] * hpg, axis=0)
            gcat = lax.dot_general(cg, b_tile, (((1,), (1,)), ((), ())),
                                   preferred_element_type=F32)
            xg = xdt[r0:r0 + q, l0:l0 + gw]
            xg_b = xg.astype(BF16)
            xbd = jnp.where(blk, jnp.concatenate([xg_b] * hpg, axis=0), jnp.zeros((), BF16))
            y_diag = jnp.dot((gcat * lcat).astype(BF16), xbd, preferred_element_type=F32)
            st = state_ref[g]
            y_off = jnp.dot(cg, st.astype(BF16), preferred_element_type=F32) \
                * jnp.exp(a_cg - a_prev[:, l0:l0 + gw])
            dte = jnp.exp(a_end[:, l0:l0 + gw] - a_cg)
            upd = lax.dot_general(bg, (xg * dte).astype(BF16), (((0,), (0,)), ((), ())),
                                  preferred_element_type=F32)
            state_ref[g] = st * jnp.exp(a_end[:, l0:l0 + gw] - a_prev[:, l0:l0 + gw]) + upd
            y_groups.append(y_diag + y_off)
        ys.append(jnp.concatenate(y_groups, axis=1))
    y = jnp.concatenate(ys, axis=0) + xs * dskip_ref[...]
    y = y * _silu(z_ref[...].astype(F32))
    outs = []
    for g in range(SSD_GROUPS):
        yg = y[:, g * gw:(g + 1) * gw]
        ms = jnp.mean(yg * yg, axis=1, keepdims=True)
        outs.append(yg * lax.rsqrt(ms + RMS_EPS))
    y_ref[...] = (jnp.concatenate(outs, axis=1) * ng_ref[...]).astype(y_ref.dtype)


def _ssd(xbc, z, gate, conv_w, conv_b, a_exp, dskip_exp, norm_g, batch, seq):
    t = xbc.shape[0]
    rows = min(256, seq)
    nj = seq // rows
    row = lambda b, j: (b * nj + j, 0)
    const = lambda b, j: (0, 0)
    return pl.pallas_call(
        _ssd_kernel,
        grid=(batch, nj),
        in_specs=[
            pl.BlockSpec((rows, SSD_CONV_DIM), row),
            pl.BlockSpec((rows, SSD_INNER), row),
            pl.BlockSpec((rows, LANES), row),
            pl.BlockSpec((CONV_WIDTH, SSD_CONV_DIM), const),
            pl.BlockSpec((1, SSD_CONV_DIM), const),
            pl.BlockSpec((1, SSD_INNER), const),
            pl.BlockSpec((1, SSD_INNER), const),
            pl.BlockSpec((1, SSD_INNER), const),
        ],
        out_specs=pl.BlockSpec((rows, SSD_INNER), row),
        out_shape=jax.ShapeDtypeStruct((t, SSD_INNER), BF16),
        scratch_shapes=[
            pltpu.VMEM((rows + 2 * SUBLANES, SSD_CONV_DIM), F32),
            pltpu.VMEM((SSD_GROUPS, SSD_STATE, SSD_INNER // SSD_GROUPS), F32),
        ],
        compiler_params=pltpu.CompilerParams(
            dimension_semantics=("arbitrary", "arbitrary"), vmem_limit_bytes=VMEM_LIMIT_BYTES),
        name="ssd",
    )(xbc, z, gate, conv_w, conv_b, a_exp, dskip_exp, norm_g)


def _attn_kernel(q_ref, k_ref, v_ref, gate_ref, ft_ref, o_ref, *, tq, tk):
    pair = pl.program_id(1)
    i = pl.program_id(2)
    qv = q_ref[...]
    gate = gate_ref[...]
    lane = lax.broadcasted_iota(jnp.int32, (1, LANES), 1)
    r_i = lax.broadcasted_iota(jnp.int32, (tq, tk), 0)
    c_i = lax.broadcasted_iota(jnp.int32, (tq, tk), 1)
    n_full = i * (tq // tk)

    outs = []
    for hh in range(2):
        h = 2 * pair + hh
        in_head = (lane < ATT_HEAD_DIM) if hh == 0 else (lane >= ATT_HEAD_DIM)
        qm = jnp.where(in_head, qv, jnp.zeros((), qv.dtype))
        f_col = jnp.sum(jnp.where(lane == GATE_F_LANE + h, gate, 0.0), axis=1, keepdims=True)

        def step(jk, carry, masked, qm=qm, f_col=f_col, h=h):
            m, l, acc = carry
            k0 = pl.multiple_of(jk * tk, tk)
            kb = k_ref[pl.ds(k0, tk), :]
            vb = v_ref[pl.ds(k0, tk), :]
            s = lax.dot_general(qm, kb, (((1,), (1,)), ((), ())), preferred_element_type=F32)
            s = s + (f_col - ft_ref[pl.ds(h, 1), pl.ds(k0, tk)])
            if masked:
                s = jnp.where(r_i + (i * tq - k0) >= c_i, s, NEG_BIG)
            m_new = jnp.maximum(m, jnp.max(s, axis=1, keepdims=True))
            alpha = jnp.exp(m - m_new)
            p = jnp.exp(s - m_new)
            l = alpha * l + jnp.sum(p, axis=1, keepdims=True)
            acc = alpha * acc + jnp.dot(p.astype(BF16), vb, preferred_element_type=F32)
            return m_new, l, acc

        carry = (jnp.full((tq, 1), NEG_BIG, F32), jnp.zeros((tq, 1), F32), jnp.zeros((tq, LANES), F32))
        carry = lax.fori_loop(0, n_full, functools.partial(step, masked=False), carry)
        for d in range(tq // tk):
            carry = step(n_full + d, carry, True)
        m, l, acc = carry
        outs.append(acc / l)
    o_ref[...] = jnp.where(lane < ATT_HEAD_DIM, outs[0], outs[1]).astype(o_ref.dtype)


def _attention(q, k, v, gate, gate_t, batch, seq):
    t = q.shape[0]
    tq = min(256, seq)
    tk = tq
    nq = seq // tq
    return pl.pallas_call(
        functools.partial(_attn_kernel, tq=tq, tk=tk),
        grid=(batch, ATT_HEADS // 2, nq),
        in_specs=[
            pl.BlockSpec((tq, LANES), lambda b, p, i: (b * nq + i, p)),
            pl.BlockSpec((seq, LANES), lambda b, p, i: (b, p)),
            pl.BlockSpec((seq, LANES), lambda b, p, i: (b, p)),
            pl.BlockSpec((tq, LANES), lambda b, p, i: (b * nq + i, 0)),
            pl.BlockSpec((SUBLANES, seq), lambda b, p, i: (GATE_F_LANE // SUBLANES, b)),
        ],
        out_specs=pl.BlockSpec((tq, LANES), lambda b, p, i: (b * nq + i, p)),
        out_shape=jax.ShapeDtypeStruct((t, ATT_INNER), BF16),
        compiler_params=pltpu.CompilerParams(
            dimension_semantics=("arbitrary", "arbitrary", "arbitrary"),
            vmem_limit_bytes=VMEM_LIMIT_BYTES),
        name="fox_attention",
    )(q, k, v, gate, gate_t)


def _layer_norm(u, g, b):
    mu = jnp.mean(u, axis=1, keepdims=True)
    d = u - mu
    var = jnp.mean(d * d, axis=1, keepdims=True)
    return d * lax.rsqrt(var + LN_EPS) * g + b


def _route(lg):
    lane = lax.broadcasted_iota(jnp.int32, (1, LANES), 1)
    lane_f = lane.astype(F32)
    is_g = lane < N_EXPERT_GROUPS
    gl = jnp.where(is_g, lg, NEG_BIG)
    gmax = jnp.max(gl, axis=1, keepdims=True)
    gsum = jnp.sum(jnp.where(is_g, jnp.exp(gl - gmax), 0.0), axis=1, keepdims=True)
    g_val = 1.0 / gsum
    g_idx = jnp.min(jnp.where(is_g & (gl == gmax), lane_f, float(LANES)), axis=1, keepdims=True)
    e_grp = ((lane - ROUTE_E_LANE) // EXPERTS_PER_GROUP).astype(F32)
    in_grp = (lane >= ROUTE_E_LANE) & (lane < ROUTE_E_LANE + N_EXPERTS) & (e_grp == g_idx)
    el = jnp.where(in_grp, lg, NEG_BIG)
    m1 = jnp.max(el, axis=1, keepdims=True)
    i1 = jnp.min(jnp.where(in_grp & (el == m1), lane_f, float(LANES)), axis=1, keepdims=True)
    rest = in_grp & (lane_f != i1)
    el2 = jnp.where(rest, lg, NEG_BIG)
    m2 = jnp.max(el2, axis=1, keepdims=True)
    i2 = jnp.min(jnp.where(rest & (el2 == m2), lane_f, float(LANES)), axis=1, keepdims=True)
    r = jnp.exp(m2 - m1)
    w1 = 1.0 / (1.0 + r)
    w2 = r * w1
    return jnp.where(lane_f == i1, g_val * w1, jnp.where(lane_f == i2, g_val * w2, 0.0))


def _out_proj_kernel(ys_ref, ya_ref, x_ref, wa_ref, wb_ref, g_ref, b_ref, wrh_ref, wrl_ref, br_ref,
                     h_ref, hb_ref, comb_ref):
    mix = jnp.dot(ys_ref[...], wa_ref[...], preferred_element_type=F32)
    mix = mix + jnp.dot(ya_ref[...], wb_ref[...], preferred_element_type=F32)
    h = _layer_norm(DEEPNORM_ALPHA * x_ref[...] + mix, g_ref[...], b_ref[...])
    h_ref[...] = h
    h_hi = h.astype(BF16)
    hb_ref[...] = h_hi
    h_lo = (h - h_hi.astype(F32)).astype(BF16)
    lg = (jnp.dot(h_hi, wrh_ref[...], preferred_element_type=F32)
          + jnp.dot(h_lo, wrh_ref[...], preferred_element_type=F32)
          + jnp.dot(h_hi, wrl_ref[...], preferred_element_type=F32)) + br_ref[...]
    comb_ref[...] = _route(lg)


def _out_proj(ys, ya, x2d, wo_a, wo_b, ln_g, ln_b, wr_hi, wr_lo, br):
    t = x2d.shape[0]
    tm = min(512, t)
    row = lambda i: (i, 0)
    const = lambda i: (0, 0)
    return pl.pallas_call(
        _out_proj_kernel,
        grid=(t // tm,),
        in_specs=[
            pl.BlockSpec((tm, SSD_INNER), row),
            pl.BlockSpec((tm, ATT_INNER), row),
            pl.BlockSpec((tm, D_MODEL), row),
            pl.BlockSpec((SSD_INNER, D_MODEL), const),
            pl.BlockSpec((ATT_INNER, D_MODEL), const),
            pl.BlockSpec((1, D_MODEL), const),
            pl.BlockSpec((1, D_MODEL), const),
            pl.BlockSpec((D_MODEL, LANES), const),
            pl.BlockSpec((D_MODEL, LANES), const),
            pl.BlockSpec((1, LANES), const),
        ],
        out_specs=[
            pl.BlockSpec((tm, D_MODEL), row),
            pl.BlockSpec((tm, D_MODEL), row),
            pl.BlockSpec((tm, LANES), row),
        ],
        out_shape=(
            jax.ShapeDtypeStruct((t, D_MODEL), F32),
            jax.ShapeDtypeStruct((t, D_MODEL), BF16),
            jax.ShapeDtypeStruct((t, LANES), F32),
        ),
        compiler_params=pltpu.CompilerParams(
            dimension_semantics=("arbitrary",), vmem_limit_bytes=VMEM_LIMIT_BYTES),
        name="out_proj_ln_router",
    )(ys, ya, x2d, wo_a, wo_b, ln_g, ln_b, wr_hi, wr_lo, br)


def _moe_kernel(hb_ref, h_ref, comb_ref, wgu_ref, wd_ref, g_ref, b_ref, o_ref, acc_ref):
    e = pl.program_id(1)

    @pl.when(e == 0)
    def _():
        acc_ref[...] = jnp.zeros_like(acc_ref)

    lane = lax.broadcasted_iota(jnp.int32, (1, LANES), 1)
    w_e = jnp.sum(jnp.where(lane == ROUTE_E_LANE + e, comb_ref[...], 0.0), axis=1, keepdims=True)
    gu = jnp.dot(hb_ref[...], wgu_ref[0], preferred_element_type=F32)
    act = _silu(gu[:, :D_FF_EXPERT]) * gu[:, D_FF_EXPERT:]
    acc_ref[...] += w_e * jnp.dot(act.astype(BF16), wd_ref[0], preferred_element_type=F32)

    @pl.when(e == N_EXPERTS - 1)
    def _():
        o_ref[...] = _layer_norm(DEEPNORM_ALPHA * h_ref[...] + acc_ref[...], g_ref[...], b_ref[...])


def _moe(hb, h, comb, wgu, wd, ln_g, ln_b):
    t = h.shape[0]
    tm = min(1024, t)
    row = lambda i, e: (i, 0)
    const = lambda i, e: (0, 0)
    return pl.pallas_call(
        _moe_kernel,
        grid=(t // tm, N_EXPERTS),
        in_specs=[
            pl.BlockSpec((tm, D_MODEL), row),
            pl.BlockSpec((tm, D_MODEL), row),
            pl.BlockSpec((tm, LANES), row),
            pl.BlockSpec((1, D_MODEL, 2 * D_FF_EXPERT), lambda i, e: (e, 0, 0)),
            pl.BlockSpec((1, D_FF_EXPERT, D_MODEL), lambda i, e: (e, 0, 0)),
            pl.BlockSpec((1, D_MODEL), const),
            pl.BlockSpec((1, D_MODEL), const),
        ],
        out_specs=pl.BlockSpec((tm, D_MODEL), row),
        out_shape=jax.ShapeDtypeStruct((t, D_MODEL), F32),
        scratch_shapes=[pltpu.VMEM((tm, D_MODEL), F32)],
        compiler_params=pltpu.CompilerParams(
            dimension_semantics=("arbitrary", "arbitrary"), vmem_limit_bytes=VMEM_LIMIT_BYTES),
        name="moe_ln",
    )(hb, h, comb, wgu, wd, ln_g, ln_b)


def _pad_lanes(a, n=LANES):
    return jnp.pad(a, ((0, 0), (0, n - a.shape[1])))


def _layer(h2d, batch, seq, w_in, b_in, conv_w, conv_b, a_log, d_skip, ssd_norm_g, w_out,
           ln1_g, ln1_b, rg_w, rg_b, re_w, re_b, w_gate, w_up, w_down, ln2_g, ln2_b):
    o_dt = SSD_INNER + SSD_CONV_DIM
    o_q = o_dt + SSD_HEADS
    o_f = o_q + 3 * ATT_INNER
    w_main = jnp.concatenate([w_in[:, :o_dt], w_in[:, o_q:o_f]], axis=1).astype(BF16)
    b_main = jnp.concatenate([b_in[:o_dt], b_in[o_q:o_f]])[None, :]
    w_dt, w_f = w_in[:, o_dt:o_q], w_in[:, o_f:]
    w_small = _pad_lanes(jnp.concatenate([w_dt, w_f, w_dt], axis=1)).astype(BF16)
    b_small = _pad_lanes(jnp.concatenate([b_in[o_dt:o_q], b_in[o_f:], b_in[o_dt:o_q]])[None, :])
    a_neg = -jnp.exp(a_log.astype(F32))
    a_row = _pad_lanes(a_neg[None, :])
    a_exp = jnp.repeat(a_neg, SSD_HEAD_DIM)[None, :]
    dskip_exp = jnp.repeat(d_skip.astype(F32), SSD_HEAD_DIM)[None, :]

    z, xbc, q, k, v, gate, gate_t = _in_proj(h2d, w_main, b_main, w_small, b_small, a_row, seq)
    y_ssd = _ssd(xbc, z, gate, conv_w, conv_b[None, :], a_exp, dskip_exp, ssd_norm_g[None, :],
                 batch, seq)
    y_att = _attention(q, k, v, gate, gate_t, batch, seq)

    w_r = _pad_lanes(jnp.concatenate(
        [rg_w, jnp.transpose(re_w, (1, 0, 2)).reshape(D_MODEL, N_EXPERTS)], axis=1))
    b_r = _pad_lanes(jnp.concatenate([rg_b, re_b.reshape(N_EXPERTS)])[None, :])
    wr_hi = w_r.astype(BF16)
    wr_lo = (w_r - wr_hi.astype(F32)).astype(BF16)
    wo = w_out.astype(BF16)
    h1, h1b, comb = _out_proj(y_ssd, y_att, h2d, wo[:SSD_INNER], wo[SSD_INNER:],
                              ln1_g[None, :], ln1_b[None, :], wr_hi, wr_lo, b_r)

    wgu = jnp.concatenate([w_gate, w_up], axis=2).astype(BF16)
    wd = w_down.astype(BF16)
    return _moe(h1b, h1, comb, wgu, wd, ln2_g[None, :], ln2_b[None, :])


def kernel(x, w_in, b_in, conv_w, conv_b, a_log, d_skip, ssd_norm_g, w_out, ln1_g, ln1_b,
           router_group_w, router_group_b, router_expert_w, router_expert_b, w_gate, w_up,
           w_down, ln2_g, ln2_b):
    batch, seq, d = x.shape
    h = x.reshape(batch * seq, d)
    for l in range(w_in.shape[0]):
        h = _layer(h, batch, seq, w_in[l], b_in[l], conv_w[l], conv_b[l], a_log[l], d_skip[l],
                   ssd_norm_g[l], w_out[l], ln1_g[l], ln1_b[l], router_group_w[l],
                   router_group_b[l], router_expert_w[l], router_expert_b[l], w_gate[l],
                   w_up[l], w_down[l], ln2_g[l], ln2_b[l])
    return h.reshape(batch, seq, d)
```

```python
import functools

import jax
import jax.numpy as jnp
import numpy as np
from jax import lax
from jax.experimental import pallas as pl
from jax.experimental.pallas import tpu as pltpu

F32 = jnp.float32
BF16 = jnp.bfloat16

D_MODEL = 1024
SSD_HEADS = 8
SSD_HEAD_DIM = 64
SSD_INNER = SSD_HEADS * SSD_HEAD_DIM
SSD_GROUPS = 2
SSD_STATE = 128
CONV_WIDTH = 4
SSD_CONV_DIM = SSD_INNER + 2 * SSD_GROUPS * SSD_STATE
ATT_HEADS = 8
ATT_HEAD_DIM = 64
ATT_INNER = ATT_HEADS * ATT_HEAD_DIM
N_EXPERT_GROUPS = 4
EXPERTS_PER_GROUP = 4
N_EXPERTS = N_EXPERT_GROUPS * EXPERTS_PER_GROUP
D_FF_EXPERT = 512
DEPTH = 1
DEEPNORM_ALPHA = (2.0 * DEPTH) ** 0.25
LN_EPS = 1e-5
RMS_EPS = 1e-5

LANES = 128
SUBLANES = 8
VMEM_LIMIT_BYTES = 56 * 1024 * 1024

GATE_A_LANE = 0
GATE_F_LANE = SSD_HEADS
GATE_DT_LANE = 2 * SSD_HEADS
ROUTE_E_LANE = N_EXPERT_GROUPS

FB_LANES = 6
ATT_BLOCK = 256
ATT_PAIRS_PER_STEP = 2

SSD_CHUNK = 64
NEG_BIG = -1e30
LOG2E = 1.4426950408889634


def _split3(v):
    hi = v.astype(BF16)
    r1 = v - hi.astype(F32)
    mid = r1.astype(BF16)
    lo = (r1 - mid.astype(F32)).astype(BF16)
    return hi, mid, lo


def _silu(v):
    return v * (1.0 / (1.0 + jnp.exp(-v)))


def _softplus(v):
    return jnp.maximum(v, 0.0) + jnp.log1p(jnp.exp(-jnp.abs(v)))


def _in_proj_kernel(x_ref, wm_ref, bm_ref, ws_ref, bs_ref, arow_ref, tri_ref, place_ref, ones_ref,
                    z_ref, xbc_ref, q_ref, kt_ref, v_ref, gate_ref, qf_ref, kft_ref,
                    carry_ref, *, tiles_per_seq):
    i = pl.program_id(0)
    tm = x_ref.shape[0]
    tk = kt_ref.shape[2]
    xb = x_ref[...].astype(BF16)

    def put_transposed(ref, val):
        for c in range(tm // tk):
            ref[c] = val[c * tk:(c + 1) * tk, :].T.astype(ref.dtype)

    col = 0
    for ref, scale in ((z_ref, None), (xbc_ref, None), (q_ref, ATT_HEAD_DIM ** -0.5 * LOG2E),
                       (kt_ref, None), (v_ref, None)):
        n = ATT_INNER if ref is kt_ref else ref.shape[1]
        acc = jnp.dot(xb, wm_ref[:, col:col + n], preferred_element_type=F32) + bm_ref[:, col:col + n]
        if scale is not None:
            acc = acc * scale
        if ref is kt_ref:
            put_transposed(ref, acc)
        else:
            ref[...] = acc.astype(ref.dtype)
        col += n

    s = jnp.dot(xb, ws_ref[...], preferred_element_type=F32) + bs_ref[...]
    lane = lax.broadcasted_iota(jnp.int32, (1, LANES), 1)
    sp = _softplus(s)
    log_f = -_softplus(-s)
    val = jnp.where(lane < GATE_F_LANE, sp * arow_ref[...],
                    jnp.where(lane < GATE_DT_LANE, log_f,
                              jnp.where(lane < GATE_DT_LANE + SSD_HEADS, sp, 0.0)))

    @pl.when(i % tiles_per_seq == 0)
    def _():
        carry_ref[...] = jnp.zeros_like(carry_ref)

    hi, mid, lo = _split3(val)
    tri = tri_ref[...]
    cs = (jnp.dot(tri, hi, preferred_element_type=F32)
          + jnp.dot(tri, mid, preferred_element_type=F32)
          + jnp.dot(tri, lo, preferred_element_type=F32)) + carry_ref[...]
    carry_ref[...] = cs[tm - 1:tm, :]
    out = jnp.where(lane < GATE_DT_LANE, cs, val)
    gate_ref[...] = out
    pieces = jnp.concatenate(_split3(cs * LOG2E), axis=1)
    fb = jnp.dot(pieces, place_ref[...], preferred_element_type=F32) + ones_ref[...]
    qf_ref[...] = fb[:, :LANES].astype(BF16)
    put_transposed(kft_ref, fb[:, LANES:])


def _forget_bias_placement():
    place = np.zeros((3 * LANES, 2 * LANES), np.float32)
    ones = np.zeros((1, 2 * LANES), np.float32)
    for h in range(ATT_HEADS):
        for piece in range(3):
            src = piece * LANES + GATE_F_LANE + h
            place[src, FB_LANES * h + piece] = 1.0
            place[src, LANES + FB_LANES * h + 3 + piece] = -1.0
            ones[0, FB_LANES * h + 3 + piece] = 1.0
            ones[0, LANES + FB_LANES * h + piece] = 1.0
    return jnp.asarray(place, BF16), jnp.asarray(ones, F32)


def _in_proj(x2d, w_main, b_main, w_small, b_small, a_row, seq):
    t = x2d.shape[0]
    tm = min(512, seq)
    tk = min(ATT_BLOCK, seq)
    tri = jnp.tri(tm, dtype=BF16)
    place, ones = _forget_bias_placement()
    n_main = w_main.shape[1]
    row = lambda i: (i, 0)
    const = lambda i: (0, 0)
    outs = (
        jax.ShapeDtypeStruct((t, SSD_INNER), BF16),
        jax.ShapeDtypeStruct((t, SSD_CONV_DIM), BF16),
        jax.ShapeDtypeStruct((t, ATT_INNER), BF16),
        jax.ShapeDtypeStruct((t // tk, ATT_INNER, tk), BF16),
        jax.ShapeDtypeStruct((t, ATT_INNER), BF16),
        jax.ShapeDtypeStruct((t, LANES), F32),
        jax.ShapeDtypeStruct((t, LANES), BF16),
        jax.ShapeDtypeStruct((t // tk, LANES, tk), BF16),
    )
    return pl.pallas_call(
        functools.partial(_in_proj_kernel, tiles_per_seq=seq // tm),
        grid=(t // tm,),
        in_specs=[
            pl.BlockSpec((tm, D_MODEL), row),
            pl.BlockSpec((D_MODEL, n_main), const),
            pl.BlockSpec((1, n_main), const),
            pl.BlockSpec((D_MODEL, LANES), const),
            pl.BlockSpec((1, LANES), const),
            pl.BlockSpec((1, LANES), const),
            pl.BlockSpec((tm, tm), const),
            pl.BlockSpec((3 * LANES, 2 * LANES), const),
            pl.BlockSpec((1, 2 * LANES), const),
        ],
        out_specs=[
            pl.BlockSpec((tm, SSD_INNER), row),
            pl.BlockSpec((tm, SSD_CONV_DIM), row),
            pl.BlockSpec((tm, ATT_INNER), row),
            pl.BlockSpec((tm // tk, ATT_INNER, tk), lambda i: (i, 0, 0)),
            pl.BlockSpec((tm, ATT_INNER), row),
            pl.BlockSpec((tm, LANES), row),
            pl.BlockSpec((tm, LANES), row),
            pl.BlockSpec((tm // tk, LANES, tk), lambda i: (i, 0, 0)),
        ],
        out_shape=outs,
        scratch_shapes=[pltpu.VMEM((1, LANES), F32)],
        compiler_params=pltpu.CompilerParams(
            dimension_semantics=("arbitrary",), vmem_limit_bytes=VMEM_LIMIT_BYTES),
        name="in_proj",
    )(x2d, w_main, b_main, w_small, b_small, a_row, tri, place, ones)


def _expand_heads(gate, lane0, rows):
    lane = lax.broadcasted_iota(jnp.int32, (1, LANES), 1)
    pieces = []
    for p in range(SSD_HEADS // 2):
        a = jnp.broadcast_to(gate[:, lane0 + 2 * p:lane0 + 2 * p + 1], (rows, LANES))
        b = jnp.broadcast_to(gate[:, lane0 + 2 * p + 1:lane0 + 2 * p + 2], (rows, LANES))
        pieces.append(jnp.where(lane < SSD_HEAD_DIM, a, b))
    return jnp.concatenate(pieces, axis=1)


def _ssd_kernel(xbc_ref, z_ref, gate_ref, cw_ref, cb_ref, aexp_ref, dskip_ref, ng_ref,
                y_ref, ext_ref, state_ref):
    j = pl.program_id(1)
    rows = xbc_ref.shape[0]
    q = SSD_CHUNK
    gw = SSD_INNER // SSD_GROUPS
    hpg = SSD_HEADS // SSD_GROUPS
    pad = SUBLANES

    @pl.when(j == 0)
    def _():
        ext_ref[0:pad, :] = jnp.zeros((pad, SSD_CONV_DIM), F32)
        state_ref[...] = jnp.zeros_like(state_ref)

    ext_ref[pad:pad + rows, :] = xbc_ref[...].astype(F32)
    conv = cb_ref[...]
    for kk in range(CONV_WIDTH):
        off = pad - (CONV_WIDTH - 1) + kk
        conv = conv + ext_ref[off:off + rows, :] * cw_ref[kk:kk + 1, :]
    ext_ref[0:pad, :] = ext_ref[rows:rows + pad, :]
    u = _silu(conv)
    xs = u[:, :SSD_INNER]
    b_all = u[:, SSD_INNER:SSD_INNER + SSD_GROUPS * SSD_STATE].astype(BF16)
    c_all = u[:, SSD_INNER + SSD_GROUPS * SSD_STATE:].astype(BF16)

    gate = gate_ref[...]
    dt_e = _expand_heads(gate, GATE_DT_LANE, rows)
    a_e = _expand_heads(gate, GATE_A_LANE, rows)
    xdt = xs * dt_e
    da_e = dt_e * aexp_ref[...]

    r_i = lax.broadcasted_iota(jnp.int32, (q, gw), 0)
    c_i = lax.broadcasted_iota(jnp.int32, (q, gw), 1)
    s_i = c_i % q
    eye_t = s_i == r_i
    tril_t = s_i <= r_i
    rb = lax.broadcasted_iota(jnp.int32, (hpg * q, gw), 0) // q
    cbk = lax.broadcasted_iota(jnp.int32, (hpg * q, gw), 1) // SSD_HEAD_DIM
    blk = rb == cbk

    ys = []
    for c in range(rows // q):
        r0 = c * q
        a_c = a_e[r0:r0 + q, :]
        a_prev = a_c[0:1, :] - da_e[r0:r0 + 1, :]
        a_end = a_c[q - 1:q, :]
        y_groups = []
        for g in range(SSD_GROUPS):
            l0 = g * gw
            a_cg = a_c[:, l0:l0 + gw]
            diag = jnp.sum(jnp.where(eye_t, a_cg, 0.0), axis=0, keepdims=True)
            seg = jnp.minimum(a_cg - diag, 0.0)
            lcat = jnp.where(tril_t, jnp.exp(seg), 0.0)
            bg = b_all[r0:r0 + q, g * SSD_STATE:(g + 1) * SSD_STATE]
            cg = c_all[r0:r0 + q, g * SSD_STATE:(g + 1) * SSD_STATE]
            b_tile = jnp.concatenate([bg] * hpg, axis=0)
            gcat = lax.dot_general(cg, b_tile, (((1,), (1,)), ((), ())),
                                   preferred_element_type=F32)
            xg = xdt[r0:r0 + q, l0:l0 + gw]
            xg_b = xg.astype(BF16)
            xbd = jnp.where(blk, jnp.concatenate([xg_b] * hpg, axis=0), jnp.zeros((), BF16))
            y_diag = jnp.dot((gcat * lcat).astype(BF16), xbd, preferred_element_type=F32)
            st = state_ref[g]
            y_off = jnp.dot(cg, st.astype(BF16), preferred_element_type=F32) \
                * jnp.exp(a_cg - a_prev[:, l0:l0 + gw])
            dte = jnp.exp(a_end[:, l0:l0 + gw] - a_cg)
            upd = lax.dot_general(bg, (xg * dte).astype(BF16), (((0,), (0,)), ((), ())),
                                  preferred_element_type=F32)
            state_ref[g] = st * jnp.exp(a_end[:, l0:l0 + gw] - a_prev[:, l0:l0 + gw]) + upd
            y_groups.append(y_diag + y_off)
        ys.append(jnp.concatenate(y_groups, axis=1))
    y = jnp.concatenate(ys, axis=0) + xs * dskip_ref[...]
    y = y * _silu(z_ref[...].astype(F32))
    outs = []
    for g in range(SSD_GROUPS):
        yg = y[:, g * gw:(g + 1) * gw]
        ms = jnp.mean(yg * yg, axis=1, keepdims=True)
        outs.append(yg * lax.rsqrt(ms + RMS_EPS))
    y_ref[...] = (jnp.concatenate(outs, axis=1) * ng_ref[...]).astype(y_ref.dtype)


def _ssd(xbc, z, gate, conv_w, conv_b, a_exp, dskip_exp, norm_g, batch, seq):
    t = xbc.shape[0]
    rows = min(256, seq)
    nj = seq // rows
    row = lambda b, j: (b * nj + j, 0)
    const = lambda b, j: (0, 0)
    return pl.pallas_call(
        _ssd_kernel,
        grid=(batch, nj),
        in_specs=[
            pl.BlockSpec((rows, SSD_CONV_DIM), row),
            pl.BlockSpec((rows, SSD_INNER), row),
            pl.BlockSpec((rows, LANES), row),
            pl.BlockSpec((CONV_WIDTH, SSD_CONV_DIM), const),
            pl.BlockSpec((1, SSD_CONV_DIM), const),
            pl.BlockSpec((1, SSD_INNER), const),
            pl.BlockSpec((1, SSD_INNER), const),
            pl.BlockSpec((1, SSD_INNER), const),
        ],
        out_specs=pl.BlockSpec((rows, SSD_INNER), row),
        out_shape=jax.ShapeDtypeStruct((t, SSD_INNER), BF16),
        scratch_shapes=[
            pltpu.VMEM((rows + 2 * SUBLANES, SSD_CONV_DIM), F32),
            pltpu.VMEM((SSD_GROUPS, SSD_STATE, SSD_INNER // SSD_GROUPS), F32),
        ],
        compiler_params=pltpu.CompilerParams(
            dimension_semantics=("arbitrary", "arbitrary"), vmem_limit_bytes=VMEM_LIMIT_BYTES),
        name="ssd",
    )(xbc, z, gate, conv_w, conv_b, a_exp, dskip_exp, norm_g)


def _attn_kernel(q_ref, kt_ref, v_ref, qf_ref, kft_ref, o_ref, vaug_ref, *, tq, pairs):
    pair0 = pl.program_id(1) * pairs
    nk = kt_ref.shape[0]
    lane = lax.broadcasted_iota(jnp.int32, (1, LANES), 1)
    in_head = (lane < ATT_HEAD_DIM, lane >= ATT_HEAD_DIM)
    ones = jnp.ones((tq, LANES), BF16)
    for pp in range(pairs):
        for c in range(nk):
            vaug_ref[pp, c] = jnp.concatenate(
                [v_ref[c * tq:(c + 1) * tq, pp * LANES:(pp + 1) * LANES], ones], axis=1)
    row = lax.broadcasted_iota(jnp.int32, (2 * tq, tq), 0)
    col = lax.broadcasted_iota(jnp.int32, (2 * tq, tq), 1)
    causal = jnp.where(row >= tq, row - tq, row) >= col
    zero = jnp.zeros((), BF16)

    def logits(qa, pp, jk):
        ka_t = jnp.concatenate([kt_ref[jk, pp * LANES:(pp + 1) * LANES, :], kft_ref[jk]], axis=0)
        return jnp.dot(qa, ka_t, preferred_element_type=F32)

    def update(s, m, acc, pp, jk):
        m_new = jnp.maximum(m, jnp.max(s, axis=1, keepdims=True))
        alpha = jnp.exp2(m - m_new)
        p = jnp.exp2(s - m_new).astype(BF16)
        acc = alpha * acc + jnp.dot(p, vaug_ref[pp, jk], preferred_element_type=F32)
        return m_new, acc

    def q_block(i, _):
        q0 = pl.multiple_of(i * tq, tq)
        qfv = qf_ref[pl.ds(q0, tq), :]
        qas = []
        for pp in range(pairs):
            qv = q_ref[pl.ds(q0, tq), pp * LANES:(pp + 1) * LANES]
            halves = []
            for hh in range(2):
                fb0 = FB_LANES * (2 * (pair0 + pp) + hh)
                in_fb = (lane >= fb0) & (lane < fb0 + FB_LANES)
                halves.append(jnp.concatenate(
                    [jnp.where(in_head[hh], qv, zero), jnp.where(in_fb, qfv, zero)], axis=1))
            qas.append(jnp.concatenate(halves, axis=0))

        def body(jk, carry):
            new = []
            for pp in range(pairs):
                s, m, acc = carry[pp]
                s_next = logits(qas[pp], pp, jk + 1)
                m, acc = update(s, m, acc, pp, jk)
                new.append((s_next, m, acc))
            return tuple(new)

        init = tuple((logits(qas[pp], pp, 0), jnp.full((2 * tq, 1), NEG_BIG, F32),
                      jnp.zeros((2 * tq, 2 * LANES), F32)) for pp in range(pairs))
        carry = lax.fori_loop(0, i, body, init)
        for pp in range(pairs):
            s, m, acc = carry[pp]
            _, acc = update(jnp.where(causal, s, NEG_BIG), m, acc, pp, i)
            inv = 1.0 / acc[:, LANES:LANES + 1]
            out = jnp.where(in_head[0], acc[:tq, :LANES] * inv[:tq], acc[tq:, :LANES] * inv[tq:])
            o_ref[pl.ds(q0, tq), pp * LANES:(pp + 1) * LANES] = out.astype(o_ref.dtype)
        return 0

    lax.fori_loop(0, nk, q_block, 0)


def _attention(q, k_t, v, qf, kf_t, batch, seq):
    t = q.shape[0]
    tq = min(ATT_BLOCK, seq)
    nk = seq // tq
    pairs = ATT_PAIRS_PER_STEP
    width = pairs * LANES
    head_pairs = lambda b, p: (b, p)
    shared = lambda b, p: (b, 0)
    return pl.pallas_call(
        functools.partial(_attn_kernel, tq=tq, pairs=pairs),
        grid=(batch, ATT_HEADS // (2 * pairs)),
        in_specs=[
            pl.BlockSpec((seq, width), head_pairs),
            pl.BlockSpec((nk, width, tq), lambda b, p: (b, p, 0)),
            pl.BlockSpec((seq, width), head_pairs),
            pl.BlockSpec((seq, LANES), shared),
            pl.BlockSpec((nk, LANES, tq), lambda b, p: (b, 0, 0)),
        ],
        out_specs=pl.BlockSpec((seq, width), head_pairs),
        out_shape=jax.ShapeDtypeStruct((t, ATT_INNER), BF16),
        scratch_shapes=[pltpu.VMEM((pairs, nk, tq, 2 * LANES), BF16)],
        compiler_params=pltpu.CompilerParams(
            dimension_semantics=("arbitrary", "arbitrary"), vmem_limit_bytes=VMEM_LIMIT_BYTES),
        name="fox_attention",
    )(q, k_t, v, qf, kf_t)


def _layer_norm(u, g, b):
    mu = jnp.mean(u, axis=1, keepdims=True)
    d = u - mu
    var = jnp.mean(d * d, axis=1, keepdims=True)
    return d * lax.rsqrt(var + LN_EPS) * g + b


def _route(lg):
    lane = lax.broadcasted_iota(jnp.int32, (1, LANES), 1)
    lane_f = lane.astype(F32)
    is_g = lane < N_EXPERT_GROUPS
    gl = jnp.where(is_g, lg, NEG_BIG)
    gmax = jnp.max(gl, axis=1, keepdims=True)
    gsum = jnp.sum(jnp.where(is_g, jnp.exp(gl - gmax), 0.0), axis=1, keepdims=True)
    g_val = 1.0 / gsum
    g_idx = jnp.min(jnp.where(is_g & (gl == gmax), lane_f, float(LANES)), axis=1, keepdims=True)
    e_grp = ((lane - ROUTE_E_LANE) // EXPERTS_PER_GROUP).astype(F32)
    in_grp = (lane >= ROUTE_E_LANE) & (lane < ROUTE_E_LANE + N_EXPERTS) & (e_grp == g_idx)
    el = jnp.where(in_grp, lg, NEG_BIG)
    m1 = jnp.max(el, axis=1, keepdims=True)
    i1 = jnp.min(jnp.where(in_grp & (el == m1), lane_f, float(LANES)), axis=1, keepdims=True)
    rest = in_grp & (lane_f != i1)
    el2 = jnp.where(rest, lg, NEG_BIG)
    m2 = jnp.max(el2, axis=1, keepdims=True)
    i2 = jnp.min(jnp.where(rest & (el2 == m2), lane_f, float(LANES)), axis=1, keepdims=True)
    r = jnp.exp(m2 - m1)
    w1 = 1.0 / (1.0 + r)
    w2 = r * w1
    return jnp.where(lane_f == i1, g_val * w1, jnp.where(lane_f == i2, g_val * w2, 0.0))


def _out_proj_kernel(ys_ref, ya_ref, x_ref, wa_ref, wb_ref, g_ref, b_ref, wrh_ref, wrl_ref, br_ref,
                     h_ref, hb_ref, comb_ref):
    mix = jnp.dot(ys_ref[...], wa_ref[...], preferred_element_type=F32)
    mix = mix + jnp.dot(ya_ref[...], wb_ref[...], preferred_element_type=F32)
    h = _layer_norm(DEEPNORM_ALPHA * x_ref[...] + mix, g_ref[...], b_ref[...])
    h_ref[...] = h
    h_hi = h.astype(BF16)
    hb_ref[...] = h_hi
    h_lo = (h - h_hi.astype(F32)).astype(BF16)
    lg = (jnp.dot(h_hi, wrh_ref[...], preferred_element_type=F32)
          + jnp.dot(h_lo, wrh_ref[...], preferred_element_type=F32)
          + jnp.dot(h_hi, wrl_ref[...], preferred_element_type=F32)) + br_ref[...]
    comb_ref[...] = _route(lg)


def _out_proj(ys, ya, x2d, wo_a, wo_b, ln_g, ln_b, wr_hi, wr_lo, br):
    t = x2d.shape[0]
    tm = min(512, t)
    row = lambda i: (i, 0)
    const = lambda i: (0, 0)
    return pl.pallas_call(
        _out_proj_kernel,
        grid=(t // tm,),
        in_specs=[
            pl.BlockSpec((tm, SSD_INNER), row),
            pl.BlockSpec((tm, ATT_INNER), row),
            pl.BlockSpec((tm, D_MODEL), row),
            pl.BlockSpec((SSD_INNER, D_MODEL), const),
            pl.BlockSpec((ATT_INNER, D_MODEL), const),
            pl.BlockSpec((1, D_MODEL), const),
            pl.BlockSpec((1, D_MODEL), const),
            pl.BlockSpec((D_MODEL, LANES), const),
            pl.BlockSpec((D_MODEL, LANES), const),
            pl.BlockSpec((1, LANES), const),
        ],
        out_specs=[
            pl.BlockSpec((tm, D_MODEL), row),
            pl.BlockSpec((tm, D_MODEL), row),
            pl.BlockSpec((tm, LANES), row),
        ],
        out_shape=(
            jax.ShapeDtypeStruct((t, D_MODEL), F32),
            jax.ShapeDtypeStruct((t, D_MODEL), BF16),
            jax.ShapeDtypeStruct((t, LANES), F32),
        ),
        compiler_params=pltpu.CompilerParams(
            dimension_semantics=("arbitrary",), vmem_limit_bytes=VMEM_LIMIT_BYTES),
        name="out_proj_ln_router",
    )(ys, ya, x2d, wo_a, wo_b, ln_g, ln_b, wr_hi, wr_lo, br)


def _moe_kernel(hb_ref, h_ref, comb_ref, wgu_ref, wd_ref, g_ref, b_ref, o_ref, acc_ref):
    e = pl.program_id(1)

    @pl.when(e == 0)
    def _():
        acc_ref[...] = jnp.zeros_like(acc_ref)

    lane = lax.broadcasted_iota(jnp.int32, (1, LANES), 1)
    w_e = jnp.sum(jnp.where(lane == ROUTE_E_LANE + e, comb_ref[...], 0.0), axis=1, keepdims=True)
    gu = jnp.dot(hb_ref[...], wgu_ref[0], preferred_element_type=F32)
    act = _silu(gu[:, :D_FF_EXPERT]) * gu[:, D_FF_EXPERT:]
    acc_ref[...] += w_e * jnp.dot(act.astype(BF16), wd_ref[0], preferred_element_type=F32)

    @pl.when(e == N_EXPERTS - 1)
    def _():
        o_ref[...] = _layer_norm(DEEPNORM_ALPHA * h_ref[...] + acc_ref[...], g_ref[...], b_ref[...])


def _moe(hb, h, comb, wgu, wd, ln_g, ln_b):
    t = h.shape[0]
    tm = min(1024, t)
    row = lambda i, e: (i, 0)
    const = lambda i, e: (0, 0)
    return pl.pallas_call(
        _moe_kernel,
        grid=(t // tm, N_EXPERTS),
        in_specs=[
            pl.BlockSpec((tm, D_MODEL), row),
            pl.BlockSpec((tm, D_MODEL), row),
            pl.BlockSpec((tm, LANES), row),
            pl.BlockSpec((1, D_MODEL, 2 * D_FF_EXPERT), lambda i, e: (e, 0, 0)),
            pl.BlockSpec((1, D_FF_EXPERT, D_MODEL), lambda i, e: (e, 0, 0)),
            pl.BlockSpec((1, D_MODEL), const),
            pl.BlockSpec((1, D_MODEL), const),
        ],
        out_specs=pl.BlockSpec((tm, D_MODEL), row),
        out_shape=jax.ShapeDtypeStruct((t, D_MODEL), F32),
        scratch_shapes=[pltpu.VMEM((tm, D_MODEL), F32)],
        compiler_params=pltpu.CompilerParams(
            dimension_semantics=("arbitrary", "arbitrary"), vmem_limit_bytes=VMEM_LIMIT_BYTES),
        name="moe_ln",
    )(hb, h, comb, wgu, wd, ln_g, ln_b)


def _pad_lanes(a, n=LANES):
    return jnp.pad(a, ((0, 0), (0, n - a.shape[1])))


def _layer(h2d, batch, seq, w_in, b_in, conv_w, conv_b, a_log, d_skip, ssd_norm_g, w_out,
           ln1_g, ln1_b, rg_w, rg_b, re_w, re_b, w_gate, w_up, w_down, ln2_g, ln2_b):
    o_dt = SSD_INNER + SSD_CONV_DIM
    o_q = o_dt + SSD_HEADS
    o_f = o_q + 3 * ATT_INNER
    w_main = jnp.concatenate([w_in[:, :o_dt], w_in[:, o_q:o_f]], axis=1).astype(BF16)
    b_main = jnp.concatenate([b_in[:o_dt], b_in[o_q:o_f]])[None, :]
    w_dt, w_f = w_in[:, o_dt:o_q], w_in[:, o_f:]
    w_small = _pad_lanes(jnp.concatenate([w_dt, w_f, w_dt], axis=1)).astype(BF16)
    b_small = _pad_lanes(jnp.concatenate([b_in[o_dt:o_q], b_in[o_f:], b_in[o_dt:o_q]])[None, :])
    a_neg = -jnp.exp(a_log.astype(F32))
    a_row = _pad_lanes(a_neg[None, :])
    a_exp = jnp.repeat(a_neg, SSD_HEAD_DIM)[None, :]
    dskip_exp = jnp.repeat(d_skip.astype(F32), SSD_HEAD_DIM)[None, :]

    z, xbc, q, k_t, v, gate, qf, kf_t = _in_proj(h2d, w_main, b_main, w_small, b_small, a_row, seq)
    y_ssd = _ssd(xbc, z, gate, conv_w, conv_b[None, :], a_exp, dskip_exp, ssd_norm_g[None, :],
                 batch, seq)
    y_att = _attention(q, k_t, v, qf, kf_t, batch, seq)

    w_r = _pad_lanes(jnp.concatenate(
        [rg_w, jnp.transpose(re_w, (1, 0, 2)).reshape(D_MODEL, N_EXPERTS)], axis=1))
    b_r = _pad_lanes(jnp.concatenate([rg_b, re_b.reshape(N_EXPERTS)])[None, :])
    wr_hi = w_r.astype(BF16)
    wr_lo = (w_r - wr_hi.astype(F32)).astype(BF16)
    wo = w_out.astype(BF16)
    h1, h1b, comb = _out_proj(y_ssd, y_att, h2d, wo[:SSD_INNER], wo[SSD_INNER:],
                              ln1_g[None, :], ln1_b[None, :], wr_hi, wr_lo, b_r)

    wgu = jnp.concatenate([w_gate, w_up], axis=2).astype(BF16)
    wd = w_down.astype(BF16)
    return _moe(h1b, h1, comb, wgu, wd, ln2_g[None, :], ln2_b[None, :])


def kernel(x, w_in, b_in, conv_w, conv_b, a_log, d_skip, ssd_norm_g, w_out, ln1_g, ln1_b,
           router_group_w, router_group_b, router_expert_w, router_expert_b, w_gate, w_up,
           w_down, ln2_g, ln2_b):
    batch, seq, d = x.shape
    h = x.reshape(batch * seq, d)
    for l in range(w_in.shape[0]):
        h = _layer(h, batch, seq, w_in[l], b_in[l], conv_w[l], conv_b[l], a_log[l], d_skip[l],
                   ssd_norm_g[l], w_out[l], ln1_g[l], ln1_b[l], router_group_w[l],
                   router_group_b[l], router_expert_w[l], router_expert_b[l], w_gate[l],
                   w_up[l], w_down[l], ln2_g[l], ln2_b[l])
    return h.reshape(batch, seq, d)
```

```python
import functools

import jax
import jax.numpy as jnp
import numpy as np
from jax import lax
from jax.experimental import pallas as pl
from jax.experimental.pallas import tpu as pltpu

F32 = jnp.float32
BF16 = jnp.bfloat16

D_MODEL = 1024
SSD_HEADS = 8
SSD_HEAD_DIM = 64
SSD_INNER = SSD_HEADS * SSD_HEAD_DIM
SSD_GROUPS = 2
SSD_STATE = 128
CONV_WIDTH = 4
SSD_CONV_DIM = SSD_INNER + 2 * SSD_GROUPS * SSD_STATE
ATT_HEADS = 8
ATT_HEAD_DIM = 64
ATT_INNER = ATT_HEADS * ATT_HEAD_DIM
N_EXPERT_GROUPS = 4
EXPERTS_PER_GROUP = 4
N_EXPERTS = N_EXPERT_GROUPS * EXPERTS_PER_GROUP
D_FF_EXPERT = 512
DEPTH = 1
DEEPNORM_ALPHA = (2.0 * DEPTH) ** 0.25
LN_EPS = 1e-5
RMS_EPS = 1e-5

LANES = 128
SUBLANES = 8
VMEM_LIMIT_BYTES = 56 * 1024 * 1024

GATE_A_LANE = 0
GATE_F_LANE = SSD_HEADS
GATE_DT_LANE = 2 * SSD_HEADS
ROUTE_E_LANE = N_EXPERT_GROUPS
MOE_SORT_TILE = 256
MOE_GRANULE = 16
MOE_ROW_TILE = 512

FB_LANES = 6
ATT_BLOCK = 256
ATT_PAIRS_PER_STEP = 2

SSD_CHUNK = 64
NEG_BIG = -1e30
LOG2E = 1.4426950408889634


def _split3(v):
    hi = v.astype(BF16)
    r1 = v - hi.astype(F32)
    mid = r1.astype(BF16)
    lo = (r1 - mid.astype(F32)).astype(BF16)
    return hi, mid, lo


def _silu(v):
    return v * (1.0 / (1.0 + jnp.exp(-v)))


def _softplus(v):
    return jnp.maximum(v, 0.0) + jnp.log1p(jnp.exp(-jnp.abs(v)))


def _in_proj_kernel(x_ref, wm_ref, bm_ref, ws_ref, bs_ref, arow_ref, tri_ref, place_ref, ones_ref,
                    z_ref, xbc_ref, q_ref, kt_ref, v_ref, gate_ref, qf_ref, kft_ref,
                    carry_ref, *, tiles_per_seq):
    i = pl.program_id(0)
    tm = x_ref.shape[0]
    tk = kt_ref.shape[2]
    xb = x_ref[...].astype(BF16)

    def put_transposed(ref, val):
        for c in range(tm // tk):
            ref[c] = val[c * tk:(c + 1) * tk, :].T.astype(ref.dtype)

    col = 0
    for ref, scale in ((z_ref, None), (xbc_ref, None), (q_ref, ATT_HEAD_DIM ** -0.5 * LOG2E),
                       (kt_ref, None), (v_ref, None)):
        n = ATT_INNER if ref is kt_ref else ref.shape[1]
        acc = jnp.dot(xb, wm_ref[:, col:col + n], preferred_element_type=F32) + bm_ref[:, col:col + n]
        if scale is not None:
            acc = acc * scale
        if ref is kt_ref:
            put_transposed(ref, acc)
        else:
            ref[...] = acc.astype(ref.dtype)
        col += n

    s = jnp.dot(xb, ws_ref[...], preferred_element_type=F32) + bs_ref[...]
    lane = lax.broadcasted_iota(jnp.int32, (1, LANES), 1)
    sp = _softplus(s)
    log_f = -_softplus(-s)
    val = jnp.where(lane < GATE_F_LANE, sp * arow_ref[...],
                    jnp.where(lane < GATE_DT_LANE, log_f,
                              jnp.where(lane < GATE_DT_LANE + SSD_HEADS, sp, 0.0)))

    @pl.when(i % tiles_per_seq == 0)
    def _():
        carry_ref[...] = jnp.zeros_like(carry_ref)

    hi, mid, lo = _split3(val)
    tri = tri_ref[...]
    cs = (jnp.dot(tri, hi, preferred_element_type=F32)
          + jnp.dot(tri, mid, preferred_element_type=F32)
          + jnp.dot(tri, lo, preferred_element_type=F32)) + carry_ref[...]
    carry_ref[...] = cs[tm - 1:tm, :]
    out = jnp.where(lane < GATE_DT_LANE, cs, val)
    gate_ref[...] = out
    pieces = jnp.concatenate(_split3(cs * LOG2E), axis=1)
    fb = jnp.dot(pieces, place_ref[...], preferred_element_type=F32) + ones_ref[...]
    qf_ref[...] = fb[:, :LANES].astype(BF16)
    put_transposed(kft_ref, fb[:, LANES:])


def _forget_bias_placement():
    place = np.zeros((3 * LANES, 2 * LANES), np.float32)
    ones = np.zeros((1, 2 * LANES), np.float32)
    for h in range(ATT_HEADS):
        for piece in range(3):
            src = piece * LANES + GATE_F_LANE + h
            place[src, FB_LANES * h + piece] = 1.0
            place[src, LANES + FB_LANES * h + 3 + piece] = -1.0
            ones[0, FB_LANES * h + 3 + piece] = 1.0
            ones[0, LANES + FB_LANES * h + piece] = 1.0
    return jnp.asarray(place, BF16), jnp.asarray(ones, F32)


def _in_proj(x2d, w_main, b_main, w_small, b_small, a_row, seq):
    t = x2d.shape[0]
    tm = min(512, seq)
    tk = min(ATT_BLOCK, seq)
    tri = jnp.tri(tm, dtype=BF16)
    place, ones = _forget_bias_placement()
    n_main = w_main.shape[1]
    row = lambda i: (i, 0)
    const = lambda i: (0, 0)
    outs = (
        jax.ShapeDtypeStruct((t, SSD_INNER), BF16),
        jax.ShapeDtypeStruct((t, SSD_CONV_DIM), BF16),
        jax.ShapeDtypeStruct((t, ATT_INNER), BF16),
        jax.ShapeDtypeStruct((t // tk, ATT_INNER, tk), BF16),
        jax.ShapeDtypeStruct((t, ATT_INNER), BF16),
        jax.ShapeDtypeStruct((t, LANES), F32),
        jax.ShapeDtypeStruct((t, LANES), BF16),
        jax.ShapeDtypeStruct((t // tk, LANES, tk), BF16),
    )
    return pl.pallas_call(
        functools.partial(_in_proj_kernel, tiles_per_seq=seq // tm),
        grid=(t // tm,),
        in_specs=[
            pl.BlockSpec((tm, D_MODEL), row),
            pl.BlockSpec((D_MODEL, n_main), const),
            pl.BlockSpec((1, n_main), const),
            pl.BlockSpec((D_MODEL, LANES), const),
            pl.BlockSpec((1, LANES), const),
            pl.BlockSpec((1, LANES), const),
            pl.BlockSpec((tm, tm), const),
            pl.BlockSpec((3 * LANES, 2 * LANES), const),
            pl.BlockSpec((1, 2 * LANES), const),
        ],
        out_specs=[
            pl.BlockSpec((tm, SSD_INNER), row),
            pl.BlockSpec((tm, SSD_CONV_DIM), row),
            pl.BlockSpec((tm, ATT_INNER), row),
            pl.BlockSpec((tm // tk, ATT_INNER, tk), lambda i: (i, 0, 0)),
            pl.BlockSpec((tm, ATT_INNER), row),
            pl.BlockSpec((tm, LANES), row),
            pl.BlockSpec((tm, LANES), row),
            pl.BlockSpec((tm // tk, LANES, tk), lambda i: (i, 0, 0)),
        ],
        out_shape=outs,
        scratch_shapes=[pltpu.VMEM((1, LANES), F32)],
        compiler_params=pltpu.CompilerParams(
            dimension_semantics=("arbitrary",), vmem_limit_bytes=VMEM_LIMIT_BYTES),
        name="in_proj",
    )(x2d, w_main, b_main, w_small, b_small, a_row, tri, place, ones)


def _expand_heads(gate, lane0, rows):
    lane = lax.broadcasted_iota(jnp.int32, (1, LANES), 1)
    pieces = []
    for p in range(SSD_HEADS // 2):
        a = jnp.broadcast_to(gate[:, lane0 + 2 * p:lane0 + 2 * p + 1], (rows, LANES))
        b = jnp.broadcast_to(gate[:, lane0 + 2 * p + 1:lane0 + 2 * p + 2], (rows, LANES))
        pieces.append(jnp.where(lane < SSD_HEAD_DIM, a, b))
    return jnp.concatenate(pieces, axis=1)


def _ssd_kernel(xbc_ref, z_ref, gate_ref, cw_ref, cb_ref, aexp_ref, dskip_ref, ng_ref,
                y_ref, ext_ref, state_ref):
    j = pl.program_id(1)
    rows = xbc_ref.shape[0]
    q = SSD_CHUNK
    gw = SSD_INNER // SSD_GROUPS
    hpg = SSD_HEADS // SSD_GROUPS
    pad = SUBLANES

    @pl.when(j == 0)
    def _():
        ext_ref[0:pad, :] = jnp.zeros((pad, SSD_CONV_DIM), F32)
        state_ref[...] = jnp.zeros_like(state_ref)

    ext_ref[pad:pad + rows, :] = xbc_ref[...].astype(F32)
    conv = cb_ref[...]
    for kk in range(CONV_WIDTH):
        off = pad - (CONV_WIDTH - 1) + kk
        conv = conv + ext_ref[off:off + rows, :] * cw_ref[kk:kk + 1, :]
    ext_ref[0:pad, :] = ext_ref[rows:rows + pad, :]
    u = _silu(conv)
    xs = u[:, :SSD_INNER]
    b_all = u[:, SSD_INNER:SSD_INNER + SSD_GROUPS * SSD_STATE].astype(BF16)
    c_all = u[:, SSD_INNER + SSD_GROUPS * SSD_STATE:].astype(BF16)

    gate = gate_ref[...]
    dt_e = _expand_heads(gate, GATE_DT_LANE, rows)
    a_e = _expand_heads(gate, GATE_A_LANE, rows)
    xdt = xs * dt_e
    da_e = dt_e * aexp_ref[...]

    r_i = lax.broadcasted_iota(jnp.int32, (q, gw), 0)
    c_i = lax.broadcasted_iota(jnp.int32, (q, gw), 1)
    s_i = c_i % q
    eye_t = s_i == r_i
    tril_t = s_i <= r_i
    rb = lax.broadcasted_iota(jnp.int32, (hpg * q, gw), 0) // q
    cbk = lax.broadcasted_iota(jnp.int32, (hpg * q, gw), 1) // SSD_HEAD_DIM
    blk = rb == cbk

    ys = []
    for c in range(rows // q):
        r0 = c * q
        a_c = a_e[r0:r0 + q, :]
        a_prev = a_c[0:1, :] - da_e[r0:r0 + 1, :]
        a_end = a_c[q - 1:q, :]
        y_groups = []
        for g in range(SSD_GROUPS):
            l0 = g * gw
            a_cg = a_c[:, l0:l0 + gw]
            diag = jnp.sum(jnp.where(eye_t, a_cg, 0.0), axis=0, keepdims=True)
            seg = jnp.minimum(a_cg - diag, 0.0)
            lcat = jnp.where(tril_t, jnp.exp(seg), 0.0)
            bg = b_all[r0:r0 + q, g * SSD_STATE:(g + 1) * SSD_STATE]
            cg = c_all[r0:r0 + q, g * SSD_STATE:(g + 1) * SSD_STATE]
            b_tile = jnp.concatenate([bg] * hpg, axis=0)
            gcat = lax.dot_general(cg, b_tile, (((1,), (1,)), ((), ())),
                                   preferred_element_type=F32)
            xg = xdt[r0:r0 + q, l0:l0 + gw]
            xg_b = xg.astype(BF16)
            xbd = jnp.where(blk, jnp.concatenate([xg_b] * hpg, axis=0), jnp.zeros((), BF16))
            y_diag = jnp.dot((gcat * lcat).astype(BF16), xbd, preferred_element_type=F32)
            st = state_ref[g]
            y_off = jnp.dot(cg, st.astype(BF16), preferred_element_type=F32) \
                * jnp.exp(a_cg - a_prev[:, l0:l0 + gw])
            dte = jnp.exp(a_end[:, l0:l0 + gw] - a_cg)
            upd = lax.dot_general(bg, (xg * dte).astype(BF16), (((0,), (0,)), ((), ())),
                                  preferred_element_type=F32)
            state_ref[g] = st * jnp.exp(a_end[:, l0:l0 + gw] - a_prev[:, l0:l0 + gw]) + upd
            y_groups.append(y_diag + y_off)
        ys.append(jnp.concatenate(y_groups, axis=1))
    y = jnp.concatenate(ys, axis=0) + xs * dskip_ref[...]
    y = y * _silu(z_ref[...].astype(F32))
    outs = []
    for g in range(SSD_GROUPS):
        yg = y[:, g * gw:(g + 1) * gw]
        ms = jnp.mean(yg * yg, axis=1, keepdims=True)
        outs.append(yg * lax.rsqrt(ms + RMS_EPS))
    y_ref[...] = (jnp.concatenate(outs, axis=1) * ng_ref[...]).astype(y_ref.dtype)


def _ssd(xbc, z, gate, conv_w, conv_b, a_exp, dskip_exp, norm_g, batch, seq):
    t = xbc.shape[0]
    rows = min(256, seq)
    nj = seq // rows
    row = lambda b, j: (b * nj + j, 0)
    const = lambda b, j: (0, 0)
    return pl.pallas_call(
        _ssd_kernel,
        grid=(batch, nj),
        in_specs=[
            pl.BlockSpec((rows, SSD_CONV_DIM), row),
            pl.BlockSpec((rows, SSD_INNER), row),
            pl.BlockSpec((rows, LANES), row),
            pl.BlockSpec((CONV_WIDTH, SSD_CONV_DIM), const),
            pl.BlockSpec((1, SSD_CONV_DIM), const),
            pl.BlockSpec((1, SSD_INNER), const),
            pl.BlockSpec((1, SSD_INNER), const),
            pl.BlockSpec((1, SSD_INNER), const),
        ],
        out_specs=pl.BlockSpec((rows, SSD_INNER), row),
        out_shape=jax.ShapeDtypeStruct((t, SSD_INNER), BF16),
        scratch_shapes=[
            pltpu.VMEM((rows + 2 * SUBLANES, SSD_CONV_DIM), F32),
            pltpu.VMEM((SSD_GROUPS, SSD_STATE, SSD_INNER // SSD_GROUPS), F32),
        ],
        compiler_params=pltpu.CompilerParams(
            dimension_semantics=("arbitrary", "arbitrary"), vmem_limit_bytes=VMEM_LIMIT_BYTES),
        name="ssd",
    )(xbc, z, gate, conv_w, conv_b, a_exp, dskip_exp, norm_g)


def _attn_kernel(q_ref, kt_ref, v_ref, qf_ref, kft_ref, o_ref, vaug_ref, *, tq, pairs):
    pair0 = pl.program_id(1) * pairs
    nk = kt_ref.shape[0]
    lane = lax.broadcasted_iota(jnp.int32, (1, LANES), 1)
    in_head = (lane < ATT_HEAD_DIM, lane >= ATT_HEAD_DIM)
    ones = jnp.ones((tq, LANES), BF16)
    for pp in range(pairs):
        for c in range(nk):
            vaug_ref[pp, c] = jnp.concatenate(
                [v_ref[c * tq:(c + 1) * tq, pp * LANES:(pp + 1) * LANES], ones], axis=1)
    row = lax.broadcasted_iota(jnp.int32, (2 * tq, tq), 0)
    col = lax.broadcasted_iota(jnp.int32, (2 * tq, tq), 1)
    causal = jnp.where(row >= tq, row - tq, row) >= col
    zero = jnp.zeros((), BF16)

    def logits(qa, pp, jk):
        ka_t = jnp.concatenate([kt_ref[jk, pp * LANES:(pp + 1) * LANES, :], kft_ref[jk]], axis=0)
        return jnp.dot(qa, ka_t, preferred_element_type=F32)

    def update(s, m, acc, pp, jk):
        m_new = jnp.maximum(m, jnp.max(s, axis=1, keepdims=True))
        alpha = jnp.exp2(m - m_new)
        p = jnp.exp2(s - m_new).astype(BF16)
        acc = alpha * acc + jnp.dot(p, vaug_ref[pp, jk], preferred_element_type=F32)
        return m_new, acc

    def q_block(i, _):
        q0 = pl.multiple_of(i * tq, tq)
        qfv = qf_ref[pl.ds(q0, tq), :]
        qas = []
        for pp in range(pairs):
            qv = q_ref[pl.ds(q0, tq), pp * LANES:(pp + 1) * LANES]
            halves = []
            for hh in range(2):
                fb0 = FB_LANES * (2 * (pair0 + pp) + hh)
                in_fb = (lane >= fb0) & (lane < fb0 + FB_LANES)
                halves.append(jnp.concatenate(
                    [jnp.where(in_head[hh], qv, zero), jnp.where(in_fb, qfv, zero)], axis=1))
            qas.append(jnp.concatenate(halves, axis=0))

        def body(jk, carry):
            new = []
            for pp in range(pairs):
                s, m, acc = carry[pp]
                s_next = logits(qas[pp], pp, jk + 1)
                m, acc = update(s, m, acc, pp, jk)
                new.append((s_next, m, acc))
            return tuple(new)

        init = tuple((logits(qas[pp], pp, 0), jnp.full((2 * tq, 1), NEG_BIG, F32),
                      jnp.zeros((2 * tq, 2 * LANES), F32)) for pp in range(pairs))
        carry = lax.fori_loop(0, i, body, init)
        for pp in range(pairs):
            s, m, acc = carry[pp]
            _, acc = update(jnp.where(causal, s, NEG_BIG), m, acc, pp, i)
            inv = 1.0 / acc[:, LANES:LANES + 1]
            out = jnp.where(in_head[0], acc[:tq, :LANES] * inv[:tq], acc[tq:, :LANES] * inv[tq:])
            o_ref[pl.ds(q0, tq), pp * LANES:(pp + 1) * LANES] = out.astype(o_ref.dtype)
        return 0

    lax.fori_loop(0, nk, q_block, 0)


def _attention(q, k_t, v, qf, kf_t, batch, seq):
    t = q.shape[0]
    tq = min(ATT_BLOCK, seq)
    nk = seq // tq
    pairs = ATT_PAIRS_PER_STEP
    width = pairs * LANES
    head_pairs = lambda b, p: (b, p)
    shared = lambda b, p: (b, 0)
    return pl.pallas_call(
        functools.partial(_attn_kernel, tq=tq, pairs=pairs),
        grid=(batch, ATT_HEADS // (2 * pairs)),
        in_specs=[
            pl.BlockSpec((seq, width), head_pairs),
            pl.BlockSpec((nk, width, tq), lambda b, p: (b, p, 0)),
            pl.BlockSpec((seq, width), head_pairs),
            pl.BlockSpec((seq, LANES), shared),
            pl.BlockSpec((nk, LANES, tq), lambda b, p: (b, 0, 0)),
        ],
        out_specs=pl.BlockSpec((seq, width), head_pairs),
        out_shape=jax.ShapeDtypeStruct((t, ATT_INNER), BF16),
        scratch_shapes=[pltpu.VMEM((pairs, nk, tq, 2 * LANES), BF16)],
        compiler_params=pltpu.CompilerParams(
            dimension_semantics=("arbitrary", "arbitrary"), vmem_limit_bytes=VMEM_LIMIT_BYTES),
        name="fox_attention",
    )(q, k_t, v, qf, kf_t)


def _layer_norm(u, g, b):
    mu = jnp.mean(u, axis=1, keepdims=True)
    d = u - mu
    var = jnp.mean(d * d, axis=1, keepdims=True)
    return d * lax.rsqrt(var + LN_EPS) * g + b


def _route(lg):
    lane = lax.broadcasted_iota(jnp.int32, (1, LANES), 1)
    lane_f = lane.astype(F32)
    is_g = lane < N_EXPERT_GROUPS
    gl = jnp.where(is_g, lg, NEG_BIG)
    gmax = jnp.max(gl, axis=1, keepdims=True)
    gsum = jnp.sum(jnp.where(is_g, jnp.exp(gl - gmax), 0.0), axis=1, keepdims=True)
    g_val = 1.0 / gsum
    g_idx = jnp.min(jnp.where(is_g & (gl == gmax), lane_f, float(LANES)), axis=1, keepdims=True)
    e_grp = ((lane - ROUTE_E_LANE) // EXPERTS_PER_GROUP).astype(F32)
    in_grp = (lane >= ROUTE_E_LANE) & (lane < ROUTE_E_LANE + N_EXPERTS) & (e_grp == g_idx)
    el = jnp.where(in_grp, lg, NEG_BIG)
    m1 = jnp.max(el, axis=1, keepdims=True)
    i1 = jnp.min(jnp.where(in_grp & (el == m1), lane_f, float(LANES)), axis=1, keepdims=True)
    rest = in_grp & (lane_f != i1)
    el2 = jnp.where(rest, lg, NEG_BIG)
    m2 = jnp.max(el2, axis=1, keepdims=True)
    i2 = jnp.min(jnp.where(rest & (el2 == m2), lane_f, float(LANES)), axis=1, keepdims=True)
    r = jnp.exp(m2 - m1)
    w1 = 1.0 / (1.0 + r)
    w2 = r * w1
    comb = jnp.where(lane_f == i1, g_val * w1, jnp.where(lane_f == i2, g_val * w2, 0.0))
    sel = jnp.where((lane_f == i1) | (lane_f == i2), 1.0, 0.0)
    return comb, sel


def _out_proj_kernel(ys_ref, ya_ref, x_ref, wa_ref, wb_ref, g_ref, b_ref, wrh_ref, wrl_ref, br_ref,
                     h_ref, hb_ref, comb_ref, sel_ref, cnt_ref):
    mix = jnp.dot(ys_ref[...], wa_ref[...], preferred_element_type=F32)
    mix = mix + jnp.dot(ya_ref[...], wb_ref[...], preferred_element_type=F32)
    h = _layer_norm(DEEPNORM_ALPHA * x_ref[...] + mix, g_ref[...], b_ref[...])
    h_ref[...] = h
    h_hi = h.astype(BF16)
    hb_ref[...] = h_hi
    h_lo = (h - h_hi.astype(F32)).astype(BF16)
    lg = (jnp.dot(h_hi, wrh_ref[...], preferred_element_type=F32)
          + jnp.dot(h_lo, wrh_ref[...], preferred_element_type=F32)
          + jnp.dot(h_hi, wrl_ref[...], preferred_element_type=F32)) + br_ref[...]
    comb, sel = _route(lg)
    comb_ref[...] = comb
    sel_ref[...] = sel
    ts = h.shape[0] // cnt_ref.shape[0]
    for c in range(cnt_ref.shape[0]):
        cnt = jnp.sum(sel[c * ts:(c + 1) * ts, :], axis=0, keepdims=True)
        cnt_ref[c] = jnp.broadcast_to(cnt, (SUBLANES, LANES))


def _out_proj(ys, ya, x2d, wo_a, wo_b, ln_g, ln_b, wr_hi, wr_lo, br, ts):
    t = x2d.shape[0]
    tm = min(512, t)
    row = lambda i: (i, 0)
    const = lambda i: (0, 0)
    return pl.pallas_call(
        _out_proj_kernel,
        grid=(t // tm,),
        in_specs=[
            pl.BlockSpec((tm, SSD_INNER), row),
            pl.BlockSpec((tm, ATT_INNER), row),
            pl.BlockSpec((tm, D_MODEL), row),
            pl.BlockSpec((SSD_INNER, D_MODEL), const),
            pl.BlockSpec((ATT_INNER, D_MODEL), const),
            pl.BlockSpec((1, D_MODEL), const),
            pl.BlockSpec((1, D_MODEL), const),
            pl.BlockSpec((D_MODEL, LANES), const),
            pl.BlockSpec((D_MODEL, LANES), const),
            pl.BlockSpec((1, LANES), const),
        ],
        out_specs=[
            pl.BlockSpec((tm, D_MODEL), row),
            pl.BlockSpec((tm, D_MODEL), row),
            pl.BlockSpec((tm, LANES), row),
            pl.BlockSpec((tm, LANES), row),
            pl.BlockSpec((tm // ts, SUBLANES, LANES), lambda i: (i, 0, 0)),
        ],
        out_shape=(
            jax.ShapeDtypeStruct((t, D_MODEL), F32),
            jax.ShapeDtypeStruct((t, D_MODEL), BF16),
            jax.ShapeDtypeStruct((t, LANES), F32),
            jax.ShapeDtypeStruct((t, LANES), F32),
            jax.ShapeDtypeStruct((t // ts, SUBLANES, LANES), F32),
        ),
        compiler_params=pltpu.CompilerParams(
            dimension_semantics=("arbitrary",), vmem_limit_bytes=VMEM_LIMIT_BYTES),
        name="out_proj_ln_router",
    )(ys, ya, x2d, wo_a, wo_b, ln_g, ln_b, wr_hi, wr_lo, br)


def _dispatch_plan(cnt, ts):
    nt = cnt.shape[0]
    c16 = (cnt + (MOE_GRANULE - 1)) // MOE_GRANULE * MOE_GRANULE
    loc_off = jnp.cumsum(c16, axis=1) - c16
    tot = jnp.sum(c16, axis=0)
    tiles = (tot + (MOE_ROW_TILE - 1)) // MOE_ROW_TILE
    cum_tiles = jnp.cumsum(tiles)
    start = (cum_tiles - tiles) * MOE_ROW_TILE
    base = start[None, :] + jnp.cumsum(c16, axis=0) - c16
    max_tiles = _moe_rows(nt * ts, ts) // MOE_ROW_TILE
    tile_expert = jnp.minimum(
        jnp.sum(jnp.arange(max_tiles, dtype=jnp.int32)[:, None] >= cum_tiles[None, :], axis=1),
        N_EXPERTS - 1).astype(jnp.int32)
    loc_vec = jnp.zeros((nt, SUBLANES, LANES), F32).at[:, :, ROUTE_E_LANE:ROUTE_E_LANE + N_EXPERTS].set(
        jnp.broadcast_to(loc_off.astype(F32)[:, None, :], (nt, SUBLANES, N_EXPERTS)))
    return dict(
        chunks=(c16 // MOE_GRANULE).reshape(-1).astype(jnp.int32),
        loc_off=loc_off.reshape(-1).astype(jnp.int32),
        base=base.reshape(-1).astype(jnp.int32),
        tile_chunks=(jnp.sum(c16, axis=1) // MOE_GRANULE).astype(jnp.int32),
        pad_start=(start + tot).astype(jnp.int32),
        pad_chunks=((tiles * MOE_ROW_TILE - tot) // MOE_GRANULE).astype(jnp.int32),
        n_tiles=cum_tiles[-1:].astype(jnp.int32),
        tile_expert=tile_expert,
        loc_vec=loc_vec,
    )


def _moe_local_rows(ts):
    rows = 2 * ts + N_EXPERTS * (MOE_GRANULE - 1)
    return (rows + LANES - 1) // LANES * LANES


def _moe_rows(t, ts):
    rows = 2 * t + (t // ts) * N_EXPERTS * (MOE_GRANULE - 1) + N_EXPERTS * MOE_ROW_TILE
    return (rows + MOE_ROW_TILE - 1) // MOE_ROW_TILE * MOE_ROW_TILE


def _chunk_copy(src, dst, sem):
    return pltpu.make_async_copy(src, dst, sem)


def _dispatch_kernel(chunks_ref, loc_ref, base_ref, tchunks_ref, pstart_ref, pchunks_ref,
                     hb_ref, sel_ref, comb_ref, locv_ref, lstrict_ref,
                     xs_ref, posw_ref, buf_ref, zero_ref, sem_ref, zsem_ref):
    t = pl.program_id(0)
    nt = pl.num_programs(0)
    slot = t % 2
    ts = hb_ref.shape[0]
    nl = buf_ref.shape[1]
    g = MOE_GRANULE

    def wait_chunks(n, sl):
        def one(_, c):
            _chunk_copy(buf_ref.at[sl, pl.ds(0, g)], xs_ref.at[pl.ds(0, g)], sem_ref.at[sl]).wait()
            return c
        lax.fori_loop(0, n, one, 0)

    @pl.when(t >= 2)
    def _():
        wait_chunks(tchunks_ref[t - 2], slot)

    lane = lax.broadcasted_iota(jnp.int32, (1, LANES), 1)
    sel = sel_ref[...]
    comb = comb_ref[...]
    picked = sel > 0.0
    rank = jnp.dot(lstrict_ref[...], sel.astype(BF16), preferred_element_type=F32)
    dest = rank + locv_ref[0][0:1, :]
    pos_a = jnp.min(jnp.where(picked, dest, 1e9), axis=1, keepdims=True)
    pos_b = jnp.max(jnp.where(picked, dest, -1.0), axis=1, keepdims=True)
    w_a = jnp.sum(jnp.where(picked & (dest == pos_a), comb, 0.0), axis=1, keepdims=True)
    w_b = jnp.sum(jnp.where(picked & (dest == pos_b), comb, 0.0), axis=1, keepdims=True)
    posw = jnp.where(lane == 0, pos_a, jnp.where(lane == 1, pos_b,
                     jnp.where(lane == 2, w_a, jnp.where(lane == 3, w_b, 0.0))))
    posw_ref[...] = posw
    posw_t = posw.T
    d = lax.broadcasted_iota(jnp.int32, (nl, ts), 0).astype(F32)
    perm = jnp.where((d == posw_t[0:1, :]) | (d == posw_t[1:2, :]), 1.0, 0.0).astype(BF16)
    buf_ref[slot] = jnp.dot(perm, hb_ref[...], preferred_element_type=F32).astype(BF16)

    for e in range(N_EXPERTS):
        idx = t * N_EXPERTS + e
        lo = loc_ref[idx]
        bs = base_ref[idx]

        def send(k, c, lo=lo, bs=bs):
            src = buf_ref.at[slot, pl.ds(pl.multiple_of(lo + k * g, g), g)]
            dst = xs_ref.at[pl.ds(pl.multiple_of(bs + k * g, g), g)]
            _chunk_copy(src, dst, sem_ref.at[slot]).start()
            return c
        lax.fori_loop(0, chunks_ref[idx], send, 0)

    @pl.when(t == nt - 1)
    def _():
        zero_ref[...] = jnp.zeros_like(zero_ref)
        for e in range(N_EXPERTS):
            ps = pstart_ref[e]

            def send_zero(k, c, ps=ps):
                dst = xs_ref.at[pl.ds(pl.multiple_of(ps + k * g, g), g)]
                _chunk_copy(zero_ref, dst, zsem_ref.at[0]).start()
                return c
            lax.fori_loop(0, pchunks_ref[e], send_zero, 0)
        for e in range(N_EXPERTS):
            def wait_zero(_, c):
                _chunk_copy(zero_ref, xs_ref.at[pl.ds(0, g)], zsem_ref.at[0]).wait()
                return c
            lax.fori_loop(0, pchunks_ref[e], wait_zero, 0)
        wait_chunks(tchunks_ref[t], slot)

        @pl.when(t >= 1)
        def _():
            wait_chunks(tchunks_ref[t - 1], 1 - slot)


def _dispatch(plan, hb, sel, comb, ts):
    t = hb.shape[0]
    nt = t // ts
    nl = _moe_local_rows(ts)
    rows = _moe_rows(t, ts)
    lstrict = jnp.tri(ts, k=-1, dtype=BF16)
    row = lambda i, *_: (i, 0)
    grid_spec = pltpu.PrefetchScalarGridSpec(
        num_scalar_prefetch=6,
        grid=(nt,),
        in_specs=[
            pl.BlockSpec((ts, D_MODEL), row),
            pl.BlockSpec((ts, LANES), row),
            pl.BlockSpec((ts, LANES), row),
            pl.BlockSpec((1, SUBLANES, LANES), lambda i, *_: (i, 0, 0)),
            pl.BlockSpec((ts, ts), lambda i, *_: (0, 0)),
        ],
        out_specs=[
            pl.BlockSpec(memory_space=pl.ANY),
            pl.BlockSpec((ts, LANES), row),
        ],
        scratch_shapes=[
            pltpu.VMEM((2, nl, D_MODEL), BF16),
            pltpu.VMEM((MOE_GRANULE, D_MODEL), BF16),
            pltpu.SemaphoreType.DMA((2,)),
            pltpu.SemaphoreType.DMA((1,)),
        ],
    )
    return pl.pallas_call(
        _dispatch_kernel,
        grid_spec=grid_spec,
        out_shape=(
            jax.ShapeDtypeStruct((rows, D_MODEL), BF16),
            jax.ShapeDtypeStruct((t, LANES), F32),
        ),
        compiler_params=pltpu.CompilerParams(
            dimension_semantics=("arbitrary",), vmem_limit_bytes=VMEM_LIMIT_BYTES),
        name="moe_dispatch",
    )(plan["chunks"], plan["loc_off"], plan["base"], plan["tile_chunks"], plan["pad_start"],
      plan["pad_chunks"], hb, sel, comb, plan["loc_vec"], lstrict)


def _expert_kernel(te_ref, nt_ref, x_ref, wgu_ref, wd_ref, y_ref):
    @pl.when(pl.program_id(0) < nt_ref[0])
    def _():
        gu = jnp.dot(x_ref[...], wgu_ref[0], preferred_element_type=F32)
        act = _silu(gu[:, :D_FF_EXPERT]) * gu[:, D_FF_EXPERT:]
        y_ref[...] = jnp.dot(act.astype(BF16), wd_ref[0],
                             preferred_element_type=F32).astype(y_ref.dtype)


def _experts(plan, xs, wgu, wd):
    rows = xs.shape[0]
    live = lambda j, te, nt: jnp.minimum(j, nt[0] - 1)
    grid_spec = pltpu.PrefetchScalarGridSpec(
        num_scalar_prefetch=2,
        grid=(rows // MOE_ROW_TILE,),
        in_specs=[
            pl.BlockSpec((MOE_ROW_TILE, D_MODEL), lambda j, te, nt: (live(j, te, nt), 0)),
            pl.BlockSpec((1, D_MODEL, 2 * D_FF_EXPERT), lambda j, te, nt: (te[live(j, te, nt)], 0, 0)),
            pl.BlockSpec((1, D_FF_EXPERT, D_MODEL), lambda j, te, nt: (te[live(j, te, nt)], 0, 0)),
        ],
        out_specs=pl.BlockSpec((MOE_ROW_TILE, D_MODEL), lambda j, te, nt: (live(j, te, nt), 0)),
    )
    return pl.pallas_call(
        _expert_kernel,
        grid_spec=grid_spec,
        out_shape=jax.ShapeDtypeStruct((rows, D_MODEL), BF16),
        compiler_params=pltpu.CompilerParams(
            dimension_semantics=("arbitrary",), vmem_limit_bytes=VMEM_LIMIT_BYTES),
        name="moe_experts",
    )(plan["tile_expert"], plan["n_tiles"], xs, wgu, wd)


def _combine_kernel(chunks_ref, loc_ref, base_ref, tchunks_ref,
                    y_ref, posw_ref, h_ref, g_ref, b_ref, o_ref, buf_ref, sem_ref):
    t = pl.program_id(0)
    nt = pl.num_programs(0)
    slot = t % 2
    nl = buf_ref.shape[1]
    g = MOE_GRANULE

    def fetch(tt, sl):
        buf_ref[sl] = jnp.zeros((nl, D_MODEL), BF16)
        for e in range(N_EXPERTS):
            idx = tt * N_EXPERTS + e
            lo = loc_ref[idx]
            bs = base_ref[idx]

            def get(k, c, lo=lo, bs=bs):
                src = y_ref.at[pl.ds(pl.multiple_of(bs + k * g, g), g)]
                dst = buf_ref.at[sl, pl.ds(pl.multiple_of(lo + k * g, g), g)]
                _chunk_copy(src, dst, sem_ref.at[sl]).start()
                return c
            lax.fori_loop(0, chunks_ref[idx], get, 0)

    @pl.when(t == 0)
    def _():
        fetch(0, 0)

    @pl.when(t + 1 < nt)
    def _():
        fetch(t + 1, 1 - slot)

    def one(_, c):
        _chunk_copy(y_ref.at[pl.ds(0, g)], buf_ref.at[slot, pl.ds(0, g)], sem_ref.at[slot]).wait()
        return c
    lax.fori_loop(0, tchunks_ref[t], one, 0)

    yb = buf_ref[slot]
    posw = posw_ref[...]
    d = lax.broadcasted_iota(jnp.int32, (1, nl), 1).astype(F32)
    ffn = jnp.zeros(o_ref.shape, F32)
    for k in range(2):
        gather = jnp.where(posw[:, k:k + 1] == d, 1.0, 0.0).astype(BF16)
        ffn = ffn + posw[:, 2 + k:3 + k] * jnp.dot(gather, yb, preferred_element_type=F32)
    o_ref[...] = _layer_norm(DEEPNORM_ALPHA * h_ref[...] + ffn, g_ref[...], b_ref[...])


def _combine(plan, y, posw, h, ln_g, ln_b, ts):
    t = h.shape[0]
    nl = _moe_local_rows(ts)
    row = lambda i, *_: (i, 0)
    const = lambda i, *_: (0, 0)
    grid_spec = pltpu.PrefetchScalarGridSpec(
        num_scalar_prefetch=4,
        grid=(t // ts,),
        in_specs=[
            pl.BlockSpec(memory_space=pl.ANY),
            pl.BlockSpec((ts, LANES), row),
            pl.BlockSpec((ts, D_MODEL), row),
            pl.BlockSpec((1, D_MODEL), const),
            pl.BlockSpec((1, D_MODEL), const),
        ],
        out_specs=pl.BlockSpec((ts, D_MODEL), row),
        scratch_shapes=[
            pltpu.VMEM((2, nl, D_MODEL), BF16),
            pltpu.SemaphoreType.DMA((2,)),
        ],
    )
    return pl.pallas_call(
        _combine_kernel,
        grid_spec=grid_spec,
        out_shape=jax.ShapeDtypeStruct((t, D_MODEL), F32),
        compiler_params=pltpu.CompilerParams(
            dimension_semantics=("arbitrary",), vmem_limit_bytes=VMEM_LIMIT_BYTES),
        name="moe_combine_ln",
    )(plan["chunks"], plan["loc_off"], plan["base"], plan["tile_chunks"], y, posw, h, ln_g, ln_b)


def _pad_lanes(a, n=LANES):
    return jnp.pad(a, ((0, 0), (0, n - a.shape[1])))


def _layer(h2d, batch, seq, w_in, b_in, conv_w, conv_b, a_log, d_skip, ssd_norm_g, w_out,
           ln1_g, ln1_b, rg_w, rg_b, re_w, re_b, w_gate, w_up, w_down, ln2_g, ln2_b):
    o_dt = SSD_INNER + SSD_CONV_DIM
    o_q = o_dt + SSD_HEADS
    o_f = o_q + 3 * ATT_INNER
    w_main = jnp.concatenate([w_in[:, :o_dt], w_in[:, o_q:o_f]], axis=1).astype(BF16)
    b_main = jnp.concatenate([b_in[:o_dt], b_in[o_q:o_f]])[None, :]
    w_dt, w_f = w_in[:, o_dt:o_q], w_in[:, o_f:]
    w_small = _pad_lanes(jnp.concatenate([w_dt, w_f, w_dt], axis=1)).astype(BF16)
    b_small = _pad_lanes(jnp.concatenate([b_in[o_dt:o_q], b_in[o_f:], b_in[o_dt:o_q]])[None, :])
    a_neg = -jnp.exp(a_log.astype(F32))
    a_row = _pad_lanes(a_neg[None, :])
    a_exp = jnp.repeat(a_neg, SSD_HEAD_DIM)[None, :]
    dskip_exp = jnp.repeat(d_skip.astype(F32), SSD_HEAD_DIM)[None, :]

    z, xbc, q, k_t, v, gate, qf, kf_t = _in_proj(h2d, w_main, b_main, w_small, b_small, a_row, seq)
    y_ssd = _ssd(xbc, z, gate, conv_w, conv_b[None, :], a_exp, dskip_exp, ssd_norm_g[None, :],
                 batch, seq)
    y_att = _attention(q, k_t, v, qf, kf_t, batch, seq)

    w_r = _pad_lanes(jnp.concatenate(
        [rg_w, jnp.transpose(re_w, (1, 0, 2)).reshape(D_MODEL, N_EXPERTS)], axis=1))
    b_r = _pad_lanes(jnp.concatenate([rg_b, re_b.reshape(N_EXPERTS)])[None, :])
    wr_hi = w_r.astype(BF16)
    wr_lo = (w_r - wr_hi.astype(F32)).astype(BF16)
    wo = w_out.astype(BF16)
    ts = min(MOE_SORT_TILE, h2d.shape[0])
    h1, h1b, comb, sel, cnt = _out_proj(y_ssd, y_att, h2d, wo[:SSD_INNER], wo[SSD_INNER:],
                                        ln1_g[None, :], ln1_b[None, :], wr_hi, wr_lo, b_r, ts)

    plan = _dispatch_plan(
        cnt[:, 0, ROUTE_E_LANE:ROUTE_E_LANE + N_EXPERTS].astype(jnp.int32), ts)
    xs, posw = _dispatch(plan, h1b, sel, comb, ts)
    wgu = jnp.concatenate([w_gate, w_up], axis=2).astype(BF16)
    y = _experts(plan, xs, wgu, w_down.astype(BF16))
    return _combine(plan, y, posw, h1, ln2_g[None, :], ln2_b[None, :], ts)


def kernel(x, w_in, b_in, conv_w, conv_b, a_log, d_skip, ssd_norm_g, w_out, ln1_g, ln1_b,
           router_group_w, router_group_b, router_expert_w, router_expert_b, w_gate, w_up,
           w_down, ln2_g, ln2_b):
    batch, seq, d = x.shape
    h = x.reshape(batch * seq, d)
    for l in range(w_in.shape[0]):
        h = _layer(h, batch, seq, w_in[l], b_in[l], conv_w[l], conv_b[l], a_log[l], d_skip[l],
                   ssd_norm_g[l], w_out[l], ln1_g[l], ln1_b[l], router_group_w[l],
                   router_group_b[l], router_expert_w[l], router_expert_b[l], w_gate[l],
                   w_up[l], w_down[l], ln2_g[l], ln2_b[l])
    return h.reshape(batch, seq, d)
```

```python
import functools

import jax
import jax.numpy as jnp
import numpy as np
from jax import lax
from jax.experimental import pallas as pl
from jax.experimental.pallas import tpu as pltpu

F32 = jnp.float32
BF16 = jnp.bfloat16

D_MODEL = 1024
SSD_HEADS = 8
SSD_HEAD_DIM = 64
SSD_INNER = SSD_HEADS * SSD_HEAD_DIM
SSD_GROUPS = 2
SSD_STATE = 128
CONV_WIDTH = 4
SSD_CONV_DIM = SSD_INNER + 2 * SSD_GROUPS * SSD_STATE
ATT_HEADS = 8
ATT_HEAD_DIM = 64
ATT_INNER = ATT_HEADS * ATT_HEAD_DIM
N_EXPERT_GROUPS = 4
EXPERTS_PER_GROUP = 4
N_EXPERTS = N_EXPERT_GROUPS * EXPERTS_PER_GROUP
D_FF_EXPERT = 512
DEPTH = 1
DEEPNORM_ALPHA = (2.0 * DEPTH) ** 0.25
LN_EPS = 1e-5
RMS_EPS = 1e-5

LANES = 128
SUBLANES = 8
VMEM_LIMIT_BYTES = 56 * 1024 * 1024

GATE_A_LANE = 0
GATE_F_LANE = SSD_HEADS
GATE_DT_LANE = 2 * SSD_HEADS
ROUTE_E_LANE = N_EXPERT_GROUPS
MOE_SORT_TILE = 256
MOE_GRANULE = 16
MOE_ROW_TILE = 512
MOE_WAIT_BATCH = 8

FB_LANES = 6
ATT_BLOCK = 256
ATT_PAIRS_PER_STEP = 4

SSD_CHUNK = 64
NEG_BIG = -1e30
LOG2E = 1.4426950408889634


def _split3(v):
    hi = v.astype(BF16)
    r1 = v - hi.astype(F32)
    mid = r1.astype(BF16)
    lo = (r1 - mid.astype(F32)).astype(BF16)
    return hi, mid, lo


def _silu(v):
    return v * (1.0 / (1.0 + jnp.exp(-v)))


def _softplus(v):
    return jnp.maximum(v, 0.0) + jnp.log1p(jnp.exp(-jnp.abs(v)))


def _in_proj_kernel(x_ref, wm_ref, bm_ref, ws_ref, bs_ref, arow_ref, tri_ref, place_ref, ones_ref,
                    z_ref, xbc_ref, qt_ref, k_ref, vt_ref, gate_ref, qft_ref, kf_ref,
                    carry_ref, *, tiles_per_seq):
    i = pl.program_id(0)
    tm = x_ref.shape[0]
    tb = qt_ref.shape[2]
    xb = x_ref[...].astype(BF16)

    def put_transposed(ref, val):
        for c in range(tm // tb):
            ref[c] = val[c * tb:(c + 1) * tb, :].T.astype(ref.dtype)

    s = jnp.dot(xb, ws_ref[...], preferred_element_type=F32) + bs_ref[...]
    lane = lax.broadcasted_iota(jnp.int32, (1, LANES), 1)
    sp = _softplus(s)
    log_f = -_softplus(-s)
    val = jnp.where(lane < GATE_F_LANE, sp * arow_ref[...],
                    jnp.where(lane < GATE_DT_LANE, log_f,
                              jnp.where(lane < GATE_DT_LANE + SSD_HEADS, sp, 0.0)))
    carry = jnp.where(i % tiles_per_seq == 0, 0.0, carry_ref[...])
    hi, mid, lo = _split3(val)
    tri = tri_ref[...]
    cs = (jnp.dot(tri, hi, preferred_element_type=F32)
          + jnp.dot(tri, mid, preferred_element_type=F32)
          + jnp.dot(tri, lo, preferred_element_type=F32)) + carry
    carry_ref[...] = cs[tm - 1:tm, :]
    out = jnp.where(lane < GATE_DT_LANE, cs, val)
    gate_ref[...] = out
    pieces = jnp.concatenate(_split3(cs * LOG2E), axis=1)
    fb = jnp.dot(pieces, place_ref[...], preferred_element_type=F32) + ones_ref[...]
    put_transposed(qft_ref, fb[:, :LANES])
    kf_ref[...] = fb[:, LANES:].astype(BF16)

    col = 0
    for ref, scale in ((z_ref, None), (xbc_ref, None), (qt_ref, ATT_HEAD_DIM ** -0.5 * LOG2E),
                       (k_ref, None), (vt_ref, None)):
        transposed = ref is qt_ref or ref is vt_ref
        n = ATT_INNER if transposed else ref.shape[1]
        acc = jnp.dot(xb, wm_ref[:, col:col + n], preferred_element_type=F32) + bm_ref[:, col:col + n]
        if scale is not None:
            acc = acc * scale
        if transposed:
            put_transposed(ref, acc)
        else:
            ref[...] = acc.astype(ref.dtype)
        col += n


def _forget_bias_placement():
    place = np.zeros((3 * LANES, 2 * LANES), np.float32)
    ones = np.zeros((1, 2 * LANES), np.float32)
    for h in range(ATT_HEADS):
        for piece in range(3):
            src = piece * LANES + GATE_F_LANE + h
            place[src, FB_LANES * h + piece] = 1.0
            place[src, LANES + FB_LANES * h + 3 + piece] = -1.0
            ones[0, FB_LANES * h + 3 + piece] = 1.0
            ones[0, LANES + FB_LANES * h + piece] = 1.0
    return jnp.asarray(place, BF16), jnp.asarray(ones, F32)


def _in_proj(x2d, w_main, b_main, w_small, b_small, a_row, seq):
    t = x2d.shape[0]
    tm = min(512, seq)
    tk = min(ATT_BLOCK, seq)
    tri = jnp.tri(tm, dtype=BF16)
    place, ones = _forget_bias_placement()
    n_main = w_main.shape[1]
    row = lambda i: (i, 0)
    const = lambda i: (0, 0)
    outs = (
        jax.ShapeDtypeStruct((t, SSD_INNER), BF16),
        jax.ShapeDtypeStruct((t, SSD_CONV_DIM), BF16),
        jax.ShapeDtypeStruct((t // tk, ATT_INNER, tk), BF16),
        jax.ShapeDtypeStruct((t, ATT_INNER), BF16),
        jax.ShapeDtypeStruct((t // tk, ATT_INNER, tk), BF16),
        jax.ShapeDtypeStruct((t, LANES), F32),
        jax.ShapeDtypeStruct((t // tk, LANES, tk), BF16),
        jax.ShapeDtypeStruct((t, LANES), BF16),
    )
    return pl.pallas_call(
        functools.partial(_in_proj_kernel, tiles_per_seq=seq // tm),
        grid=(t // tm,),
        in_specs=[
            pl.BlockSpec((tm, D_MODEL), row),
            pl.BlockSpec((D_MODEL, n_main), const),
            pl.BlockSpec((1, n_main), const),
            pl.BlockSpec((D_MODEL, LANES), const),
            pl.BlockSpec((1, LANES), const),
            pl.BlockSpec((1, LANES), const),
            pl.BlockSpec((tm, tm), const),
            pl.BlockSpec((3 * LANES, 2 * LANES), const),
            pl.BlockSpec((1, 2 * LANES), const),
        ],
        out_specs=[
            pl.BlockSpec((tm, SSD_INNER), row),
            pl.BlockSpec((tm, SSD_CONV_DIM), row),
            pl.BlockSpec((tm // tk, ATT_INNER, tk), lambda i: (i, 0, 0)),
            pl.BlockSpec((tm, ATT_INNER), row),
            pl.BlockSpec((tm // tk, ATT_INNER, tk), lambda i: (i, 0, 0)),
            pl.BlockSpec((tm, LANES), row),
            pl.BlockSpec((tm // tk, LANES, tk), lambda i: (i, 0, 0)),
            pl.BlockSpec((tm, LANES), row),
        ],
        out_shape=outs,
        scratch_shapes=[pltpu.VMEM((1, LANES), F32)],
        compiler_params=pltpu.CompilerParams(
            dimension_semantics=("arbitrary",), vmem_limit_bytes=VMEM_LIMIT_BYTES),
        name="in_proj",
    )(x2d, w_main, b_main, w_small, b_small, a_row, tri, place, ones)


def _expand_heads(gate, lane0, rows):
    lane = lax.broadcasted_iota(jnp.int32, (1, LANES), 1)
    pieces = []
    for p in range(SSD_HEADS // 2):
        a = jnp.broadcast_to(gate[:, lane0 + 2 * p:lane0 + 2 * p + 1], (rows, LANES))
        b = jnp.broadcast_to(gate[:, lane0 + 2 * p + 1:lane0 + 2 * p + 2], (rows, LANES))
        pieces.append(jnp.where(lane < SSD_HEAD_DIM, a, b))
    return jnp.concatenate(pieces, axis=1)


def _ssd_kernel(xbc_ref, z_ref, gate_ref, cw_ref, cb_ref, aexp_ref, dskip_ref, ng_ref, shift_ref,
                y_ref, ext_ref, state_ref):
    j = pl.program_id(1)
    rows = xbc_ref.shape[0]
    q = SSD_CHUNK
    gw = SSD_INNER // SSD_GROUPS
    hpg = SSD_HEADS // SSD_GROUPS
    pad = SUBLANES

    @pl.when(j == 0)
    def _():
        ext_ref[0:pad, :] = jnp.zeros((pad, SSD_CONV_DIM), F32)
        state_ref[...] = jnp.zeros_like(state_ref)

    xb = xbc_ref[...]
    xf = xb.astype(F32)
    conv = cb_ref[...] + xf * cw_ref[CONV_WIDTH - 1:CONV_WIDTH, :]
    for s in range(1, CONV_WIDTH):
        shifted = jnp.dot(shift_ref[s - 1], xb, preferred_element_type=F32)
        conv = conv + shifted * cw_ref[CONV_WIDTH - 1 - s:CONV_WIDTH - s, :]
    ext_ref[pad:2 * pad, :] = xf[0:pad, :]
    head = cb_ref[...]
    for kk in range(CONV_WIDTH):
        off = pad - (CONV_WIDTH - 1) + kk
        head = head + ext_ref[off:off + pad, :] * cw_ref[kk:kk + 1, :]
    ext_ref[0:pad, :] = xf[rows - pad:rows, :]
    u = _silu(jnp.concatenate([head, conv[pad:, :]], axis=0))
    xs = u[:, :SSD_INNER]
    b_all = u[:, SSD_INNER:SSD_INNER + SSD_GROUPS * SSD_STATE].astype(BF16)
    c_all = u[:, SSD_INNER + SSD_GROUPS * SSD_STATE:].astype(BF16)

    gate = gate_ref[...]
    dt_e = _expand_heads(gate, GATE_DT_LANE, rows)
    a_e = _expand_heads(gate, GATE_A_LANE, rows)
    xdt = xs * dt_e
    da_e = dt_e * aexp_ref[...]

    r_i = lax.broadcasted_iota(jnp.int32, (q, gw), 0)
    c_i = lax.broadcasted_iota(jnp.int32, (q, gw), 1)
    s_i = c_i % q
    eye_t = s_i == r_i
    tril_t = s_i <= r_i
    rb = lax.broadcasted_iota(jnp.int32, (hpg * q, gw), 0) // q
    cbk = lax.broadcasted_iota(jnp.int32, (hpg * q, gw), 1) // SSD_HEAD_DIM
    blk = rb == cbk

    ys = []
    for c in range(rows // q):
        r0 = c * q
        a_c = a_e[r0:r0 + q, :]
        a_prev = a_c[0:1, :] - da_e[r0:r0 + 1, :]
        a_end = a_c[q - 1:q, :]
        y_groups = []
        for g in range(SSD_GROUPS):
            l0 = g * gw
            a_cg = a_c[:, l0:l0 + gw]
            diag = jnp.sum(jnp.where(eye_t, a_cg, 0.0), axis=0, keepdims=True)
            seg = jnp.minimum(a_cg - diag, 0.0)
            lcat = jnp.where(tril_t, jnp.exp(seg), 0.0)
            bg = b_all[r0:r0 + q, g * SSD_STATE:(g + 1) * SSD_STATE]
            cg = c_all[r0:r0 + q, g * SSD_STATE:(g + 1) * SSD_STATE]
            b_tile = jnp.concatenate([bg] * hpg, axis=0)
            gcat = lax.dot_general(cg, b_tile, (((1,), (1,)), ((), ())),
                                   preferred_element_type=F32)
            xg = xdt[r0:r0 + q, l0:l0 + gw]
            xg_b = xg.astype(BF16)
            xbd = jnp.where(blk, jnp.concatenate([xg_b] * hpg, axis=0), jnp.zeros((), BF16))
            y_diag = jnp.dot((gcat * lcat).astype(BF16), xbd, preferred_element_type=F32)
            st = state_ref[g]
            y_off = jnp.dot(cg, st.astype(BF16), preferred_element_type=F32) \
                * jnp.exp(a_cg - a_prev[:, l0:l0 + gw])
            dte = jnp.exp(a_end[:, l0:l0 + gw] - a_cg)
            upd = lax.dot_general(bg, (xg * dte).astype(BF16), (((0,), (0,)), ((), ())),
                                  preferred_element_type=F32)
            state_ref[g] = st * jnp.exp(a_end[:, l0:l0 + gw] - a_prev[:, l0:l0 + gw]) + upd
            y_groups.append(y_diag + y_off)
        ys.append(jnp.concatenate(y_groups, axis=1))
    y = jnp.concatenate(ys, axis=0) + xs * dskip_ref[...]
    y = y * _silu(z_ref[...].astype(F32))
    outs = []
    for g in range(SSD_GROUPS):
        yg = y[:, g * gw:(g + 1) * gw]
        ms = jnp.mean(yg * yg, axis=1, keepdims=True)
        outs.append(yg * lax.rsqrt(ms + RMS_EPS))
    y_ref[...] = (jnp.concatenate(outs, axis=1) * ng_ref[...]).astype(y_ref.dtype)


def _ssd(xbc, z, gate, conv_w, conv_b, a_exp, dskip_exp, norm_g, batch, seq):
    t = xbc.shape[0]
    rows = min(256, seq)
    nj = seq // rows
    row = lambda b, j: (b * nj + j, 0)
    const = lambda b, j: (0, 0)
    shifts = jnp.stack([jnp.eye(rows, k=-s, dtype=BF16) for s in range(1, CONV_WIDTH)])
    return pl.pallas_call(
        _ssd_kernel,
        grid=(batch, nj),
        in_specs=[
            pl.BlockSpec((rows, SSD_CONV_DIM), row),
            pl.BlockSpec((rows, SSD_INNER), row),
            pl.BlockSpec((rows, LANES), row),
            pl.BlockSpec((CONV_WIDTH, SSD_CONV_DIM), const),
            pl.BlockSpec((1, SSD_CONV_DIM), const),
            pl.BlockSpec((1, SSD_INNER), const),
            pl.BlockSpec((1, SSD_INNER), const),
            pl.BlockSpec((1, SSD_INNER), const),
            pl.BlockSpec((CONV_WIDTH - 1, rows, rows), lambda b, j: (0, 0, 0)),
        ],
        out_specs=pl.BlockSpec((rows, SSD_INNER), row),
        out_shape=jax.ShapeDtypeStruct((t, SSD_INNER), BF16),
        scratch_shapes=[
            pltpu.VMEM((2 * SUBLANES, SSD_CONV_DIM), F32),
            pltpu.VMEM((SSD_GROUPS, SSD_STATE, SSD_INNER // SSD_GROUPS), F32),
        ],
        compiler_params=pltpu.CompilerParams(
            dimension_semantics=("arbitrary", "arbitrary"), vmem_limit_bytes=VMEM_LIMIT_BYTES),
        name="ssd",
    )(xbc, z, gate, conv_w, conv_b, a_exp, dskip_exp, norm_g, shifts)


def _attn_kernel(qt_ref, k_ref, vt_ref, qft_ref, kf_ref, o_ref, *, tb, pairs):
    pair0 = pl.program_id(1) * pairs
    nb = qt_ref.shape[0]
    feat = lax.broadcasted_iota(jnp.int32, (LANES, 1), 0)
    in_head = (feat < ATT_HEAD_DIM, feat >= ATT_HEAD_DIM)
    ones = jnp.ones((2 * SUBLANES, tb), BF16)
    key = lax.broadcasted_iota(jnp.int32, (tb, 2 * tb), 0)
    qry = lax.broadcasted_iota(jnp.int32, (tb, 2 * tb), 1)
    causal = jnp.where(qry >= tb, qry - tb, qry) >= key
    zero = jnp.zeros((), BF16)

    def logits_t(qa_t, pp, jk):
        k0 = pl.multiple_of(jk * tb, tb)
        ka = jnp.concatenate([k_ref[pl.ds(k0, tb), pp * LANES:(pp + 1) * LANES],
                              kf_ref[pl.ds(k0, tb), :]], axis=1)
        return jnp.dot(ka, qa_t, preferred_element_type=F32)

    def update(s_t, m, acc_t, pp, jk):
        m_new = jnp.maximum(m, jnp.max(s_t, axis=0, keepdims=True))
        alpha = jnp.exp2(m - m_new)
        p_t = jnp.exp2(s_t - m_new).astype(BF16)
        va_t = jnp.concatenate([vt_ref[jk, pp * LANES:(pp + 1) * LANES, :], ones], axis=0)
        acc_t = alpha * acc_t + jnp.dot(va_t, p_t, preferred_element_type=F32)
        return m_new, acc_t

    def q_block(i, _):
        q0 = pl.multiple_of(i * tb, tb)
        qf_t = qft_ref[i]
        qas = []
        for pp in range(pairs):
            q_t = qt_ref[i, pp * LANES:(pp + 1) * LANES, :]
            cols = []
            for hh in range(2):
                fb0 = FB_LANES * (2 * (pair0 + pp) + hh)
                in_fb = (feat >= fb0) & (feat < fb0 + FB_LANES)
                cols.append(jnp.concatenate(
                    [jnp.where(in_head[hh], q_t, zero), jnp.where(in_fb, qf_t, zero)], axis=0))
            qas.append(jnp.concatenate(cols, axis=1))

        def body(jk, carry):
            new = []
            for pp in range(pairs):
                s_t, m, acc_t = carry[pp]
                s_next = logits_t(qas[pp], pp, jk + 1)
                m, acc_t = update(s_t, m, acc_t, pp, jk)
                new.append((s_next, m, acc_t))
            return tuple(new)

        init = tuple((logits_t(qas[pp], pp, 0), jnp.full((1, 2 * tb), NEG_BIG, F32),
                      jnp.zeros((LANES + 2 * SUBLANES, 2 * tb), F32)) for pp in range(pairs))
        carry = lax.fori_loop(0, i, body, init)
        for pp in range(pairs):
            s_t, m, acc_t = carry[pp]
            _, acc_t = update(jnp.where(causal, s_t, NEG_BIG), m, acc_t, pp, i)
            inv = 1.0 / acc_t[LANES:LANES + 1, :]
            out_t = jnp.where(in_head[0], acc_t[:LANES, :tb] * inv[:, :tb],
                              acc_t[:LANES, tb:] * inv[:, tb:])
            o_ref[pl.ds(q0, tb), pp * LANES:(pp + 1) * LANES] = out_t.T.astype(o_ref.dtype)
        return 0

    lax.fori_loop(0, nb, q_block, 0)


def _attention(q_t, k, v_t, qf_t, kf, batch, seq):
    t = k.shape[0]
    tb = min(ATT_BLOCK, seq)
    nb = seq // tb
    pairs = ATT_PAIRS_PER_STEP
    width = pairs * LANES
    return pl.pallas_call(
        functools.partial(_attn_kernel, tb=tb, pairs=pairs),
        grid=(batch, ATT_HEADS // (2 * pairs)),
        in_specs=[
            pl.BlockSpec((nb, width, tb), lambda b, p: (b, p, 0)),
            pl.BlockSpec((seq, width), lambda b, p: (b, p)),
            pl.BlockSpec((nb, width, tb), lambda b, p: (b, p, 0)),
            pl.BlockSpec((nb, LANES, tb), lambda b, p: (b, 0, 0)),
            pl.BlockSpec((seq, LANES), lambda b, p: (b, 0)),
        ],
        out_specs=pl.BlockSpec((seq, width), lambda b, p: (b, p)),
        out_shape=jax.ShapeDtypeStruct((t, ATT_INNER), BF16),
        compiler_params=pltpu.CompilerParams(
            dimension_semantics=("arbitrary", "arbitrary"), vmem_limit_bytes=VMEM_LIMIT_BYTES),
        name="fox_attention",
    )(q_t, k, v_t, qf_t, kf)


def _layer_norm(u, g, b):
    mu = jnp.mean(u, axis=1, keepdims=True)
    d = u - mu
    var = jnp.mean(d * d, axis=1, keepdims=True)
    return d * lax.rsqrt(var + LN_EPS) * g + b


def _route(lg):
    lane = lax.broadcasted_iota(jnp.int32, (1, LANES), 1)
    lane_f = lane.astype(F32)
    is_g = lane < N_EXPERT_GROUPS
    gl = jnp.where(is_g, lg, NEG_BIG)
    gmax = jnp.max(gl, axis=1, keepdims=True)
    gsum = jnp.sum(jnp.where(is_g, jnp.exp(gl - gmax), 0.0), axis=1, keepdims=True)
    g_val = 1.0 / gsum
    g_idx = jnp.min(jnp.where(is_g & (gl == gmax), lane_f, float(LANES)), axis=1, keepdims=True)
    e_grp = ((lane - ROUTE_E_LANE) // EXPERTS_PER_GROUP).astype(F32)
    in_grp = (lane >= ROUTE_E_LANE) & (lane < ROUTE_E_LANE + N_EXPERTS) & (e_grp == g_idx)
    el = jnp.where(in_grp, lg, NEG_BIG)
    m1 = jnp.max(el, axis=1, keepdims=True)
    i1 = jnp.min(jnp.where(in_grp & (el == m1), lane_f, float(LANES)), axis=1, keepdims=True)
    rest = in_grp & (lane_f != i1)
    el2 = jnp.where(rest, lg, NEG_BIG)
    m2 = jnp.max(el2, axis=1, keepdims=True)
    i2 = jnp.min(jnp.where(rest & (el2 == m2), lane_f, float(LANES)), axis=1, keepdims=True)
    r = jnp.exp(m2 - m1)
    w1 = 1.0 / (1.0 + r)
    w2 = r * w1
    comb = jnp.where(lane_f == i1, g_val * w1, jnp.where(lane_f == i2, g_val * w2, 0.0))
    sel = jnp.where((lane_f == i1) | (lane_f == i2), 1.0, 0.0)
    return comb, sel


def _out_proj_kernel(ys_ref, ya_ref, x_ref, wa_ref, wb_ref, g_ref, b_ref, wr_ref, br_ref,
                     h_ref, hb_ref, comb_ref, sel_ref, cnt_ref):
    mix = jnp.dot(ys_ref[...], wa_ref[...], preferred_element_type=F32)
    mix = mix + jnp.dot(ya_ref[...], wb_ref[...], preferred_element_type=F32)
    h = _layer_norm(DEEPNORM_ALPHA * x_ref[...] + mix, g_ref[...], b_ref[...])
    h_ref[...] = h
    h_hi = h.astype(BF16)
    hb_ref[...] = h_hi
    h_lo = (h - h_hi.astype(F32)).astype(BF16)
    w_hi = wr_ref[:, :LANES]
    lg = (jnp.dot(h_hi, w_hi, preferred_element_type=F32)
          + jnp.dot(h_lo, w_hi, preferred_element_type=F32)
          + jnp.dot(h_hi, wr_ref[:, LANES:], preferred_element_type=F32)) + br_ref[...]
    comb, sel = _route(lg)
    comb_ref[...] = comb
    sel_ref[...] = sel
    ts = h.shape[0] // cnt_ref.shape[0]
    for c in range(cnt_ref.shape[0]):
        cnt = jnp.sum(sel[c * ts:(c + 1) * ts, :], axis=0, keepdims=True)
        cnt_ref[c] = jnp.broadcast_to(cnt, (SUBLANES, LANES))


def _out_proj(ys, ya, x2d, wo_a, wo_b, ln_g, ln_b, wr, br, ts):
    t = x2d.shape[0]
    tm = min(512, t)
    row = lambda i: (i, 0)
    const = lambda i: (0, 0)
    return pl.pallas_call(
        _out_proj_kernel,
        grid=(t // tm,),
        in_specs=[
            pl.BlockSpec((tm, SSD_INNER), row),
            pl.BlockSpec((tm, ATT_INNER), row),
            pl.BlockSpec((tm, D_MODEL), row),
            pl.BlockSpec((SSD_INNER, D_MODEL), const),
            pl.BlockSpec((ATT_INNER, D_MODEL), const),
            pl.BlockSpec((1, D_MODEL), const),
            pl.BlockSpec((1, D_MODEL), const),
            pl.BlockSpec((D_MODEL, 2 * LANES), const),
            pl.BlockSpec((1, LANES), const),
        ],
        out_specs=[
            pl.BlockSpec((tm, D_MODEL), row),
            pl.BlockSpec((tm, D_MODEL), row),
            pl.BlockSpec((tm, LANES), row),
            pl.BlockSpec((tm, LANES), row),
            pl.BlockSpec((tm // ts, SUBLANES, LANES), lambda i: (i, 0, 0)),
        ],
        out_shape=(
            jax.ShapeDtypeStruct((t, D_MODEL), F32),
            jax.ShapeDtypeStruct((t, D_MODEL), BF16),
            jax.ShapeDtypeStruct((t, LANES), F32),
            jax.ShapeDtypeStruct((t, LANES), F32),
            jax.ShapeDtypeStruct((t // ts, SUBLANES, LANES), F32),
        ),
        compiler_params=pltpu.CompilerParams(
            dimension_semantics=("arbitrary",), vmem_limit_bytes=VMEM_LIMIT_BYTES),
        name="out_proj_ln_router",
    )(ys, ya, x2d, wo_a, wo_b, ln_g, ln_b, wr, br)


def _dispatch_plan(cnt, ts):
    nt = cnt.shape[0]
    c16 = (cnt + (MOE_GRANULE - 1)) // MOE_GRANULE * MOE_GRANULE
    loc_off = jnp.cumsum(c16, axis=1) - c16
    tot = jnp.sum(c16, axis=0)
    tiles = (tot + (MOE_ROW_TILE - 1)) // MOE_ROW_TILE
    cum_tiles = jnp.cumsum(tiles)
    start = (cum_tiles - tiles) * MOE_ROW_TILE
    base = start[None, :] + jnp.cumsum(c16, axis=0) - c16
    max_tiles = _moe_rows(nt * ts, ts) // MOE_ROW_TILE
    tile_expert = jnp.minimum(
        jnp.sum(jnp.arange(max_tiles, dtype=jnp.int32)[:, None] >= cum_tiles[None, :], axis=1),
        N_EXPERTS - 1).astype(jnp.int32)
    loc_vec = jnp.zeros((nt, SUBLANES, LANES), F32).at[:, :, ROUTE_E_LANE:ROUTE_E_LANE + N_EXPERTS].set(
        jnp.broadcast_to(loc_off.astype(F32)[:, None, :], (nt, SUBLANES, N_EXPERTS)))
    return dict(
        chunks=(c16 // MOE_GRANULE).reshape(-1).astype(jnp.int32),
        loc_off=loc_off.reshape(-1).astype(jnp.int32),
        base=base.reshape(-1).astype(jnp.int32),
        tile_chunks=(jnp.sum(c16, axis=1) // MOE_GRANULE).astype(jnp.int32),
        pad_start=(start + tot).astype(jnp.int32),
        pad_chunks=((tiles * MOE_ROW_TILE - tot) // MOE_GRANULE).astype(jnp.int32),
        n_tiles=cum_tiles[-1:].astype(jnp.int32),
        tile_expert=tile_expert,
        loc_vec=loc_vec,
    )


def _moe_local_rows(ts):
    rows = 2 * ts + N_EXPERTS * (MOE_GRANULE - 1)
    return (rows + LANES - 1) // LANES * LANES


def _moe_rows(t, ts):
    rows = 2 * t + (t // ts) * N_EXPERTS * (MOE_GRANULE - 1) + N_EXPERTS * MOE_ROW_TILE
    return (rows + MOE_ROW_TILE - 1) // MOE_ROW_TILE * MOE_ROW_TILE


def _chunk_copy(src, dst, sem):
    return pltpu.make_async_copy(src, dst, sem)


def _wait_rows(copy_of_rows, n_chunks):
    def wait(rows):
        def one(_, c):
            copy_of_rows(rows).wait()
            return c
        return one
    shift = MOE_WAIT_BATCH.bit_length() - 1
    lax.fori_loop(0, lax.shift_right_logical(n_chunks, shift), wait(MOE_WAIT_BATCH * MOE_GRANULE), 0)
    lax.fori_loop(0, n_chunks & (MOE_WAIT_BATCH - 1), wait(MOE_GRANULE), 0)


def _dispatch_kernel(chunks_ref, loc_ref, base_ref, tchunks_ref, pstart_ref, pchunks_ref,
                     hb_ref, sel_ref, comb_ref, locv_ref, lstrict_ref,
                     xs_ref, posw_ref, buf_ref, zero_ref, sem_ref, zsem_ref):
    t = pl.program_id(0)
    nt = pl.num_programs(0)
    slot = t % 2
    ts = hb_ref.shape[0]
    nl = buf_ref.shape[1]
    g = MOE_GRANULE

    def wait_chunks(n, sl):
        _wait_rows(lambda r: _chunk_copy(buf_ref.at[sl, pl.ds(0, r)], xs_ref.at[pl.ds(0, r)],
                                         sem_ref.at[sl]), n)

    @pl.when(t >= 2)
    def _():
        wait_chunks(tchunks_ref[t - 2], slot)

    lane = lax.broadcasted_iota(jnp.int32, (1, LANES), 1)
    sel = sel_ref[...]
    comb = comb_ref[...]
    picked = sel > 0.0
    rank = jnp.dot(lstrict_ref[...], sel.astype(BF16), preferred_element_type=F32)
    dest = rank + locv_ref[0][0:1, :]
    pos_a = jnp.min(jnp.where(picked, dest, 1e9), axis=1, keepdims=True)
    pos_b = jnp.max(jnp.where(picked, dest, -1.0), axis=1, keepdims=True)
    w_a = jnp.sum(jnp.where(picked & (dest == pos_a), comb, 0.0), axis=1, keepdims=True)
    w_b = jnp.sum(jnp.where(picked & (dest == pos_b), comb, 0.0), axis=1, keepdims=True)
    posw = jnp.where(lane == 0, pos_a, jnp.where(lane == 1, pos_b,
                     jnp.where(lane == 2, w_a, jnp.where(lane == 3, w_b, 0.0))))
    posw_ref[...] = posw
    posw_t = posw.T
    d = lax.broadcasted_iota(jnp.int32, (nl, ts), 0).astype(F32)
    perm = jnp.where((d == posw_t[0:1, :]) | (d == posw_t[1:2, :]), 1.0, 0.0).astype(BF16)
    buf_ref[slot] = jnp.dot(perm, hb_ref[...], preferred_element_type=F32).astype(BF16)

    for e in range(N_EXPERTS):
        idx = t * N_EXPERTS + e
        lo = loc_ref[idx]
        bs = base_ref[idx]

        def send(k, c, lo=lo, bs=bs):
            src = buf_ref.at[slot, pl.ds(pl.multiple_of(lo + k * g, g), g)]
            dst = xs_ref.at[pl.ds(pl.multiple_of(bs + k * g, g), g)]
            _chunk_copy(src, dst, sem_ref.at[slot]).start()
            return c
        lax.fori_loop(0, chunks_ref[idx], send, 0)

    @pl.when(t == nt - 1)
    def _():
        zero_ref[...] = jnp.zeros_like(zero_ref)
        for e in range(N_EXPERTS):
            ps = pstart_ref[e]

            def send_zero(k, c, ps=ps):
                dst = xs_ref.at[pl.ds(pl.multiple_of(ps + k * g, g), g)]
                _chunk_copy(zero_ref, dst, zsem_ref.at[0]).start()
                return c
            lax.fori_loop(0, pchunks_ref[e], send_zero, 0)
        for e in range(N_EXPERTS):
            def wait_zero(_, c):
                _chunk_copy(zero_ref, xs_ref.at[pl.ds(0, g)], zsem_ref.at[0]).wait()
                return c
            lax.fori_loop(0, pchunks_ref[e], wait_zero, 0)
        wait_chunks(tchunks_ref[t], slot)

        @pl.when(t >= 1)
        def _():
            wait_chunks(tchunks_ref[t - 1], 1 - slot)


def _dispatch(plan, hb, sel, comb, ts):
    t = hb.shape[0]
    nt = t // ts
    nl = _moe_local_rows(ts)
    rows = _moe_rows(t, ts)
    lstrict = jnp.tri(ts, k=-1, dtype=BF16)
    row = lambda i, *_: (i, 0)
    grid_spec = pltpu.PrefetchScalarGridSpec(
        num_scalar_prefetch=6,
        grid=(nt,),
        in_specs=[
            pl.BlockSpec((ts, D_MODEL), row),
            pl.BlockSpec((ts, LANES), row),
            pl.BlockSpec((ts, LANES), row),
            pl.BlockSpec((1, SUBLANES, LANES), lambda i, *_: (i, 0, 0)),
            pl.BlockSpec((ts, ts), lambda i, *_: (0, 0)),
        ],
        out_specs=[
            pl.BlockSpec(memory_space=pl.ANY),
            pl.BlockSpec((ts, LANES), row),
        ],
        scratch_shapes=[
            pltpu.VMEM((2, nl, D_MODEL), BF16),
            pltpu.VMEM((MOE_GRANULE, D_MODEL), BF16),
            pltpu.SemaphoreType.DMA((2,)),
            pltpu.SemaphoreType.DMA((1,)),
        ],
    )
    return pl.pallas_call(
        _dispatch_kernel,
        grid_spec=grid_spec,
        out_shape=(
            jax.ShapeDtypeStruct((rows, D_MODEL), BF16),
            jax.ShapeDtypeStruct((t, LANES), F32),
        ),
        compiler_params=pltpu.CompilerParams(
            dimension_semantics=("arbitrary",), vmem_limit_bytes=VMEM_LIMIT_BYTES),
        name="moe_dispatch",
    )(plan["chunks"], plan["loc_off"], plan["base"], plan["tile_chunks"], plan["pad_start"],
      plan["pad_chunks"], hb, sel, comb, plan["loc_vec"], lstrict)


def _expert_kernel(te_ref, nt_ref, x_ref, wg_ref, wu_ref, wd_ref, y_ref, wgb_ref, wub_ref, wdb_ref):
    j = pl.program_id(0)

    @pl.when(j < nt_ref[0])
    def _():
        @pl.when((j == 0) | (te_ref[j] != te_ref[jnp.maximum(j - 1, 0)]))
        def _():
            wgb_ref[...] = wg_ref[0].astype(BF16)
            wub_ref[...] = wu_ref[0].astype(BF16)
            wdb_ref[...] = wd_ref[0].astype(BF16)

        x = x_ref[...]
        gate = jnp.dot(x, wgb_ref[...], preferred_element_type=F32)
        up = jnp.dot(x, wub_ref[...], preferred_element_type=F32)
        act = (_silu(gate) * up).astype(BF16)
        y_ref[...] = jnp.dot(act, wdb_ref[...], preferred_element_type=F32).astype(y_ref.dtype)


def _experts(plan, xs, w_gate, w_up, w_down):
    rows = xs.shape[0]
    live = lambda j, te, nt: jnp.minimum(j, nt[0] - 1)
    expert = lambda j, te, nt: (te[live(j, te, nt)], 0, 0)
    grid_spec = pltpu.PrefetchScalarGridSpec(
        num_scalar_prefetch=2,
        grid=(rows // MOE_ROW_TILE,),
        in_specs=[
            pl.BlockSpec((MOE_ROW_TILE, D_MODEL), lambda j, te, nt: (live(j, te, nt), 0)),
            pl.BlockSpec((1, D_MODEL, D_FF_EXPERT), expert),
            pl.BlockSpec((1, D_MODEL, D_FF_EXPERT), expert),
            pl.BlockSpec((1, D_FF_EXPERT, D_MODEL), expert),
        ],
        out_specs=pl.BlockSpec((MOE_ROW_TILE, D_MODEL), lambda j, te, nt: (live(j, te, nt), 0)),
        scratch_shapes=[
            pltpu.VMEM((D_MODEL, D_FF_EXPERT), BF16),
            pltpu.VMEM((D_MODEL, D_FF_EXPERT), BF16),
            pltpu.VMEM((D_FF_EXPERT, D_MODEL), BF16),
        ],
    )
    return pl.pallas_call(
        _expert_kernel,
        grid_spec=grid_spec,
        out_shape=jax.ShapeDtypeStruct((rows, D_MODEL), BF16),
        compiler_params=pltpu.CompilerParams(
            dimension_semantics=("arbitrary",), vmem_limit_bytes=VMEM_LIMIT_BYTES),
        name="moe_experts",
    )(plan["tile_expert"], plan["n_tiles"], xs, w_gate, w_up, w_down)


def _combine_kernel(chunks_ref, loc_ref, base_ref, tchunks_ref,
                    y_ref, posw_ref, h_ref, g_ref, b_ref, o_ref, buf_ref, sem_ref):
    t = pl.program_id(0)
    nt = pl.num_programs(0)
    slot = t % 2
    nl = buf_ref.shape[1]
    g = MOE_GRANULE

    def fetch(tt, sl):
        buf_ref[sl] = jnp.zeros((nl, D_MODEL), BF16)
        for e in range(N_EXPERTS):
            idx = tt * N_EXPERTS + e
            lo = loc_ref[idx]
            bs = base_ref[idx]

            def get(k, c, lo=lo, bs=bs):
                src = y_ref.at[pl.ds(pl.multiple_of(bs + k * g, g), g)]
                dst = buf_ref.at[sl, pl.ds(pl.multiple_of(lo + k * g, g), g)]
                _chunk_copy(src, dst, sem_ref.at[sl]).start()
                return c
            lax.fori_loop(0, chunks_ref[idx], get, 0)

    @pl.when(t == 0)
    def _():
        fetch(0, 0)

    @pl.when(t + 1 < nt)
    def _():
        fetch(t + 1, 1 - slot)

    _wait_rows(lambda r: _chunk_copy(y_ref.at[pl.ds(0, r)], buf_ref.at[slot, pl.ds(0, r)],
                                     sem_ref.at[slot]), tchunks_ref[t])

    yb = buf_ref[slot]
    posw = posw_ref[...]
    d = lax.broadcasted_iota(jnp.int32, (1, nl), 1).astype(F32)
    ffn = jnp.zeros(o_ref.shape, F32)
    for k in range(2):
        gather = jnp.where(posw[:, k:k + 1] == d, 1.0, 0.0).astype(BF16)
        ffn = ffn + posw[:, 2 + k:3 + k] * jnp.dot(gather, yb, preferred_element_type=F32)
    o_ref[...] = _layer_norm(DEEPNORM_ALPHA * h_ref[...] + ffn, g_ref[...], b_ref[...])


def _combine(plan, y, posw, h, ln_g, ln_b, ts):
    t = h.shape[0]
    nl = _moe_local_rows(ts)
    row = lambda i, *_: (i, 0)
    const = lambda i, *_: (0, 0)
    grid_spec = pltpu.PrefetchScalarGridSpec(
        num_scalar_prefetch=4,
        grid=(t // ts,),
        in_specs=[
            pl.BlockSpec(memory_space=pl.ANY),
            pl.BlockSpec((ts, LANES), row),
            pl.BlockSpec((ts, D_MODEL), row),
            pl.BlockSpec((1, D_MODEL), const),
            pl.BlockSpec((1, D_MODEL), const),
        ],
        out_specs=pl.BlockSpec((ts, D_MODEL), row),
        scratch_shapes=[
            pltpu.VMEM((2, nl, D_MODEL), BF16),
            pltpu.SemaphoreType.DMA((2,)),
        ],
    )
    return pl.pallas_call(
        _combine_kernel,
        grid_spec=grid_spec,
        out_shape=jax.ShapeDtypeStruct((t, D_MODEL), F32),
        compiler_params=pltpu.CompilerParams(
            dimension_semantics=("arbitrary",), vmem_limit_bytes=VMEM_LIMIT_BYTES),
        name="moe_combine_ln",
    )(plan["chunks"], plan["loc_off"], plan["base"], plan["tile_chunks"], y, posw, h, ln_g, ln_b)


def _pad_lanes(a, n=LANES):
    return jnp.pad(a, ((0, 0), (0, n - a.shape[1])))


def _layer(h2d, batch, seq, w_in, b_in, conv_w, conv_b, a_log, d_skip, ssd_norm_g, w_out,
           ln1_g, ln1_b, rg_w, rg_b, re_w, re_b, w_gate, w_up, w_down, ln2_g, ln2_b):
    o_dt = SSD_INNER + SSD_CONV_DIM
    o_q = o_dt + SSD_HEADS
    o_f = o_q + 3 * ATT_INNER
    w_main = jnp.concatenate([w_in[:, :o_dt], w_in[:, o_q:o_f]], axis=1).astype(BF16)
    b_main = jnp.concatenate([b_in[:o_dt], b_in[o_q:o_f]])[None, :]
    w_dt, w_f = w_in[:, o_dt:o_q], w_in[:, o_f:]
    w_small = _pad_lanes(jnp.concatenate([w_dt, w_f, w_dt], axis=1)).astype(BF16)
    b_small = _pad_lanes(jnp.concatenate([b_in[o_dt:o_q], b_in[o_f:], b_in[o_dt:o_q]])[None, :])
    a_neg = -jnp.exp(a_log.astype(F32))
    a_row = _pad_lanes(a_neg[None, :])
    a_exp = jnp.repeat(a_neg, SSD_HEAD_DIM)[None, :]
    dskip_exp = jnp.repeat(d_skip.astype(F32), SSD_HEAD_DIM)[None, :]

    z, xbc, q_t, k, v_t, gate, qf_t, kf = _in_proj(h2d, w_main, b_main, w_small, b_small, a_row, seq)
    y_ssd = _ssd(xbc, z, gate, conv_w, conv_b[None, :], a_exp, dskip_exp, ssd_norm_g[None, :],
                 batch, seq)
    y_att = _attention(q_t, k, v_t, qf_t, kf, batch, seq)

    w_r = _pad_lanes(jnp.concatenate(
        [rg_w, jnp.transpose(re_w, (1, 0, 2)).reshape(D_MODEL, N_EXPERTS)], axis=1))
    b_r = _pad_lanes(jnp.concatenate([rg_b, re_b.reshape(N_EXPERTS)])[None, :])
    wr_hi = w_r.astype(BF16)
    wr_lo = (w_r - wr_hi.astype(F32)).astype(BF16)
    wo = w_out.astype(BF16)
    ts = min(MOE_SORT_TILE, h2d.shape[0])
    h1, h1b, comb, sel, cnt = _out_proj(y_ssd, y_att, h2d, wo[:SSD_INNER], wo[SSD_INNER:],
                                        ln1_g[None, :], ln1_b[None, :],
                                        jnp.concatenate([wr_hi, wr_lo], axis=1), b_r, ts)

    plan = _dispatch_plan(
        cnt[:, 0, ROUTE_E_LANE:ROUTE_E_LANE + N_EXPERTS].astype(jnp.int32), ts)
    xs, posw = _dispatch(plan, h1b, sel, comb, ts)
    y = _experts(plan, xs, w_gate, w_up, w_down)
    return _combine(plan, y, posw, h1, ln2_g[None, :], ln2_b[None, :], ts)


def kernel(x, w_in, b_in, conv_w, conv_b, a_log, d_skip, ssd_norm_g, w_out, ln1_g, ln1_b,
           router_group_w, router_group_b, router_expert_w, router_expert_b, w_gate, w_up,
           w_down, ln2_g, ln2_b):
    batch, seq, d = x.shape
    h = x.reshape(batch * seq, d)
    for l in range(w_in.shape[0]):
        h = _layer(h, batch, seq, w_in[l], b_in[l], conv_w[l], conv_b[l], a_log[l], d_skip[l],
                   ssd_norm_g[l], w_out[l], ln1_g[l], ln1_b[l], router_group_w[l],
                   router_group_b[l], router_expert_w[l], router_expert_b[l], w_gate[l],
                   w_up[l], w_down[l], ln2_g[l], ln2_b[l])
    return h.reshape(batch, seq, d)
```

```python
import functools

import jax
import jax.numpy as jnp
import numpy as np
from jax import lax
from jax.experimental import pallas as pl
from jax.experimental.pallas import tpu as pltpu

F32 = jnp.float32
BF16 = jnp.bfloat16

D_MODEL = 1024
SSD_HEADS = 8
SSD_HEAD_DIM = 64
SSD_INNER = SSD_HEADS * SSD_HEAD_DIM
SSD_GROUPS = 2
SSD_STATE = 128
CONV_WIDTH = 4
SSD_CONV_DIM = SSD_INNER + 2 * SSD_GROUPS * SSD_STATE
ATT_HEADS = 8
ATT_HEAD_DIM = 64
ATT_INNER = ATT_HEADS * ATT_HEAD_DIM
N_EXPERT_GROUPS = 4
EXPERTS_PER_GROUP = 4
N_EXPERTS = N_EXPERT_GROUPS * EXPERTS_PER_GROUP
D_FF_EXPERT = 512
DEPTH = 1
DEEPNORM_ALPHA = (2.0 * DEPTH) ** 0.25
LN_EPS = 1e-5
RMS_EPS = 1e-5

LANES = 128
SUBLANES = 8
VMEM_LIMIT_BYTES = 56 * 1024 * 1024

GATE_A_LANE = 0
GATE_F_LANE = SSD_HEADS
GATE_DT_LANE = 2 * SSD_HEADS
ROUTE_E_LANE = N_EXPERT_GROUPS
MOE_SORT_TILE = 256
MOE_GRANULE = 16
MOE_ROW_TILE = 512
MOE_TILES_PER_STEP = 2
MOE_WAIT_BATCH = 8

IN_PROJ_ROWS = 1024
CUMSUM_BLOCK = 512
FB_LANES = 6
ATT_BLOCK = 256
ATT_PAIRS_PER_STEP = 4

SSD_CHUNK = 64
NEG_BIG = -1e30
LOG2E = 1.4426950408889634


def _split3(v):
    hi = v.astype(BF16)
    r1 = v - hi.astype(F32)
    mid = r1.astype(BF16)
    lo = (r1 - mid.astype(F32)).astype(BF16)
    return hi, mid, lo


def _silu(v):
    half = 0.5 * v
    return half * (1.0 + jnp.tanh(half))


def _softplus(v):
    return jnp.maximum(v, 0.0) + jnp.log1p(jnp.exp(-jnp.abs(v)))


def _in_proj_kernel(x_ref, wm_ref, bm_ref, ws_ref, bs_ref, arow_ref, tri_ref, place_ref, ones_ref,
                    z_ref, xbc_ref, qt_ref, k_ref, vt_ref, gate_ref, qft_ref, kf_ref,
                    carry_ref, *, tiles_per_seq):
    i = pl.program_id(0)
    tm = x_ref.shape[0]
    tb = qt_ref.shape[2]
    xb = x_ref[...].astype(BF16)

    def put_transposed(ref, val):
        for c in range(tm // tb):
            ref[c] = val[c * tb:(c + 1) * tb, :].T.astype(ref.dtype)

    s = jnp.dot(xb, ws_ref[...], preferred_element_type=F32) + bs_ref[...]
    lane = lax.broadcasted_iota(jnp.int32, (1, LANES), 1)
    sp = _softplus(s)
    log_f = -_softplus(-s)
    val = jnp.where(lane < GATE_F_LANE, sp * arow_ref[...],
                    jnp.where(lane < GATE_DT_LANE, log_f,
                              jnp.where(lane < GATE_DT_LANE + SSD_HEADS, sp, 0.0)))
    carry = jnp.where(i % tiles_per_seq == 0, 0.0, carry_ref[...])
    tri = tri_ref[...]
    cb = tri.shape[0]
    sums = []
    for c in range(tm // cb):
        hi, mid, lo = _split3(val[c * cb:(c + 1) * cb, :])
        part = (jnp.dot(tri, hi, preferred_element_type=F32)
                + jnp.dot(tri, mid, preferred_element_type=F32)
                + jnp.dot(tri, lo, preferred_element_type=F32)) + carry
        carry = part[cb - 1:cb, :]
        sums.append(part)
    cs = jnp.concatenate(sums, axis=0)
    carry_ref[...] = carry
    out = jnp.where(lane < GATE_DT_LANE, cs, val)
    gate_ref[...] = out
    pieces = jnp.concatenate(_split3(cs * LOG2E), axis=1)
    fb = jnp.dot(pieces, place_ref[...], preferred_element_type=F32) + ones_ref[...]
    put_transposed(qft_ref, fb[:, :LANES])
    kf_ref[...] = fb[:, LANES:].astype(BF16)

    col = 0
    for ref, scale in ((z_ref, None), (xbc_ref, None), (qt_ref, ATT_HEAD_DIM ** -0.5 * LOG2E),
                       (k_ref, None), (vt_ref, None)):
        transposed = ref is qt_ref or ref is vt_ref
        n = ATT_INNER if transposed else ref.shape[1]
        acc = jnp.dot(xb, wm_ref[:, col:col + n], preferred_element_type=F32) + bm_ref[:, col:col + n]
        if scale is not None:
            acc = acc * scale
        if transposed:
            put_transposed(ref, acc)
        else:
            ref[...] = acc.astype(ref.dtype)
        col += n


def _forget_bias_placement():
    place = np.zeros((3 * LANES, 2 * LANES), np.float32)
    ones = np.zeros((1, 2 * LANES), np.float32)
    for h in range(ATT_HEADS):
        for piece in range(3):
            src = piece * LANES + GATE_F_LANE + h
            place[src, FB_LANES * h + piece] = 1.0
            place[src, LANES + FB_LANES * h + 3 + piece] = -1.0
            ones[0, FB_LANES * h + 3 + piece] = 1.0
            ones[0, LANES + FB_LANES * h + piece] = 1.0
    return jnp.asarray(place, BF16), jnp.asarray(ones, F32)


def _in_proj(x2d, w_main, b_main, w_small, b_small, a_row, seq):
    t = x2d.shape[0]
    tm = min(IN_PROJ_ROWS, seq)
    tk = min(ATT_BLOCK, seq)
    cum = min(CUMSUM_BLOCK, tm)
    tri = jnp.tri(cum, dtype=BF16)
    place, ones = _forget_bias_placement()
    n_main = w_main.shape[1]
    row = lambda i: (i, 0)
    const = lambda i: (0, 0)
    outs = (
        jax.ShapeDtypeStruct((t, SSD_INNER), BF16),
        jax.ShapeDtypeStruct((t, SSD_CONV_DIM), BF16),
        jax.ShapeDtypeStruct((t // tk, ATT_INNER, tk), BF16),
        jax.ShapeDtypeStruct((t, ATT_INNER), BF16),
        jax.ShapeDtypeStruct((t // tk, ATT_INNER, tk), BF16),
        jax.ShapeDtypeStruct((t, LANES), F32),
        jax.ShapeDtypeStruct((t // tk, LANES, tk), BF16),
        jax.ShapeDtypeStruct((t, LANES), BF16),
    )
    return pl.pallas_call(
        functools.partial(_in_proj_kernel, tiles_per_seq=seq // tm),
        grid=(t // tm,),
        in_specs=[
            pl.BlockSpec((tm, D_MODEL), row),
            pl.BlockSpec((D_MODEL, n_main), const),
            pl.BlockSpec((1, n_main), const),
            pl.BlockSpec((D_MODEL, LANES), const),
            pl.BlockSpec((1, LANES), const),
            pl.BlockSpec((1, LANES), const),
            pl.BlockSpec((cum, cum), const),
            pl.BlockSpec((3 * LANES, 2 * LANES), const),
            pl.BlockSpec((1, 2 * LANES), const),
        ],
        out_specs=[
            pl.BlockSpec((tm, SSD_INNER), row),
            pl.BlockSpec((tm, SSD_CONV_DIM), row),
            pl.BlockSpec((tm // tk, ATT_INNER, tk), lambda i: (i, 0, 0)),
            pl.BlockSpec((tm, ATT_INNER), row),
            pl.BlockSpec((tm // tk, ATT_INNER, tk), lambda i: (i, 0, 0)),
            pl.BlockSpec((tm, LANES), row),
            pl.BlockSpec((tm // tk, LANES, tk), lambda i: (i, 0, 0)),
            pl.BlockSpec((tm, LANES), row),
        ],
        out_shape=outs,
        scratch_shapes=[pltpu.VMEM((1, LANES), F32)],
        compiler_params=pltpu.CompilerParams(
            dimension_semantics=("arbitrary",), vmem_limit_bytes=VMEM_LIMIT_BYTES),
        name="in_proj",
    )(x2d, w_main, b_main, w_small, b_small, a_row, tri, place, ones)


def _expand_heads(gate, lane0, rows):
    lane = lax.broadcasted_iota(jnp.int32, (1, LANES), 1)
    pieces = []
    for p in range(SSD_HEADS // 2):
        a = jnp.broadcast_to(gate[:, lane0 + 2 * p:lane0 + 2 * p + 1], (rows, LANES))
        b = jnp.broadcast_to(gate[:, lane0 + 2 * p + 1:lane0 + 2 * p + 2], (rows, LANES))
        pieces.append(jnp.where(lane < SSD_HEAD_DIM, a, b))
    return jnp.concatenate(pieces, axis=1)


def _ssd_kernel(xbc_ref, z_ref, gate_ref, cw_ref, cb_ref, aexp_ref, dskip_ref, ng_ref, shift_ref,
                y_ref, ext_ref, state_ref):
    j = pl.program_id(1)
    rows = xbc_ref.shape[0]
    q = SSD_CHUNK
    gw = SSD_INNER // SSD_GROUPS
    hpg = SSD_HEADS // SSD_GROUPS
    pad = SUBLANES

    @pl.when(j == 0)
    def _():
        ext_ref[0:pad, :] = jnp.zeros((pad, SSD_CONV_DIM), F32)
        state_ref[...] = jnp.zeros_like(state_ref)

    xb = xbc_ref[...]
    xf = xb.astype(F32)
    conv = cb_ref[...] + xf * cw_ref[CONV_WIDTH - 1:CONV_WIDTH, :]
    for s in range(1, CONV_WIDTH):
        shifted = jnp.dot(shift_ref[s - 1], xb, preferred_element_type=F32)
        conv = conv + shifted * cw_ref[CONV_WIDTH - 1 - s:CONV_WIDTH - s, :]
    ext_ref[pad:2 * pad, :] = xf[0:pad, :]
    head = cb_ref[...]
    for kk in range(CONV_WIDTH):
        off = pad - (CONV_WIDTH - 1) + kk
        head = head + ext_ref[off:off + pad, :] * cw_ref[kk:kk + 1, :]
    ext_ref[0:pad, :] = xf[rows - pad:rows, :]
    u = _silu(jnp.concatenate([head, conv[pad:, :]], axis=0))
    xs = u[:, :SSD_INNER]
    b_all = u[:, SSD_INNER:SSD_INNER + SSD_GROUPS * SSD_STATE].astype(BF16)
    c_all = u[:, SSD_INNER + SSD_GROUPS * SSD_STATE:].astype(BF16)

    gate = gate_ref[...]
    dt_e = _expand_heads(gate, GATE_DT_LANE, rows)
    a_e = _expand_heads(gate, GATE_A_LANE, rows)
    xdt = xs * dt_e
    da_e = dt_e * aexp_ref[...]

    r_i = lax.broadcasted_iota(jnp.int32, (q, gw), 0)
    c_i = lax.broadcasted_iota(jnp.int32, (q, gw), 1)
    s_i = c_i % q
    eye_t = s_i == r_i
    tril_t = s_i <= r_i
    rb = lax.broadcasted_iota(jnp.int32, (hpg * q, gw), 0) // q
    cbk = lax.broadcasted_iota(jnp.int32, (hpg * q, gw), 1) // SSD_HEAD_DIM
    blk = rb == cbk

    ys = []
    for c in range(rows // q):
        r0 = c * q
        a_c = a_e[r0:r0 + q, :]
        a_prev = a_c[0:1, :] - da_e[r0:r0 + 1, :]
        a_end = a_c[q - 1:q, :]
        y_groups = []
        for g in range(SSD_GROUPS):
            l0 = g * gw
            a_cg = a_c[:, l0:l0 + gw]
            diag = jnp.sum(jnp.where(eye_t, a_cg, 0.0), axis=0, keepdims=True)
            seg = jnp.minimum(a_cg - diag, 0.0)
            lcat = jnp.where(tril_t, jnp.exp(seg), 0.0)
            bg = b_all[r0:r0 + q, g * SSD_STATE:(g + 1) * SSD_STATE]
            cg = c_all[r0:r0 + q, g * SSD_STATE:(g + 1) * SSD_STATE]
            b_tile = jnp.concatenate([bg] * hpg, axis=0)
            gcat = lax.dot_general(cg, b_tile, (((1,), (1,)), ((), ())),
                                   preferred_element_type=F32)
            xg = xdt[r0:r0 + q, l0:l0 + gw]
            xg_b = xg.astype(BF16)
            xbd = jnp.where(blk, jnp.concatenate([xg_b] * hpg, axis=0), jnp.zeros((), BF16))
            y_diag = jnp.dot((gcat * lcat).astype(BF16), xbd, preferred_element_type=F32)
            st = state_ref[g]
            y_off = jnp.dot(cg, st.astype(BF16), preferred_element_type=F32) \
                * jnp.exp(a_cg - a_prev[:, l0:l0 + gw])
            dte = jnp.exp(a_end[:, l0:l0 + gw] - a_cg)
            upd = lax.dot_general(bg, (xg * dte).astype(BF16), (((0,), (0,)), ((), ())),
                                  preferred_element_type=F32)
            state_ref[g] = st * jnp.exp(a_end[:, l0:l0 + gw] - a_prev[:, l0:l0 + gw]) + upd
            y_groups.append(y_diag + y_off)
        ys.append(jnp.concatenate(y_groups, axis=1))
    y = jnp.concatenate(ys, axis=0) + xs * dskip_ref[...]
    y = y * _silu(z_ref[...].astype(F32))
    outs = []
    for g in range(SSD_GROUPS):
        yg = y[:, g * gw:(g + 1) * gw]
        ms = jnp.mean(yg * yg, axis=1, keepdims=True)
        outs.append(yg * lax.rsqrt(ms + RMS_EPS))
    y_ref[...] = (jnp.concatenate(outs, axis=1) * ng_ref[...]).astype(y_ref.dtype)


def _ssd(xbc, z, gate, conv_w, conv_b, a_exp, dskip_exp, norm_g, batch, seq):
    t = xbc.shape[0]
    rows = min(256, seq)
    nj = seq // rows
    row = lambda b, j: (b * nj + j, 0)
    const = lambda b, j: (0, 0)
    shifts = jnp.stack([jnp.eye(rows, k=-s, dtype=BF16) for s in range(1, CONV_WIDTH)])
    return pl.pallas_call(
        _ssd_kernel,
        grid=(batch, nj),
        in_specs=[
            pl.BlockSpec((rows, SSD_CONV_DIM), row),
            pl.BlockSpec((rows, SSD_INNER), row),
            pl.BlockSpec((rows, LANES), row),
            pl.BlockSpec((CONV_WIDTH, SSD_CONV_DIM), const),
            pl.BlockSpec((1, SSD_CONV_DIM), const),
            pl.BlockSpec((1, SSD_INNER), const),
            pl.BlockSpec((1, SSD_INNER), const),
            pl.BlockSpec((1, SSD_INNER), const),
            pl.BlockSpec((CONV_WIDTH - 1, rows, rows), lambda b, j: (0, 0, 0)),
        ],
        out_specs=pl.BlockSpec((rows, SSD_INNER), row),
        out_shape=jax.ShapeDtypeStruct((t, SSD_INNER), BF16),
        scratch_shapes=[
            pltpu.VMEM((2 * SUBLANES, SSD_CONV_DIM), F32),
            pltpu.VMEM((SSD_GROUPS, SSD_STATE, SSD_INNER // SSD_GROUPS), F32),
        ],
        compiler_params=pltpu.CompilerParams(
            dimension_semantics=("arbitrary", "arbitrary"), vmem_limit_bytes=VMEM_LIMIT_BYTES),
        name="ssd",
    )(xbc, z, gate, conv_w, conv_b, a_exp, dskip_exp, norm_g, shifts)


def _attn_kernel(qt_ref, k_ref, vt_ref, qft_ref, kf_ref, o_ref, *, tb, pairs):
    pair0 = pl.program_id(1) * pairs
    nb = qt_ref.shape[0]
    feat = lax.broadcasted_iota(jnp.int32, (LANES, 1), 0)
    in_head = (feat < ATT_HEAD_DIM, feat >= ATT_HEAD_DIM)
    ones = jnp.ones((2 * SUBLANES, tb), BF16)
    key = lax.broadcasted_iota(jnp.int32, (tb, 2 * tb), 0)
    qry = lax.broadcasted_iota(jnp.int32, (tb, 2 * tb), 1)
    causal = jnp.where(qry >= tb, qry - tb, qry) >= key
    zero = jnp.zeros((), BF16)

    def logits_t(qa_t, pp, jk):
        k0 = pl.multiple_of(jk * tb, tb)
        ka = jnp.concatenate([k_ref[pl.ds(k0, tb), pp * LANES:(pp + 1) * LANES],
                              kf_ref[pl.ds(k0, tb), :]], axis=1)
        return jnp.dot(ka, qa_t, preferred_element_type=F32)

    def update(s_t, m, acc_t, pp, jk):
        m_new = jnp.maximum(m, jnp.max(s_t, axis=0, keepdims=True))
        alpha = jnp.exp2(m - m_new)
        p_t = jnp.exp2(s_t - m_new).astype(BF16)
        va_t = jnp.concatenate([vt_ref[jk, pp * LANES:(pp + 1) * LANES, :], ones], axis=0)
        acc_t = alpha * acc_t + jnp.dot(va_t, p_t, preferred_element_type=F32)
        return m_new, acc_t

    def q_block(i, _):
        q0 = pl.multiple_of(i * tb, tb)
        qf_t = qft_ref[i]
        qas = []
        for pp in range(pairs):
            q_t = qt_ref[i, pp * LANES:(pp + 1) * LANES, :]
            cols = []
            for hh in range(2):
                fb0 = FB_LANES * (2 * (pair0 + pp) + hh)
                in_fb = (feat >= fb0) & (feat < fb0 + FB_LANES)
                cols.append(jnp.concatenate(
                    [jnp.where(in_head[hh], q_t, zero), jnp.where(in_fb, qf_t, zero)], axis=0))
            qas.append(jnp.concatenate(cols, axis=1))

        def body(jk, carry):
            new = []
            for pp in range(pairs):
                s_t, m, acc_t = carry[pp]
                s_next = logits_t(qas[pp], pp, jk + 1)
                m, acc_t = update(s_t, m, acc_t, pp, jk)
                new.append((s_next, m, acc_t))
            return tuple(new)

        init = tuple((logits_t(qas[pp], pp, 0), jnp.full((1, 2 * tb), NEG_BIG, F32),
                      jnp.zeros((LANES + 2 * SUBLANES, 2 * tb), F32)) for pp in range(pairs))
        carry = lax.fori_loop(0, i, body, init)
        for pp in range(pairs):
            s_t, m, acc_t = carry[pp]
            _, acc_t = update(jnp.where(causal, s_t, NEG_BIG), m, acc_t, pp, i)
            inv = 1.0 / acc_t[LANES:LANES + 1, :]
            out_t = jnp.where(in_head[0], acc_t[:LANES, :tb] * inv[:, :tb],
                              acc_t[:LANES, tb:] * inv[:, tb:])
            o_ref[pl.ds(q0, tb), pp * LANES:(pp + 1) * LANES] = out_t.T.astype(o_ref.dtype)
        return 0

    lax.fori_loop(0, nb, q_block, 0)


def _attention(q_t, k, v_t, qf_t, kf, batch, seq):
    t = k.shape[0]
    tb = min(ATT_BLOCK, seq)
    nb = seq // tb
    pairs = ATT_PAIRS_PER_STEP
    width = pairs * LANES
    return pl.pallas_call(
        functools.partial(_attn_kernel, tb=tb, pairs=pairs),
        grid=(batch, ATT_HEADS // (2 * pairs)),
        in_specs=[
            pl.BlockSpec((nb, width, tb), lambda b, p: (b, p, 0)),
            pl.BlockSpec((seq, width), lambda b, p: (b, p)),
            pl.BlockSpec((nb, width, tb), lambda b, p: (b, p, 0)),
            pl.BlockSpec((nb, LANES, tb), lambda b, p: (b, 0, 0)),
            pl.BlockSpec((seq, LANES), lambda b, p: (b, 0)),
        ],
        out_specs=pl.BlockSpec((seq, width), lambda b, p: (b, p)),
        out_shape=jax.ShapeDtypeStruct((t, ATT_INNER), BF16),
        compiler_params=pltpu.CompilerParams(
            dimension_semantics=("arbitrary", "arbitrary"), vmem_limit_bytes=VMEM_LIMIT_BYTES),
        name="fox_attention",
    )(q_t, k, v_t, qf_t, kf)


def _layer_norm(u, g, b):
    mu = jnp.mean(u, axis=1, keepdims=True)
    d = u - mu
    var = jnp.mean(d * d, axis=1, keepdims=True)
    return d * lax.rsqrt(var + LN_EPS) * g + b


def _route(lg):
    lane = lax.broadcasted_iota(jnp.int32, (1, LANES), 1)
    lane_f = lane.astype(F32)
    is_g = lane < N_EXPERT_GROUPS
    gl = jnp.where(is_g, lg, NEG_BIG)
    gmax = jnp.max(gl, axis=1, keepdims=True)
    gsum = jnp.sum(jnp.where(is_g, jnp.exp(gl - gmax), 0.0), axis=1, keepdims=True)
    g_val = 1.0 / gsum
    g_idx = jnp.min(jnp.where(is_g & (gl == gmax), lane_f, float(LANES)), axis=1, keepdims=True)
    e_grp = ((lane - ROUTE_E_LANE) // EXPERTS_PER_GROUP).astype(F32)
    in_grp = (lane >= ROUTE_E_LANE) & (lane < ROUTE_E_LANE + N_EXPERTS) & (e_grp == g_idx)
    el = jnp.where(in_grp, lg, NEG_BIG)
    m1 = jnp.max(el, axis=1, keepdims=True)
    i1 = jnp.min(jnp.where(in_grp & (el == m1), lane_f, float(LANES)), axis=1, keepdims=True)
    rest = in_grp & (lane_f != i1)
    el2 = jnp.where(rest, lg, NEG_BIG)
    m2 = jnp.max(el2, axis=1, keepdims=True)
    i2 = jnp.min(jnp.where(rest & (el2 == m2), lane_f, float(LANES)), axis=1, keepdims=True)
    r = jnp.exp(m2 - m1)
    w1 = 1.0 / (1.0 + r)
    w2 = r * w1
    comb = jnp.where(lane_f == i1, g_val * w1, jnp.where(lane_f == i2, g_val * w2, 0.0))
    sel = jnp.where((lane_f == i1) | (lane_f == i2), 1.0, 0.0)
    return comb, sel


def _out_proj_kernel(ys_ref, ya_ref, x_ref, wa_ref, wb_ref, g_ref, b_ref, wr_ref, br_ref,
                     h_ref, hb_ref, comb_ref, sel_ref, cnt_ref):
    mix = jnp.dot(ys_ref[...], wa_ref[...], preferred_element_type=F32)
    mix = mix + jnp.dot(ya_ref[...], wb_ref[...], preferred_element_type=F32)
    h = _layer_norm(DEEPNORM_ALPHA * x_ref[...] + mix, g_ref[...], b_ref[...])
    h_ref[...] = h
    h_hi = h.astype(BF16)
    hb_ref[...] = h_hi
    h_lo = (h - h_hi.astype(F32)).astype(BF16)
    w_hi = wr_ref[:, :LANES]
    lg = (jnp.dot(h_hi, w_hi, preferred_element_type=F32)
          + jnp.dot(h_lo, w_hi, preferred_element_type=F32)
          + jnp.dot(h_hi, wr_ref[:, LANES:], preferred_element_type=F32)) + br_ref[...]
    comb, sel = _route(lg)
    comb_ref[...] = comb
    sel_ref[...] = sel
    ts = h.shape[0] // cnt_ref.shape[0]
    for c in range(cnt_ref.shape[0]):
        cnt = jnp.sum(sel[c * ts:(c + 1) * ts, :], axis=0, keepdims=True)
        cnt_ref[c] = jnp.broadcast_to(cnt, (SUBLANES, LANES))


def _out_proj(ys, ya, x2d, wo_a, wo_b, ln_g, ln_b, wr, br, ts):
    t = x2d.shape[0]
    tm = min(512, t)
    row = lambda i: (i, 0)
    const = lambda i: (0, 0)
    return pl.pallas_call(
        _out_proj_kernel,
        grid=(t // tm,),
        in_specs=[
            pl.BlockSpec((tm, SSD_INNER), row),
            pl.BlockSpec((tm, ATT_INNER), row),
            pl.BlockSpec((tm, D_MODEL), row),
            pl.BlockSpec((SSD_INNER, D_MODEL), const),
            pl.BlockSpec((ATT_INNER, D_MODEL), const),
            pl.BlockSpec((1, D_MODEL), const),
            pl.BlockSpec((1, D_MODEL), const),
            pl.BlockSpec((D_MODEL, 2 * LANES), const),
            pl.BlockSpec((1, LANES), const),
        ],
        out_specs=[
            pl.BlockSpec((tm, D_MODEL), row),
            pl.BlockSpec((tm, D_MODEL), row),
            pl.BlockSpec((tm, LANES), row),
            pl.BlockSpec((tm, LANES), row),
            pl.BlockSpec((tm // ts, SUBLANES, LANES), lambda i: (i, 0, 0)),
        ],
        out_shape=(
            jax.ShapeDtypeStruct((t, D_MODEL), F32),
            jax.ShapeDtypeStruct((t, D_MODEL), BF16),
            jax.ShapeDtypeStruct((t, LANES), F32),
            jax.ShapeDtypeStruct((t, LANES), F32),
            jax.ShapeDtypeStruct((t // ts, SUBLANES, LANES), F32),
        ),
        compiler_params=pltpu.CompilerParams(
            dimension_semantics=("arbitrary",), vmem_limit_bytes=VMEM_LIMIT_BYTES),
        name="out_proj_ln_router",
    )(ys, ya, x2d, wo_a, wo_b, ln_g, ln_b, wr, br)


def _dispatch_plan(cnt, ts):
    nt = cnt.shape[0]
    c16 = (cnt + (MOE_GRANULE - 1)) // MOE_GRANULE * MOE_GRANULE
    loc_off = jnp.cumsum(c16, axis=1) - c16
    tot = jnp.sum(c16, axis=0)
    tiles = (tot + (MOE_ROW_TILE - 1)) // MOE_ROW_TILE
    cum_tiles = jnp.cumsum(tiles)
    start = (cum_tiles - tiles) * MOE_ROW_TILE
    base = start[None, :] + jnp.cumsum(c16, axis=0) - c16
    max_tiles = _moe_rows(nt * ts, ts) // MOE_ROW_TILE
    tile_expert = jnp.minimum(
        jnp.sum(jnp.arange(max_tiles, dtype=jnp.int32)[:, None] >= cum_tiles[None, :], axis=1),
        N_EXPERTS - 1).astype(jnp.int32)
    loc_vec = jnp.zeros((nt, SUBLANES, LANES), F32).at[:, :, ROUTE_E_LANE:ROUTE_E_LANE + N_EXPERTS].set(
        jnp.broadcast_to(loc_off.astype(F32)[:, None, :], (nt, SUBLANES, N_EXPERTS)))
    return dict(
        chunks=(c16 // MOE_GRANULE).reshape(-1).astype(jnp.int32),
        loc_off=loc_off.reshape(-1).astype(jnp.int32),
        base=base.reshape(-1).astype(jnp.int32),
        step_chunks=(jnp.sum(c16, axis=1).reshape(-1, _moe_group(nt * ts, ts)).sum(axis=1)
                     // MOE_GRANULE).astype(jnp.int32),
        pad_start=(start + tot).astype(jnp.int32),
        pad_chunks=((tiles * MOE_ROW_TILE - tot) // MOE_GRANULE).astype(jnp.int32),
        n_tiles=cum_tiles[-1:].astype(jnp.int32),
        tile_expert=tile_expert,
        loc_vec=loc_vec,
    )


def _moe_group(t, ts):
    return MOE_TILES_PER_STEP if t % (MOE_TILES_PER_STEP * ts) == 0 else 1


def _moe_local_rows(ts):
    rows = 2 * ts + N_EXPERTS * (MOE_GRANULE - 1)
    return (rows + LANES - 1) // LANES * LANES


def _moe_rows(t, ts):
    rows = 2 * t + (t // ts) * N_EXPERTS * (MOE_GRANULE - 1) + N_EXPERTS * MOE_ROW_TILE
    return (rows + MOE_ROW_TILE - 1) // MOE_ROW_TILE * MOE_ROW_TILE


def _chunk_copy(src, dst, sem):
    return pltpu.make_async_copy(src, dst, sem)


def _wait_rows(copy_of_rows, n_chunks):
    def wait(rows):
        def one(_, c):
            copy_of_rows(rows).wait()
            return c
        return one
    shift = MOE_WAIT_BATCH.bit_length() - 1
    lax.fori_loop(0, lax.shift_right_logical(n_chunks, shift), wait(MOE_WAIT_BATCH * MOE_GRANULE), 0)
    lax.fori_loop(0, n_chunks & (MOE_WAIT_BATCH - 1), wait(MOE_GRANULE), 0)


def _dispatch_kernel(chunks_ref, loc_ref, base_ref, schunks_ref, pstart_ref, pchunks_ref,
                     hb_ref, sel_ref, comb_ref, locv_ref, lstrict_ref,
                     xs_ref, posw_ref, buf_ref, zero_ref, sem_ref, zsem_ref):
    step = pl.program_id(0)
    n_steps = pl.num_programs(0)
    par = step % 2
    group = locv_ref.shape[0]
    ts = hb_ref.shape[0] // group
    nl = buf_ref.shape[1]
    g = MOE_GRANULE

    def wait_chunks(n, pp):
        _wait_rows(lambda r: _chunk_copy(buf_ref.at[0, pl.ds(0, r)], xs_ref.at[pl.ds(0, r)],
                                         sem_ref.at[pp]), n)

    @pl.when(step >= 2)
    def _():
        wait_chunks(schunks_ref[step - 2], par)

    lane = lax.broadcasted_iota(jnp.int32, (1, LANES), 1)
    d = lax.broadcasted_iota(jnp.int32, (nl, ts), 0).astype(F32)
    for u in range(group):
        rows = slice(u * ts, (u + 1) * ts)
        sel = sel_ref[rows, :]
        comb = comb_ref[rows, :]
        picked = sel > 0.0
        rank = jnp.dot(lstrict_ref[...], sel.astype(BF16), preferred_element_type=F32)
        dest = rank + locv_ref[u][0:1, :]
        pos_a = jnp.min(jnp.where(picked, dest, 1e9), axis=1, keepdims=True)
        pos_b = jnp.max(jnp.where(picked, dest, -1.0), axis=1, keepdims=True)
        w_a = jnp.sum(jnp.where(picked & (dest == pos_a), comb, 0.0), axis=1, keepdims=True)
        w_b = jnp.sum(jnp.where(picked & (dest == pos_b), comb, 0.0), axis=1, keepdims=True)
        posw = jnp.where(lane == 0, pos_a, jnp.where(lane == 1, pos_b,
                         jnp.where(lane == 2, w_a, jnp.where(lane == 3, w_b, 0.0))))
        posw_ref[rows, :] = posw
        posw_t = posw.T
        perm = jnp.where((d == posw_t[0:1, :]) | (d == posw_t[1:2, :]), 1.0, 0.0).astype(BF16)
        buf_ref[par * group + u] = jnp.dot(perm, hb_ref[rows, :],
                                           preferred_element_type=F32).astype(BF16)

    for u in range(group):
        for e in range(N_EXPERTS):
            idx = (step * group + u) * N_EXPERTS + e
            lo = loc_ref[idx]
            bs = base_ref[idx]

            def send(k, c, lo=lo, bs=bs, u=u):
                src = buf_ref.at[par * group + u, pl.ds(pl.multiple_of(lo + k * g, g), g)]
                dst = xs_ref.at[pl.ds(pl.multiple_of(bs + k * g, g), g)]
                _chunk_copy(src, dst, sem_ref.at[par]).start()
                return c
            lax.fori_loop(0, chunks_ref[idx], send, 0)

    @pl.when(step == n_steps - 1)
    def _():
        zero_ref[...] = jnp.zeros_like(zero_ref)
        for e in range(N_EXPERTS):
            ps = pstart_ref[e]

            def send_zero(k, c, ps=ps):
                dst = xs_ref.at[pl.ds(pl.multiple_of(ps + k * g, g), g)]
                _chunk_copy(zero_ref, dst, zsem_ref.at[0]).start()
                return c
            lax.fori_loop(0, pchunks_ref[e], send_zero, 0)
        for e in range(N_EXPERTS):
            def wait_zero(_, c):
                _chunk_copy(zero_ref, xs_ref.at[pl.ds(0, g)], zsem_ref.at[0]).wait()
                return c
            lax.fori_loop(0, pchunks_ref[e], wait_zero, 0)
        wait_chunks(schunks_ref[step], par)

        @pl.when(step >= 1)
        def _():
            wait_chunks(schunks_ref[step - 1], 1 - par)


def _dispatch(plan, hb, sel, comb, ts):
    t = hb.shape[0]
    group = _moe_group(t, ts)
    nl = _moe_local_rows(ts)
    rows = _moe_rows(t, ts)
    lstrict = jnp.tri(ts, k=-1, dtype=BF16)
    row = lambda i, *_: (i, 0)
    grid_spec = pltpu.PrefetchScalarGridSpec(
        num_scalar_prefetch=6,
        grid=(t // (group * ts),),
        in_specs=[
            pl.BlockSpec((group * ts, D_MODEL), row),
            pl.BlockSpec((group * ts, LANES), row),
            pl.BlockSpec((group * ts, LANES), row),
            pl.BlockSpec((group, SUBLANES, LANES), lambda i, *_: (i, 0, 0)),
            pl.BlockSpec((ts, ts), lambda i, *_: (0, 0)),
        ],
        out_specs=[
            pl.BlockSpec(memory_space=pl.ANY),
            pl.BlockSpec((group * ts, LANES), row),
        ],
        scratch_shapes=[
            pltpu.VMEM((2 * group, nl, D_MODEL), BF16),
            pltpu.VMEM((MOE_GRANULE, D_MODEL), BF16),
            pltpu.SemaphoreType.DMA((2,)),
            pltpu.SemaphoreType.DMA((1,)),
        ],
    )
    return pl.pallas_call(
        _dispatch_kernel,
        grid_spec=grid_spec,
        out_shape=(
            jax.ShapeDtypeStruct((rows, D_MODEL), BF16),
            jax.ShapeDtypeStruct((t, LANES), F32),
        ),
        compiler_params=pltpu.CompilerParams(
            dimension_semantics=("arbitrary",), vmem_limit_bytes=VMEM_LIMIT_BYTES),
        name="moe_dispatch",
    )(plan["chunks"], plan["loc_off"], plan["base"], plan["step_chunks"], plan["pad_start"],
      plan["pad_chunks"], hb, sel, comb, plan["loc_vec"], lstrict)


def _expert_kernel(te_ref, nt_ref, x_ref, wg_ref, wu_ref, wd_ref, y_ref, wgb_ref, wub_ref, wdb_ref):
    j = pl.program_id(0)

    @pl.when(j < nt_ref[0])
    def _():
        @pl.when((j == 0) | (te_ref[j] != te_ref[jnp.maximum(j - 1, 0)]))
        def _():
            wgb_ref[...] = wg_ref[0].astype(BF16)
            wub_ref[...] = wu_ref[0].astype(BF16)
            wdb_ref[...] = wd_ref[0].astype(BF16)

        x = x_ref[...]
        gate = jnp.dot(x, wgb_ref[...], preferred_element_type=F32)
        up = jnp.dot(x, wub_ref[...], preferred_element_type=F32)
        act = (_silu(gate) * up).astype(BF16)
        y_ref[...] = jnp.dot(act, wdb_ref[...], preferred_element_type=F32).astype(y_ref.dtype)


def _experts(plan, xs, w_gate, w_up, w_down):
    rows = xs.shape[0]
    live = lambda j, te, nt: jnp.minimum(j, nt[0] - 1)
    expert = lambda j, te, nt: (te[live(j, te, nt)], 0, 0)
    grid_spec = pltpu.PrefetchScalarGridSpec(
        num_scalar_prefetch=2,
        grid=(rows // MOE_ROW_TILE,),
        in_specs=[
            pl.BlockSpec((MOE_ROW_TILE, D_MODEL), lambda j, te, nt: (live(j, te, nt), 0)),
            pl.BlockSpec((1, D_MODEL, D_FF_EXPERT), expert),
            pl.BlockSpec((1, D_MODEL, D_FF_EXPERT), expert),
            pl.BlockSpec((1, D_FF_EXPERT, D_MODEL), expert),
        ],
        out_specs=pl.BlockSpec((MOE_ROW_TILE, D_MODEL), lambda j, te, nt: (live(j, te, nt), 0)),
        scratch_shapes=[
            pltpu.VMEM((D_MODEL, D_FF_EXPERT), BF16),
            pltpu.VMEM((D_MODEL, D_FF_EXPERT), BF16),
            pltpu.VMEM((D_FF_EXPERT, D_MODEL), BF16),
        ],
    )
    return pl.pallas_call(
        _expert_kernel,
        grid_spec=grid_spec,
        out_shape=jax.ShapeDtypeStruct((rows, D_MODEL), BF16),
        compiler_params=pltpu.CompilerParams(
            dimension_semantics=("arbitrary",), vmem_limit_bytes=VMEM_LIMIT_BYTES),
        name="moe_experts",
    )(plan["tile_expert"], plan["n_tiles"], xs, w_gate, w_up, w_down)


def _combine_kernel(chunks_ref, loc_ref, base_ref, schunks_ref,
                    y_ref, posw_ref, h_ref, g_ref, b_ref, o_ref, buf_ref, sem_ref):
    step = pl.program_id(0)
    n_steps = pl.num_programs(0)
    par = step % 2
    nl = buf_ref.shape[1]
    group = buf_ref.shape[0] // 2
    ts = h_ref.shape[0] // group
    g = MOE_GRANULE

    def fetch(ss, pp):
        for u in range(group):
            buf_ref[pp * group + u] = jnp.zeros((nl, D_MODEL), BF16)
        for u in range(group):
            for e in range(N_EXPERTS):
                idx = (ss * group + u) * N_EXPERTS + e
                lo = loc_ref[idx]
                bs = base_ref[idx]

                def get(k, c, lo=lo, bs=bs, u=u):
                    src = y_ref.at[pl.ds(pl.multiple_of(bs + k * g, g), g)]
                    dst = buf_ref.at[pp * group + u, pl.ds(pl.multiple_of(lo + k * g, g), g)]
                    _chunk_copy(src, dst, sem_ref.at[pp]).start()
                    return c
                lax.fori_loop(0, chunks_ref[idx], get, 0)

    @pl.when(step == 0)
    def _():
        fetch(0, 0)

    @pl.when(step + 1 < n_steps)
    def _():
        fetch(step + 1, 1 - par)

    _wait_rows(lambda r: _chunk_copy(y_ref.at[pl.ds(0, r)], buf_ref.at[0, pl.ds(0, r)],
                                     sem_ref.at[par]), schunks_ref[step])

    d = lax.broadcasted_iota(jnp.int32, (1, nl), 1).astype(F32)
    for u in range(group):
        rows = slice(u * ts, (u + 1) * ts)
        yb = buf_ref[par * group + u]
        posw = posw_ref[rows, :]
        ffn = jnp.zeros((ts, D_MODEL), F32)
        for k in range(2):
            gather = jnp.where(posw[:, k:k + 1] == d, 1.0, 0.0).astype(BF16)
            ffn = ffn + posw[:, 2 + k:3 + k] * jnp.dot(gather, yb, preferred_element_type=F32)
        o_ref[rows, :] = _layer_norm(DEEPNORM_ALPHA * h_ref[rows, :] + ffn, g_ref[...], b_ref[...])


def _combine(plan, y, posw, h, ln_g, ln_b, ts):
    t = h.shape[0]
    group = _moe_group(t, ts)
    nl = _moe_local_rows(ts)
    row = lambda i, *_: (i, 0)
    const = lambda i, *_: (0, 0)
    grid_spec = pltpu.PrefetchScalarGridSpec(
        num_scalar_prefetch=4,
        grid=(t // (group * ts),),
        in_specs=[
            pl.BlockSpec(memory_space=pl.ANY),
            pl.BlockSpec((group * ts, LANES), row),
            pl.BlockSpec((group * ts, D_MODEL), row),
            pl.BlockSpec((1, D_MODEL), const),
            pl.BlockSpec((1, D_MODEL), const),
        ],
        out_specs=pl.BlockSpec((group * ts, D_MODEL), row),
        scratch_shapes=[
            pltpu.VMEM((2 * group, nl, D_MODEL), BF16),
            pltpu.SemaphoreType.DMA((2,)),
        ],
    )
    return pl.pallas_call(
        _combine_kernel,
        grid_spec=grid_spec,
        out_shape=jax.ShapeDtypeStruct((t, D_MODEL), F32),
        compiler_params=pltpu.CompilerParams(
            dimension_semantics=("arbitrary",), vmem_limit_bytes=VMEM_LIMIT_BYTES),
        name="moe_combine_ln",
    )(plan["chunks"], plan["loc_off"], plan["base"], plan["step_chunks"], y, posw, h, ln_g, ln_b)


def _pad_lanes(a, n=LANES):
    return jnp.pad(a, ((0, 0), (0, n - a.shape[1])))


def _layer(h2d, batch, seq, w_in, b_in, conv_w, conv_b, a_log, d_skip, ssd_norm_g, w_out,
           ln1_g, ln1_b, rg_w, rg_b, re_w, re_b, w_gate, w_up, w_down, ln2_g, ln2_b):
    o_dt = SSD_INNER + SSD_CONV_DIM
    o_q = o_dt + SSD_HEADS
    o_f = o_q + 3 * ATT_INNER
    w_main = jnp.concatenate([w_in[:, :o_dt], w_in[:, o_q:o_f]], axis=1).astype(BF16)
    b_main = jnp.concatenate([b_in[:o_dt], b_in[o_q:o_f]])[None, :]
    w_dt, w_f = w_in[:, o_dt:o_q], w_in[:, o_f:]
    w_small = _pad_lanes(jnp.concatenate([w_dt, w_f, w_dt], axis=1)).astype(BF16)
    b_small = _pad_lanes(jnp.concatenate([b_in[o_dt:o_q], b_in[o_f:], b_in[o_dt:o_q]])[None, :])
    a_neg = -jnp.exp(a_log.astype(F32))
    a_row = _pad_lanes(a_neg[None, :])
    a_exp = jnp.repeat(a_neg, SSD_HEAD_DIM)[None, :]
    dskip_exp = jnp.repeat(d_skip.astype(F32), SSD_HEAD_DIM)[None, :]

    z, xbc, q_t, k, v_t, gate, qf_t, kf = _in_proj(h2d, w_main, b_main, w_small, b_small, a_row, seq)
    y_ssd = _ssd(xbc, z, gate, conv_w, conv_b[None, :], a_exp, dskip_exp, ssd_norm_g[None, :],
                 batch, seq)
    y_att = _attention(q_t, k, v_t, qf_t, kf, batch, seq)

    w_r = _pad_lanes(jnp.concatenate(
        [rg_w, jnp.transpose(re_w, (1, 0, 2)).reshape(D_MODEL, N_EXPERTS)], axis=1))
    b_r = _pad_lanes(jnp.concatenate([rg_b, re_b.reshape(N_EXPERTS)])[None, :])
    wr_hi = w_r.astype(BF16)
    wr_lo = (w_r - wr_hi.astype(F32)).astype(BF16)
    wo = w_out.astype(BF16)
    ts = min(MOE_SORT_TILE, h2d.shape[0])
    h1, h1b, comb, sel, cnt = _out_proj(y_ssd, y_att, h2d, wo[:SSD_INNER], wo[SSD_INNER:],
                                        ln1_g[None, :], ln1_b[None, :],
                                        jnp.concatenate([wr_hi, wr_lo], axis=1), b_r, ts)

    plan = _dispatch_plan(
        cnt[:, 0, ROUTE_E_LANE:ROUTE_E_LANE + N_EXPERTS].astype(jnp.int32), ts)
    xs, posw = _dispatch(plan, h1b, sel, comb, ts)
    y = _experts(plan, xs, w_gate, w_up, w_down)
    return _combine(plan, y, posw, h1, ln2_g[None, :], ln2_b[None, :], ts)


def kernel(x, w_in, b_in, conv_w, conv_b, a_log, d_skip, ssd_norm_g, w_out, ln1_g, ln1_b,
           router_group_w, router_group_b, router_expert_w, router_expert_b, w_gate, w_up,
           w_down, ln2_g, ln2_b):
    batch, seq, d = x.shape
    h = x.reshape(batch * seq, d)
    for l in range(w_in.shape[0]):
        h = _layer(h, batch, seq, w_in[l], b_in[l], conv_w[l], conv_b[l], a_log[l], d_skip[l],
                   ssd_norm_g[l], w_out[l], ln1_g[l], ln1_b[l], router_group_w[l],
                   router_group_b[l], router_expert_w[l], router_expert_b[l], w_gate[l],
                   w_up[l], w_down[l], ln2_g[l], ln2_b[l])
    return h.reshape(batch, seq, d)
```

```python
import functools

import jax
import jax.numpy as jnp
import numpy as np
from jax import lax
from jax.experimental import pallas as pl
from jax.experimental.pallas import tpu as pltpu

F32 = jnp.float32
BF16 = jnp.bfloat16

D_MODEL = 1024
SSD_HEADS = 8
SSD_HEAD_DIM = 64
SSD_INNER = SSD_HEADS * SSD_HEAD_DIM
SSD_GROUPS = 2
SSD_STATE = 128
CONV_WIDTH = 4
SSD_CONV_DIM = SSD_INNER + 2 * SSD_GROUPS * SSD_STATE
ATT_HEADS = 8
ATT_HEAD_DIM = 64
ATT_INNER = ATT_HEADS * ATT_HEAD_DIM
N_EXPERT_GROUPS = 4
EXPERTS_PER_GROUP = 4
N_EXPERTS = N_EXPERT_GROUPS * EXPERTS_PER_GROUP
D_FF_EXPERT = 512
DEPTH = 1
DEEPNORM_ALPHA = (2.0 * DEPTH) ** 0.25
LN_EPS = 1e-5
RMS_EPS = 1e-5

LANES = 128
SUBLANES = 8
VMEM_LIMIT_BYTES = 56 * 1024 * 1024

GATE_A_LANE = 0
GATE_F_LANE = SSD_HEADS
GATE_DT_LANE = 2 * SSD_HEADS
ROUTE_E_LANE = N_EXPERT_GROUPS
MOE_SORT_TILE = 256
MOE_GRANULE = 16
MOE_ROW_TILE = 512
MOE_TILES_PER_STEP = 2
MOE_WAIT_BATCH = 8

IN_PROJ_ROWS = 1024
CUMSUM_BLOCK = 256
FB_LANES = 6
ATT_BLOCK = 256
ATT_PAIRS_PER_STEP = 4

SSD_CHUNK = 64
NEG_BIG = -1e30
LOG2E = 1.4426950408889634


def _split3(v):
    hi = v.astype(BF16)
    r1 = v - hi.astype(F32)
    mid = r1.astype(BF16)
    lo = (r1 - mid.astype(F32)).astype(BF16)
    return hi, mid, lo


def _silu(v):
    half = 0.5 * v
    return half * (1.0 + jnp.tanh(half))


def _softplus(v):
    return jnp.maximum(v, 0.0) + jnp.log1p(jnp.exp(-jnp.abs(v)))


def _in_proj_kernel(x_ref, wm_ref, bm_ref, ws_ref, bs_ref, arow_ref, tri_ref, place_ref, ones_ref,
                    z_ref, xbc_ref, qt_ref, k_ref, vt_ref, gate_ref, qft_ref, kf_ref,
                    carry_ref, *, tiles_per_seq):
    i = pl.program_id(0)
    tm = x_ref.shape[0]
    tb = qt_ref.shape[2]
    xb = x_ref[...].astype(BF16)

    def put_transposed(ref, val):
        for c in range(tm // tb):
            ref[c] = val[c * tb:(c + 1) * tb, :].T.astype(ref.dtype)

    s = jnp.dot(xb, ws_ref[...], preferred_element_type=F32) + bs_ref[...]
    lane = lax.broadcasted_iota(jnp.int32, (1, LANES), 1)
    sp = _softplus(s)
    log_f = -_softplus(-s)
    val = jnp.where(lane < GATE_F_LANE, sp * arow_ref[...],
                    jnp.where(lane < GATE_DT_LANE, log_f,
                              jnp.where(lane < GATE_DT_LANE + SSD_HEADS, sp, 0.0)))
    carry = jnp.where(i % tiles_per_seq == 0, 0.0, carry_ref[...])
    tri = tri_ref[...]
    cb = tri.shape[0]
    sums = []
    for c in range(tm // cb):
        hi, mid, lo = _split3(val[c * cb:(c + 1) * cb, :])
        part = (jnp.dot(tri, hi, preferred_element_type=F32)
                + jnp.dot(tri, mid, preferred_element_type=F32)
                + jnp.dot(tri, lo, preferred_element_type=F32)) + carry
        carry = part[cb - 1:cb, :]
        sums.append(part)
    cs = jnp.concatenate(sums, axis=0)
    carry_ref[...] = carry
    out = jnp.where(lane < GATE_DT_LANE, cs, val)
    gate_ref[...] = out
    pieces = jnp.concatenate(_split3(cs * LOG2E), axis=1)
    fb = jnp.dot(pieces, place_ref[...], preferred_element_type=F32) + ones_ref[...]
    put_transposed(qft_ref, fb[:, :LANES])
    kf_ref[...] = fb[:, LANES:].astype(BF16)

    col = 0
    for ref, scale in ((z_ref, None), (xbc_ref, None), (qt_ref, ATT_HEAD_DIM ** -0.5 * LOG2E),
                       (k_ref, None), (vt_ref, None)):
        transposed = ref is qt_ref or ref is vt_ref
        n = ATT_INNER if transposed else ref.shape[1]
        acc = jnp.dot(xb, wm_ref[:, col:col + n], preferred_element_type=F32) + bm_ref[:, col:col + n]
        if scale is not None:
            acc = acc * scale
        if transposed:
            put_transposed(ref, acc)
        else:
            ref[...] = acc.astype(ref.dtype)
        col += n


def _forget_bias_placement():
    place = np.zeros((3 * LANES, 2 * LANES), np.float32)
    ones = np.zeros((1, 2 * LANES), np.float32)
    for h in range(ATT_HEADS):
        for piece in range(3):
            src = piece * LANES + GATE_F_LANE + h
            place[src, FB_LANES * h + piece] = 1.0
            place[src, LANES + FB_LANES * h + 3 + piece] = -1.0
            ones[0, FB_LANES * h + 3 + piece] = 1.0
            ones[0, LANES + FB_LANES * h + piece] = 1.0
    return jnp.asarray(place, BF16), jnp.asarray(ones, F32)


def _in_proj(x2d, w_main, b_main, w_small, b_small, a_row, seq):
    t = x2d.shape[0]
    tm = min(IN_PROJ_ROWS, seq)
    tk = min(ATT_BLOCK, seq)
    cum = min(CUMSUM_BLOCK, tm)
    tri = jnp.tri(cum, dtype=BF16)
    place, ones = _forget_bias_placement()
    n_main = w_main.shape[1]
    row = lambda i: (i, 0)
    const = lambda i: (0, 0)
    outs = (
        jax.ShapeDtypeStruct((t, SSD_INNER), BF16),
        jax.ShapeDtypeStruct((t, SSD_CONV_DIM), BF16),
        jax.ShapeDtypeStruct((t // tk, ATT_INNER, tk), BF16),
        jax.ShapeDtypeStruct((t, ATT_INNER), BF16),
        jax.ShapeDtypeStruct((t // tk, ATT_INNER, tk), BF16),
        jax.ShapeDtypeStruct((t, LANES), F32),
        jax.ShapeDtypeStruct((t // tk, LANES, tk), BF16),
        jax.ShapeDtypeStruct((t, LANES), BF16),
    )
    return pl.pallas_call(
        functools.partial(_in_proj_kernel, tiles_per_seq=seq // tm),
        grid=(t // tm,),
        in_specs=[
            pl.BlockSpec((tm, D_MODEL), row),
            pl.BlockSpec((D_MODEL, n_main), const),
            pl.BlockSpec((1, n_main), const),
            pl.BlockSpec((D_MODEL, LANES), const),
            pl.BlockSpec((1, LANES), const),
            pl.BlockSpec((1, LANES), const),
            pl.BlockSpec((cum, cum), const),
            pl.BlockSpec((3 * LANES, 2 * LANES), const),
            pl.BlockSpec((1, 2 * LANES), const),
        ],
        out_specs=[
            pl.BlockSpec((tm, SSD_INNER), row),
            pl.BlockSpec((tm, SSD_CONV_DIM), row),
            pl.BlockSpec((tm // tk, ATT_INNER, tk), lambda i: (i, 0, 0)),
            pl.BlockSpec((tm, ATT_INNER), row),
            pl.BlockSpec((tm // tk, ATT_INNER, tk), lambda i: (i, 0, 0)),
            pl.BlockSpec((tm, LANES), row),
            pl.BlockSpec((tm // tk, LANES, tk), lambda i: (i, 0, 0)),
            pl.BlockSpec((tm, LANES), row),
        ],
        out_shape=outs,
        scratch_shapes=[pltpu.VMEM((1, LANES), F32)],
        compiler_params=pltpu.CompilerParams(
            dimension_semantics=("arbitrary",), vmem_limit_bytes=VMEM_LIMIT_BYTES),
        name="in_proj",
    )(x2d, w_main, b_main, w_small, b_small, a_row, tri, place, ones)


def _expand_heads(gate, lane0, rows):
    lane = lax.broadcasted_iota(jnp.int32, (1, LANES), 1)
    pieces = []
    for p in range(SSD_HEADS // 2):
        a = jnp.broadcast_to(gate[:, lane0 + 2 * p:lane0 + 2 * p + 1], (rows, LANES))
        b = jnp.broadcast_to(gate[:, lane0 + 2 * p + 1:lane0 + 2 * p + 2], (rows, LANES))
        pieces.append(jnp.where(lane < SSD_HEAD_DIM, a, b))
    return jnp.concatenate(pieces, axis=1)


def _ssd_kernel(xbc_ref, z_ref, gate_ref, cw_ref, cb_ref, aexp_ref, dskip_ref, ng_ref, shift_ref,
                y_ref, ext_ref, state_ref):
    j = pl.program_id(1)
    rows = xbc_ref.shape[0]
    q = SSD_CHUNK
    gw = SSD_INNER // SSD_GROUPS
    hpg = SSD_HEADS // SSD_GROUPS
    pad = SUBLANES

    @pl.when(j == 0)
    def _():
        ext_ref[0:pad, :] = jnp.zeros((pad, SSD_CONV_DIM), F32)
        state_ref[...] = jnp.zeros_like(state_ref)

    xb = xbc_ref[...]
    xf = xb.astype(F32)
    conv = cb_ref[...] + xf * cw_ref[CONV_WIDTH - 1:CONV_WIDTH, :]
    for s in range(1, CONV_WIDTH):
        shifted = jnp.dot(shift_ref[s - 1], xb, preferred_element_type=F32)
        conv = conv + shifted * cw_ref[CONV_WIDTH - 1 - s:CONV_WIDTH - s, :]
    ext_ref[pad:2 * pad, :] = xf[0:pad, :]
    head = cb_ref[...]
    for kk in range(CONV_WIDTH):
        off = pad - (CONV_WIDTH - 1) + kk
        head = head + ext_ref[off:off + pad, :] * cw_ref[kk:kk + 1, :]
    ext_ref[0:pad, :] = xf[rows - pad:rows, :]
    u = _silu(jnp.concatenate([head, conv[pad:, :]], axis=0))
    xs = u[:, :SSD_INNER]
    b_all = u[:, SSD_INNER:SSD_INNER + SSD_GROUPS * SSD_STATE].astype(BF16)
    c_all = u[:, SSD_INNER + SSD_GROUPS * SSD_STATE:].astype(BF16)

    gate = gate_ref[...]
    dt_e = _expand_heads(gate, GATE_DT_LANE, rows)
    a_e = _expand_heads(gate, GATE_A_LANE, rows)
    xdt = xs * dt_e
    da_e = dt_e * aexp_ref[...]

    r_i = lax.broadcasted_iota(jnp.int32, (q, gw), 0)
    c_i = lax.broadcasted_iota(jnp.int32, (q, gw), 1)
    s_i = c_i % q
    eye_t = s_i == r_i
    tril_t = s_i <= r_i
    rb = lax.broadcasted_iota(jnp.int32, (hpg * q, gw), 0) // q
    cbk = lax.broadcasted_iota(jnp.int32, (hpg * q, gw), 1) // SSD_HEAD_DIM
    blk = rb == cbk

    ys = []
    for c in range(rows // q):
        r0 = c * q
        a_c = a_e[r0:r0 + q, :]
        a_prev = a_c[0:1, :] - da_e[r0:r0 + 1, :]
        a_end = a_c[q - 1:q, :]
        y_groups = []
        for g in range(SSD_GROUPS):
            l0 = g * gw
            a_cg = a_c[:, l0:l0 + gw]
            diag = jnp.sum(jnp.where(eye_t, a_cg, 0.0), axis=0, keepdims=True)
            seg = jnp.minimum(a_cg - diag, 0.0)
            lcat = jnp.where(tril_t, jnp.exp(seg), 0.0)
            bg = b_all[r0:r0 + q, g * SSD_STATE:(g + 1) * SSD_STATE]
            cg = c_all[r0:r0 + q, g * SSD_STATE:(g + 1) * SSD_STATE]
            b_tile = jnp.concatenate([bg] * hpg, axis=0)
            gcat = lax.dot_general(cg, b_tile, (((1,), (1,)), ((), ())),
                                   preferred_element_type=F32)
            xg = xdt[r0:r0 + q, l0:l0 + gw]
            xg_b = xg.astype(BF16)
            xbd = jnp.where(blk, jnp.concatenate([xg_b] * hpg, axis=0), jnp.zeros((), BF16))
            y_diag = jnp.dot((gcat * lcat).astype(BF16), xbd, preferred_element_type=F32)
            st = state_ref[g]
            y_off = jnp.dot(cg, st.astype(BF16), preferred_element_type=F32) \
                * jnp.exp(a_cg - a_prev[:, l0:l0 + gw])
            dte = jnp.exp(a_end[:, l0:l0 + gw] - a_cg)
            upd = lax.dot_general(bg, (xg * dte).astype(BF16), (((0,), (0,)), ((), ())),
                                  preferred_element_type=F32)
            state_ref[g] = st * jnp.exp(a_end[:, l0:l0 + gw] - a_prev[:, l0:l0 + gw]) + upd
            y_groups.append(y_diag + y_off)
        ys.append(jnp.concatenate(y_groups, axis=1))
    y = jnp.concatenate(ys, axis=0) + xs * dskip_ref[...]
    y = y * _silu(z_ref[...].astype(F32))
    outs = []
    for g in range(SSD_GROUPS):
        yg = y[:, g * gw:(g + 1) * gw]
        ms = jnp.mean(yg * yg, axis=1, keepdims=True)
        outs.append(yg * lax.rsqrt(ms + RMS_EPS))
    y_ref[...] = (jnp.concatenate(outs, axis=1) * ng_ref[...]).astype(y_ref.dtype)


def _ssd(xbc, z, gate, conv_w, conv_b, a_exp, dskip_exp, norm_g, batch, seq):
    t = xbc.shape[0]
    rows = min(256, seq)
    nj = seq // rows
    row = lambda b, j: (b * nj + j, 0)
    const = lambda b, j: (0, 0)
    shifts = jnp.stack([jnp.eye(rows, k=-s, dtype=BF16) for s in range(1, CONV_WIDTH)])
    return pl.pallas_call(
        _ssd_kernel,
        grid=(batch, nj),
        in_specs=[
            pl.BlockSpec((rows, SSD_CONV_DIM), row),
            pl.BlockSpec((rows, SSD_INNER), row),
            pl.BlockSpec((rows, LANES), row),
            pl.BlockSpec((CONV_WIDTH, SSD_CONV_DIM), const),
            pl.BlockSpec((1, SSD_CONV_DIM), const),
            pl.BlockSpec((1, SSD_INNER), const),
            pl.BlockSpec((1, SSD_INNER), const),
            pl.BlockSpec((1, SSD_INNER), const),
            pl.BlockSpec((CONV_WIDTH - 1, rows, rows), lambda b, j: (0, 0, 0)),
        ],
        out_specs=pl.BlockSpec((rows, SSD_INNER), row),
        out_shape=jax.ShapeDtypeStruct((t, SSD_INNER), BF16),
        scratch_shapes=[
            pltpu.VMEM((2 * SUBLANES, SSD_CONV_DIM), F32),
            pltpu.VMEM((SSD_GROUPS, SSD_STATE, SSD_INNER // SSD_GROUPS), F32),
        ],
        compiler_params=pltpu.CompilerParams(
            dimension_semantics=("arbitrary", "arbitrary"), vmem_limit_bytes=VMEM_LIMIT_BYTES),
        name="ssd",
    )(xbc, z, gate, conv_w, conv_b, a_exp, dskip_exp, norm_g, shifts)


def _attn_kernel(qt_ref, k_ref, vt_ref, qft_ref, kf_ref, o_ref, *scratch, tb, pairs):
    pair0 = pl.program_id(1) * pairs
    nb = qt_ref.shape[0]
    s_refs, acc_refs = scratch[:pairs], scratch[pairs:]
    feat = lax.broadcasted_iota(jnp.int32, (LANES, 1), 0)
    in_head = (feat < ATT_HEAD_DIM, feat >= ATT_HEAD_DIM)
    ones = jnp.ones((2 * SUBLANES, tb), BF16)
    key = lax.broadcasted_iota(jnp.int32, (tb, 2 * tb), 0)
    qry = lax.broadcasted_iota(jnp.int32, (tb, 2 * tb), 1)
    causal = jnp.where(qry >= tb, qry - tb, qry) >= key
    zero = jnp.zeros((), BF16)

    def logits_t(qa_t, pp, jk):
        k0 = pl.multiple_of(jk * tb, tb)
        ka = jnp.concatenate([k_ref[pl.ds(k0, tb), pp * LANES:(pp + 1) * LANES],
                              kf_ref[pl.ds(k0, tb), :]], axis=1)
        return jnp.dot(ka, qa_t, preferred_element_type=F32)

    def update(s_t, m, pp, jk):
        m_new = jnp.maximum(m, jnp.max(s_t, axis=0, keepdims=True))
        alpha = jnp.exp2(m - m_new)
        p_t = jnp.exp2(s_t - m_new).astype(BF16)
        for hh in range(2):
            r0 = pp * LANES + hh * ATT_HEAD_DIM
            va_t = jnp.concatenate([vt_ref[jk, r0:r0 + ATT_HEAD_DIM, :], ones], axis=0)
            cols = slice(hh * tb, (hh + 1) * tb)
            acc_refs[pp][hh] = alpha[:, cols] * acc_refs[pp][hh] + jnp.dot(
                va_t, p_t[:, cols], preferred_element_type=F32)
        return m_new

    def q_block(i, _):
        q0 = pl.multiple_of(i * tb, tb)
        qf_t = qft_ref[i]
        qas = []
        for pp in range(pairs):
            q_t = qt_ref[i, pp * LANES:(pp + 1) * LANES, :]
            cols = []
            for hh in range(2):
                fb0 = FB_LANES * (2 * (pair0 + pp) + hh)
                in_fb = (feat >= fb0) & (feat < fb0 + FB_LANES)
                cols.append(jnp.concatenate(
                    [jnp.where(in_head[hh], q_t, zero), jnp.where(in_fb, qf_t, zero)], axis=0))
            qas.append(jnp.concatenate(cols, axis=1))

        for pp in range(pairs):
            s_refs[pp][0] = logits_t(qas[pp], pp, 0)
            acc_refs[pp][...] = jnp.zeros(acc_refs[pp].shape, F32)

        def body(jk, ms):
            cur = jk % 2
            new = []
            for pp in range(pairs):
                s_t = s_refs[pp][cur]
                s_refs[pp][1 - cur] = logits_t(qas[pp], pp, jk + 1)
                new.append(update(s_t, ms[pp], pp, jk))
            return tuple(new)

        ms = lax.fori_loop(0, i, body,
                           tuple(jnp.full((1, 2 * tb), NEG_BIG, F32) for _ in range(pairs)))
        for pp in range(pairs):
            update(jnp.where(causal, s_refs[pp][i % 2], NEG_BIG), ms[pp], pp, i)
            heads = []
            for hh in range(2):
                acc_t = acc_refs[pp][hh]
                inv = 1.0 / acc_t[ATT_HEAD_DIM:ATT_HEAD_DIM + 1, :]
                heads.append(acc_t[:ATT_HEAD_DIM, :] * inv)
            out_t = jnp.concatenate(heads, axis=0)
            o_ref[pl.ds(q0, tb), pp * LANES:(pp + 1) * LANES] = out_t.T.astype(o_ref.dtype)
        return 0

    lax.fori_loop(0, nb, q_block, 0)


def _attention(q_t, k, v_t, qf_t, kf, batch, seq):
    t = k.shape[0]
    tb = min(ATT_BLOCK, seq)
    nb = seq // tb
    pairs = ATT_PAIRS_PER_STEP
    width = pairs * LANES
    return pl.pallas_call(
        functools.partial(_attn_kernel, tb=tb, pairs=pairs),
        grid=(batch, ATT_HEADS // (2 * pairs)),
        in_specs=[
            pl.BlockSpec((nb, width, tb), lambda b, p: (b, p, 0)),
            pl.BlockSpec((seq, width), lambda b, p: (b, p)),
            pl.BlockSpec((nb, width, tb), lambda b, p: (b, p, 0)),
            pl.BlockSpec((nb, LANES, tb), lambda b, p: (b, 0, 0)),
            pl.BlockSpec((seq, LANES), lambda b, p: (b, 0)),
        ],
        out_specs=pl.BlockSpec((seq, width), lambda b, p: (b, p)),
        out_shape=jax.ShapeDtypeStruct((t, ATT_INNER), BF16),
        scratch_shapes=([pltpu.VMEM((2, tb, 2 * tb), F32) for _ in range(pairs)]
                        + [pltpu.VMEM((2, ATT_HEAD_DIM + 2 * SUBLANES, tb), F32)
                           for _ in range(pairs)]),
        compiler_params=pltpu.CompilerParams(
            dimension_semantics=("arbitrary", "arbitrary"), vmem_limit_bytes=VMEM_LIMIT_BYTES),
        name="fox_attention",
    )(q_t, k, v_t, qf_t, kf)


def _layer_norm(u, g, b):
    mu = jnp.mean(u, axis=1, keepdims=True)
    d = u - mu
    var = jnp.mean(d * d, axis=1, keepdims=True)
    return d * lax.rsqrt(var + LN_EPS) * g + b


def _route(lg):
    lane = lax.broadcasted_iota(jnp.int32, (1, LANES), 1)
    lane_f = lane.astype(F32)
    is_g = lane < N_EXPERT_GROUPS
    gl = jnp.where(is_g, lg, NEG_BIG)
    gmax = jnp.max(gl, axis=1, keepdims=True)
    gsum = jnp.sum(jnp.where(is_g, jnp.exp(gl - gmax), 0.0), axis=1, keepdims=True)
    g_val = 1.0 / gsum
    g_idx = jnp.min(jnp.where(is_g & (gl == gmax), lane_f, float(LANES)), axis=1, keepdims=True)
    e_grp = ((lane - ROUTE_E_LANE) // EXPERTS_PER_GROUP).astype(F32)
    in_grp = (lane >= ROUTE_E_LANE) & (lane < ROUTE_E_LANE + N_EXPERTS) & (e_grp == g_idx)
    el = jnp.where(in_grp, lg, NEG_BIG)
    m1 = jnp.max(el, axis=1, keepdims=True)
    i1 = jnp.min(jnp.where(in_grp & (el == m1), lane_f, float(LANES)), axis=1, keepdims=True)
    rest = in_grp & (lane_f != i1)
    el2 = jnp.where(rest, lg, NEG_BIG)
    m2 = jnp.max(el2, axis=1, keepdims=True)
    i2 = jnp.min(jnp.where(rest & (el2 == m2), lane_f, float(LANES)), axis=1, keepdims=True)
    r = jnp.exp(m2 - m1)
    w1 = 1.0 / (1.0 + r)
    w2 = r * w1
    comb = jnp.where(lane_f == i1, g_val * w1, jnp.where(lane_f == i2, g_val * w2, 0.0))
    sel = jnp.where((lane_f == i1) | (lane_f == i2), 1.0, 0.0)
    return comb, sel


def _out_proj_kernel(ys_ref, ya_ref, x_ref, wa_ref, wb_ref, g_ref, b_ref, wr_ref, br_ref,
                     h_ref, hb_ref, comb_ref, sel_ref, cnt_ref):
    mix = jnp.dot(ys_ref[...], wa_ref[...], preferred_element_type=F32)
    mix = mix + jnp.dot(ya_ref[...], wb_ref[...], preferred_element_type=F32)
    h = _layer_norm(DEEPNORM_ALPHA * x_ref[...] + mix, g_ref[...], b_ref[...])
    h_ref[...] = h
    h_hi = h.astype(BF16)
    hb_ref[...] = h_hi
    h_lo = (h - h_hi.astype(F32)).astype(BF16)
    w_hi = wr_ref[:, :LANES]
    lg = (jnp.dot(h_hi, w_hi, preferred_element_type=F32)
          + jnp.dot(h_lo, w_hi, preferred_element_type=F32)
          + jnp.dot(h_hi, wr_ref[:, LANES:], preferred_element_type=F32)) + br_ref[...]
    comb, sel = _route(lg)
    comb_ref[...] = comb
    sel_ref[...] = sel
    ts = h.shape[0] // cnt_ref.shape[0]
    for c in range(cnt_ref.shape[0]):
        cnt = jnp.sum(sel[c * ts:(c + 1) * ts, :], axis=0, keepdims=True)
        cnt_ref[c] = jnp.broadcast_to(cnt, (SUBLANES, LANES))


def _out_proj(ys, ya, x2d, wo_a, wo_b, ln_g, ln_b, wr, br, ts):
    t = x2d.shape[0]
    tm = min(512, t)
    row = lambda i: (i, 0)
    const = lambda i: (0, 0)
    return pl.pallas_call(
        _out_proj_kernel,
        grid=(t // tm,),
        in_specs=[
            pl.BlockSpec((tm, SSD_INNER), row),
            pl.BlockSpec((tm, ATT_INNER), row),
            pl.BlockSpec((tm, D_MODEL), row),
            pl.BlockSpec((SSD_INNER, D_MODEL), const),
            pl.BlockSpec((ATT_INNER, D_MODEL), const),
            pl.BlockSpec((1, D_MODEL), const),
            pl.BlockSpec((1, D_MODEL), const),
            pl.BlockSpec((D_MODEL, 2 * LANES), const),
            pl.BlockSpec((1, LANES), const),
        ],
        out_specs=[
            pl.BlockSpec((tm, D_MODEL), row),
            pl.BlockSpec((tm, D_MODEL), row),
            pl.BlockSpec((tm, LANES), row),
            pl.BlockSpec((tm, LANES), row),
            pl.BlockSpec((tm // ts, SUBLANES, LANES), lambda i: (i, 0, 0)),
        ],
        out_shape=(
            jax.ShapeDtypeStruct((t, D_MODEL), F32),
            jax.ShapeDtypeStruct((t, D_MODEL), BF16),
            jax.ShapeDtypeStruct((t, LANES), F32),
            jax.ShapeDtypeStruct((t, LANES), F32),
            jax.ShapeDtypeStruct((t // ts, SUBLANES, LANES), F32),
        ),
        compiler_params=pltpu.CompilerParams(
            dimension_semantics=("arbitrary",), vmem_limit_bytes=VMEM_LIMIT_BYTES),
        name="out_proj_ln_router",
    )(ys, ya, x2d, wo_a, wo_b, ln_g, ln_b, wr, br)


def _dispatch_plan(cnt, ts):
    nt = cnt.shape[0]
    c16 = (cnt + (MOE_GRANULE - 1)) // MOE_GRANULE * MOE_GRANULE
    loc_off = jnp.cumsum(c16, axis=1) - c16
    tot = jnp.sum(c16, axis=0)
    tiles = (tot + (MOE_ROW_TILE - 1)) // MOE_ROW_TILE
    cum_tiles = jnp.cumsum(tiles)
    start = (cum_tiles - tiles) * MOE_ROW_TILE
    base = start[None, :] + jnp.cumsum(c16, axis=0) - c16
    max_tiles = _moe_rows(nt * ts, ts) // MOE_ROW_TILE
    tile_expert = jnp.minimum(
        jnp.sum(jnp.arange(max_tiles, dtype=jnp.int32)[:, None] >= cum_tiles[None, :], axis=1),
        N_EXPERTS - 1).astype(jnp.int32)
    loc_vec = jnp.zeros((nt, SUBLANES, LANES), F32).at[:, :, ROUTE_E_LANE:ROUTE_E_LANE + N_EXPERTS].set(
        jnp.broadcast_to(loc_off.astype(F32)[:, None, :], (nt, SUBLANES, N_EXPERTS)))
    return dict(
        chunks=(c16 // MOE_GRANULE).reshape(-1).astype(jnp.int32),
        loc_off=loc_off.reshape(-1).astype(jnp.int32),
        base=base.reshape(-1).astype(jnp.int32),
        step_chunks=(jnp.sum(c16, axis=1).reshape(-1, _moe_group(nt * ts, ts)).sum(axis=1)
                     // MOE_GRANULE).astype(jnp.int32),
        pad_start=(start + tot).astype(jnp.int32),
        pad_chunks=((tiles * MOE_ROW_TILE - tot) // MOE_GRANULE).astype(jnp.int32),
        n_tiles=cum_tiles[-1:].astype(jnp.int32),
        tile_expert=tile_expert,
        loc_vec=loc_vec,
    )


def _moe_group(t, ts):
    return MOE_TILES_PER_STEP if t % (MOE_TILES_PER_STEP * ts) == 0 else 1


def _moe_local_rows(ts):
    rows = 2 * ts + N_EXPERTS * (MOE_GRANULE - 1)
    return (rows + LANES - 1) // LANES * LANES


def _moe_rows(t, ts):
    rows = 2 * t + (t // ts) * N_EXPERTS * (MOE_GRANULE - 1) + N_EXPERTS * MOE_ROW_TILE
    return (rows + MOE_ROW_TILE - 1) // MOE_ROW_TILE * MOE_ROW_TILE


def _chunk_copy(src, dst, sem):
    return pltpu.make_async_copy(src, dst, sem)


def _wait_rows(copy_of_rows, n_chunks):
    def wait(rows):
        def one(_, c):
            copy_of_rows(rows).wait()
            return c
        return one
    shift = MOE_WAIT_BATCH.bit_length() - 1
    lax.fori_loop(0, lax.shift_right_logical(n_chunks, shift), wait(MOE_WAIT_BATCH * MOE_GRANULE), 0)
    lax.fori_loop(0, n_chunks & (MOE_WAIT_BATCH - 1), wait(MOE_GRANULE), 0)


def _dispatch_kernel(chunks_ref, loc_ref, base_ref, schunks_ref, pstart_ref, pchunks_ref,
                     hb_ref, sel_ref, comb_ref, locv_ref, lstrict_ref,
                     xs_ref, posw_ref, buf_ref, zero_ref, sem_ref, zsem_ref):
    step = pl.program_id(0)
    n_steps = pl.num_programs(0)
    par = step % 2
    group = locv_ref.shape[0]
    ts = hb_ref.shape[0] // group
    nl = buf_ref.shape[1]
    g = MOE_GRANULE

    def wait_chunks(n, pp):
        _wait_rows(lambda r: _chunk_copy(buf_ref.at[0, pl.ds(0, r)], xs_ref.at[pl.ds(0, r)],
                                         sem_ref.at[pp]), n)

    @pl.when(step >= 2)
    def _():
        wait_chunks(schunks_ref[step - 2], par)

    lane = lax.broadcasted_iota(jnp.int32, (1, LANES), 1)
    d = lax.broadcasted_iota(jnp.int32, (nl, ts), 0).astype(F32)
    for u in range(group):
        rows = slice(u * ts, (u + 1) * ts)
        sel = sel_ref[rows, :]
        comb = comb_ref[rows, :]
        picked = sel > 0.0
        rank = jnp.dot(lstrict_ref[...], sel.astype(BF16), preferred_element_type=F32)
        dest = rank + locv_ref[u][0:1, :]
        pos_a = jnp.min(jnp.where(picked, dest, 1e9), axis=1, keepdims=True)
        pos_b = jnp.max(jnp.where(picked, dest, -1.0), axis=1, keepdims=True)
        w_a = jnp.sum(jnp.where(picked & (dest == pos_a), comb, 0.0), axis=1, keepdims=True)
        w_b = jnp.sum(jnp.where(picked & (dest == pos_b), comb, 0.0), axis=1, keepdims=True)
        posw = jnp.where(lane == 0, pos_a, jnp.where(lane == 1, pos_b,
                         jnp.where(lane == 2, w_a, jnp.where(lane == 3, w_b, 0.0))))
        posw_ref[rows, :] = posw
        posw_t = posw.T
        perm = jnp.where((d == posw_t[0:1, :]) | (d == posw_t[1:2, :]), 1.0, 0.0).astype(BF16)
        buf_ref[par * group + u] = jnp.dot(perm, hb_ref[rows, :],
                                           preferred_element_type=F32).astype(BF16)

    for u in range(group):
        for e in range(N_EXPERTS):
            idx = (step * group + u) * N_EXPERTS + e
            lo = loc_ref[idx]
            bs = base_ref[idx]

            def send(k, c, lo=lo, bs=bs, u=u):
                src = buf_ref.at[par * group + u, pl.ds(pl.multiple_of(lo + k * g, g), g)]
                dst = xs_ref.at[pl.ds(pl.multiple_of(bs + k * g, g), g)]
                _chunk_copy(src, dst, sem_ref.at[par]).start()
                return c
            lax.fori_loop(0, chunks_ref[idx], send, 0)

    @pl.when(step == n_steps - 1)
    def _():
        zero_ref[...] = jnp.zeros_like(zero_ref)
        for e in range(N_EXPERTS):
            ps = pstart_ref[e]

            def send_zero(k, c, ps=ps):
                dst = xs_ref.at[pl.ds(pl.multiple_of(ps + k * g, g), g)]
                _chunk_copy(zero_ref, dst, zsem_ref.at[0]).start()
                return c
            lax.fori_loop(0, pchunks_ref[e], send_zero, 0)
        for e in range(N_EXPERTS):
            def wait_zero(_, c):
                _chunk_copy(zero_ref, xs_ref.at[pl.ds(0, g)], zsem_ref.at[0]).wait()
                return c
            lax.fori_loop(0, pchunks_ref[e], wait_zero, 0)
        wait_chunks(schunks_ref[step], par)

        @pl.when(step >= 1)
        def _():
            wait_chunks(schunks_ref[step - 1], 1 - par)


def _dispatch(plan, hb, sel, comb, ts):
    t = hb.shape[0]
    group = _moe_group(t, ts)
    nl = _moe_local_rows(ts)
    rows = _moe_rows(t, ts)
    lstrict = jnp.tri(ts, k=-1, dtype=BF16)
    row = lambda i, *_: (i, 0)
    grid_spec = pltpu.PrefetchScalarGridSpec(
        num_scalar_prefetch=6,
        grid=(t // (group * ts),),
        in_specs=[
            pl.BlockSpec((group * ts, D_MODEL), row),
            pl.BlockSpec((group * ts, LANES), row),
            pl.BlockSpec((group * ts, LANES), row),
            pl.BlockSpec((group, SUBLANES, LANES), lambda i, *_: (i, 0, 0)),
            pl.BlockSpec((ts, ts), lambda i, *_: (0, 0)),
        ],
        out_specs=[
            pl.BlockSpec(memory_space=pl.ANY),
            pl.BlockSpec((group * ts, LANES), row),
        ],
        scratch_shapes=[
            pltpu.VMEM((2 * group, nl, D_MODEL), BF16),
            pltpu.VMEM((MOE_GRANULE, D_MODEL), BF16),
            pltpu.SemaphoreType.DMA((2,)),
            pltpu.SemaphoreType.DMA((1,)),
        ],
    )
    return pl.pallas_call(
        _dispatch_kernel,
        grid_spec=grid_spec,
        out_shape=(
            jax.ShapeDtypeStruct((rows, D_MODEL), BF16),
            jax.ShapeDtypeStruct((t, LANES), F32),
        ),
        compiler_params=pltpu.CompilerParams(
            dimension_semantics=("arbitrary",), vmem_limit_bytes=VMEM_LIMIT_BYTES),
        name="moe_dispatch",
    )(plan["chunks"], plan["loc_off"], plan["base"], plan["step_chunks"], plan["pad_start"],
      plan["pad_chunks"], hb, sel, comb, plan["loc_vec"], lstrict)


def _expert_kernel(te_ref, nt_ref, x_ref, wg_ref, wu_ref, wd_ref, y_ref, wgb_ref, wub_ref, wdb_ref):
    j = pl.program_id(0)

    @pl.when(j < nt_ref[0])
    def _():
        @pl.when((j == 0) | (te_ref[j] != te_ref[jnp.maximum(j - 1, 0)]))
        def _():
            wgb_ref[...] = wg_ref[0].astype(BF16)
            wub_ref[...] = wu_ref[0].astype(BF16)
            wdb_ref[...] = wd_ref[0].astype(BF16)

        x = x_ref[...]
        gate = jnp.dot(x, wgb_ref[...], preferred_element_type=F32)
        up = jnp.dot(x, wub_ref[...], preferred_element_type=F32)
        act = (_silu(gate) * up).astype(BF16)
        y_ref[...] = jnp.dot(act, wdb_ref[...], preferred_element_type=F32).astype(y_ref.dtype)


def _experts(plan, xs, w_gate, w_up, w_down):
    rows = xs.shape[0]
    live = lambda j, te, nt: jnp.minimum(j, nt[0] - 1)
    expert = lambda j, te, nt: (te[live(j, te, nt)], 0, 0)
    grid_spec = pltpu.PrefetchScalarGridSpec(
        num_scalar_prefetch=2,
        grid=(rows // MOE_ROW_TILE,),
        in_specs=[
            pl.BlockSpec((MOE_ROW_TILE, D_MODEL), lambda j, te, nt: (live(j, te, nt), 0)),
            pl.BlockSpec((1, D_MODEL, D_FF_EXPERT), expert),
            pl.BlockSpec((1, D_MODEL, D_FF_EXPERT), expert),
            pl.BlockSpec((1, D_FF_EXPERT, D_MODEL), expert),
        ],
        out_specs=pl.BlockSpec((MOE_ROW_TILE, D_MODEL), lambda j, te, nt: (live(j, te, nt), 0)),
        scratch_shapes=[
            pltpu.VMEM((D_MODEL, D_FF_EXPERT), BF16),
            pltpu.VMEM((D_MODEL, D_FF_EXPERT), BF16),
            pltpu.VMEM((D_FF_EXPERT, D_MODEL), BF16),
        ],
    )
    return pl.pallas_call(
        _expert_kernel,
        grid_spec=grid_spec,
        out_shape=jax.ShapeDtypeStruct((rows, D_MODEL), BF16),
        compiler_params=pltpu.CompilerParams(
            dimension_semantics=("arbitrary",), vmem_limit_bytes=VMEM_LIMIT_BYTES),
        name="moe_experts",
    )(plan["tile_expert"], plan["n_tiles"], xs, w_gate, w_up, w_down)


def _combine_kernel(chunks_ref, loc_ref, base_ref, schunks_ref,
                    y_ref, posw_ref, h_ref, g_ref, b_ref, o_ref, buf_ref, sem_ref):
    step = pl.program_id(0)
    n_steps = pl.num_programs(0)
    par = step % 2
    nl = buf_ref.shape[1]
    group = buf_ref.shape[0] // 2
    ts = h_ref.shape[0] // group
    g = MOE_GRANULE

    def fetch(ss, pp):
        for u in range(group):
            buf_ref[pp * group + u] = jnp.zeros((nl, D_MODEL), BF16)
        for u in range(group):
            for e in range(N_EXPERTS):
                idx = (ss * group + u) * N_EXPERTS + e
                lo = loc_ref[idx]
                bs = base_ref[idx]

                def get(k, c, lo=lo, bs=bs, u=u):
                    src = y_ref.at[pl.ds(pl.multiple_of(bs + k * g, g), g)]
                    dst = buf_ref.at[pp * group + u, pl.ds(pl.multiple_of(lo + k * g, g), g)]
                    _chunk_copy(src, dst, sem_ref.at[pp]).start()
                    return c
                lax.fori_loop(0, chunks_ref[idx], get, 0)

    @pl.when(step == 0)
    def _():
        fetch(0, 0)

    @pl.when(step + 1 < n_steps)
    def _():
        fetch(step + 1, 1 - par)

    _wait_rows(lambda r: _chunk_copy(y_ref.at[pl.ds(0, r)], buf_ref.at[0, pl.ds(0, r)],
                                     sem_ref.at[par]), schunks_ref[step])

    d = lax.broadcasted_iota(jnp.int32, (1, nl), 1).astype(F32)
    for u in range(group):
        rows = slice(u * ts, (u + 1) * ts)
        yb = buf_ref[par * group + u]
        posw = posw_ref[rows, :]
        ffn = jnp.zeros((ts, D_MODEL), F32)
        for k in range(2):
            gather = jnp.where(posw[:, k:k + 1] == d, 1.0, 0.0).astype(BF16)
            ffn = ffn + posw[:, 2 + k:3 + k] * jnp.dot(gather, yb, preferred_element_type=F32)
        o_ref[rows, :] = _layer_norm(DEEPNORM_ALPHA * h_ref[rows, :] + ffn, g_ref[...], b_ref[...])


def _combine(plan, y, posw, h, ln_g, ln_b, ts):
    t = h.shape[0]
    group = _moe_group(t, ts)
    nl = _moe_local_rows(ts)
    row = lambda i, *_: (i, 0)
    const = lambda i, *_: (0, 0)
    grid_spec = pltpu.PrefetchScalarGridSpec(
        num_scalar_prefetch=4,
        grid=(t // (group * ts),),
        in_specs=[
            pl.BlockSpec(memory_space=pl.ANY),
            pl.BlockSpec((group * ts, LANES), row),
            pl.BlockSpec((group * ts, D_MODEL), row),
            pl.BlockSpec((1, D_MODEL), const),
            pl.BlockSpec((1, D_MODEL), const),
        ],
        out_specs=pl.BlockSpec((group * ts, D_MODEL), row),
        scratch_shapes=[
            pltpu.VMEM((2 * group, nl, D_MODEL), BF16),
            pltpu.SemaphoreType.DMA((2,)),
        ],
    )
    return pl.pallas_call(
        _combine_kernel,
        grid_spec=grid_spec,
        out_shape=jax.ShapeDtypeStruct((t, D_MODEL), F32),
        compiler_params=pltpu.CompilerParams(
            dimension_semantics=("arbitrary",), vmem_limit_bytes=VMEM_LIMIT_BYTES),
        name="moe_combine_ln",
    )(plan["chunks"], plan["loc_off"], plan["base"], plan["step_chunks"], y, posw, h, ln_g, ln_b)


def _pad_lanes(a, n=LANES):
    return jnp.pad(a, ((0, 0), (0, n - a.shape[1])))


def _layer(h2d, batch, seq, w_in, b_in, conv_w, conv_b, a_log, d_skip, ssd_norm_g, w_out,
           ln1_g, ln1_b, rg_w, rg_b, re_w, re_b, w_gate, w_up, w_down, ln2_g, ln2_b):
    o_dt = SSD_INNER + SSD_CONV_DIM
    o_q = o_dt + SSD_HEADS
    o_f = o_q + 3 * ATT_INNER
    w_main = jnp.concatenate([w_in[:, :o_dt], w_in[:, o_q:o_f]], axis=1).astype(BF16)
    b_main = jnp.concatenate([b_in[:o_dt], b_in[o_q:o_f]])[None, :]
    w_dt, w_f = w_in[:, o_dt:o_q], w_in[:, o_f:]
    w_small = _pad_lanes(jnp.concatenate([w_dt, w_f, w_dt], axis=1)).astype(BF16)
    b_small = _pad_lanes(jnp.concatenate([b_in[o_dt:o_q], b_in[o_f:], b_in[o_dt:o_q]])[None, :])
    a_neg = -jnp.exp(a_log.astype(F32))
    a_row = _pad_lanes(a_neg[None, :])
    a_exp = jnp.repeat(a_neg, SSD_HEAD_DIM)[None, :]
    dskip_exp = jnp.repeat(d_skip.astype(F32), SSD_HEAD_DIM)[None, :]

    z, xbc, q_t, k, v_t, gate, qf_t, kf = _in_proj(h2d, w_main, b_main, w_small, b_small, a_row, seq)
    y_ssd = _ssd(xbc, z, gate, conv_w, conv_b[None, :], a_exp, dskip_exp, ssd_norm_g[None, :],
                 batch, seq)
    y_att = _attention(q_t, k, v_t, qf_t, kf, batch, seq)

    w_r = _pad_lanes(jnp.concatenate(
        [rg_w, jnp.transpose(re_w, (1, 0, 2)).reshape(D_MODEL, N_EXPERTS)], axis=1))
    b_r = _pad_lanes(jnp.concatenate([rg_b, re_b.reshape(N_EXPERTS)])[None, :])
    wr_hi = w_r.astype(BF16)
    wr_lo = (w_r - wr_hi.astype(F32)).astype(BF16)
    wo = w_out.astype(BF16)
    ts = min(MOE_SORT_TILE, h2d.shape[0])
    h1, h1b, comb, sel, cnt = _out_proj(y_ssd, y_att, h2d, wo[:SSD_INNER], wo[SSD_INNER:],
                                        ln1_g[None, :], ln1_b[None, :],
                                        jnp.concatenate([wr_hi, wr_lo], axis=1), b_r, ts)

    plan = _dispatch_plan(
        cnt[:, 0, ROUTE_E_LANE:ROUTE_E_LANE + N_EXPERTS].astype(jnp.int32), ts)
    xs, posw = _dispatch(plan, h1b, sel, comb, ts)
    y = _experts(plan, xs, w_gate, w_up, w_down)
    return _combine(plan, y, posw, h1, ln2_g[None, :], ln2_b[None, :], ts)


def kernel(x, w_in, b_in, conv_w, conv_b, a_log, d_skip, ssd_norm_g, w_out, ln1_g, ln1_b,
           router_group_w, router_group_b, router_expert_w, router_expert_b, w_gate, w_up,
           w_down, ln2_g, ln2_b):
    batch, seq, d = x.shape
    h = x.reshape(batch * seq, d)
    for l in range(w_in.shape[0]):
        h = _layer(h, batch, seq, w_in[l], b_in[l], conv_w[l], conv_b[l], a_log[l], d_skip[l],
                   ssd_norm_g[l], w_out[l], ln1_g[l], ln1_b[l], router_group_w[l],
                   router_group_b[l], router_expert_w[l], router_expert_b[l], w_gate[l],
                   w_up[l], w_down[l], ln2_g[l], ln2_b[l])
    return h.reshape(batch, seq, d)
```

```python
import functools

import jax
import jax.numpy as jnp
import numpy as np
from jax import lax
from jax.experimental import pallas as pl
from jax.experimental.pallas import tpu as pltpu

F32 = jnp.float32
BF16 = jnp.bfloat16

D_MODEL = 1024
SSD_HEADS = 8
SSD_HEAD_DIM = 64
SSD_INNER = SSD_HEADS * SSD_HEAD_DIM
SSD_GROUPS = 2
SSD_STATE = 128
CONV_WIDTH = 4
SSD_CONV_DIM = SSD_INNER + 2 * SSD_GROUPS * SSD_STATE
ATT_HEADS = 8
ATT_HEAD_DIM = 64
ATT_INNER = ATT_HEADS * ATT_HEAD_DIM
N_EXPERT_GROUPS = 4
EXPERTS_PER_GROUP = 4
N_EXPERTS = N_EXPERT_GROUPS * EXPERTS_PER_GROUP
D_FF_EXPERT = 512
DEPTH = 1
DEEPNORM_ALPHA = (2.0 * DEPTH) ** 0.25
LN_EPS = 1e-5
RMS_EPS = 1e-5

LANES = 128
SUBLANES = 8
VMEM_LIMIT_BYTES = 56 * 1024 * 1024

GATE_A_LANE = 0
GATE_F_LANE = SSD_HEADS
GATE_DT_LANE = 2 * SSD_HEADS
ROUTE_E_LANE = N_EXPERT_GROUPS
MOE_SORT_TILE = 256
MOE_GRANULE = 16
MOE_ROW_TILE = 512
MOE_TILES_PER_STEP = 4
MOE_WAIT_BATCH = 8

IN_PROJ_ROWS = 1024
CUMSUM_BLOCK = 256
FB_LANES = 6
ATT_BLOCK = 256
ATT_PAIRS_PER_STEP = 4

SSD_CHUNK = 64
NEG_BIG = -1e30
LOG2E = 1.4426950408889634


def _split3(v):
    hi = v.astype(BF16)
    r1 = v - hi.astype(F32)
    mid = r1.astype(BF16)
    lo = (r1 - mid.astype(F32)).astype(BF16)
    return hi, mid, lo


def _silu(v):
    half = 0.5 * v
    return half * (1.0 + jnp.tanh(half))


def _softplus(v):
    return jnp.maximum(v, 0.0) + jnp.log1p(jnp.exp(-jnp.abs(v)))


def _in_proj_kernel(x_ref, wm_ref, bm_ref, ws_ref, bs_ref, arow_ref, tri_ref, place_ref, ones_ref,
                    z_ref, xbc_ref, qt_ref, k_ref, vt_ref, gate_ref, qft_ref, kf_ref,
                    carry_ref, *, tiles_per_seq):
    i = pl.program_id(0)
    tm = x_ref.shape[0]
    tb = qt_ref.shape[2]
    xb = x_ref[...].astype(BF16)

    def put_transposed(ref, val):
        for c in range(tm // tb):
            ref[c] = val[c * tb:(c + 1) * tb, :].T.astype(ref.dtype)

    s = jnp.dot(xb, ws_ref[...], preferred_element_type=F32) + bs_ref[...]
    lane = lax.broadcasted_iota(jnp.int32, (1, LANES), 1)
    sp = _softplus(s)
    log_f = -_softplus(-s)
    val = jnp.where(lane < GATE_F_LANE, sp * arow_ref[...],
                    jnp.where(lane < GATE_DT_LANE, log_f,
                              jnp.where(lane < GATE_DT_LANE + SSD_HEADS, sp, 0.0)))
    carry = jnp.where(i % tiles_per_seq == 0, 0.0, carry_ref[...])
    tri = tri_ref[...]
    cb = tri.shape[0]
    sums = []
    for c in range(tm // cb):
        hi, mid, lo = _split3(val[c * cb:(c + 1) * cb, :])
        part = (jnp.dot(tri, hi, preferred_element_type=F32)
                + jnp.dot(tri, mid, preferred_element_type=F32)
                + jnp.dot(tri, lo, preferred_element_type=F32)) + carry
        carry = part[cb - 1:cb, :]
        sums.append(part)
    cs = jnp.concatenate(sums, axis=0)
    carry_ref[...] = carry
    out = jnp.where(lane < GATE_DT_LANE, cs, val)
    gate_ref[...] = out
    pieces = jnp.concatenate(_split3(cs * LOG2E), axis=1)
    fb = jnp.dot(pieces, place_ref[...], preferred_element_type=F32) + ones_ref[...]
    put_transposed(qft_ref, fb[:, :LANES])
    kf_ref[...] = fb[:, LANES:].astype(BF16)

    col = 0
    for ref, scale in ((z_ref, None), (xbc_ref, None), (qt_ref, ATT_HEAD_DIM ** -0.5 * LOG2E),
                       (k_ref, None), (vt_ref, None)):
        transposed = ref is qt_ref or ref is vt_ref
        n = ATT_INNER if transposed else ref.shape[1]
        acc = jnp.dot(xb, wm_ref[:, col:col + n], preferred_element_type=F32) + bm_ref[:, col:col + n]
        if scale is not None:
            acc = acc * scale
        if transposed:
            put_transposed(ref, acc)
        else:
            ref[...] = acc.astype(ref.dtype)
        col += n


def _forget_bias_placement():
    place = np.zeros((3 * LANES, 2 * LANES), np.float32)
    ones = np.zeros((1, 2 * LANES), np.float32)
    for h in range(ATT_HEADS):
        for piece in range(3):
            src = piece * LANES + GATE_F_LANE + h
            place[src, FB_LANES * h + piece] = 1.0
            place[src, LANES + FB_LANES * h + 3 + piece] = -1.0
            ones[0, FB_LANES * h + 3 + piece] = 1.0
            ones[0, LANES + FB_LANES * h + piece] = 1.0
    return jnp.asarray(place, BF16), jnp.asarray(ones, F32)


def _in_proj(x2d, w_main, b_main, w_small, b_small, a_row, seq):
    t = x2d.shape[0]
    tm = min(IN_PROJ_ROWS, seq)
    tk = min(ATT_BLOCK, seq)
    cum = min(CUMSUM_BLOCK, tm)
    tri = jnp.tri(cum, dtype=BF16)
    place, ones = _forget_bias_placement()
    n_main = w_main.shape[1]
    row = lambda i: (i, 0)
    const = lambda i: (0, 0)
    outs = (
        jax.ShapeDtypeStruct((t, SSD_INNER), BF16),
        jax.ShapeDtypeStruct((t, SSD_CONV_DIM), BF16),
        jax.ShapeDtypeStruct((t // tk, ATT_INNER, tk), BF16),
        jax.ShapeDtypeStruct((t, ATT_INNER), BF16),
        jax.ShapeDtypeStruct((t // tk, ATT_INNER, tk), BF16),
        jax.ShapeDtypeStruct((t, LANES), F32),
        jax.ShapeDtypeStruct((t // tk, LANES, tk), BF16),
        jax.ShapeDtypeStruct((t, LANES), BF16),
    )
    return pl.pallas_call(
        functools.partial(_in_proj_kernel, tiles_per_seq=seq // tm),
        grid=(t // tm,),
        in_specs=[
            pl.BlockSpec((tm, D_MODEL), row),
            pl.BlockSpec((D_MODEL, n_main), const),
            pl.BlockSpec((1, n_main), const),
            pl.BlockSpec((D_MODEL, LANES), const),
            pl.BlockSpec((1, LANES), const),
            pl.BlockSpec((1, LANES), const),
            pl.BlockSpec((cum, cum), const),
            pl.BlockSpec((3 * LANES, 2 * LANES), const),
            pl.BlockSpec((1, 2 * LANES), const),
        ],
        out_specs=[
            pl.BlockSpec((tm, SSD_INNER), row),
            pl.BlockSpec((tm, SSD_CONV_DIM), row),
            pl.BlockSpec((tm // tk, ATT_INNER, tk), lambda i: (i, 0, 0)),
            pl.BlockSpec((tm, ATT_INNER), row),
            pl.BlockSpec((tm // tk, ATT_INNER, tk), lambda i: (i, 0, 0)),
            pl.BlockSpec((tm, LANES), row),
            pl.BlockSpec((tm // tk, LANES, tk), lambda i: (i, 0, 0)),
            pl.BlockSpec((tm, LANES), row),
        ],
        out_shape=outs,
        scratch_shapes=[pltpu.VMEM((1, LANES), F32)],
        compiler_params=pltpu.CompilerParams(
            dimension_semantics=("arbitrary",), vmem_limit_bytes=VMEM_LIMIT_BYTES),
        name="in_proj",
    )(x2d, w_main, b_main, w_small, b_small, a_row, tri, place, ones)


def _expand_heads(gate, lane0, rows):
    lane = lax.broadcasted_iota(jnp.int32, (1, LANES), 1)
    pieces = []
    for p in range(SSD_HEADS // 2):
        a = jnp.broadcast_to(gate[:, lane0 + 2 * p:lane0 + 2 * p + 1], (rows, LANES))
        b = jnp.broadcast_to(gate[:, lane0 + 2 * p + 1:lane0 + 2 * p + 2], (rows, LANES))
        pieces.append(jnp.where(lane < SSD_HEAD_DIM, a, b))
    return jnp.concatenate(pieces, axis=1)


def _ssd_kernel(xbc_ref, z_ref, gate_ref, cw_ref, cb_ref, aexp_ref, dskip_ref, ng_ref, shift_ref,
                y_ref, ext_ref, state_ref):
    j = pl.program_id(1)
    rows = xbc_ref.shape[0]
    q = SSD_CHUNK
    gw = SSD_INNER // SSD_GROUPS
    hpg = SSD_HEADS // SSD_GROUPS
    pad = SUBLANES

    @pl.when(j == 0)
    def _():
        ext_ref[0:pad, :] = jnp.zeros((pad, SSD_CONV_DIM), F32)
        state_ref[...] = jnp.zeros_like(state_ref)

    xb = xbc_ref[...]
    xf = xb.astype(F32)
    conv = cb_ref[...] + xf * cw_ref[CONV_WIDTH - 1:CONV_WIDTH, :]
    for s in range(1, CONV_WIDTH):
        shifted = jnp.dot(shift_ref[s - 1], xb, preferred_element_type=F32)
        conv = conv + shifted * cw_ref[CONV_WIDTH - 1 - s:CONV_WIDTH - s, :]
    ext_ref[pad:2 * pad, :] = xf[0:pad, :]
    head = cb_ref[...]
    for kk in range(CONV_WIDTH):
        off = pad - (CONV_WIDTH - 1) + kk
        head = head + ext_ref[off:off + pad, :] * cw_ref[kk:kk + 1, :]
    ext_ref[0:pad, :] = xf[rows - pad:rows, :]
    u = _silu(jnp.concatenate([head, conv[pad:, :]], axis=0))
    xs = u[:, :SSD_INNER]
    b_all = u[:, SSD_INNER:SSD_INNER + SSD_GROUPS * SSD_STATE].astype(BF16)
    c_all = u[:, SSD_INNER + SSD_GROUPS * SSD_STATE:].astype(BF16)

    gate = gate_ref[...]
    dt_e = _expand_heads(gate, GATE_DT_LANE, rows)
    a_e = _expand_heads(gate, GATE_A_LANE, rows)
    xdt = xs * dt_e
    da_e = dt_e * aexp_ref[...]

    r_i = lax.broadcasted_iota(jnp.int32, (q, gw), 0)
    c_i = lax.broadcasted_iota(jnp.int32, (q, gw), 1)
    s_i = c_i % q
    eye_t = s_i == r_i
    tril_t = s_i <= r_i
    rb = lax.broadcasted_iota(jnp.int32, (hpg * q, gw), 0) // q
    cbk = lax.broadcasted_iota(jnp.int32, (hpg * q, gw), 1) // SSD_HEAD_DIM
    blk = rb == cbk

    ys = []
    for c in range(rows // q):
        r0 = c * q
        a_c = a_e[r0:r0 + q, :]
        a_prev = a_c[0:1, :] - da_e[r0:r0 + 1, :]
        a_end = a_c[q - 1:q, :]
        y_groups = []
        for g in range(SSD_GROUPS):
            l0 = g * gw
            a_cg = a_c[:, l0:l0 + gw]
            diag = jnp.sum(jnp.where(eye_t, a_cg, 0.0), axis=0, keepdims=True)
            seg = jnp.minimum(a_cg - diag, 0.0)
            lcat = jnp.where(tril_t, jnp.exp(seg), 0.0)
            bg = b_all[r0:r0 + q, g * SSD_STATE:(g + 1) * SSD_STATE]
            cg = c_all[r0:r0 + q, g * SSD_STATE:(g + 1) * SSD_STATE]
            b_tile = jnp.concatenate([bg] * hpg, axis=0)
            gcat = lax.dot_general(cg, b_tile, (((1,), (1,)), ((), ())),
                                   preferred_element_type=F32)
            xg = xdt[r0:r0 + q, l0:l0 + gw]
            xg_b = xg.astype(BF16)
            xbd = jnp.where(blk, jnp.concatenate([xg_b] * hpg, axis=0), jnp.zeros((), BF16))
            y_diag = jnp.dot((gcat * lcat).astype(BF16), xbd, preferred_element_type=F32)
            st = state_ref[g]
            y_off = jnp.dot(cg, st.astype(BF16), preferred_element_type=F32) \
                * jnp.exp(a_cg - a_prev[:, l0:l0 + gw])
            dte = jnp.exp(a_end[:, l0:l0 + gw] - a_cg)
            upd = lax.dot_general(bg, (xg * dte).astype(BF16), (((0,), (0,)), ((), ())),
                                  preferred_element_type=F32)
            state_ref[g] = st * jnp.exp(a_end[:, l0:l0 + gw] - a_prev[:, l0:l0 + gw]) + upd
            y_groups.append(y_diag + y_off)
        ys.append(jnp.concatenate(y_groups, axis=1))
    y = jnp.concatenate(ys, axis=0) + xs * dskip_ref[...]
    y = y * _silu(z_ref[...].astype(F32))
    outs = []
    for g in range(SSD_GROUPS):
        yg = y[:, g * gw:(g + 1) * gw]
        ms = jnp.mean(yg * yg, axis=1, keepdims=True)
        outs.append(yg * lax.rsqrt(ms + RMS_EPS))
    y_ref[...] = (jnp.concatenate(outs, axis=1) * ng_ref[...]).astype(y_ref.dtype)


def _ssd(xbc, z, gate, conv_w, conv_b, a_exp, dskip_exp, norm_g, batch, seq):
    t = xbc.shape[0]
    rows = min(256, seq)
    nj = seq // rows
    row = lambda b, j: (b * nj + j, 0)
    const = lambda b, j: (0, 0)
    shifts = jnp.stack([jnp.eye(rows, k=-s, dtype=BF16) for s in range(1, CONV_WIDTH)])
    return pl.pallas_call(
        _ssd_kernel,
        grid=(batch, nj),
        in_specs=[
            pl.BlockSpec((rows, SSD_CONV_DIM), row),
            pl.BlockSpec((rows, SSD_INNER), row),
            pl.BlockSpec((rows, LANES), row),
            pl.BlockSpec((CONV_WIDTH, SSD_CONV_DIM), const),
            pl.BlockSpec((1, SSD_CONV_DIM), const),
            pl.BlockSpec((1, SSD_INNER), const),
            pl.BlockSpec((1, SSD_INNER), const),
            pl.BlockSpec((1, SSD_INNER), const),
            pl.BlockSpec((CONV_WIDTH - 1, rows, rows), lambda b, j: (0, 0, 0)),
        ],
        out_specs=pl.BlockSpec((rows, SSD_INNER), row),
        out_shape=jax.ShapeDtypeStruct((t, SSD_INNER), BF16),
        scratch_shapes=[
            pltpu.VMEM((2 * SUBLANES, SSD_CONV_DIM), F32),
            pltpu.VMEM((SSD_GROUPS, SSD_STATE, SSD_INNER // SSD_GROUPS), F32),
        ],
        compiler_params=pltpu.CompilerParams(
            dimension_semantics=("arbitrary", "arbitrary"), vmem_limit_bytes=VMEM_LIMIT_BYTES),
        name="ssd",
    )(xbc, z, gate, conv_w, conv_b, a_exp, dskip_exp, norm_g, shifts)


def _attn_kernel(qt_ref, k_ref, vt_ref, qft_ref, kf_ref, o_ref, *scratch, tb, pairs):
    pair0 = pl.program_id(1) * pairs
    nb = qt_ref.shape[0]
    s_refs, acc_refs = scratch[:pairs], scratch[pairs:]
    feat = lax.broadcasted_iota(jnp.int32, (LANES, 1), 0)
    in_head = (feat < ATT_HEAD_DIM, feat >= ATT_HEAD_DIM)
    ones = jnp.ones((2 * SUBLANES, tb), BF16)
    key = lax.broadcasted_iota(jnp.int32, (tb, 2 * tb), 0)
    qry = lax.broadcasted_iota(jnp.int32, (tb, 2 * tb), 1)
    causal = jnp.where(qry >= tb, qry - tb, qry) >= key
    zero = jnp.zeros((), BF16)

    def logits_t(qa_t, pp, jk):
        k0 = pl.multiple_of(jk * tb, tb)
        ka = jnp.concatenate([k_ref[pl.ds(k0, tb), pp * LANES:(pp + 1) * LANES],
                              kf_ref[pl.ds(k0, tb), :]], axis=1)
        return jnp.dot(ka, qa_t, preferred_element_type=F32)

    def update(s_t, m, pp, jk):
        m_new = jnp.maximum(m, jnp.max(s_t, axis=0, keepdims=True))
        alpha = jnp.exp2(m - m_new)
        p_t = jnp.exp2(s_t - m_new).astype(BF16)
        for hh in range(2):
            r0 = pp * LANES + hh * ATT_HEAD_DIM
            va_t = jnp.concatenate([vt_ref[jk, r0:r0 + ATT_HEAD_DIM, :], ones], axis=0)
            cols = slice(hh * tb, (hh + 1) * tb)
            acc_refs[pp][hh] = alpha[:, cols] * acc_refs[pp][hh] + jnp.dot(
                va_t, p_t[:, cols], preferred_element_type=F32)
        return m_new

    def q_block(i, _):
        q0 = pl.multiple_of(i * tb, tb)
        qf_t = qft_ref[i]
        qas = []
        for pp in range(pairs):
            q_t = qt_ref[i, pp * LANES:(pp + 1) * LANES, :]
            cols = []
            for hh in range(2):
                fb0 = FB_LANES * (2 * (pair0 + pp) + hh)
                in_fb = (feat >= fb0) & (feat < fb0 + FB_LANES)
                cols.append(jnp.concatenate(
                    [jnp.where(in_head[hh], q_t, zero), jnp.where(in_fb, qf_t, zero)], axis=0))
            qas.append(jnp.concatenate(cols, axis=1))

        for pp in range(pairs):
            s_refs[pp][0] = logits_t(qas[pp], pp, 0)
            acc_refs[pp][...] = jnp.zeros(acc_refs[pp].shape, F32)

        def body(jk, ms):
            cur = jk % 2
            new = []
            for pp in range(pairs):
                s_t = s_refs[pp][cur]
                s_refs[pp][1 - cur] = logits_t(qas[pp], pp, jk + 1)
                new.append(update(s_t, ms[pp], pp, jk))
            return tuple(new)

        ms = lax.fori_loop(0, i, body,
                           tuple(jnp.full((1, 2 * tb), NEG_BIG, F32) for _ in range(pairs)))
        for pp in range(pairs):
            update(jnp.where(causal, s_refs[pp][i % 2], NEG_BIG), ms[pp], pp, i)
            heads = []
            for hh in range(2):
                acc_t = acc_refs[pp][hh]
                inv = 1.0 / acc_t[ATT_HEAD_DIM:ATT_HEAD_DIM + 1, :]
                heads.append(acc_t[:ATT_HEAD_DIM, :] * inv)
            out_t = jnp.concatenate(heads, axis=0)
            o_ref[pl.ds(q0, tb), pp * LANES:(pp + 1) * LANES] = out_t.T.astype(o_ref.dtype)
        return 0

    lax.fori_loop(0, nb, q_block, 0)


def _attention(q_t, k, v_t, qf_t, kf, batch, seq):
    t = k.shape[0]
    tb = min(ATT_BLOCK, seq)
    nb = seq // tb
    pairs = ATT_PAIRS_PER_STEP
    width = pairs * LANES
    return pl.pallas_call(
        functools.partial(_attn_kernel, tb=tb, pairs=pairs),
        grid=(batch, ATT_HEADS // (2 * pairs)),
        in_specs=[
            pl.BlockSpec((nb, width, tb), lambda b, p: (b, p, 0)),
            pl.BlockSpec((seq, width), lambda b, p: (b, p)),
            pl.BlockSpec((nb, width, tb), lambda b, p: (b, p, 0)),
            pl.BlockSpec((nb, LANES, tb), lambda b, p: (b, 0, 0)),
            pl.BlockSpec((seq, LANES), lambda b, p: (b, 0)),
        ],
        out_specs=pl.BlockSpec((seq, width), lambda b, p: (b, p)),
        out_shape=jax.ShapeDtypeStruct((t, ATT_INNER), BF16),
        scratch_shapes=([pltpu.VMEM((2, tb, 2 * tb), F32) for _ in range(pairs)]
                        + [pltpu.VMEM((2, ATT_HEAD_DIM + 2 * SUBLANES, tb), F32)
                           for _ in range(pairs)]),
        compiler_params=pltpu.CompilerParams(
            dimension_semantics=("arbitrary", "arbitrary"), vmem_limit_bytes=VMEM_LIMIT_BYTES),
        name="fox_attention",
    )(q_t, k, v_t, qf_t, kf)


def _layer_norm(u, g, b):
    mu = jnp.mean(u, axis=1, keepdims=True)
    d = u - mu
    var = jnp.mean(d * d, axis=1, keepdims=True)
    return d * lax.rsqrt(var + LN_EPS) * g + b


def _route(lg):
    lane = lax.broadcasted_iota(jnp.int32, (1, LANES), 1)
    lane_f = lane.astype(F32)
    is_g = lane < N_EXPERT_GROUPS
    gl = jnp.where(is_g, lg, NEG_BIG)
    gmax = jnp.max(gl, axis=1, keepdims=True)
    gsum = jnp.sum(jnp.where(is_g, jnp.exp(gl - gmax), 0.0), axis=1, keepdims=True)
    g_val = 1.0 / gsum
    g_idx = jnp.min(jnp.where(is_g & (gl == gmax), lane_f, float(LANES)), axis=1, keepdims=True)
    e_grp = ((lane - ROUTE_E_LANE) // EXPERTS_PER_GROUP).astype(F32)
    in_grp = (lane >= ROUTE_E_LANE) & (lane < ROUTE_E_LANE + N_EXPERTS) & (e_grp == g_idx)
    el = jnp.where(in_grp, lg, NEG_BIG)
    m1 = jnp.max(el, axis=1, keepdims=True)
    i1 = jnp.min(jnp.where(in_grp & (el == m1), lane_f, float(LANES)), axis=1, keepdims=True)
    rest = in_grp & (lane_f != i1)
    el2 = jnp.where(rest, lg, NEG_BIG)
    m2 = jnp.max(el2, axis=1, keepdims=True)
    i2 = jnp.min(jnp.where(rest & (el2 == m2), lane_f, float(LANES)), axis=1, keepdims=True)
    r = jnp.exp(m2 - m1)
    w1 = 1.0 / (1.0 + r)
    w2 = r * w1
    comb = jnp.where(lane_f == i1, g_val * w1, jnp.where(lane_f == i2, g_val * w2, 0.0))
    sel = jnp.where((lane_f == i1) | (lane_f == i2), 1.0, 0.0)
    return comb, sel


def _out_proj_kernel(ys_ref, ya_ref, x_ref, wa_ref, wb_ref, g_ref, b_ref, wr_ref, br_ref,
                     h_ref, hb_ref, comb_ref, sel_ref, cnt_ref):
    mix = jnp.dot(ys_ref[...], wa_ref[...], preferred_element_type=F32)
    mix = mix + jnp.dot(ya_ref[...], wb_ref[...], preferred_element_type=F32)
    h = _layer_norm(DEEPNORM_ALPHA * x_ref[...] + mix, g_ref[...], b_ref[...])
    h_ref[...] = h
    h_hi = h.astype(BF16)
    hb_ref[...] = h_hi
    h_lo = (h - h_hi.astype(F32)).astype(BF16)
    w_hi = wr_ref[:, :LANES]
    lg = (jnp.dot(h_hi, w_hi, preferred_element_type=F32)
          + jnp.dot(h_lo, w_hi, preferred_element_type=F32)
          + jnp.dot(h_hi, wr_ref[:, LANES:], preferred_element_type=F32)) + br_ref[...]
    comb, sel = _route(lg)
    comb_ref[...] = comb
    sel_ref[...] = sel
    ts = h.shape[0] // cnt_ref.shape[0]
    for c in range(cnt_ref.shape[0]):
        cnt = jnp.sum(sel[c * ts:(c + 1) * ts, :], axis=0, keepdims=True)
        cnt_ref[c] = jnp.broadcast_to(cnt, (SUBLANES, LANES))


def _out_proj(ys, ya, x2d, wo_a, wo_b, ln_g, ln_b, wr, br, ts):
    t = x2d.shape[0]
    tm = min(512, t)
    row = lambda i: (i, 0)
    const = lambda i: (0, 0)
    return pl.pallas_call(
        _out_proj_kernel,
        grid=(t // tm,),
        in_specs=[
            pl.BlockSpec((tm, SSD_INNER), row),
            pl.BlockSpec((tm, ATT_INNER), row),
            pl.BlockSpec((tm, D_MODEL), row),
            pl.BlockSpec((SSD_INNER, D_MODEL), const),
            pl.BlockSpec((ATT_INNER, D_MODEL), const),
            pl.BlockSpec((1, D_MODEL), const),
            pl.BlockSpec((1, D_MODEL), const),
            pl.BlockSpec((D_MODEL, 2 * LANES), const),
            pl.BlockSpec((1, LANES), const),
        ],
        out_specs=[
            pl.BlockSpec((tm, D_MODEL), row),
            pl.BlockSpec((tm, D_MODEL), row),
            pl.BlockSpec((tm, LANES), row),
            pl.BlockSpec((tm, LANES), row),
            pl.BlockSpec((tm // ts, SUBLANES, LANES), lambda i: (i, 0, 0)),
        ],
        out_shape=(
            jax.ShapeDtypeStruct((t, D_MODEL), F32),
            jax.ShapeDtypeStruct((t, D_MODEL), BF16),
            jax.ShapeDtypeStruct((t, LANES), F32),
            jax.ShapeDtypeStruct((t, LANES), F32),
            jax.ShapeDtypeStruct((t // ts, SUBLANES, LANES), F32),
        ),
        compiler_params=pltpu.CompilerParams(
            dimension_semantics=("arbitrary",), vmem_limit_bytes=VMEM_LIMIT_BYTES),
        name="out_proj_ln_router",
    )(ys, ya, x2d, wo_a, wo_b, ln_g, ln_b, wr, br)


def _dispatch_plan(cnt, ts):
    nt = cnt.shape[0]
    c16 = (cnt + (MOE_GRANULE - 1)) // MOE_GRANULE * MOE_GRANULE
    loc_off = jnp.cumsum(c16, axis=1) - c16
    tot = jnp.sum(c16, axis=0)
    tiles = (tot + (MOE_ROW_TILE - 1)) // MOE_ROW_TILE
    cum_tiles = jnp.cumsum(tiles)
    start = (cum_tiles - tiles) * MOE_ROW_TILE
    base = start[None, :] + jnp.cumsum(c16, axis=0) - c16
    max_tiles = _moe_rows(nt * ts, ts) // MOE_ROW_TILE
    tile_expert = jnp.minimum(
        jnp.sum(jnp.arange(max_tiles, dtype=jnp.int32)[:, None] >= cum_tiles[None, :], axis=1),
        N_EXPERTS - 1).astype(jnp.int32)
    loc_vec = jnp.zeros((nt, SUBLANES, LANES), F32).at[:, :, ROUTE_E_LANE:ROUTE_E_LANE + N_EXPERTS].set(
        jnp.broadcast_to(loc_off.astype(F32)[:, None, :], (nt, SUBLANES, N_EXPERTS)))
    return dict(
        chunks=(c16 // MOE_GRANULE).reshape(-1).astype(jnp.int32),
        loc_off=loc_off.reshape(-1).astype(jnp.int32),
        base=base.reshape(-1).astype(jnp.int32),
        step_chunks=(jnp.sum(c16, axis=1).reshape(-1, _moe_group(nt * ts, ts)).sum(axis=1)
                     // MOE_GRANULE).astype(jnp.int32),
        pad_start=(start + tot).astype(jnp.int32),
        pad_chunks=((tiles * MOE_ROW_TILE - tot) // MOE_GRANULE).astype(jnp.int32),
        n_tiles=cum_tiles[-1:].astype(jnp.int32),
        tile_expert=tile_expert,
        loc_vec=loc_vec,
    )


def _moe_group(t, ts):
    return MOE_TILES_PER_STEP if t % (MOE_TILES_PER_STEP * ts) == 0 else 1


def _moe_local_rows(ts):
    rows = 2 * ts + N_EXPERTS * (MOE_GRANULE - 1)
    return (rows + LANES - 1) // LANES * LANES


def _moe_rows(t, ts):
    rows = 2 * t + (t // ts) * N_EXPERTS * (MOE_GRANULE - 1) + N_EXPERTS * MOE_ROW_TILE
    return (rows + MOE_ROW_TILE - 1) // MOE_ROW_TILE * MOE_ROW_TILE


def _chunk_copy(src, dst, sem):
    return pltpu.make_async_copy(src, dst, sem)


def _wait_rows(copy_of_rows, n_chunks):
    def wait(rows):
        def one(_, c):
            copy_of_rows(rows).wait()
            return c
        return one
    shift = MOE_WAIT_BATCH.bit_length() - 1
    lax.fori_loop(0, lax.shift_right_logical(n_chunks, shift), wait(MOE_WAIT_BATCH * MOE_GRANULE), 0)
    lax.fori_loop(0, n_chunks & (MOE_WAIT_BATCH - 1), wait(MOE_GRANULE), 0)


def _dispatch_kernel(chunks_ref, loc_ref, base_ref, schunks_ref, pstart_ref, pchunks_ref,
                     hb_ref, sel_ref, comb_ref, locv_ref, lstrict_ref,
                     xs_ref, posw_ref, buf_ref, zero_ref, sem_ref, zsem_ref):
    step = pl.program_id(0)
    n_steps = pl.num_programs(0)
    par = step % 2
    group = locv_ref.shape[0]
    ts = hb_ref.shape[0] // group
    nl = buf_ref.shape[1]
    g = MOE_GRANULE

    def wait_chunks(n, pp):
        _wait_rows(lambda r: _chunk_copy(buf_ref.at[0, pl.ds(0, r)], xs_ref.at[pl.ds(0, r)],
                                         sem_ref.at[pp]), n)

    @pl.when(step >= 2)
    def _():
        wait_chunks(schunks_ref[step - 2], par)

    lane = lax.broadcasted_iota(jnp.int32, (1, LANES), 1)
    d = lax.broadcasted_iota(jnp.int32, (nl, ts), 0).astype(F32)
    for u in range(group):
        rows = slice(u * ts, (u + 1) * ts)
        sel = sel_ref[rows, :]
        comb = comb_ref[rows, :]
        picked = sel > 0.0
        rank = jnp.dot(lstrict_ref[...], sel.astype(BF16), preferred_element_type=F32)
        dest = rank + locv_ref[u][0:1, :]
        pos_a = jnp.min(jnp.where(picked, dest, 1e9), axis=1, keepdims=True)
        pos_b = jnp.max(jnp.where(picked, dest, -1.0), axis=1, keepdims=True)
        w_a = jnp.sum(jnp.where(picked & (dest == pos_a), comb, 0.0), axis=1, keepdims=True)
        w_b = jnp.sum(jnp.where(picked & (dest == pos_b), comb, 0.0), axis=1, keepdims=True)
        posw = jnp.where(lane == 0, pos_a, jnp.where(lane == 1, pos_b,
                         jnp.where(lane == 2, w_a, jnp.where(lane == 3, w_b, 0.0))))
        posw_ref[rows, :] = posw
        posw_t = posw.T
        perm = jnp.where((d == posw_t[0:1, :]) | (d == posw_t[1:2, :]), 1.0, 0.0).astype(BF16)
        buf_ref[par * group + u] = jnp.dot(perm, hb_ref[rows, :],
                                           preferred_element_type=F32).astype(BF16)

    for u in range(group):
        for e in range(N_EXPERTS):
            idx = (step * group + u) * N_EXPERTS + e
            lo = loc_ref[idx]
            bs = base_ref[idx]

            def send(k, c, lo=lo, bs=bs, u=u):
                src = buf_ref.at[par * group + u, pl.ds(pl.multiple_of(lo + k * g, g), g)]
                dst = xs_ref.at[pl.ds(pl.multiple_of(bs + k * g, g), g)]
                _chunk_copy(src, dst, sem_ref.at[par]).start()
                return c
            lax.fori_loop(0, chunks_ref[idx], send, 0)

    @pl.when(step == n_steps - 1)
    def _():
        zero_ref[...] = jnp.zeros_like(zero_ref)
        for e in range(N_EXPERTS):
            ps = pstart_ref[e]

            def send_zero(k, c, ps=ps):
                dst = xs_ref.at[pl.ds(pl.multiple_of(ps + k * g, g), g)]
                _chunk_copy(zero_ref, dst, zsem_ref.at[0]).start()
                return c
            lax.fori_loop(0, pchunks_ref[e], send_zero, 0)
        for e in range(N_EXPERTS):
            def wait_zero(_, c):
                _chunk_copy(zero_ref, xs_ref.at[pl.ds(0, g)], zsem_ref.at[0]).wait()
                return c
            lax.fori_loop(0, pchunks_ref[e], wait_zero, 0)
        wait_chunks(schunks_ref[step], par)

        @pl.when(step >= 1)
        def _():
            wait_chunks(schunks_ref[step - 1], 1 - par)


def _dispatch(plan, hb, sel, comb, ts):
    t = hb.shape[0]
    group = _moe_group(t, ts)
    nl = _moe_local_rows(ts)
    rows = _moe_rows(t, ts)
    lstrict = jnp.tri(ts, k=-1, dtype=BF16)
    row = lambda i, *_: (i, 0)
    grid_spec = pltpu.PrefetchScalarGridSpec(
        num_scalar_prefetch=6,
        grid=(t // (group * ts),),
        in_specs=[
            pl.BlockSpec((group * ts, D_MODEL), row),
            pl.BlockSpec((group * ts, LANES), row),
            pl.BlockSpec((group * ts, LANES), row),
            pl.BlockSpec((group, SUBLANES, LANES), lambda i, *_: (i, 0, 0)),
            pl.BlockSpec((ts, ts), lambda i, *_: (0, 0)),
        ],
        out_specs=[
            pl.BlockSpec(memory_space=pl.ANY),
            pl.BlockSpec((group * ts, LANES), row),
        ],
        scratch_shapes=[
            pltpu.VMEM((2 * group, nl, D_MODEL), BF16),
            pltpu.VMEM((MOE_GRANULE, D_MODEL), BF16),
            pltpu.SemaphoreType.DMA((2,)),
            pltpu.SemaphoreType.DMA((1,)),
        ],
    )
    return pl.pallas_call(
        _dispatch_kernel,
        grid_spec=grid_spec,
        out_shape=(
            jax.ShapeDtypeStruct((rows, D_MODEL), BF16),
            jax.ShapeDtypeStruct((t, LANES), F32),
        ),
        compiler_params=pltpu.CompilerParams(
            dimension_semantics=("arbitrary",), vmem_limit_bytes=VMEM_LIMIT_BYTES),
        name="moe_dispatch",
    )(plan["chunks"], plan["loc_off"], plan["base"], plan["step_chunks"], plan["pad_start"],
      plan["pad_chunks"], hb, sel, comb, plan["loc_vec"], lstrict)


def _expert_kernel(te_ref, nt_ref, x_ref, wg_ref, wu_ref, wd_ref, y_ref, wgb_ref, wub_ref, wdb_ref):
    j = pl.program_id(0)

    @pl.when(j < nt_ref[0])
    def _():
        @pl.when((j == 0) | (te_ref[j] != te_ref[jnp.maximum(j - 1, 0)]))
        def _():
            wgb_ref[...] = wg_ref[0].astype(BF16)
            wub_ref[...] = wu_ref[0].astype(BF16)
            wdb_ref[...] = wd_ref[0].astype(BF16)

        x = x_ref[...]
        gate = jnp.dot(x, wgb_ref[...], preferred_element_type=F32)
        up = jnp.dot(x, wub_ref[...], preferred_element_type=F32)
        act = (_silu(gate) * up).astype(BF16)
        y_ref[...] = jnp.dot(act, wdb_ref[...], preferred_element_type=F32).astype(y_ref.dtype)


def _experts(plan, xs, w_gate, w_up, w_down):
    rows = xs.shape[0]
    live = lambda j, te, nt: jnp.minimum(j, nt[0] - 1)
    expert = lambda j, te, nt: (te[live(j, te, nt)], 0, 0)
    grid_spec = pltpu.PrefetchScalarGridSpec(
        num_scalar_prefetch=2,
        grid=(rows // MOE_ROW_TILE,),
        in_specs=[
            pl.BlockSpec((MOE_ROW_TILE, D_MODEL), lambda j, te, nt: (live(j, te, nt), 0)),
            pl.BlockSpec((1, D_MODEL, D_FF_EXPERT), expert),
            pl.BlockSpec((1, D_MODEL, D_FF_EXPERT), expert),
            pl.BlockSpec((1, D_FF_EXPERT, D_MODEL), expert),
        ],
        out_specs=pl.BlockSpec((MOE_ROW_TILE, D_MODEL), lambda j, te, nt: (live(j, te, nt), 0)),
        scratch_shapes=[
            pltpu.VMEM((D_MODEL, D_FF_EXPERT), BF16),
            pltpu.VMEM((D_MODEL, D_FF_EXPERT), BF16),
            pltpu.VMEM((D_FF_EXPERT, D_MODEL), BF16),
        ],
    )
    return pl.pallas_call(
        _expert_kernel,
        grid_spec=grid_spec,
        out_shape=jax.ShapeDtypeStruct((rows, D_MODEL), BF16),
        compiler_params=pltpu.CompilerParams(
            dimension_semantics=("arbitrary",), vmem_limit_bytes=VMEM_LIMIT_BYTES),
        name="moe_experts",
    )(plan["tile_expert"], plan["n_tiles"], xs, w_gate, w_up, w_down)


def _combine_kernel(chunks_ref, loc_ref, base_ref, schunks_ref,
                    y_ref, posw_ref, h_ref, g_ref, b_ref, o_ref, buf_ref, sem_ref):
    step = pl.program_id(0)
    n_steps = pl.num_programs(0)
    par = step % 2
    nl = buf_ref.shape[1]
    group = buf_ref.shape[0] // 2
    ts = h_ref.shape[0] // group
    g = MOE_GRANULE

    def fetch(ss, pp):
        for u in range(group):
            for e in range(N_EXPERTS):
                idx = (ss * group + u) * N_EXPERTS + e
                lo = loc_ref[idx]
                bs = base_ref[idx]

                def get(k, c, lo=lo, bs=bs, u=u):
                    src = y_ref.at[pl.ds(pl.multiple_of(bs + k * g, g), g)]
                    dst = buf_ref.at[pp * group + u, pl.ds(pl.multiple_of(lo + k * g, g), g)]
                    _chunk_copy(src, dst, sem_ref.at[pp]).start()
                    return c
                lax.fori_loop(0, chunks_ref[idx], get, 0)

    @pl.when(step == 0)
    def _():
        buf_ref[...] = jnp.zeros(buf_ref.shape, BF16)
        fetch(0, 0)

    @pl.when(step + 1 < n_steps)
    def _():
        fetch(step + 1, 1 - par)

    _wait_rows(lambda r: _chunk_copy(y_ref.at[pl.ds(0, r)], buf_ref.at[0, pl.ds(0, r)],
                                     sem_ref.at[par]), schunks_ref[step])

    d = lax.broadcasted_iota(jnp.int32, (1, nl), 1).astype(F32)
    for u in range(group):
        rows = slice(u * ts, (u + 1) * ts)
        yb = buf_ref[par * group + u]
        posw = posw_ref[rows, :]
        ffn = jnp.zeros((ts, D_MODEL), F32)
        for k in range(2):
            gather = jnp.where(posw[:, k:k + 1] == d, 1.0, 0.0).astype(BF16)
            ffn = ffn + posw[:, 2 + k:3 + k] * jnp.dot(gather, yb, preferred_element_type=F32)
        o_ref[rows, :] = _layer_norm(DEEPNORM_ALPHA * h_ref[rows, :] + ffn, g_ref[...], b_ref[...])


def _combine(plan, y, posw, h, ln_g, ln_b, ts):
    t = h.shape[0]
    group = _moe_group(t, ts)
    nl = _moe_local_rows(ts)
    row = lambda i, *_: (i, 0)
    const = lambda i, *_: (0, 0)
    grid_spec = pltpu.PrefetchScalarGridSpec(
        num_scalar_prefetch=4,
        grid=(t // (group * ts),),
        in_specs=[
            pl.BlockSpec(memory_space=pl.ANY),
            pl.BlockSpec((group * ts, LANES), row),
            pl.BlockSpec((group * ts, D_MODEL), row),
            pl.BlockSpec((1, D_MODEL), const),
            pl.BlockSpec((1, D_MODEL), const),
        ],
        out_specs=pl.BlockSpec((group * ts, D_MODEL), row),
        scratch_shapes=[
            pltpu.VMEM((2 * group, nl, D_MODEL), BF16),
            pltpu.SemaphoreType.DMA((2,)),
        ],
    )
    return pl.pallas_call(
        _combine_kernel,
        grid_spec=grid_spec,
        out_shape=jax.ShapeDtypeStruct((t, D_MODEL), F32),
        compiler_params=pltpu.CompilerParams(
            dimension_semantics=("arbitrary",), vmem_limit_bytes=VMEM_LIMIT_BYTES),
        name="moe_combine_ln",
    )(plan["chunks"], plan["loc_off"], plan["base"], plan["step_chunks"], y, posw, h, ln_g, ln_b)


def _pad_lanes(a, n=LANES):
    return jnp.pad(a, ((0, 0), (0, n - a.shape[1])))


def _layer(h2d, batch, seq, w_in, b_in, conv_w, conv_b, a_log, d_skip, ssd_norm_g, w_out,
           ln1_g, ln1_b, rg_w, rg_b, re_w, re_b, w_gate, w_up, w_down, ln2_g, ln2_b):
    o_dt = SSD_INNER + SSD_CONV_DIM
    o_q = o_dt + SSD_HEADS
    o_f = o_q + 3 * ATT_INNER
    w_main = jnp.concatenate([w_in[:, :o_dt], w_in[:, o_q:o_f]], axis=1).astype(BF16)
    b_main = jnp.concatenate([b_in[:o_dt], b_in[o_q:o_f]])[None, :]
    w_dt, w_f = w_in[:, o_dt:o_q], w_in[:, o_f:]
    w_small = _pad_lanes(jnp.concatenate([w_dt, w_f, w_dt], axis=1)).astype(BF16)
    b_small = _pad_lanes(jnp.concatenate([b_in[o_dt:o_q], b_in[o_f:], b_in[o_dt:o_q]])[None, :])
    a_neg = -jnp.exp(a_log.astype(F32))
    a_row = _pad_lanes(a_neg[None, :])
    a_exp = jnp.repeat(a_neg, SSD_HEAD_DIM)[None, :]
    dskip_exp = jnp.repeat(d_skip.astype(F32), SSD_HEAD_DIM)[None, :]

    z, xbc, q_t, k, v_t, gate, qf_t, kf = _in_proj(h2d, w_main, b_main, w_small, b_small, a_row, seq)
    y_ssd = _ssd(xbc, z, gate, conv_w, conv_b[None, :], a_exp, dskip_exp, ssd_norm_g[None, :],
                 batch, seq)
    y_att = _attention(q_t, k, v_t, qf_t, kf, batch, seq)

    w_r = _pad_lanes(jnp.concatenate(
        [rg_w, jnp.transpose(re_w, (1, 0, 2)).reshape(D_MODEL, N_EXPERTS)], axis=1))
    b_r = _pad_lanes(jnp.concatenate([rg_b, re_b.reshape(N_EXPERTS)])[None, :])
    wr_hi = w_r.astype(BF16)
    wr_lo = (w_r - wr_hi.astype(F32)).astype(BF16)
    wo = w_out.astype(BF16)
    ts = min(MOE_SORT_TILE, h2d.shape[0])
    h1, h1b, comb, sel, cnt = _out_proj(y_ssd, y_att, h2d, wo[:SSD_INNER], wo[SSD_INNER:],
                                        ln1_g[None, :], ln1_b[None, :],
                                        jnp.concatenate([wr_hi, wr_lo], axis=1), b_r, ts)

    plan = _dispatch_plan(
        cnt[:, 0, ROUTE_E_LANE:ROUTE_E_LANE + N_EXPERTS].astype(jnp.int32), ts)
    xs, posw = _dispatch(plan, h1b, sel, comb, ts)
    y = _experts(plan, xs, w_gate, w_up, w_down)
    return _combine(plan, y, posw, h1, ln2_g[None, :], ln2_b[None, :], ts)


def kernel(x, w_in, b_in, conv_w, conv_b, a_log, d_skip, ssd_norm_g, w_out, ln1_g, ln1_b,
           router_group_w, router_group_b, router_expert_w, router_expert_b, w_gate, w_up,
           w_down, ln2_g, ln2_b):
    batch, seq, d = x.shape
    h = x.reshape(batch * seq, d)
    for l in range(w_in.shape[0]):
        h = _layer(h, batch, seq, w_in[l], b_in[l], conv_w[l], conv_b[l], a_log[l], d_skip[l],
                   ssd_norm_g[l], w_out[l], ln1_g[l], ln1_b[l], router_group_w[l],
                   router_group_b[l], router_expert_w[l], router_expert_b[l], w_gate[l],
                   w_up[l], w_down[l], ln2_g[l], ln2_b[l])
    return h.reshape(batch, seq, d)
```

```python
import functools

import jax
import jax.numpy as jnp
import numpy as np
from jax import lax
from jax.experimental import pallas as pl
from jax.experimental.pallas import tpu as pltpu

F32 = jnp.float32
BF16 = jnp.bfloat16

D_MODEL = 1024
SSD_HEADS = 8
SSD_HEAD_DIM = 64
SSD_INNER = SSD_HEADS * SSD_HEAD_DIM
SSD_GROUPS = 2
SSD_STATE = 128
CONV_WIDTH = 4
SSD_CONV_DIM = SSD_INNER + 2 * SSD_GROUPS * SSD_STATE
ATT_HEADS = 8
ATT_HEAD_DIM = 64
ATT_INNER = ATT_HEADS * ATT_HEAD_DIM
N_EXPERT_GROUPS = 4
EXPERTS_PER_GROUP = 4
N_EXPERTS = N_EXPERT_GROUPS * EXPERTS_PER_GROUP
D_FF_EXPERT = 512
DEPTH = 1
DEEPNORM_ALPHA = (2.0 * DEPTH) ** 0.25
LN_EPS = 1e-5
RMS_EPS = 1e-5

LANES = 128
SUBLANES = 8
VMEM_LIMIT_BYTES = 56 * 1024 * 1024

GATE_A_LANE = 0
GATE_F_LANE = SSD_HEADS
GATE_DT_LANE = 2 * SSD_HEADS
ROUTE_E_LANE = SUBLANES
MOE_SORT_TILE = 256
MOE_GRANULE = 16
MOE_ROW_TILE = 512
MOE_TILES_PER_STEP = 4
MOE_WAIT_BATCH = 8

IN_PROJ_ROWS = 1024
CUMSUM_BLOCK = 256
FB_LANES = 6
ATT_BLOCK = 256
ATT_PAIRS_PER_STEP = 4

SSD_CHUNK = 64
NEG_BIG = -1e30
LOG2E = 1.4426950408889634


def _split3(v):
    hi = v.astype(BF16)
    r1 = v - hi.astype(F32)
    mid = r1.astype(BF16)
    lo = (r1 - mid.astype(F32)).astype(BF16)
    return hi, mid, lo


def _silu(v):
    half = 0.5 * v
    return half * (1.0 + jnp.tanh(half))


def _softplus(v):
    return jnp.maximum(v, 0.0) + jnp.log1p(jnp.exp(-jnp.abs(v)))


def _in_proj_kernel(x_ref, wm_ref, bm_ref, ws_ref, bs_ref, arow_ref, tri_ref, place_ref, ones_ref,
                    z_ref, xbc_ref, qt_ref, k_ref, vt_ref, gate_ref, qft_ref, kf_ref,
                    carry_ref, *, tiles_per_seq):
    i = pl.program_id(0)
    tm = x_ref.shape[0]
    tb = qt_ref.shape[2]
    xb = x_ref[...].astype(BF16)

    def put_transposed(ref, val):
        for c in range(tm // tb):
            ref[c] = val[c * tb:(c + 1) * tb, :].T.astype(ref.dtype)

    s = jnp.dot(xb, ws_ref[...], preferred_element_type=F32) + bs_ref[...]
    lane = lax.broadcasted_iota(jnp.int32, (1, LANES), 1)
    sp = _softplus(s)
    log_f = -_softplus(-s)
    val = jnp.where(lane < GATE_F_LANE, sp * arow_ref[...],
                    jnp.where(lane < GATE_DT_LANE, log_f,
                              jnp.where(lane < GATE_DT_LANE + SSD_HEADS, sp, 0.0)))
    carry = jnp.where(i % tiles_per_seq == 0, 0.0, carry_ref[...])
    tri = tri_ref[...]
    cb = tri.shape[0]
    sums = []
    for c in range(tm // cb):
        hi, mid, lo = _split3(val[c * cb:(c + 1) * cb, :])
        part = (jnp.dot(tri, hi, preferred_element_type=F32)
                + jnp.dot(tri, mid, preferred_element_type=F32)
                + jnp.dot(tri, lo, preferred_element_type=F32)) + carry
        carry = part[cb - 1:cb, :]
        sums.append(part)
    cs = jnp.concatenate(sums, axis=0)
    carry_ref[...] = carry
    out = jnp.where(lane < GATE_DT_LANE, cs, val)
    gate_ref[...] = out
    pieces = jnp.concatenate(_split3(cs * LOG2E), axis=1)
    fb = jnp.dot(pieces, place_ref[...], preferred_element_type=F32) + ones_ref[...]
    put_transposed(qft_ref, fb[:, :LANES])
    kf_ref[...] = fb[:, LANES:].astype(BF16)

    col = 0
    for ref, scale in ((z_ref, None), (xbc_ref, None), (qt_ref, ATT_HEAD_DIM ** -0.5 * LOG2E),
                       (k_ref, None), (vt_ref, None)):
        transposed = ref is qt_ref or ref is vt_ref
        n = ATT_INNER if transposed else ref.shape[1]
        acc = jnp.dot(xb, wm_ref[:, col:col + n], preferred_element_type=F32) + bm_ref[:, col:col + n]
        if scale is not None:
            acc = acc * scale
        if transposed:
            put_transposed(ref, acc)
        else:
            ref[...] = acc.astype(ref.dtype)
        col += n


def _forget_bias_placement():
    place = np.zeros((3 * LANES, 2 * LANES), np.float32)
    ones = np.zeros((1, 2 * LANES), np.float32)
    for h in range(ATT_HEADS):
        for piece in range(3):
            src = piece * LANES + GATE_F_LANE + h
            place[src, FB_LANES * h + piece] = 1.0
            place[src, LANES + FB_LANES * h + 3 + piece] = -1.0
            ones[0, FB_LANES * h + 3 + piece] = 1.0
            ones[0, LANES + FB_LANES * h + piece] = 1.0
    return jnp.asarray(place, BF16), jnp.asarray(ones, F32)


def _in_proj(x2d, w_main, b_main, w_small, b_small, a_row, seq):
    t = x2d.shape[0]
    tm = min(IN_PROJ_ROWS, seq)
    tk = min(ATT_BLOCK, seq)
    cum = min(CUMSUM_BLOCK, tm)
    tri = jnp.tri(cum, dtype=BF16)
    place, ones = _forget_bias_placement()
    n_main = w_main.shape[1]
    row = lambda i: (i, 0)
    const = lambda i: (0, 0)
    outs = (
        jax.ShapeDtypeStruct((t, SSD_INNER), BF16),
        jax.ShapeDtypeStruct((t, SSD_CONV_DIM), BF16),
        jax.ShapeDtypeStruct((t // tk, ATT_INNER, tk), BF16),
        jax.ShapeDtypeStruct((t, ATT_INNER), BF16),
        jax.ShapeDtypeStruct((t // tk, ATT_INNER, tk), BF16),
        jax.ShapeDtypeStruct((t, LANES), F32),
        jax.ShapeDtypeStruct((t // tk, LANES, tk), BF16),
        jax.ShapeDtypeStruct((t, LANES), BF16),
    )
    return pl.pallas_call(
        functools.partial(_in_proj_kernel, tiles_per_seq=seq // tm),
        grid=(t // tm,),
        in_specs=[
            pl.BlockSpec((tm, D_MODEL), row),
            pl.BlockSpec((D_MODEL, n_main), const),
            pl.BlockSpec((1, n_main), const),
            pl.BlockSpec((D_MODEL, LANES), const),
            pl.BlockSpec((1, LANES), const),
            pl.BlockSpec((1, LANES), const),
            pl.BlockSpec((cum, cum), const),
            pl.BlockSpec((3 * LANES, 2 * LANES), const),
            pl.BlockSpec((1, 2 * LANES), const),
        ],
        out_specs=[
            pl.BlockSpec((tm, SSD_INNER), row),
            pl.BlockSpec((tm, SSD_CONV_DIM), row),
            pl.BlockSpec((tm // tk, ATT_INNER, tk), lambda i: (i, 0, 0)),
            pl.BlockSpec((tm, ATT_INNER), row),
            pl.BlockSpec((tm // tk, ATT_INNER, tk), lambda i: (i, 0, 0)),
            pl.BlockSpec((tm, LANES), row),
            pl.BlockSpec((tm // tk, LANES, tk), lambda i: (i, 0, 0)),
            pl.BlockSpec((tm, LANES), row),
        ],
        out_shape=outs,
        scratch_shapes=[pltpu.VMEM((1, LANES), F32)],
        compiler_params=pltpu.CompilerParams(
            dimension_semantics=("arbitrary",), vmem_limit_bytes=VMEM_LIMIT_BYTES),
        name="in_proj",
    )(x2d, w_main, b_main, w_small, b_small, a_row, tri, place, ones)


def _expand_heads(gate, lane0, rows):
    lane = lax.broadcasted_iota(jnp.int32, (1, LANES), 1)
    pieces = []
    for p in range(SSD_HEADS // 2):
        a = jnp.broadcast_to(gate[:, lane0 + 2 * p:lane0 + 2 * p + 1], (rows, LANES))
        b = jnp.broadcast_to(gate[:, lane0 + 2 * p + 1:lane0 + 2 * p + 2], (rows, LANES))
        pieces.append(jnp.where(lane < SSD_HEAD_DIM, a, b))
    return jnp.concatenate(pieces, axis=1)


def _ssd_kernel(xbc_ref, z_ref, gate_ref, cw_ref, cb_ref, aexp_ref, dskip_ref, ng_ref, shift_ref,
                y_ref, ext_ref, state_ref):
    j = pl.program_id(1)
    rows = xbc_ref.shape[0]
    q = SSD_CHUNK
    gw = SSD_INNER // SSD_GROUPS
    hpg = SSD_HEADS // SSD_GROUPS
    pad = SUBLANES

    @pl.when(j == 0)
    def _():
        ext_ref[0:pad, :] = jnp.zeros((pad, SSD_CONV_DIM), F32)
        state_ref[...] = jnp.zeros_like(state_ref)

    xb = xbc_ref[...]
    xf = xb.astype(F32)
    conv = cb_ref[...] + xf * cw_ref[CONV_WIDTH - 1:CONV_WIDTH, :]
    for s in range(1, CONV_WIDTH):
        shifted = jnp.dot(shift_ref[s - 1], xb, preferred_element_type=F32)
        conv = conv + shifted * cw_ref[CONV_WIDTH - 1 - s:CONV_WIDTH - s, :]
    ext_ref[pad:2 * pad, :] = xf[0:pad, :]
    head = cb_ref[...]
    for kk in range(CONV_WIDTH):
        off = pad - (CONV_WIDTH - 1) + kk
        head = head + ext_ref[off:off + pad, :] * cw_ref[kk:kk + 1, :]
    ext_ref[0:pad, :] = xf[rows - pad:rows, :]
    u = _silu(jnp.concatenate([head, conv[pad:, :]], axis=0))
    xs = u[:, :SSD_INNER]
    b_all = u[:, SSD_INNER:SSD_INNER + SSD_GROUPS * SSD_STATE].astype(BF16)
    c_all = u[:, SSD_INNER + SSD_GROUPS * SSD_STATE:].astype(BF16)

    gate = gate_ref[...]
    dt_e = _expand_heads(gate, GATE_DT_LANE, rows)
    a_e = _expand_heads(gate, GATE_A_LANE, rows)
    xdt = xs * dt_e
    da_e = dt_e * aexp_ref[...]

    r_i = lax.broadcasted_iota(jnp.int32, (q, gw), 0)
    c_i = lax.broadcasted_iota(jnp.int32, (q, gw), 1)
    s_i = c_i % q
    eye_t = s_i == r_i
    tril_t = s_i <= r_i
    rb = lax.broadcasted_iota(jnp.int32, (hpg * q, gw), 0) // q
    cbk = lax.broadcasted_iota(jnp.int32, (hpg * q, gw), 1) // SSD_HEAD_DIM
    blk = rb == cbk

    ys = []
    for c in range(rows // q):
        r0 = c * q
        a_c = a_e[r0:r0 + q, :]
        a_prev = a_c[0:1, :] - da_e[r0:r0 + 1, :]
        a_end = a_c[q - 1:q, :]
        y_groups = []
        for g in range(SSD_GROUPS):
            l0 = g * gw
            a_cg = a_c[:, l0:l0 + gw]
            diag = jnp.sum(jnp.where(eye_t, a_cg, 0.0), axis=0, keepdims=True)
            seg = jnp.minimum(a_cg - diag, 0.0)
            lcat = jnp.where(tril_t, jnp.exp(seg), 0.0)
            bg = b_all[r0:r0 + q, g * SSD_STATE:(g + 1) * SSD_STATE]
            cg = c_all[r0:r0 + q, g * SSD_STATE:(g + 1) * SSD_STATE]
            b_tile = jnp.concatenate([bg] * hpg, axis=0)
            gcat = lax.dot_general(cg, b_tile, (((1,), (1,)), ((), ())),
                                   preferred_element_type=F32)
            xg = xdt[r0:r0 + q, l0:l0 + gw]
            xg_b = xg.astype(BF16)
            xbd = jnp.where(blk, jnp.concatenate([xg_b] * hpg, axis=0), jnp.zeros((), BF16))
            y_diag = jnp.dot((gcat * lcat).astype(BF16), xbd, preferred_element_type=F32)
            st = state_ref[g]
            y_off = jnp.dot(cg, st.astype(BF16), preferred_element_type=F32) \
                * jnp.exp(a_cg - a_prev[:, l0:l0 + gw])
            dte = jnp.exp(a_end[:, l0:l0 + gw] - a_cg)
            upd = lax.dot_general(bg, (xg * dte).astype(BF16), (((0,), (0,)), ((), ())),
                                  preferred_element_type=F32)
            state_ref[g] = st * jnp.exp(a_end[:, l0:l0 + gw] - a_prev[:, l0:l0 + gw]) + upd
            y_groups.append(y_diag + y_off)
        ys.append(jnp.concatenate(y_groups, axis=1))
    y = jnp.concatenate(ys, axis=0) + xs * dskip_ref[...]
    y = y * _silu(z_ref[...].astype(F32))
    outs = []
    for g in range(SSD_GROUPS):
        yg = y[:, g * gw:(g + 1) * gw]
        ms = jnp.mean(yg * yg, axis=1, keepdims=True)
        outs.append(yg * lax.rsqrt(ms + RMS_EPS))
    y_ref[...] = (jnp.concatenate(outs, axis=1) * ng_ref[...]).astype(y_ref.dtype)


def _ssd(xbc, z, gate, conv_w, conv_b, a_exp, dskip_exp, norm_g, batch, seq):
    t = xbc.shape[0]
    rows = min(256, seq)
    nj = seq // rows
    row = lambda b, j: (b * nj + j, 0)
    const = lambda b, j: (0, 0)
    shifts = jnp.stack([jnp.eye(rows, k=-s, dtype=BF16) for s in range(1, CONV_WIDTH)])
    return pl.pallas_call(
        _ssd_kernel,
        grid=(batch, nj),
        in_specs=[
            pl.BlockSpec((rows, SSD_CONV_DIM), row),
            pl.BlockSpec((rows, SSD_INNER), row),
            pl.BlockSpec((rows, LANES), row),
            pl.BlockSpec((CONV_WIDTH, SSD_CONV_DIM), const),
            pl.BlockSpec((1, SSD_CONV_DIM), const),
            pl.BlockSpec((1, SSD_INNER), const),
            pl.BlockSpec((1, SSD_INNER), const),
            pl.BlockSpec((1, SSD_INNER), const),
            pl.BlockSpec((CONV_WIDTH - 1, rows, rows), lambda b, j: (0, 0, 0)),
        ],
        out_specs=pl.BlockSpec((rows, SSD_INNER), row),
        out_shape=jax.ShapeDtypeStruct((t, SSD_INNER), BF16),
        scratch_shapes=[
            pltpu.VMEM((2 * SUBLANES, SSD_CONV_DIM), F32),
            pltpu.VMEM((SSD_GROUPS, SSD_STATE, SSD_INNER // SSD_GROUPS), F32),
        ],
        compiler_params=pltpu.CompilerParams(
            dimension_semantics=("arbitrary", "arbitrary"), vmem_limit_bytes=VMEM_LIMIT_BYTES),
        name="ssd",
    )(xbc, z, gate, conv_w, conv_b, a_exp, dskip_exp, norm_g, shifts)


def _attn_kernel(qt_ref, k_ref, vt_ref, qft_ref, kf_ref, o_ref, *scratch, tb, pairs):
    pair0 = pl.program_id(1) * pairs
    nb = qt_ref.shape[0]
    s_refs, acc_refs = scratch[:pairs], scratch[pairs:]
    feat = lax.broadcasted_iota(jnp.int32, (LANES, 1), 0)
    in_head = (feat < ATT_HEAD_DIM, feat >= ATT_HEAD_DIM)
    ones = jnp.ones((2 * SUBLANES, tb), BF16)
    key = lax.broadcasted_iota(jnp.int32, (tb, 2 * tb), 0)
    qry = lax.broadcasted_iota(jnp.int32, (tb, 2 * tb), 1)
    causal = jnp.where(qry >= tb, qry - tb, qry) >= key
    zero = jnp.zeros((), BF16)

    def logits_t(qa_t, pp, jk):
        k0 = pl.multiple_of(jk * tb, tb)
        ka = jnp.concatenate([k_ref[pl.ds(k0, tb), pp * LANES:(pp + 1) * LANES],
                              kf_ref[pl.ds(k0, tb), :]], axis=1)
        return jnp.dot(ka, qa_t, preferred_element_type=F32)

    def update(s_t, m, pp, jk):
        m_new = jnp.maximum(m, jnp.max(s_t, axis=0, keepdims=True))
        alpha = jnp.exp2(m - m_new)
        p_t = jnp.exp2(s_t - m_new).astype(BF16)
        for hh in range(2):
            r0 = pp * LANES + hh * ATT_HEAD_DIM
            va_t = jnp.concatenate([vt_ref[jk, r0:r0 + ATT_HEAD_DIM, :], ones], axis=0)
            cols = slice(hh * tb, (hh + 1) * tb)
            acc_refs[pp][hh] = alpha[:, cols] * acc_refs[pp][hh] + jnp.dot(
                va_t, p_t[:, cols], preferred_element_type=F32)
        return m_new

    def build_queries(i):
        qf_t = qft_ref[i]
        qas = []
        for pp in range(pairs):
            q_t = qt_ref[i, pp * LANES:(pp + 1) * LANES, :]
            cols = []
            for hh in range(2):
                fb0 = FB_LANES * (2 * (pair0 + pp) + hh)
                in_fb = (feat >= fb0) & (feat < fb0 + FB_LANES)
                cols.append(jnp.concatenate(
                    [jnp.where(in_head[hh], q_t, zero), jnp.where(in_fb, qf_t, zero)], axis=0))
            qas.append(jnp.concatenate(cols, axis=1))
        return qas

    def first_logits(i):
        qas = build_queries(i)
        for pp in range(pairs):
            s_refs[pp][2] = logits_t(qas[pp], pp, 0)

    def q_block(i, _):
        q0 = pl.multiple_of(i * tb, tb)
        qas = build_queries(i)
        for pp in range(pairs):
            acc_refs[pp][...] = jnp.zeros(acc_refs[pp].shape, F32)

        def step(jk, ms, rd, wr):
            new = []
            for pp in range(pairs):
                s_t = s_refs[pp][rd]
                s_refs[pp][wr] = logits_t(qas[pp], pp, jk + 1)
                new.append(update(s_t, ms[pp], pp, jk))
            return tuple(new)

        def two_steps(t2, ms):
            return step(2 * t2 + 2, step(2 * t2 + 1, ms, 1, 0), 0, 1)

        ms = tuple(jnp.full((1, 2 * tb), NEG_BIG, F32) for _ in range(pairs))
        ms = lax.cond(i >= 1, lambda v: step(0, v, 2, 1), lambda v: v, ms)
        ms = lax.fori_loop(0, lax.shift_right_logical(jnp.maximum(i - 1, 0), 1), two_steps, ms)
        ms = lax.cond((i >= 2) & (i % 2 == 0), lambda v: step(i - 1, v, 1, 0), lambda v: v, ms)

        last = jnp.where(i >= 1, i % 2, 2)
        diag = [jnp.where(causal, s_refs[pp][last], NEG_BIG) for pp in range(pairs)]
        first_logits(jnp.minimum(i + 1, nb - 1))
        for pp in range(pairs):
            update(diag[pp], ms[pp], pp, i)
            heads = []
            for hh in range(2):
                acc_t = acc_refs[pp][hh]
                inv = 1.0 / acc_t[ATT_HEAD_DIM:ATT_HEAD_DIM + 1, :]
                heads.append(acc_t[:ATT_HEAD_DIM, :] * inv)
            out_t = jnp.concatenate(heads, axis=0)
            o_ref[pl.ds(q0, tb), pp * LANES:(pp + 1) * LANES] = out_t.T.astype(o_ref.dtype)
        return 0

    first_logits(0)
    lax.fori_loop(0, nb, q_block, 0)


def _attention(q_t, k, v_t, qf_t, kf, batch, seq):
    t = k.shape[0]
    tb = min(ATT_BLOCK, seq)
    nb = seq // tb
    pairs = ATT_PAIRS_PER_STEP
    width = pairs * LANES
    return pl.pallas_call(
        functools.partial(_attn_kernel, tb=tb, pairs=pairs),
        grid=(batch, ATT_HEADS // (2 * pairs)),
        in_specs=[
            pl.BlockSpec((nb, width, tb), lambda b, p: (b, p, 0)),
            pl.BlockSpec((seq, width), lambda b, p: (b, p)),
            pl.BlockSpec((nb, width, tb), lambda b, p: (b, p, 0)),
            pl.BlockSpec((nb, LANES, tb), lambda b, p: (b, 0, 0)),
            pl.BlockSpec((seq, LANES), lambda b, p: (b, 0)),
        ],
        out_specs=pl.BlockSpec((seq, width), lambda b, p: (b, p)),
        out_shape=jax.ShapeDtypeStruct((t, ATT_INNER), BF16),
        scratch_shapes=([pltpu.VMEM((3, tb, 2 * tb), F32) for _ in range(pairs)]
                        + [pltpu.VMEM((2, ATT_HEAD_DIM + 2 * SUBLANES, tb), F32)
                           for _ in range(pairs)]),
        compiler_params=pltpu.CompilerParams(
            dimension_semantics=("arbitrary", "arbitrary"), vmem_limit_bytes=VMEM_LIMIT_BYTES),
        name="fox_attention",
    )(q_t, k, v_t, qf_t, kf)


def _layer_norm(u, g, b):
    mu = jnp.mean(u, axis=1, keepdims=True)
    d = u - mu
    var = jnp.mean(d * d, axis=1, keepdims=True)
    return d * lax.rsqrt(var + LN_EPS) * g + b


def _route(lg):
    rows = lg.shape[0]
    lt = lg.T
    gl = lt[0:N_EXPERT_GROUPS, :]
    g_row = lax.broadcasted_iota(jnp.int32, (N_EXPERT_GROUPS, 1), 0).astype(F32)
    gmax = jnp.max(gl, axis=0, keepdims=True)
    gsum = jnp.sum(jnp.exp(gl - gmax), axis=0, keepdims=True)
    g_val = 1.0 / gsum
    g_idx = jnp.min(jnp.where(gl == gmax, g_row, float(N_EXPERT_GROUPS)), axis=0, keepdims=True)
    el = lt[ROUTE_E_LANE:ROUTE_E_LANE + N_EXPERTS, :]
    e_row = lax.broadcasted_iota(jnp.int32, (N_EXPERTS, 1), 0)
    e_grp = (e_row // EXPERTS_PER_GROUP).astype(F32)
    e_row = e_row.astype(F32)
    in_grp = e_grp == g_idx
    m1 = jnp.max(jnp.where(in_grp, el, NEG_BIG), axis=0, keepdims=True)
    i1 = jnp.min(jnp.where(in_grp & (el == m1), e_row, float(N_EXPERTS)), axis=0, keepdims=True)
    rest = in_grp & (e_row != i1)
    m2 = jnp.max(jnp.where(rest, el, NEG_BIG), axis=0, keepdims=True)
    i2 = jnp.min(jnp.where(rest & (el == m2), e_row, float(N_EXPERTS)), axis=0, keepdims=True)
    r = jnp.exp(m2 - m1)
    w1 = 1.0 / (1.0 + r)
    w2 = r * w1
    comb_e = jnp.where(e_row == i1, g_val * w1, jnp.where(e_row == i2, g_val * w2, 0.0))
    sel_e = jnp.where((e_row == i1) | (e_row == i2), 1.0, 0.0)
    top = jnp.zeros((ROUTE_E_LANE, rows), F32)
    bottom = jnp.zeros((LANES - ROUTE_E_LANE - N_EXPERTS, rows), F32)
    comb = jnp.concatenate([top, comb_e, bottom], axis=0).T
    sel = jnp.concatenate([top, sel_e, bottom], axis=0).T
    return comb, sel


def _out_proj_kernel(ys_ref, ya_ref, x_ref, wa_ref, wb_ref, g_ref, b_ref, wr_ref, br_ref,
                     h_ref, hb_ref, comb_ref, sel_ref, cnt_ref):
    mix = jnp.dot(ys_ref[...], wa_ref[...], preferred_element_type=F32)
    mix = mix + jnp.dot(ya_ref[...], wb_ref[...], preferred_element_type=F32)
    h = _layer_norm(DEEPNORM_ALPHA * x_ref[...] + mix, g_ref[...], b_ref[...])
    h_ref[...] = h
    h_hi = h.astype(BF16)
    hb_ref[...] = h_hi
    h_lo = (h - h_hi.astype(F32)).astype(BF16)
    w_hi = wr_ref[:, :LANES]
    lg = (jnp.dot(h_hi, w_hi, preferred_element_type=F32)
          + jnp.dot(h_lo, w_hi, preferred_element_type=F32)
          + jnp.dot(h_hi, wr_ref[:, LANES:], preferred_element_type=F32)) + br_ref[...]
    comb, sel = _route(lg)
    comb_ref[...] = comb
    sel_ref[...] = sel
    ts = h.shape[0] // cnt_ref.shape[0]
    for c in range(cnt_ref.shape[0]):
        cnt = jnp.sum(sel[c * ts:(c + 1) * ts, :], axis=0, keepdims=True)
        cnt_ref[c] = jnp.broadcast_to(cnt, (SUBLANES, LANES))


def _out_proj(ys, ya, x2d, wo_a, wo_b, ln_g, ln_b, wr, br, ts):
    t = x2d.shape[0]
    tm = min(512, t)
    row = lambda i: (i, 0)
    const = lambda i: (0, 0)
    return pl.pallas_call(
        _out_proj_kernel,
        grid=(t // tm,),
        in_specs=[
            pl.BlockSpec((tm, SSD_INNER), row),
            pl.BlockSpec((tm, ATT_INNER), row),
            pl.BlockSpec((tm, D_MODEL), row),
            pl.BlockSpec((SSD_INNER, D_MODEL), const),
            pl.BlockSpec((ATT_INNER, D_MODEL), const),
            pl.BlockSpec((1, D_MODEL), const),
            pl.BlockSpec((1, D_MODEL), const),
            pl.BlockSpec((D_MODEL, 2 * LANES), const),
            pl.BlockSpec((1, LANES), const),
        ],
        out_specs=[
            pl.BlockSpec((tm, D_MODEL), row),
            pl.BlockSpec((tm, D_MODEL), row),
            pl.BlockSpec((tm, LANES), row),
            pl.BlockSpec((tm, LANES), row),
            pl.BlockSpec((tm // ts, SUBLANES, LANES), lambda i: (i, 0, 0)),
        ],
        out_shape=(
            jax.ShapeDtypeStruct((t, D_MODEL), F32),
            jax.ShapeDtypeStruct((t, D_MODEL), BF16),
            jax.ShapeDtypeStruct((t, LANES), F32),
            jax.ShapeDtypeStruct((t, LANES), F32),
            jax.ShapeDtypeStruct((t // ts, SUBLANES, LANES), F32),
        ),
        compiler_params=pltpu.CompilerParams(
            dimension_semantics=("arbitrary",), vmem_limit_bytes=VMEM_LIMIT_BYTES),
        name="out_proj_ln_router",
    )(ys, ya, x2d, wo_a, wo_b, ln_g, ln_b, wr, br)


def _dispatch_plan(cnt, ts):
    nt = cnt.shape[0]
    c16 = (cnt + (MOE_GRANULE - 1)) // MOE_GRANULE * MOE_GRANULE
    loc_off = jnp.cumsum(c16, axis=1) - c16
    tot = jnp.sum(c16, axis=0)
    tiles = (tot + (MOE_ROW_TILE - 1)) // MOE_ROW_TILE
    cum_tiles = jnp.cumsum(tiles)
    start = (cum_tiles - tiles) * MOE_ROW_TILE
    base = start[None, :] + jnp.cumsum(c16, axis=0) - c16
    max_tiles = _moe_rows(nt * ts, ts) // MOE_ROW_TILE
    tile_expert = jnp.minimum(
        jnp.sum(jnp.arange(max_tiles, dtype=jnp.int32)[:, None] >= cum_tiles[None, :], axis=1),
        N_EXPERTS - 1).astype(jnp.int32)
    loc_vec = jnp.zeros((nt, SUBLANES, LANES), F32).at[:, :, ROUTE_E_LANE:ROUTE_E_LANE + N_EXPERTS].set(
        jnp.broadcast_to(loc_off.astype(F32)[:, None, :], (nt, SUBLANES, N_EXPERTS)))
    return dict(
        chunks=(c16 // MOE_GRANULE).reshape(-1).astype(jnp.int32),
        loc_off=loc_off.reshape(-1).astype(jnp.int32),
        base=base.reshape(-1).astype(jnp.int32),
        step_chunks=(jnp.sum(c16, axis=1).reshape(-1, _moe_group(nt * ts, ts)).sum(axis=1)
                     // MOE_GRANULE).astype(jnp.int32),
        pad_start=(start + tot).astype(jnp.int32),
        pad_chunks=((tiles * MOE_ROW_TILE - tot) // MOE_GRANULE).astype(jnp.int32),
        n_tiles=cum_tiles[-1:].astype(jnp.int32),
        tile_expert=tile_expert,
        loc_vec=loc_vec,
    )


def _moe_group(t, ts):
    return MOE_TILES_PER_STEP if t % (MOE_TILES_PER_STEP * ts) == 0 else 1


def _moe_local_rows(ts):
    rows = 2 * ts + N_EXPERTS * (MOE_GRANULE - 1)
    return (rows + LANES - 1) // LANES * LANES


def _moe_rows(t, ts):
    rows = 2 * t + (t // ts) * N_EXPERTS * (MOE_GRANULE - 1) + N_EXPERTS * MOE_ROW_TILE
    return (rows + MOE_ROW_TILE - 1) // MOE_ROW_TILE * MOE_ROW_TILE


def _chunk_copy(src, dst, sem):
    return pltpu.make_async_copy(src, dst, sem)


def _wait_rows(copy_of_rows, n_chunks):
    def wait(rows):
        def one(_, c):
            copy_of_rows(rows).wait()
            return c
        return one
    shift = MOE_WAIT_BATCH.bit_length() - 1
    lax.fori_loop(0, lax.shift_right_logical(n_chunks, shift), wait(MOE_WAIT_BATCH * MOE_GRANULE), 0)
    lax.fori_loop(0, n_chunks & (MOE_WAIT_BATCH - 1), wait(MOE_GRANULE), 0)


def _dispatch_kernel(chunks_ref, loc_ref, base_ref, schunks_ref, pstart_ref, pchunks_ref,
                     hb_ref, sel_ref, comb_ref, locv_ref, lstrict_ref,
                     xs_ref, posw_ref, buf_ref, zero_ref, sem_ref, zsem_ref):
    step = pl.program_id(0)
    n_steps = pl.num_programs(0)
    par = step % 2
    group = locv_ref.shape[0]
    ts = hb_ref.shape[0] // group
    nl = buf_ref.shape[1]
    g = MOE_GRANULE

    def wait_chunks(n, pp):
        _wait_rows(lambda r: _chunk_copy(buf_ref.at[0, pl.ds(0, r)], xs_ref.at[pl.ds(0, r)],
                                         sem_ref.at[pp]), n)

    @pl.when(step >= 2)
    def _():
        wait_chunks(schunks_ref[step - 2], par)

    lane = lax.broadcasted_iota(jnp.int32, (1, LANES), 1)
    d = lax.broadcasted_iota(jnp.int32, (nl, ts), 0).astype(F32)
    for u in range(group):
        rows = slice(u * ts, (u + 1) * ts)
        sel = sel_ref[rows, :]
        comb = comb_ref[rows, :]
        picked = sel > 0.0
        rank = jnp.dot(lstrict_ref[...], sel.astype(BF16), preferred_element_type=F32)
        dest = rank + locv_ref[u][0:1, :]
        pos_a = jnp.min(jnp.where(picked, dest, 1e9), axis=1, keepdims=True)
        pos_b = jnp.max(jnp.where(picked, dest, -1.0), axis=1, keepdims=True)
        w_a = jnp.sum(jnp.where(picked & (dest == pos_a), comb, 0.0), axis=1, keepdims=True)
        w_b = jnp.sum(jnp.where(picked & (dest == pos_b), comb, 0.0), axis=1, keepdims=True)
        posw = jnp.where(lane == 0, pos_a, jnp.where(lane == 1, pos_b,
                         jnp.where(lane == 2, w_a, jnp.where(lane == 3, w_b, 0.0))))
        posw_ref[rows, :] = posw
        posw_t = posw.T
        perm = jnp.where((d == posw_t[0:1, :]) | (d == posw_t[1:2, :]), 1.0, 0.0).astype(BF16)
        buf_ref[par * group + u] = jnp.dot(perm, hb_ref[rows, :],
                                           preferred_element_type=F32).astype(BF16)

    for u in range(group):
        for e in range(N_EXPERTS):
            idx = (step * group + u) * N_EXPERTS + e
            lo = loc_ref[idx]
            bs = base_ref[idx]

            def send(k, c, lo=lo, bs=bs, u=u):
                src = buf_ref.at[par * group + u, pl.ds(pl.multiple_of(lo + k * g, g), g)]
                dst = xs_ref.at[pl.ds(pl.multiple_of(bs + k * g, g), g)]
                _chunk_copy(src, dst, sem_ref.at[par]).start()
                return c
            lax.fori_loop(0, chunks_ref[idx], send, 0)

    @pl.when(step == n_steps - 1)
    def _():
        zero_ref[...] = jnp.zeros_like(zero_ref)
        for e in range(N_EXPERTS):
            ps = pstart_ref[e]

            def send_zero(k, c, ps=ps):
                dst = xs_ref.at[pl.ds(pl.multiple_of(ps + k * g, g), g)]
                _chunk_copy(zero_ref, dst, zsem_ref.at[0]).start()
                return c
            lax.fori_loop(0, pchunks_ref[e], send_zero, 0)
        for e in range(N_EXPERTS):
            def wait_zero(_, c):
                _chunk_copy(zero_ref, xs_ref.at[pl.ds(0, g)], zsem_ref.at[0]).wait()
                return c
            lax.fori_loop(0, pchunks_ref[e], wait_zero, 0)
        wait_chunks(schunks_ref[step], par)

        @pl.when(step >= 1)
        def _():
            wait_chunks(schunks_ref[step - 1], 1 - par)


def _dispatch(plan, hb, sel, comb, ts):
    t = hb.shape[0]
    group = _moe_group(t, ts)
    nl = _moe_local_rows(ts)
    rows = _moe_rows(t, ts)
    lstrict = jnp.tri(ts, k=-1, dtype=BF16)
    row = lambda i, *_: (i, 0)
    grid_spec = pltpu.PrefetchScalarGridSpec(
        num_scalar_prefetch=6,
        grid=(t // (group * ts),),
        in_specs=[
            pl.BlockSpec((group * ts, D_MODEL), row),
            pl.BlockSpec((group * ts, LANES), row),
            pl.BlockSpec((group * ts, LANES), row),
            pl.BlockSpec((group, SUBLANES, LANES), lambda i, *_: (i, 0, 0)),
            pl.BlockSpec((ts, ts), lambda i, *_: (0, 0)),
        ],
        out_specs=[
            pl.BlockSpec(memory_space=pl.ANY),
            pl.BlockSpec((group * ts, LANES), row),
        ],
        scratch_shapes=[
            pltpu.VMEM((2 * group, nl, D_MODEL), BF16),
            pltpu.VMEM((MOE_GRANULE, D_MODEL), BF16),
            pltpu.SemaphoreType.DMA((2,)),
            pltpu.SemaphoreType.DMA((1,)),
        ],
    )
    return pl.pallas_call(
        _dispatch_kernel,
        grid_spec=grid_spec,
        out_shape=(
            jax.ShapeDtypeStruct((rows, D_MODEL), BF16),
            jax.ShapeDtypeStruct((t, LANES), F32),
        ),
        compiler_params=pltpu.CompilerParams(
            dimension_semantics=("arbitrary",), vmem_limit_bytes=VMEM_LIMIT_BYTES),
        name="moe_dispatch",
    )(plan["chunks"], plan["loc_off"], plan["base"], plan["step_chunks"], plan["pad_start"],
      plan["pad_chunks"], hb, sel, comb, plan["loc_vec"], lstrict)


def _expert_kernel(te_ref, nt_ref, x_ref, wg_ref, wu_ref, wd_ref, y_ref, wgb_ref, wub_ref, wdb_ref):
    j = pl.program_id(0)

    @pl.when(j < nt_ref[0])
    def _():
        @pl.when((j == 0) | (te_ref[j] != te_ref[jnp.maximum(j - 1, 0)]))
        def _():
            wgb_ref[...] = wg_ref[0].astype(BF16)
            wub_ref[...] = wu_ref[0].astype(BF16)
            wdb_ref[...] = wd_ref[0].astype(BF16)

        x = x_ref[...]
        gate = jnp.dot(x, wgb_ref[...], preferred_element_type=F32)
        up = jnp.dot(x, wub_ref[...], preferred_element_type=F32)
        act = (_silu(gate) * up).astype(BF16)
        y_ref[...] = jnp.dot(act, wdb_ref[...], preferred_element_type=F32).astype(y_ref.dtype)


def _experts(plan, xs, w_gate, w_up, w_down):
    rows = xs.shape[0]
    live = lambda j, te, nt: jnp.minimum(j, nt[0] - 1)
    expert = lambda j, te, nt: (te[live(j, te, nt)], 0, 0)
    grid_spec = pltpu.PrefetchScalarGridSpec(
        num_scalar_prefetch=2,
        grid=(rows // MOE_ROW_TILE,),
        in_specs=[
            pl.BlockSpec((MOE_ROW_TILE, D_MODEL), lambda j, te, nt: (live(j, te, nt), 0)),
            pl.BlockSpec((1, D_MODEL, D_FF_EXPERT), expert),
            pl.BlockSpec((1, D_MODEL, D_FF_EXPERT), expert),
            pl.BlockSpec((1, D_FF_EXPERT, D_MODEL), expert),
        ],
        out_specs=pl.BlockSpec((MOE_ROW_TILE, D_MODEL), lambda j, te, nt: (live(j, te, nt), 0)),
        scratch_shapes=[
            pltpu.VMEM((D_MODEL, D_FF_EXPERT), BF16),
            pltpu.VMEM((D_MODEL, D_FF_EXPERT), BF16),
            pltpu.VMEM((D_FF_EXPERT, D_MODEL), BF16),
        ],
    )
    return pl.pallas_call(
        _expert_kernel,
        grid_spec=grid_spec,
        out_shape=jax.ShapeDtypeStruct((rows, D_MODEL), BF16),
        compiler_params=pltpu.CompilerParams(
            dimension_semantics=("arbitrary",), vmem_limit_bytes=VMEM_LIMIT_BYTES),
        name="moe_experts",
    )(plan["tile_expert"], plan["n_tiles"], xs, w_gate, w_up, w_down)


def _combine_kernel(chunks_ref, loc_ref, base_ref, schunks_ref,
                    y_ref, posw_ref, h_ref, g_ref, b_ref, o_ref, buf_ref, sem_ref):
    step = pl.program_id(0)
    n_steps = pl.num_programs(0)
    par = step % 2
    nl = buf_ref.shape[1]
    group = buf_ref.shape[0] // 2
    ts = h_ref.shape[0] // group
    g = MOE_GRANULE

    def fetch(ss, pp):
        for u in range(group):
            for e in range(N_EXPERTS):
                idx = (ss * group + u) * N_EXPERTS + e
                lo = loc_ref[idx]
                bs = base_ref[idx]

                def get(k, c, lo=lo, bs=bs, u=u):
                    src = y_ref.at[pl.ds(pl.multiple_of(bs + k * g, g), g)]
                    dst = buf_ref.at[pp * group + u, pl.ds(pl.multiple_of(lo + k * g, g), g)]
                    _chunk_copy(src, dst, sem_ref.at[pp]).start()
                    return c
                lax.fori_loop(0, chunks_ref[idx], get, 0)

    @pl.when(step == 0)
    def _():
        buf_ref[...] = jnp.zeros(buf_ref.shape, BF16)
        fetch(0, 0)

    @pl.when(step + 1 < n_steps)
    def _():
        fetch(step + 1, 1 - par)

    _wait_rows(lambda r: _chunk_copy(y_ref.at[pl.ds(0, r)], buf_ref.at[0, pl.ds(0, r)],
                                     sem_ref.at[par]), schunks_ref[step])

    d = lax.broadcasted_iota(jnp.int32, (1, nl), 1).astype(F32)
    for u in range(group):
        rows = slice(u * ts, (u + 1) * ts)
        yb = buf_ref[par * group + u]
        posw = posw_ref[rows, :]
        ffn = jnp.zeros((ts, D_MODEL), F32)
        for k in range(2):
            gather = jnp.where(posw[:, k:k + 1] == d, 1.0, 0.0).astype(BF16)
            ffn = ffn + posw[:, 2 + k:3 + k] * jnp.dot(gather, yb, preferred_element_type=F32)
        o_ref[rows, :] = _layer_norm(DEEPNORM_ALPHA * h_ref[rows, :] + ffn, g_ref[...], b_ref[...])


def _combine(plan, y, posw, h, ln_g, ln_b, ts):
    t = h.shape[0]
    group = _moe_group(t, ts)
    nl = _moe_local_rows(ts)
    row = lambda i, *_: (i, 0)
    const = lambda i, *_: (0, 0)
    grid_spec = pltpu.PrefetchScalarGridSpec(
        num_scalar_prefetch=4,
        grid=(t // (group * ts),),
        in_specs=[
            pl.BlockSpec(memory_space=pl.ANY),
            pl.BlockSpec((group * ts, LANES), row),
            pl.BlockSpec((group * ts, D_MODEL), row),
            pl.BlockSpec((1, D_MODEL), const),
            pl.BlockSpec((1, D_MODEL), const),
        ],
        out_specs=pl.BlockSpec((group * ts, D_MODEL), row),
        scratch_shapes=[
            pltpu.VMEM((2 * group, nl, D_MODEL), BF16),
            pltpu.SemaphoreType.DMA((2,)),
        ],
    )
    return pl.pallas_call(
        _combine_kernel,
        grid_spec=grid_spec,
        out_shape=jax.ShapeDtypeStruct((t, D_MODEL), F32),
        compiler_params=pltpu.CompilerParams(
            dimension_semantics=("arbitrary",), vmem_limit_bytes=VMEM_LIMIT_BYTES),
        name="moe_combine_ln",
    )(plan["chunks"], plan["loc_off"], plan["base"], plan["step_chunks"], y, posw, h, ln_g, ln_b)


def _pad_lanes(a, n=LANES):
    return jnp.pad(a, ((0, 0), (0, n - a.shape[1])))


def _layer(h2d, batch, seq, w_in, b_in, conv_w, conv_b, a_log, d_skip, ssd_norm_g, w_out,
           ln1_g, ln1_b, rg_w, rg_b, re_w, re_b, w_gate, w_up, w_down, ln2_g, ln2_b):
    o_dt = SSD_INNER + SSD_CONV_DIM
    o_q = o_dt + SSD_HEADS
    o_f = o_q + 3 * ATT_INNER
    w_main = jnp.concatenate([w_in[:, :o_dt], w_in[:, o_q:o_f]], axis=1).astype(BF16)
    b_main = jnp.concatenate([b_in[:o_dt], b_in[o_q:o_f]])[None, :]
    w_dt, w_f = w_in[:, o_dt:o_q], w_in[:, o_f:]
    w_small = _pad_lanes(jnp.concatenate([w_dt, w_f, w_dt], axis=1)).astype(BF16)
    b_small = _pad_lanes(jnp.concatenate([b_in[o_dt:o_q], b_in[o_f:], b_in[o_dt:o_q]])[None, :])
    a_neg = -jnp.exp(a_log.astype(F32))
    a_row = _pad_lanes(a_neg[None, :])
    a_exp = jnp.repeat(a_neg, SSD_HEAD_DIM)[None, :]
    dskip_exp = jnp.repeat(d_skip.astype(F32), SSD_HEAD_DIM)[None, :]

    z, xbc, q_t, k, v_t, gate, qf_t, kf = _in_proj(h2d, w_main, b_main, w_small, b_small, a_row, seq)
    y_ssd = _ssd(xbc, z, gate, conv_w, conv_b[None, :], a_exp, dskip_exp, ssd_norm_g[None, :],
                 batch, seq)
    y_att = _attention(q_t, k, v_t, qf_t, kf, batch, seq)

    gap = ROUTE_E_LANE - N_EXPERT_GROUPS
    w_r = _pad_lanes(jnp.concatenate(
        [rg_w, jnp.zeros((D_MODEL, gap), F32),
         jnp.transpose(re_w, (1, 0, 2)).reshape(D_MODEL, N_EXPERTS)], axis=1))
    b_r = _pad_lanes(jnp.concatenate(
        [rg_b, jnp.zeros((gap,), F32), re_b.reshape(N_EXPERTS)])[None, :])
    wr_hi = w_r.astype(BF16)
    wr_lo = (w_r - wr_hi.astype(F32)).astype(BF16)
    wo = w_out.astype(BF16)
    ts = min(MOE_SORT_TILE, h2d.shape[0])
    h1, h1b, comb, sel, cnt = _out_proj(y_ssd, y_att, h2d, wo[:SSD_INNER], wo[SSD_INNER:],
                                        ln1_g[None, :], ln1_b[None, :],
                                        jnp.concatenate([wr_hi, wr_lo], axis=1), b_r, ts)

    plan = _dispatch_plan(
        cnt[:, 0, ROUTE_E_LANE:ROUTE_E_LANE + N_EXPERTS].astype(jnp.int32), ts)
    xs, posw = _dispatch(plan, h1b, sel, comb, ts)
    y = _experts(plan, xs, w_gate, w_up, w_down)
    return _combine(plan, y, posw, h1, ln2_g[None, :], ln2_b[None, :], ts)


def kernel(x, w_in, b_in, conv_w, conv_b, a_log, d_skip, ssd_norm_g, w_out, ln1_g, ln1_b,
           router_group_w, router_group_b, router_expert_w, router_expert_b, w_gate, w_up,
           w_down, ln2_g, ln2_b):
    batch, seq, d = x.shape
    h = x.reshape(batch * seq, d)
    for l in range(w_in.shape[0]):
        h = _layer(h, batch, seq, w_in[l], b_in[l], conv_w[l], conv_b[l], a_log[l], d_skip[l],
                   ssd_norm_g[l], w_out[l], ln1_g[l], ln1_b[l], router_group_w[l],
                   router_group_b[l], router_expert_w[l], router_expert_b[l], w_gate[l],
                   w_up[l], w_down[l], ln2_g[l], ln2_b[l])
    return h.reshape(batch, seq, d)
```

```python
import functools

import jax
import jax.numpy as jnp
import numpy as np
from jax import lax
from jax.experimental import pallas as pl
from jax.experimental.pallas import tpu as pltpu

F32 = jnp.float32
BF16 = jnp.bfloat16

D_MODEL = 1024
SSD_HEADS = 8
SSD_HEAD_DIM = 64
SSD_INNER = SSD_HEADS * SSD_HEAD_DIM
SSD_GROUPS = 2
SSD_STATE = 128
CONV_WIDTH = 4
SSD_CONV_DIM = SSD_INNER + 2 * SSD_GROUPS * SSD_STATE
ATT_HEADS = 8
ATT_HEAD_DIM = 64
ATT_INNER = ATT_HEADS * ATT_HEAD_DIM
N_EXPERT_GROUPS = 4
EXPERTS_PER_GROUP = 4
N_EXPERTS = N_EXPERT_GROUPS * EXPERTS_PER_GROUP
D_FF_EXPERT = 512
DEPTH = 1
DEEPNORM_ALPHA = (2.0 * DEPTH) ** 0.25
LN_EPS = 1e-5
RMS_EPS = 1e-5

LANES = 128
SUBLANES = 8
VMEM_LIMIT_BYTES = 56 * 1024 * 1024

GATE_A_LANE = 0
GATE_F_LANE = SSD_HEADS
GATE_DT_LANE = 2 * SSD_HEADS
ROUTE_E_LANE = SUBLANES
MOE_SORT_TILE = 256
MOE_GRANULE = 16
MOE_ROW_TILE = 1024
MOE_TILES_PER_STEP = 4
MOE_WAIT_BATCH = 8

IN_PROJ_ROWS = 1024
OUT_PROJ_ROWS = 1024
CUMSUM_BLOCK = 256
FB_LANES = 6
ATT_BLOCK = 256
ATT_PAIRS_PER_STEP = 4

SSD_ROWS = 256
SSD_STEP_ROWS = 512
SSD_CHUNK = 64
NEG_BIG = -1e30
LOG2E = 1.4426950408889634


def _split3(v):
    hi = v.astype(BF16)
    r1 = v - hi.astype(F32)
    mid = r1.astype(BF16)
    lo = (r1 - mid.astype(F32)).astype(BF16)
    return hi, mid, lo


def _silu(v):
    half = 0.5 * v
    return half * (1.0 + jnp.tanh(half))


def _softplus(v):
    return jnp.maximum(v, 0.0) + jnp.log1p(jnp.exp(-jnp.abs(v)))


def _in_proj_kernel(x_ref, wm_ref, bm_ref, ws_ref, bs_ref, arow_ref, tri_ref, place_ref, ones_ref,
                    z_ref, xbc_ref, qt_ref, k_ref, vt_ref, gate_ref, qft_ref, kf_ref,
                    carry_ref, *, tiles_per_seq):
    i = pl.program_id(0)
    tm = x_ref.shape[0]
    tb = qt_ref.shape[2]
    xb = x_ref[...].astype(BF16)

    def put_transposed(ref, val):
        for c in range(tm // tb):
            ref[c] = val[c * tb:(c + 1) * tb, :].T.astype(ref.dtype)

    s = jnp.dot(xb, ws_ref[...], preferred_element_type=F32) + bs_ref[...]
    lane = lax.broadcasted_iota(jnp.int32, (1, LANES), 1)
    sp = _softplus(s)
    log_f = -_softplus(-s)
    val = jnp.where(lane < GATE_F_LANE, sp * arow_ref[...],
                    jnp.where(lane < GATE_DT_LANE, log_f,
                              jnp.where(lane < GATE_DT_LANE + SSD_HEADS, sp, 0.0)))
    carry = jnp.where(i % tiles_per_seq == 0, 0.0, carry_ref[...])
    tri = tri_ref[...]
    cb = tri.shape[0]
    sums = []
    for c in range(tm // cb):
        hi, mid, lo = _split3(val[c * cb:(c + 1) * cb, :])
        part = (jnp.dot(tri, hi, preferred_element_type=F32)
                + jnp.dot(tri, mid, preferred_element_type=F32)
                + jnp.dot(tri, lo, preferred_element_type=F32)) + carry
        carry = part[cb - 1:cb, :]
        sums.append(part)
    cs = jnp.concatenate(sums, axis=0)
    carry_ref[...] = carry
    out = jnp.where(lane < GATE_DT_LANE, cs, val)
    gate_ref[...] = out
    pieces = jnp.concatenate(_split3(cs * LOG2E), axis=1)
    fb = jnp.dot(pieces, place_ref[...], preferred_element_type=F32) + ones_ref[...]
    put_transposed(qft_ref, fb[:, :LANES])
    kf_ref[...] = fb[:, LANES:].astype(BF16)

    col = 0
    for ref, scale in ((z_ref, None), (xbc_ref, None), (qt_ref, ATT_HEAD_DIM ** -0.5 * LOG2E),
                       (k_ref, None), (vt_ref, None)):
        transposed = ref is qt_ref or ref is vt_ref
        n = ATT_INNER if transposed else ref.shape[1]
        acc = jnp.dot(xb, wm_ref[:, col:col + n], preferred_element_type=F32) + bm_ref[:, col:col + n]
        if scale is not None:
            acc = acc * scale
        if transposed:
            put_transposed(ref, acc)
        else:
            ref[...] = acc.astype(ref.dtype)
        col += n


def _forget_bias_placement():
    place = np.zeros((3 * LANES, 2 * LANES), np.float32)
    ones = np.zeros((1, 2 * LANES), np.float32)
    for h in range(ATT_HEADS):
        for piece in range(3):
            src = piece * LANES + GATE_F_LANE + h
            place[src, FB_LANES * h + piece] = 1.0
            place[src, LANES + FB_LANES * h + 3 + piece] = -1.0
            ones[0, FB_LANES * h + 3 + piece] = 1.0
            ones[0, LANES + FB_LANES * h + piece] = 1.0
    return jnp.asarray(place, BF16), jnp.asarray(ones, F32)


def _in_proj(x2d, w_main, b_main, w_small, b_small, a_row, seq):
    t = x2d.shape[0]
    tm = min(IN_PROJ_ROWS, seq)
    tk = min(ATT_BLOCK, seq)
    cum = min(CUMSUM_BLOCK, tm)
    tri = jnp.tri(cum, dtype=BF16)
    place, ones = _forget_bias_placement()
    n_main = w_main.shape[1]
    row = lambda i: (i, 0)
    const = lambda i: (0, 0)
    outs = (
        jax.ShapeDtypeStruct((t, SSD_INNER), BF16),
        jax.ShapeDtypeStruct((t, SSD_CONV_DIM), BF16),
        jax.ShapeDtypeStruct((t // tk, ATT_INNER, tk), BF16),
        jax.ShapeDtypeStruct((t, ATT_INNER), BF16),
        jax.ShapeDtypeStruct((t // tk, ATT_INNER, tk), BF16),
        jax.ShapeDtypeStruct((t, LANES), F32),
        jax.ShapeDtypeStruct((t // tk, LANES, tk), BF16),
        jax.ShapeDtypeStruct((t, LANES), BF16),
    )
    return pl.pallas_call(
        functools.partial(_in_proj_kernel, tiles_per_seq=seq // tm),
        grid=(t // tm,),
        in_specs=[
            pl.BlockSpec((tm, D_MODEL), row),
            pl.BlockSpec((D_MODEL, n_main), const),
            pl.BlockSpec((1, n_main), const),
            pl.BlockSpec((D_MODEL, LANES), const),
            pl.BlockSpec((1, LANES), const),
            pl.BlockSpec((1, LANES), const),
            pl.BlockSpec((cum, cum), const),
            pl.BlockSpec((3 * LANES, 2 * LANES), const),
            pl.BlockSpec((1, 2 * LANES), const),
        ],
        out_specs=[
            pl.BlockSpec((tm, SSD_INNER), row),
            pl.BlockSpec((tm, SSD_CONV_DIM), row),
            pl.BlockSpec((tm // tk, ATT_INNER, tk), lambda i: (i, 0, 0)),
            pl.BlockSpec((tm, ATT_INNER), row),
            pl.BlockSpec((tm // tk, ATT_INNER, tk), lambda i: (i, 0, 0)),
            pl.BlockSpec((tm, LANES), row),
            pl.BlockSpec((tm // tk, LANES, tk), lambda i: (i, 0, 0)),
            pl.BlockSpec((tm, LANES), row),
        ],
        out_shape=outs,
        scratch_shapes=[pltpu.VMEM((1, LANES), F32)],
        compiler_params=pltpu.CompilerParams(
            dimension_semantics=("arbitrary",), vmem_limit_bytes=VMEM_LIMIT_BYTES),
        name="in_proj",
    )(x2d, w_main, b_main, w_small, b_small, a_row, tri, place, ones)


def _expand_heads(gate, lane0, rows):
    lane = lax.broadcasted_iota(jnp.int32, (1, LANES), 1)
    pieces = []
    for p in range(SSD_HEADS // 2):
        a = jnp.broadcast_to(gate[:, lane0 + 2 * p:lane0 + 2 * p + 1], (rows, LANES))
        b = jnp.broadcast_to(gate[:, lane0 + 2 * p + 1:lane0 + 2 * p + 2], (rows, LANES))
        pieces.append(jnp.where(lane < SSD_HEAD_DIM, a, b))
    return jnp.concatenate(pieces, axis=1)


def _ssd_kernel(xbc_ref, z_ref, gate_ref, cw_ref, cb_ref, aexp_ref, dskip_ref, ng_ref, shift_ref,
                y_ref, ext_ref, state_ref):
    j = pl.program_id(1)
    rows = shift_ref.shape[1]
    q = SSD_CHUNK
    gw = SSD_INNER // SSD_GROUPS
    hpg = SSD_HEADS // SSD_GROUPS
    pad = SUBLANES

    @pl.when(j == 0)
    def _():
        ext_ref[0:pad, :] = jnp.zeros((pad, SSD_CONV_DIM), F32)
        state_ref[...] = jnp.zeros_like(state_ref)

    r_i = lax.broadcasted_iota(jnp.int32, (q, gw), 0)
    c_i = lax.broadcasted_iota(jnp.int32, (q, gw), 1)
    s_i = c_i % q
    eye_t = s_i == r_i
    tril_t = s_i <= r_i
    rb = lax.broadcasted_iota(jnp.int32, (hpg * q, gw), 0) // q
    cbk = lax.broadcasted_iota(jnp.int32, (hpg * q, gw), 1) // SSD_HEAD_DIM
    blk = rb == cbk

    for t in range(xbc_ref.shape[0] // rows):
        sub = slice(t * rows, (t + 1) * rows)
        xb = xbc_ref[sub, :]
        xf = xb.astype(F32)
        conv = cb_ref[...] + xf * cw_ref[CONV_WIDTH - 1:CONV_WIDTH, :]
        for s in range(1, CONV_WIDTH):
            shifted = jnp.dot(shift_ref[s - 1], xb, preferred_element_type=F32)
            conv = conv + shifted * cw_ref[CONV_WIDTH - 1 - s:CONV_WIDTH - s, :]
        ext_ref[pad:2 * pad, :] = xf[0:pad, :]
        head = cb_ref[...]
        for kk in range(CONV_WIDTH):
            off = pad - (CONV_WIDTH - 1) + kk
            head = head + ext_ref[off:off + pad, :] * cw_ref[kk:kk + 1, :]
        ext_ref[0:pad, :] = xf[rows - pad:rows, :]
        u = _silu(jnp.concatenate([head, conv[pad:, :]], axis=0))
        xs = u[:, :SSD_INNER]
        b_all = u[:, SSD_INNER:SSD_INNER + SSD_GROUPS * SSD_STATE].astype(BF16)
        c_all = u[:, SSD_INNER + SSD_GROUPS * SSD_STATE:].astype(BF16)

        gate = gate_ref[sub, :]
        dt_e = _expand_heads(gate, GATE_DT_LANE, rows)
        a_e = _expand_heads(gate, GATE_A_LANE, rows)
        xdt = xs * dt_e
        da_e = dt_e * aexp_ref[...]

        ys = []
        for c in range(rows // q):
            r0 = c * q
            a_c = a_e[r0:r0 + q, :]
            a_prev = a_c[0:1, :] - da_e[r0:r0 + 1, :]
            a_end = a_c[q - 1:q, :]
            y_groups = []
            for g in range(SSD_GROUPS):
                l0 = g * gw
                a_cg = a_c[:, l0:l0 + gw]
                diag = jnp.sum(jnp.where(eye_t, a_cg, 0.0), axis=0, keepdims=True)
                seg = jnp.minimum(a_cg - diag, 0.0)
                lcat = jnp.where(tril_t, jnp.exp(seg), 0.0)
                bg = b_all[r0:r0 + q, g * SSD_STATE:(g + 1) * SSD_STATE]
                cg = c_all[r0:r0 + q, g * SSD_STATE:(g + 1) * SSD_STATE]
                b_tile = jnp.concatenate([bg] * hpg, axis=0)
                gcat = lax.dot_general(cg, b_tile, (((1,), (1,)), ((), ())),
                                       preferred_element_type=F32)
                xg = xdt[r0:r0 + q, l0:l0 + gw]
                xg_b = xg.astype(BF16)
                xbd = jnp.where(blk, jnp.concatenate([xg_b] * hpg, axis=0), jnp.zeros((), BF16))
                y_diag = jnp.dot((gcat * lcat).astype(BF16), xbd, preferred_element_type=F32)
                st = state_ref[g]
                y_off = jnp.dot(cg, st.astype(BF16), preferred_element_type=F32) \
                    * jnp.exp(a_cg - a_prev[:, l0:l0 + gw])
                dte = jnp.exp(a_end[:, l0:l0 + gw] - a_cg)
                upd = lax.dot_general(bg, (xg * dte).astype(BF16), (((0,), (0,)), ((), ())),
                                      preferred_element_type=F32)
                state_ref[g] = st * jnp.exp(a_end[:, l0:l0 + gw] - a_prev[:, l0:l0 + gw]) + upd
                y_groups.append(y_diag + y_off)
            ys.append(jnp.concatenate(y_groups, axis=1))
        y = jnp.concatenate(ys, axis=0) + xs * dskip_ref[...]
        y = y * _silu(z_ref[sub, :].astype(F32))
        outs = []
        for g in range(SSD_GROUPS):
            yg = y[:, g * gw:(g + 1) * gw]
            ms = jnp.mean(yg * yg, axis=1, keepdims=True)
            outs.append(yg * lax.rsqrt(ms + RMS_EPS))
        y_ref[sub, :] = (jnp.concatenate(outs, axis=1) * ng_ref[...]).astype(y_ref.dtype)


def _ssd(xbc, z, gate, conv_w, conv_b, a_exp, dskip_exp, norm_g, batch, seq):
    t = xbc.shape[0]
    rows = min(SSD_ROWS, seq)
    step_rows = min(SSD_STEP_ROWS, seq)
    nj = seq // step_rows
    row = lambda b, j: (b * nj + j, 0)
    const = lambda b, j: (0, 0)
    shifts = jnp.stack([jnp.eye(rows, k=-s, dtype=BF16) for s in range(1, CONV_WIDTH)])
    return pl.pallas_call(
        _ssd_kernel,
        grid=(batch, nj),
        in_specs=[
            pl.BlockSpec((step_rows, SSD_CONV_DIM), row),
            pl.BlockSpec((step_rows, SSD_INNER), row),
            pl.BlockSpec((step_rows, LANES), row),
            pl.BlockSpec((CONV_WIDTH, SSD_CONV_DIM), const),
            pl.BlockSpec((1, SSD_CONV_DIM), const),
            pl.BlockSpec((1, SSD_INNER), const),
            pl.BlockSpec((1, SSD_INNER), const),
            pl.BlockSpec((1, SSD_INNER), const),
            pl.BlockSpec((CONV_WIDTH - 1, rows, rows), lambda b, j: (0, 0, 0)),
        ],
        out_specs=pl.BlockSpec((step_rows, SSD_INNER), row),
        out_shape=jax.ShapeDtypeStruct((t, SSD_INNER), BF16),
        scratch_shapes=[
            pltpu.VMEM((2 * SUBLANES, SSD_CONV_DIM), F32),
            pltpu.VMEM((SSD_GROUPS, SSD_STATE, SSD_INNER // SSD_GROUPS), F32),
        ],
        compiler_params=pltpu.CompilerParams(
            dimension_semantics=("arbitrary", "arbitrary"), vmem_limit_bytes=VMEM_LIMIT_BYTES),
        name="ssd",
    )(xbc, z, gate, conv_w, conv_b, a_exp, dskip_exp, norm_g, shifts)


def _attn_kernel(qt_ref, k_ref, vt_ref, qft_ref, kf_ref, o_ref, *scratch, tb, pairs):
    pair0 = pl.program_id(1) * pairs
    nb = qt_ref.shape[0]
    s_refs, acc_refs = scratch[:pairs], scratch[pairs:]
    feat = lax.broadcasted_iota(jnp.int32, (LANES, 1), 0)
    in_head = (feat < ATT_HEAD_DIM, feat >= ATT_HEAD_DIM)
    ones = jnp.ones((2 * SUBLANES, tb), BF16)
    key = lax.broadcasted_iota(jnp.int32, (tb, 2 * tb), 0)
    qry = lax.broadcasted_iota(jnp.int32, (tb, 2 * tb), 1)
    causal = jnp.where(qry >= tb, qry - tb, qry) >= key
    zero = jnp.zeros((), BF16)

    def logits_t(qa_t, pp, jk):
        k0 = pl.multiple_of(jk * tb, tb)
        ka = jnp.concatenate([k_ref[pl.ds(k0, tb), pp * LANES:(pp + 1) * LANES],
                              kf_ref[pl.ds(k0, tb), :]], axis=1)
        return jnp.dot(ka, qa_t, preferred_element_type=F32)

    def update(s_t, m, pp, jk):
        m_new = jnp.maximum(m, jnp.max(s_t, axis=0, keepdims=True))
        alpha = jnp.exp2(m - m_new)
        p_t = jnp.exp2(s_t - m_new).astype(BF16)
        for hh in range(2):
            r0 = pp * LANES + hh * ATT_HEAD_DIM
            va_t = jnp.concatenate([vt_ref[jk, r0:r0 + ATT_HEAD_DIM, :], ones], axis=0)
            cols = slice(hh * tb, (hh + 1) * tb)
            acc_refs[pp][hh] = alpha[:, cols] * acc_refs[pp][hh] + jnp.dot(
                va_t, p_t[:, cols], preferred_element_type=F32)
        return m_new

    def build_queries(i):
        qf_t = qft_ref[i]
        qas = []
        for pp in range(pairs):
            q_t = qt_ref[i, pp * LANES:(pp + 1) * LANES, :]
            cols = []
            for hh in range(2):
                fb0 = FB_LANES * (2 * (pair0 + pp) + hh)
                in_fb = (feat >= fb0) & (feat < fb0 + FB_LANES)
                cols.append(jnp.concatenate(
                    [jnp.where(in_head[hh], q_t, zero), jnp.where(in_fb, qf_t, zero)], axis=0))
            qas.append(jnp.concatenate(cols, axis=1))
        return qas

    def first_logits(i):
        qas = build_queries(i)
        for pp in range(pairs):
            s_refs[pp][2] = logits_t(qas[pp], pp, 0)

    def q_block(i, _):
        q0 = pl.multiple_of(i * tb, tb)
        qas = build_queries(i)
        for pp in range(pairs):
            acc_refs[pp][...] = jnp.zeros(acc_refs[pp].shape, F32)

        def step(jk, ms, rd, wr):
            new = []
            for pp in range(pairs):
                s_t = s_refs[pp][rd]
                s_refs[pp][wr] = logits_t(qas[pp], pp, jk + 1)
                new.append(update(s_t, ms[pp], pp, jk))
            return tuple(new)

        def two_steps(t2, ms):
            return step(2 * t2 + 2, step(2 * t2 + 1, ms, 1, 0), 0, 1)

        ms = tuple(jnp.full((1, 2 * tb), NEG_BIG, F32) for _ in range(pairs))
        ms = lax.cond(i >= 1, lambda v: step(0, v, 2, 1), lambda v: v, ms)
        ms = lax.fori_loop(0, lax.shift_right_logical(jnp.maximum(i - 1, 0), 1), two_steps, ms)
        ms = lax.cond((i >= 2) & (i % 2 == 0), lambda v: step(i - 1, v, 1, 0), lambda v: v, ms)

        last = jnp.where(i >= 1, i % 2, 2)
        diag = [jnp.where(causal, s_refs[pp][last], NEG_BIG) for pp in range(pairs)]
        first_logits(jnp.minimum(i + 1, nb - 1))
        for pp in range(pairs):
            update(diag[pp], ms[pp], pp, i)
            heads = []
            for hh in range(2):
                acc_t = acc_refs[pp][hh]
                inv = 1.0 / acc_t[ATT_HEAD_DIM:ATT_HEAD_DIM + 1, :]
                heads.append(acc_t[:ATT_HEAD_DIM, :] * inv)
            out_t = jnp.concatenate(heads, axis=0)
            o_ref[pl.ds(q0, tb), pp * LANES:(pp + 1) * LANES] = out_t.T.astype(o_ref.dtype)
        return 0

    first_logits(0)
    lax.fori_loop(0, nb, q_block, 0)


def _attention(q_t, k, v_t, qf_t, kf, batch, seq):
    t = k.shape[0]
    tb = min(ATT_BLOCK, seq)
    nb = seq // tb
    pairs = ATT_PAIRS_PER_STEP
    width = pairs * LANES
    return pl.pallas_call(
        functools.partial(_attn_kernel, tb=tb, pairs=pairs),
        grid=(batch, ATT_HEADS // (2 * pairs)),
        in_specs=[
            pl.BlockSpec((nb, width, tb), lambda b, p: (b, p, 0)),
            pl.BlockSpec((seq, width), lambda b, p: (b, p)),
            pl.BlockSpec((nb, width, tb), lambda b, p: (b, p, 0)),
            pl.BlockSpec((nb, LANES, tb), lambda b, p: (b, 0, 0)),
            pl.BlockSpec((seq, LANES), lambda b, p: (b, 0)),
        ],
        out_specs=pl.BlockSpec((seq, width), lambda b, p: (b, p)),
        out_shape=jax.ShapeDtypeStruct((t, ATT_INNER), BF16),
        scratch_shapes=([pltpu.VMEM((3, tb, 2 * tb), F32) for _ in range(pairs)]
                        + [pltpu.VMEM((2, ATT_HEAD_DIM + 2 * SUBLANES, tb), F32)
                           for _ in range(pairs)]),
        compiler_params=pltpu.CompilerParams(
            dimension_semantics=("arbitrary", "arbitrary"), vmem_limit_bytes=VMEM_LIMIT_BYTES),
        name="fox_attention",
    )(q_t, k, v_t, qf_t, kf)


def _layer_norm(u, g, b):
    mu = jnp.mean(u, axis=1, keepdims=True)
    d = u - mu
    var = jnp.mean(d * d, axis=1, keepdims=True)
    return d * lax.rsqrt(var + LN_EPS) * g + b


def _route(lg):
    rows = lg.shape[0]
    lt = lg.T
    gl = lt[0:N_EXPERT_GROUPS, :]
    g_row = lax.broadcasted_iota(jnp.int32, (N_EXPERT_GROUPS, 1), 0).astype(F32)
    gmax = jnp.max(gl, axis=0, keepdims=True)
    gsum = jnp.sum(jnp.exp(gl - gmax), axis=0, keepdims=True)
    g_val = 1.0 / gsum
    g_idx = jnp.min(jnp.where(gl == gmax, g_row, float(N_EXPERT_GROUPS)), axis=0, keepdims=True)
    el = lt[ROUTE_E_LANE:ROUTE_E_LANE + N_EXPERTS, :]
    e_row = lax.broadcasted_iota(jnp.int32, (N_EXPERTS, 1), 0)
    e_grp = (e_row // EXPERTS_PER_GROUP).astype(F32)
    e_row = e_row.astype(F32)
    in_grp = e_grp == g_idx
    m1 = jnp.max(jnp.where(in_grp, el, NEG_BIG), axis=0, keepdims=True)
    i1 = jnp.min(jnp.where(in_grp & (el == m1), e_row, float(N_EXPERTS)), axis=0, keepdims=True)
    rest = in_grp & (e_row != i1)
    m2 = jnp.max(jnp.where(rest, el, NEG_BIG), axis=0, keepdims=True)
    i2 = jnp.min(jnp.where(rest & (el == m2), e_row, float(N_EXPERTS)), axis=0, keepdims=True)
    r = jnp.exp(m2 - m1)
    w1 = 1.0 / (1.0 + r)
    w2 = r * w1
    comb_e = jnp.where(e_row == i1, g_val * w1, jnp.where(e_row == i2, g_val * w2, 0.0))
    sel_e = jnp.where((e_row == i1) | (e_row == i2), 1.0, 0.0)
    top = jnp.zeros((ROUTE_E_LANE, rows), F32)
    bottom = jnp.zeros((LANES - ROUTE_E_LANE - N_EXPERTS, rows), F32)
    comb = jnp.concatenate([top, comb_e, bottom], axis=0).T
    sel = jnp.concatenate([top, sel_e, bottom], axis=0).T
    return comb, sel


def _out_proj_kernel(ys_ref, ya_ref, x_ref, wa_ref, wb_ref, g_ref, b_ref, wr_ref, br_ref,
                     h_ref, hb_ref, comb_ref, sel_ref, cnt_ref):
    mix = jnp.dot(ys_ref[...], wa_ref[...], preferred_element_type=F32)
    mix = mix + jnp.dot(ya_ref[...], wb_ref[...], preferred_element_type=F32)
    h = _layer_norm(DEEPNORM_ALPHA * x_ref[...] + mix, g_ref[...], b_ref[...])
    h_ref[...] = h
    h_hi = h.astype(BF16)
    hb_ref[...] = h_hi
    h_lo = (h - h_hi.astype(F32)).astype(BF16)
    w_hi = wr_ref[:, :LANES]
    lg = (jnp.dot(h_hi, w_hi, preferred_element_type=F32)
          + jnp.dot(h_lo, w_hi, preferred_element_type=F32)
          + jnp.dot(h_hi, wr_ref[:, LANES:], preferred_element_type=F32)) + br_ref[...]
    comb, sel = _route(lg)
    comb_ref[...] = comb
    sel_ref[...] = sel
    ts = h.shape[0] // cnt_ref.shape[0]
    for c in range(cnt_ref.shape[0]):
        cnt = jnp.sum(sel[c * ts:(c + 1) * ts, :], axis=0, keepdims=True)
        cnt_ref[c] = jnp.broadcast_to(cnt, (SUBLANES, LANES))


def _out_proj(ys, ya, x2d, wo_a, wo_b, ln_g, ln_b, wr, br, ts):
    t = x2d.shape[0]
    tm = min(OUT_PROJ_ROWS, t)
    row = lambda i: (i, 0)
    const = lambda i: (0, 0)
    return pl.pallas_call(
        _out_proj_kernel,
        grid=(t // tm,),
        in_specs=[
            pl.BlockSpec((tm, SSD_INNER), row),
            pl.BlockSpec((tm, ATT_INNER), row),
            pl.BlockSpec((tm, D_MODEL), row),
            pl.BlockSpec((SSD_INNER, D_MODEL), const),
            pl.BlockSpec((ATT_INNER, D_MODEL), const),
            pl.BlockSpec((1, D_MODEL), const),
            pl.BlockSpec((1, D_MODEL), const),
            pl.BlockSpec((D_MODEL, 2 * LANES), const),
            pl.BlockSpec((1, LANES), const),
        ],
        out_specs=[
            pl.BlockSpec((tm, D_MODEL), row),
            pl.BlockSpec((tm, D_MODEL), row),
            pl.BlockSpec((tm, LANES), row),
            pl.BlockSpec((tm, LANES), row),
            pl.BlockSpec((tm // ts, SUBLANES, LANES), lambda i: (i, 0, 0)),
        ],
        out_shape=(
            jax.ShapeDtypeStruct((t, D_MODEL), F32),
            jax.ShapeDtypeStruct((t, D_MODEL), BF16),
            jax.ShapeDtypeStruct((t, LANES), F32),
            jax.ShapeDtypeStruct((t, LANES), F32),
            jax.ShapeDtypeStruct((t // ts, SUBLANES, LANES), F32),
        ),
        compiler_params=pltpu.CompilerParams(
            dimension_semantics=("arbitrary",), vmem_limit_bytes=VMEM_LIMIT_BYTES),
        name="out_proj_ln_router",
    )(ys, ya, x2d, wo_a, wo_b, ln_g, ln_b, wr, br)


def _dispatch_plan(cnt, ts):
    nt = cnt.shape[0]
    c16 = (cnt + (MOE_GRANULE - 1)) // MOE_GRANULE * MOE_GRANULE
    loc_off = jnp.cumsum(c16, axis=1) - c16
    tot = jnp.sum(c16, axis=0)
    tiles = (tot + (MOE_ROW_TILE - 1)) // MOE_ROW_TILE
    cum_tiles = jnp.cumsum(tiles)
    start = (cum_tiles - tiles) * MOE_ROW_TILE
    base = start[None, :] + jnp.cumsum(c16, axis=0) - c16
    max_tiles = _moe_rows(nt * ts, ts) // MOE_ROW_TILE
    tile_expert = jnp.minimum(
        jnp.sum(jnp.arange(max_tiles, dtype=jnp.int32)[:, None] >= cum_tiles[None, :], axis=1),
        N_EXPERTS - 1).astype(jnp.int32)
    loc_vec = jnp.zeros((nt, SUBLANES, LANES), F32).at[:, :, ROUTE_E_LANE:ROUTE_E_LANE + N_EXPERTS].set(
        jnp.broadcast_to(loc_off.astype(F32)[:, None, :], (nt, SUBLANES, N_EXPERTS)))
    return dict(
        chunks=(c16 // MOE_GRANULE).reshape(-1).astype(jnp.int32),
        loc_off=loc_off.reshape(-1).astype(jnp.int32),
        base=base.reshape(-1).astype(jnp.int32),
        step_chunks=(jnp.sum(c16, axis=1).reshape(-1, _moe_group(nt * ts, ts)).sum(axis=1)
                     // MOE_GRANULE).astype(jnp.int32),
        pad_start=(start + tot).astype(jnp.int32),
        pad_chunks=((tiles * MOE_ROW_TILE - tot) // MOE_GRANULE).astype(jnp.int32),
        n_tiles=cum_tiles[-1:].astype(jnp.int32),
        tile_expert=tile_expert,
        loc_vec=loc_vec,
    )


def _moe_group(t, ts):
    return MOE_TILES_PER_STEP if t % (MOE_TILES_PER_STEP * ts) == 0 else 1


def _moe_local_rows(ts):
    rows = 2 * ts + N_EXPERTS * (MOE_GRANULE - 1)
    return (rows + LANES - 1) // LANES * LANES


def _moe_rows(t, ts):
    rows = 2 * t + (t // ts) * N_EXPERTS * (MOE_GRANULE - 1) + N_EXPERTS * MOE_ROW_TILE
    return (rows + MOE_ROW_TILE - 1) // MOE_ROW_TILE * MOE_ROW_TILE


def _chunk_copy(src, dst, sem):
    return pltpu.make_async_copy(src, dst, sem)


def _wait_rows(copy_of_rows, n_chunks):
    def wait(rows):
        def one(_, c):
            copy_of_rows(rows).wait()
            return c
        return one
    shift = MOE_WAIT_BATCH.bit_length() - 1
    lax.fori_loop(0, lax.shift_right_logical(n_chunks, shift), wait(MOE_WAIT_BATCH * MOE_GRANULE), 0)
    lax.fori_loop(0, n_chunks & (MOE_WAIT_BATCH - 1), wait(MOE_GRANULE), 0)


def _dispatch_kernel(chunks_ref, loc_ref, base_ref, schunks_ref, pstart_ref, pchunks_ref,
                     hb_ref, sel_ref, comb_ref, locv_ref, lstrict_ref,
                     xs_ref, posw_ref, buf_ref, zero_ref, sem_ref, zsem_ref):
    step = pl.program_id(0)
    n_steps = pl.num_programs(0)
    par = step % 2
    group = locv_ref.shape[0]
    ts = hb_ref.shape[0] // group
    nl = buf_ref.shape[1]
    g = MOE_GRANULE

    def wait_chunks(n, pp):
        _wait_rows(lambda r: _chunk_copy(buf_ref.at[0, pl.ds(0, r)], xs_ref.at[pl.ds(0, r)],
                                         sem_ref.at[pp]), n)

    @pl.when(step >= 2)
    def _():
        wait_chunks(schunks_ref[step - 2], par)

    lane = lax.broadcasted_iota(jnp.int32, (1, LANES), 1)
    d = lax.broadcasted_iota(jnp.int32, (nl, ts), 0).astype(F32)
    for u in range(group):
        rows = slice(u * ts, (u + 1) * ts)
        sel = sel_ref[rows, :]
        comb = comb_ref[rows, :]
        picked = sel > 0.0
        rank = jnp.dot(lstrict_ref[...], sel.astype(BF16), preferred_element_type=F32)
        dest = rank + locv_ref[u][0:1, :]
        pos_a = jnp.min(jnp.where(picked, dest, 1e9), axis=1, keepdims=True)
        pos_b = jnp.max(jnp.where(picked, dest, -1.0), axis=1, keepdims=True)
        w_a = jnp.sum(jnp.where(picked & (dest == pos_a), comb, 0.0), axis=1, keepdims=True)
        w_b = jnp.sum(jnp.where(picked & (dest == pos_b), comb, 0.0), axis=1, keepdims=True)
        posw = jnp.where(lane == 0, pos_a, jnp.where(lane == 1, pos_b,
                         jnp.where(lane == 2, w_a, jnp.where(lane == 3, w_b, 0.0))))
        posw_ref[rows, :] = posw
        posw_t = posw.T
        perm = jnp.where((d == posw_t[0:1, :]) | (d == posw_t[1:2, :]), 1.0, 0.0).astype(BF16)
        buf_ref[par * group + u] = jnp.dot(perm, hb_ref[rows, :],
                                           preferred_element_type=F32).astype(BF16)

    for u in range(group):
        for e in range(N_EXPERTS):
            idx = (step * group + u) * N_EXPERTS + e
            lo = loc_ref[idx]
            bs = base_ref[idx]

            def send(k, c, lo=lo, bs=bs, u=u):
                src = buf_ref.at[par * group + u, pl.ds(pl.multiple_of(lo + k * g, g), g)]
                dst = xs_ref.at[pl.ds(pl.multiple_of(bs + k * g, g), g)]
                _chunk_copy(src, dst, sem_ref.at[par]).start()
                return c
            lax.fori_loop(0, chunks_ref[idx], send, 0)

    @pl.when(step == n_steps - 1)
    def _():
        zero_ref[...] = jnp.zeros_like(zero_ref)
        for e in range(N_EXPERTS):
            ps = pstart_ref[e]

            def send_zero(k, c, ps=ps):
                dst = xs_ref.at[pl.ds(pl.multiple_of(ps + k * g, g), g)]
                _chunk_copy(zero_ref, dst, zsem_ref.at[0]).start()
                return c
            lax.fori_loop(0, pchunks_ref[e], send_zero, 0)
        for e in range(N_EXPERTS):
            def wait_zero(_, c):
                _chunk_copy(zero_ref, xs_ref.at[pl.ds(0, g)], zsem_ref.at[0]).wait()
                return c
            lax.fori_loop(0, pchunks_ref[e], wait_zero, 0)
        wait_chunks(schunks_ref[step], par)

        @pl.when(step >= 1)
        def _():
            wait_chunks(schunks_ref[step - 1], 1 - par)


def _dispatch(plan, hb, sel, comb, ts):
    t = hb.shape[0]
    group = _moe_group(t, ts)
    nl = _moe_local_rows(ts)
    rows = _moe_rows(t, ts)
    lstrict = jnp.tri(ts, k=-1, dtype=BF16)
    row = lambda i, *_: (i, 0)
    grid_spec = pltpu.PrefetchScalarGridSpec(
        num_scalar_prefetch=6,
        grid=(t // (group * ts),),
        in_specs=[
            pl.BlockSpec((group * ts, D_MODEL), row),
            pl.BlockSpec((group * ts, LANES), row),
            pl.BlockSpec((group * ts, LANES), row),
            pl.BlockSpec((group, SUBLANES, LANES), lambda i, *_: (i, 0, 0)),
            pl.BlockSpec((ts, ts), lambda i, *_: (0, 0)),
        ],
        out_specs=[
            pl.BlockSpec(memory_space=pl.ANY),
            pl.BlockSpec((group * ts, LANES), row),
        ],
        scratch_shapes=[
            pltpu.VMEM((2 * group, nl, D_MODEL), BF16),
            pltpu.VMEM((MOE_GRANULE, D_MODEL), BF16),
            pltpu.SemaphoreType.DMA((2,)),
            pltpu.SemaphoreType.DMA((1,)),
        ],
    )
    return pl.pallas_call(
        _dispatch_kernel,
        grid_spec=grid_spec,
        out_shape=(
            jax.ShapeDtypeStruct((rows, D_MODEL), BF16),
            jax.ShapeDtypeStruct((t, LANES), F32),
        ),
        compiler_params=pltpu.CompilerParams(
            dimension_semantics=("arbitrary",), vmem_limit_bytes=VMEM_LIMIT_BYTES),
        name="moe_dispatch",
    )(plan["chunks"], plan["loc_off"], plan["base"], plan["step_chunks"], plan["pad_start"],
      plan["pad_chunks"], hb, sel, comb, plan["loc_vec"], lstrict)


def _expert_kernel(te_ref, nt_ref, x_ref, wg_ref, wu_ref, wd_ref, y_ref, wgb_ref, wub_ref, wdb_ref):
    j = pl.program_id(0)

    @pl.when(j < nt_ref[0])
    def _():
        @pl.when((j == 0) | (te_ref[j] != te_ref[jnp.maximum(j - 1, 0)]))
        def _():
            wgb_ref[...] = wg_ref[0].astype(BF16)
            wub_ref[...] = wu_ref[0].astype(BF16)
            wdb_ref[...] = wd_ref[0].astype(BF16)

        x = x_ref[...]
        gate = jnp.dot(x, wgb_ref[...], preferred_element_type=F32)
        up = jnp.dot(x, wub_ref[...], preferred_element_type=F32)
        act = (_silu(gate) * up).astype(BF16)
        y_ref[...] = jnp.dot(act, wdb_ref[...], preferred_element_type=F32).astype(y_ref.dtype)


def _experts(plan, xs, w_gate, w_up, w_down):
    rows = xs.shape[0]
    live = lambda j, te, nt: jnp.minimum(j, nt[0] - 1)
    expert = lambda j, te, nt: (te[live(j, te, nt)], 0, 0)
    grid_spec = pltpu.PrefetchScalarGridSpec(
        num_scalar_prefetch=2,
        grid=(rows // MOE_ROW_TILE,),
        in_specs=[
            pl.BlockSpec((MOE_ROW_TILE, D_MODEL), lambda j, te, nt: (live(j, te, nt), 0)),
            pl.BlockSpec((1, D_MODEL, D_FF_EXPERT), expert),
            pl.BlockSpec((1, D_MODEL, D_FF_EXPERT), expert),
            pl.BlockSpec((1, D_FF_EXPERT, D_MODEL), expert),
        ],
        out_specs=pl.BlockSpec((MOE_ROW_TILE, D_MODEL), lambda j, te, nt: (live(j, te, nt), 0)),
        scratch_shapes=[
            pltpu.VMEM((D_MODEL, D_FF_EXPERT), BF16),
            pltpu.VMEM((D_MODEL, D_FF_EXPERT), BF16),
            pltpu.VMEM((D_FF_EXPERT, D_MODEL), BF16),
        ],
    )
    return pl.pallas_call(
        _expert_kernel,
        grid_spec=grid_spec,
        out_shape=jax.ShapeDtypeStruct((rows, D_MODEL), BF16),
        compiler_params=pltpu.CompilerParams(
            dimension_semantics=("arbitrary",), vmem_limit_bytes=VMEM_LIMIT_BYTES),
        name="moe_experts",
    )(plan["tile_expert"], plan["n_tiles"], xs, w_gate, w_up, w_down)


def _combine_kernel(chunks_ref, loc_ref, base_ref, schunks_ref,
                    y_ref, posw_ref, h_ref, g_ref, b_ref, o_ref, buf_ref, sem_ref):
    step = pl.program_id(0)
    n_steps = pl.num_programs(0)
    par = step % 2
    nl = buf_ref.shape[1]
    group = buf_ref.shape[0] // 2
    ts = h_ref.shape[0] // group
    g = MOE_GRANULE

    def fetch(ss, pp):
        for u in range(group):
            for e in range(N_EXPERTS):
                idx = (ss * group + u) * N_EXPERTS + e
                lo = loc_ref[idx]
                bs = base_ref[idx]

                def get(k, c, lo=lo, bs=bs, u=u):
                    src = y_ref.at[pl.ds(pl.multiple_of(bs + k * g, g), g)]
                    dst = buf_ref.at[pp * group + u, pl.ds(pl.multiple_of(lo + k * g, g), g)]
                    _chunk_copy(src, dst, sem_ref.at[pp]).start()
                    return c
                lax.fori_loop(0, chunks_ref[idx], get, 0)

    @pl.when(step == 0)
    def _():
        buf_ref[...] = jnp.zeros(buf_ref.shape, BF16)
        fetch(0, 0)

    @pl.when(step + 1 < n_steps)
    def _():
        fetch(step + 1, 1 - par)

    _wait_rows(lambda r: _chunk_copy(y_ref.at[pl.ds(0, r)], buf_ref.at[0, pl.ds(0, r)],
                                     sem_ref.at[par]), schunks_ref[step])

    d = lax.broadcasted_iota(jnp.int32, (1, nl), 1).astype(F32)
    for u in range(group):
        rows = slice(u * ts, (u + 1) * ts)
        yb = buf_ref[par * group + u]
        posw = posw_ref[rows, :]
        ffn = jnp.zeros((ts, D_MODEL), F32)
        for k in range(2):
            gather = jnp.where(posw[:, k:k + 1] == d, 1.0, 0.0).astype(BF16)
            ffn = ffn + posw[:, 2 + k:3 + k] * jnp.dot(gather, yb, preferred_element_type=F32)
        o_ref[rows, :] = _layer_norm(DEEPNORM_ALPHA * h_ref[rows, :] + ffn, g_ref[...], b_ref[...])


def _combine(plan, y, posw, h, ln_g, ln_b, ts):
    t = h.shape[0]
    group = _moe_group(t, ts)
    nl = _moe_local_rows(ts)
    row = lambda i, *_: (i, 0)
    const = lambda i, *_: (0, 0)
    grid_spec = pltpu.PrefetchScalarGridSpec(
        num_scalar_prefetch=4,
        grid=(t // (group * ts),),
        in_specs=[
            pl.BlockSpec(memory_space=pl.ANY),
            pl.BlockSpec((group * ts, LANES), row),
            pl.BlockSpec((group * ts, D_MODEL), row),
            pl.BlockSpec((1, D_MODEL), const),
            pl.BlockSpec((1, D_MODEL), const),
        ],
        out_specs=pl.BlockSpec((group * ts, D_MODEL), row),
        scratch_shapes=[
            pltpu.VMEM((2 * group, nl, D_MODEL), BF16),
            pltpu.SemaphoreType.DMA((2,)),
        ],
    )
    return pl.pallas_call(
        _combine_kernel,
        grid_spec=grid_spec,
        out_shape=jax.ShapeDtypeStruct((t, D_MODEL), F32),
        compiler_params=pltpu.CompilerParams(
            dimension_semantics=("arbitrary",), vmem_limit_bytes=VMEM_LIMIT_BYTES),
        name="moe_combine_ln",
    )(plan["chunks"], plan["loc_off"], plan["base"], plan["step_chunks"], y, posw, h, ln_g, ln_b)


def _pad_lanes(a, n=LANES):
    return jnp.pad(a, ((0, 0), (0, n - a.shape[1])))


def _layer(h2d, batch, seq, w_in, b_in, conv_w, conv_b, a_log, d_skip, ssd_norm_g, w_out,
           ln1_g, ln1_b, rg_w, rg_b, re_w, re_b, w_gate, w_up, w_down, ln2_g, ln2_b):
    o_dt = SSD_INNER + SSD_CONV_DIM
    o_q = o_dt + SSD_HEADS
    o_f = o_q + 3 * ATT_INNER
    w_main = jnp.concatenate([w_in[:, :o_dt], w_in[:, o_q:o_f]], axis=1).astype(BF16)
    b_main = jnp.concatenate([b_in[:o_dt], b_in[o_q:o_f]])[None, :]
    w_dt, w_f = w_in[:, o_dt:o_q], w_in[:, o_f:]
    w_small = _pad_lanes(jnp.concatenate([w_dt, w_f, w_dt], axis=1)).astype(BF16)
    b_small = _pad_lanes(jnp.concatenate([b_in[o_dt:o_q], b_in[o_f:], b_in[o_dt:o_q]])[None, :])
    a_neg = -jnp.exp(a_log.astype(F32))
    a_row = _pad_lanes(a_neg[None, :])
    a_exp = jnp.repeat(a_neg, SSD_HEAD_DIM)[None, :]
    dskip_exp = jnp.repeat(d_skip.astype(F32), SSD_HEAD_DIM)[None, :]

    z, xbc, q_t, k, v_t, gate, qf_t, kf = _in_proj(h2d, w_main, b_main, w_small, b_small, a_row, seq)
    y_ssd = _ssd(xbc, z, gate, conv_w, conv_b[None, :], a_exp, dskip_exp, ssd_norm_g[None, :],
                 batch, seq)
    y_att = _attention(q_t, k, v_t, qf_t, kf, batch, seq)

    gap = ROUTE_E_LANE - N_EXPERT_GROUPS
    w_r = _pad_lanes(jnp.concatenate(
        [rg_w, jnp.zeros((D_MODEL, gap), F32),
         jnp.transpose(re_w, (1, 0, 2)).reshape(D_MODEL, N_EXPERTS)], axis=1))
    b_r = _pad_lanes(jnp.concatenate(
        [rg_b, jnp.zeros((gap,), F32), re_b.reshape(N_EXPERTS)])[None, :])
    wr_hi = w_r.astype(BF16)
    wr_lo = (w_r - wr_hi.astype(F32)).astype(BF16)
    wo = w_out.astype(BF16)
    ts = min(MOE_SORT_TILE, h2d.shape[0])
    h1, h1b, comb, sel, cnt = _out_proj(y_ssd, y_att, h2d, wo[:SSD_INNER], wo[SSD_INNER:],
                                        ln1_g[None, :], ln1_b[None, :],
                                        jnp.concatenate([wr_hi, wr_lo], axis=1), b_r, ts)

    plan = _dispatch_plan(
        cnt[:, 0, ROUTE_E_LANE:ROUTE_E_LANE + N_EXPERTS].astype(jnp.int32), ts)
    xs, posw = _dispatch(plan, h1b, sel, comb, ts)
    y = _experts(plan, xs, w_gate, w_up, w_down)
    return _combine(plan, y, posw, h1, ln2_g[None, :], ln2_b[None, :], ts)


def kernel(x, w_in, b_in, conv_w, conv_b, a_log, d_skip, ssd_norm_g, w_out, ln1_g, ln1_b,
           router_group_w, router_group_b, router_expert_w, router_expert_b, w_gate, w_up,
           w_down, ln2_g, ln2_b):
    batch, seq, d = x.shape
    h = x.reshape(batch * seq, d)
    for l in range(w_in.shape[0]):
        h = _layer(h, batch, seq, w_in[l], b_in[l], conv_w[l], conv_b[l], a_log[l], d_skip[l],
                   ssd_norm_g[l], w_out[l], ln1_g[l], ln1_b[l], router_group_w[l],
                   router_group_b[l], router_expert_w[l], router_expert_b[l], w_gate[l],
                   w_up[l], w_down[l], ln2_g[l], ln2_b[l])
    return h.reshape(batch, seq, d)
```

```python
import functools

import jax
import jax.numpy as jnp
import numpy as np
from jax import lax
from jax.experimental import pallas as pl
from jax.experimental.pallas import tpu as pltpu

F32 = jnp.float32
BF16 = jnp.bfloat16

D_MODEL = 1024
SSD_HEADS = 8
SSD_HEAD_DIM = 64
SSD_INNER = SSD_HEADS * SSD_HEAD_DIM
SSD_GROUPS = 2
SSD_STATE = 128
CONV_WIDTH = 4
SSD_CONV_DIM = SSD_INNER + 2 * SSD_GROUPS * SSD_STATE
ATT_HEADS = 8
ATT_HEAD_DIM = 64
ATT_INNER = ATT_HEADS * ATT_HEAD_DIM
N_EXPERT_GROUPS = 4
EXPERTS_PER_GROUP = 4
N_EXPERTS = N_EXPERT_GROUPS * EXPERTS_PER_GROUP
D_FF_EXPERT = 512
DEPTH = 1
DEEPNORM_ALPHA = (2.0 * DEPTH) ** 0.25
LN_EPS = 1e-5
RMS_EPS = 1e-5

LANES = 128
SUBLANES = 8
VMEM_LIMIT_BYTES = 56 * 1024 * 1024

GATE_A_LANE = 0
GATE_F_LANE = SSD_HEADS
GATE_DT_LANE = 2 * SSD_HEADS
ROUTE_E_LANE = SUBLANES
MOE_SORT_TILE = 256
MOE_GRANULE = 16
MOE_ROW_TILE = 1024
MOE_TILES_PER_STEP = 4
MOE_ROW_LANES = D_MODEL + LANES
MOE_WAIT_BATCH = 8

IN_PROJ_ROWS = 1024
OUT_PROJ_ROWS = 1024
CUMSUM_BLOCK = 256
FB_LANES = 6
ATT_BLOCK = 256
ATT_PAIRS_PER_STEP = 4
ATT_BATCHES_PER_STEP = 1

SSD_ROWS = 256
SSD_STEP_ROWS = 512
SSD_CHUNK = 64
NEG_BIG = -1e30
LOG2E = 1.4426950408889634


def _split3(v):
    hi = v.astype(BF16)
    r1 = v - hi.astype(F32)
    mid = r1.astype(BF16)
    lo = (r1 - mid.astype(F32)).astype(BF16)
    return hi, mid, lo


def _silu(v):
    half = 0.5 * v
    return half * (1.0 + jnp.tanh(half))


def _softplus(v):
    return jnp.maximum(v, 0.0) + jnp.log1p(jnp.exp(-jnp.abs(v)))


def _in_proj_kernel(x_ref, wm_ref, bm_ref, ws_ref, bs_ref, arow_ref, tri_ref, place_ref, ones_ref,
                    z_ref, xbc_ref, qt_ref, k_ref, vt_ref, gate_ref, qft_ref, kf_ref,
                    carry_ref, *, tiles_per_seq):
    i = pl.program_id(0)
    tm = x_ref.shape[0]
    tb = qt_ref.shape[2]
    xb = x_ref[...].astype(BF16)

    def put_transposed(ref, val):
        for c in range(tm // tb):
            ref[c] = val[c * tb:(c + 1) * tb, :].T.astype(ref.dtype)

    s = jnp.dot(xb, ws_ref[...], preferred_element_type=F32) + bs_ref[...]
    lane = lax.broadcasted_iota(jnp.int32, (1, LANES), 1)
    sp = _softplus(s)
    log_f = -_softplus(-s)
    val = jnp.where(lane < GATE_F_LANE, sp * arow_ref[...],
                    jnp.where(lane < GATE_DT_LANE, log_f,
                              jnp.where(lane < GATE_DT_LANE + SSD_HEADS, sp, 0.0)))
    carry = jnp.where(i % tiles_per_seq == 0, 0.0, carry_ref[...])
    tri = tri_ref[...]
    cb = tri.shape[0]
    sums = []
    for c in range(tm // cb):
        hi, mid, lo = _split3(val[c * cb:(c + 1) * cb, :])
        part = (jnp.dot(tri, hi, preferred_element_type=F32)
                + jnp.dot(tri, mid, preferred_element_type=F32)
                + jnp.dot(tri, lo, preferred_element_type=F32)) + carry
        carry = part[cb - 1:cb, :]
        sums.append(part)
    cs = jnp.concatenate(sums, axis=0)
    carry_ref[...] = carry
    out = jnp.where(lane < GATE_DT_LANE, cs, val)
    gate_ref[...] = out
    pieces = jnp.concatenate(_split3(cs * LOG2E), axis=1)
    fb = jnp.dot(pieces, place_ref[...], preferred_element_type=F32) + ones_ref[...]
    put_transposed(qft_ref, fb[:, :LANES])
    kf_ref[...] = fb[:, LANES:].astype(BF16)

    col = 0
    for ref, scale in ((z_ref, None), (xbc_ref, None), (qt_ref, ATT_HEAD_DIM ** -0.5 * LOG2E),
                       (k_ref, None), (vt_ref, None)):
        transposed = ref is qt_ref or ref is vt_ref
        n = ATT_INNER if transposed else ref.shape[1]
        acc = jnp.dot(xb, wm_ref[:, col:col + n], preferred_element_type=F32) + bm_ref[:, col:col + n]
        if scale is not None:
            acc = acc * scale
        if transposed:
            put_transposed(ref, acc)
        else:
            ref[...] = acc.astype(ref.dtype)
        col += n


def _forget_bias_placement():
    place = np.zeros((3 * LANES, 2 * LANES), np.float32)
    ones = np.zeros((1, 2 * LANES), np.float32)
    for h in range(ATT_HEADS):
        for piece in range(3):
            src = piece * LANES + GATE_F_LANE + h
            place[src, FB_LANES * h + piece] = 1.0
            place[src, LANES + FB_LANES * h + 3 + piece] = -1.0
            ones[0, FB_LANES * h + 3 + piece] = 1.0
            ones[0, LANES + FB_LANES * h + piece] = 1.0
    return jnp.asarray(place, BF16), jnp.asarray(ones, F32)


def _in_proj(x2d, w_main, b_main, w_small, b_small, a_row, seq):
    t = x2d.shape[0]
    tm = min(IN_PROJ_ROWS, seq)
    tk = min(ATT_BLOCK, seq)
    cum = min(CUMSUM_BLOCK, tm)
    tri = jnp.tri(cum, dtype=BF16)
    place, ones = _forget_bias_placement()
    n_main = w_main.shape[1]
    row = lambda i: (i, 0)
    const = lambda i: (0, 0)
    outs = (
        jax.ShapeDtypeStruct((t, SSD_INNER), BF16),
        jax.ShapeDtypeStruct((t, SSD_CONV_DIM), BF16),
        jax.ShapeDtypeStruct((t // tk, ATT_INNER, tk), BF16),
        jax.ShapeDtypeStruct((t, ATT_INNER), BF16),
        jax.ShapeDtypeStruct((t // tk, ATT_INNER, tk), BF16),
        jax.ShapeDtypeStruct((t, LANES), F32),
        jax.ShapeDtypeStruct((t // tk, LANES, tk), BF16),
        jax.ShapeDtypeStruct((t, LANES), BF16),
    )
    return pl.pallas_call(
        functools.partial(_in_proj_kernel, tiles_per_seq=seq // tm),
        grid=(t // tm,),
        in_specs=[
            pl.BlockSpec((tm, D_MODEL), row),
            pl.BlockSpec((D_MODEL, n_main), const),
            pl.BlockSpec((1, n_main), const),
            pl.BlockSpec((D_MODEL, LANES), const),
            pl.BlockSpec((1, LANES), const),
            pl.BlockSpec((1, LANES), const),
            pl.BlockSpec((cum, cum), const),
            pl.BlockSpec((3 * LANES, 2 * LANES), const),
            pl.BlockSpec((1, 2 * LANES), const),
        ],
        out_specs=[
            pl.BlockSpec((tm, SSD_INNER), row),
            pl.BlockSpec((tm, SSD_CONV_DIM), row),
            pl.BlockSpec((tm // tk, ATT_INNER, tk), lambda i: (i, 0, 0)),
            pl.BlockSpec((tm, ATT_INNER), row),
            pl.BlockSpec((tm // tk, ATT_INNER, tk), lambda i: (i, 0, 0)),
            pl.BlockSpec((tm, LANES), row),
            pl.BlockSpec((tm // tk, LANES, tk), lambda i: (i, 0, 0)),
            pl.BlockSpec((tm, LANES), row),
        ],
        out_shape=outs,
        scratch_shapes=[pltpu.VMEM((1, LANES), F32)],
        compiler_params=pltpu.CompilerParams(
            dimension_semantics=("arbitrary",), vmem_limit_bytes=VMEM_LIMIT_BYTES),
        name="in_proj",
    )(x2d, w_main, b_main, w_small, b_small, a_row, tri, place, ones)


def _expand_heads(gate, lane0, rows):
    lane = lax.broadcasted_iota(jnp.int32, (1, LANES), 1)
    pieces = []
    for p in range(SSD_HEADS // 2):
        a = jnp.broadcast_to(gate[:, lane0 + 2 * p:lane0 + 2 * p + 1], (rows, LANES))
        b = jnp.broadcast_to(gate[:, lane0 + 2 * p + 1:lane0 + 2 * p + 2], (rows, LANES))
        pieces.append(jnp.where(lane < SSD_HEAD_DIM, a, b))
    return jnp.concatenate(pieces, axis=1)


def _ssd_kernel(xbc_ref, z_ref, gate_ref, cw_ref, cb_ref, aexp_ref, dskip_ref, ng_ref, shift_ref,
                y_ref, ext_ref, state_ref):
    j = pl.program_id(1)
    rows = shift_ref.shape[1]
    q = SSD_CHUNK
    gw = SSD_INNER // SSD_GROUPS
    hpg = SSD_HEADS // SSD_GROUPS
    pad = SUBLANES

    @pl.when(j == 0)
    def _():
        ext_ref[0:pad, :] = jnp.zeros((pad, SSD_CONV_DIM), F32)
        state_ref[...] = jnp.zeros_like(state_ref)

    r_i = lax.broadcasted_iota(jnp.int32, (q, gw), 0)
    c_i = lax.broadcasted_iota(jnp.int32, (q, gw), 1)
    s_i = c_i % q
    eye_t = s_i == r_i
    tril_t = s_i <= r_i
    rb = lax.broadcasted_iota(jnp.int32, (hpg * q, gw), 0) // q
    cbk = lax.broadcasted_iota(jnp.int32, (hpg * q, gw), 1) // SSD_HEAD_DIM
    blk = rb == cbk

    for t in range(xbc_ref.shape[0] // rows):
        sub = slice(t * rows, (t + 1) * rows)
        xb = xbc_ref[sub, :]
        xf = xb.astype(F32)
        conv = cb_ref[...] + xf * cw_ref[CONV_WIDTH - 1:CONV_WIDTH, :]
        for s in range(1, CONV_WIDTH):
            shifted = jnp.dot(shift_ref[s - 1], xb, preferred_element_type=F32)
            conv = conv + shifted * cw_ref[CONV_WIDTH - 1 - s:CONV_WIDTH - s, :]
        ext_ref[pad:2 * pad, :] = xf[0:pad, :]
        head = cb_ref[...]
        for kk in range(CONV_WIDTH):
            off = pad - (CONV_WIDTH - 1) + kk
            head = head + ext_ref[off:off + pad, :] * cw_ref[kk:kk + 1, :]
        ext_ref[0:pad, :] = xf[rows - pad:rows, :]
        u = _silu(jnp.concatenate([head, conv[pad:, :]], axis=0))
        xs = u[:, :SSD_INNER]
        b_all = u[:, SSD_INNER:SSD_INNER + SSD_GROUPS * SSD_STATE].astype(BF16)
        c_all = u[:, SSD_INNER + SSD_GROUPS * SSD_STATE:].astype(BF16)

        gate = gate_ref[sub, :]
        dt_e = _expand_heads(gate, GATE_DT_LANE, rows)
        a_e = _expand_heads(gate, GATE_A_LANE, rows)
        xdt = xs * dt_e
        da_e = dt_e * aexp_ref[...]

        ys = []
        for c in range(rows // q):
            r0 = c * q
            a_c = a_e[r0:r0 + q, :]
            a_prev = a_c[0:1, :] - da_e[r0:r0 + 1, :]
            a_end = a_c[q - 1:q, :]
            y_groups = []
            for g in range(SSD_GROUPS):
                l0 = g * gw
                a_cg = a_c[:, l0:l0 + gw]
                diag = jnp.sum(jnp.where(eye_t, a_cg, 0.0), axis=0, keepdims=True)
                seg = jnp.minimum(a_cg - diag, 0.0)
                lcat = jnp.where(tril_t, jnp.exp(seg), 0.0)
                bg = b_all[r0:r0 + q, g * SSD_STATE:(g + 1) * SSD_STATE]
                cg = c_all[r0:r0 + q, g * SSD_STATE:(g + 1) * SSD_STATE]
                b_tile = jnp.concatenate([bg] * hpg, axis=0)
                gcat = lax.dot_general(cg, b_tile, (((1,), (1,)), ((), ())),
                                       preferred_element_type=F32)
                xg = xdt[r0:r0 + q, l0:l0 + gw]
                xg_b = xg.astype(BF16)
                xbd = jnp.where(blk, jnp.concatenate([xg_b] * hpg, axis=0), jnp.zeros((), BF16))
                y_diag = jnp.dot((gcat * lcat).astype(BF16), xbd, preferred_element_type=F32)
                st = state_ref[g]
                y_off = jnp.dot(cg, st.astype(BF16), preferred_element_type=F32) \
                    * jnp.exp(a_cg - a_prev[:, l0:l0 + gw])
                dte = jnp.exp(a_end[:, l0:l0 + gw] - a_cg)
                upd = lax.dot_general(bg, (xg * dte).astype(BF16), (((0,), (0,)), ((), ())),
                                      preferred_element_type=F32)
                state_ref[g] = st * jnp.exp(a_end[:, l0:l0 + gw] - a_prev[:, l0:l0 + gw]) + upd
                y_groups.append(y_diag + y_off)
            ys.append(jnp.concatenate(y_groups, axis=1))
        y = jnp.concatenate(ys, axis=0) + xs * dskip_ref[...]
        y = y * _silu(z_ref[sub, :].astype(F32))
        outs = []
        for g in range(SSD_GROUPS):
            yg = y[:, g * gw:(g + 1) * gw]
            ms = jnp.mean(yg * yg, axis=1, keepdims=True)
            outs.append(yg * lax.rsqrt(ms + RMS_EPS))
        y_ref[sub, :] = (jnp.concatenate(outs, axis=1) * ng_ref[...]).astype(y_ref.dtype)


def _ssd(xbc, z, gate, conv_w, conv_b, a_exp, dskip_exp, norm_g, batch, seq):
    t = xbc.shape[0]
    rows = min(SSD_ROWS, seq)
    step_rows = min(SSD_STEP_ROWS, seq)
    nj = seq // step_rows
    row = lambda b, j: (b * nj + j, 0)
    const = lambda b, j: (0, 0)
    shifts = jnp.stack([jnp.eye(rows, k=-s, dtype=BF16) for s in range(1, CONV_WIDTH)])
    return pl.pallas_call(
        _ssd_kernel,
        grid=(batch, nj),
        in_specs=[
            pl.BlockSpec((step_rows, SSD_CONV_DIM), row),
            pl.BlockSpec((step_rows, SSD_INNER), row),
            pl.BlockSpec((step_rows, LANES), row),
            pl.BlockSpec((CONV_WIDTH, SSD_CONV_DIM), const),
            pl.BlockSpec((1, SSD_CONV_DIM), const),
            pl.BlockSpec((1, SSD_INNER), const),
            pl.BlockSpec((1, SSD_INNER), const),
            pl.BlockSpec((1, SSD_INNER), const),
            pl.BlockSpec((CONV_WIDTH - 1, rows, rows), lambda b, j: (0, 0, 0)),
        ],
        out_specs=pl.BlockSpec((step_rows, SSD_INNER), row),
        out_shape=jax.ShapeDtypeStruct((t, SSD_INNER), BF16),
        scratch_shapes=[
            pltpu.VMEM((2 * SUBLANES, SSD_CONV_DIM), F32),
            pltpu.VMEM((SSD_GROUPS, SSD_STATE, SSD_INNER // SSD_GROUPS), F32),
        ],
        compiler_params=pltpu.CompilerParams(
            dimension_semantics=("arbitrary", "arbitrary"), vmem_limit_bytes=VMEM_LIMIT_BYTES),
        name="ssd",
    )(xbc, z, gate, conv_w, conv_b, a_exp, dskip_exp, norm_g, shifts)


def _attn_kernel(qt_ref, k_ref, vt_ref, qft_ref, kf_ref, o_ref, *scratch, tb, pairs, batches):
    pair0 = pl.program_id(1) * pairs
    nb = qt_ref.shape[0] // batches
    seq = nb * tb
    chains = [(bb, pp) for bb in range(batches) for pp in range(pairs)]
    s_refs, acc_refs = scratch[:len(chains)], scratch[len(chains):]
    feat = lax.broadcasted_iota(jnp.int32, (LANES, 1), 0)
    in_head = (feat < ATT_HEAD_DIM, feat >= ATT_HEAD_DIM)
    ones = jnp.ones((2 * SUBLANES, tb), BF16)
    key = lax.broadcasted_iota(jnp.int32, (tb, 2 * tb), 0)
    qry = lax.broadcasted_iota(jnp.int32, (tb, 2 * tb), 1)
    causal = jnp.where(qry >= tb, qry - tb, qry) >= key
    zero = jnp.zeros((), BF16)

    def logits_t(qa_t, c, jk):
        bb, pp = chains[c]
        k0 = pl.multiple_of(bb * seq + jk * tb, tb)
        ka = jnp.concatenate([k_ref[pl.ds(k0, tb), pp * LANES:(pp + 1) * LANES],
                              kf_ref[pl.ds(k0, tb), :]], axis=1)
        return jnp.dot(ka, qa_t, preferred_element_type=F32)

    def update(s_t, m, c, jk):
        bb, pp = chains[c]
        m_new = jnp.maximum(m, jnp.max(s_t, axis=0, keepdims=True))
        alpha = jnp.exp2(m - m_new)
        p_t = jnp.exp2(s_t - m_new).astype(BF16)
        for hh in range(2):
            r0 = pp * LANES + hh * ATT_HEAD_DIM
            va_t = jnp.concatenate([vt_ref[bb * nb + jk, r0:r0 + ATT_HEAD_DIM, :], ones], axis=0)
            cols = slice(hh * tb, (hh + 1) * tb)
            acc_refs[c][hh] = alpha[:, cols] * acc_refs[c][hh] + jnp.dot(
                va_t, p_t[:, cols], preferred_element_type=F32)
        return m_new

    def build_queries(i):
        qas = []
        for bb, pp in chains:
            qf_t = qft_ref[bb * nb + i]
            q_t = qt_ref[bb * nb + i, pp * LANES:(pp + 1) * LANES, :]
            cols = []
            for hh in range(2):
                fb0 = FB_LANES * (2 * (pair0 + pp) + hh)
                in_fb = (feat >= fb0) & (feat < fb0 + FB_LANES)
                cols.append(jnp.concatenate(
                    [jnp.where(in_head[hh], q_t, zero), jnp.where(in_fb, qf_t, zero)], axis=0))
            qas.append(jnp.concatenate(cols, axis=1))
        return qas

    def first_logits(i):
        qas = build_queries(i)
        for c in range(len(chains)):
            s_refs[c][2] = logits_t(qas[c], c, 0)

    def q_block(i, _):
        q0 = pl.multiple_of(i * tb, tb)
        qas = build_queries(i)
        for c in range(len(chains)):
            acc_refs[c][...] = jnp.zeros(acc_refs[c].shape, F32)

        def step(jk, ms, rd, wr):
            new = []
            for c in range(len(chains)):
                s_t = s_refs[c][rd]
                s_refs[c][wr] = logits_t(qas[c], c, jk + 1)
                new.append(update(s_t, ms[c], c, jk))
            return tuple(new)

        def two_steps(t2, ms):
            return step(2 * t2 + 2, step(2 * t2 + 1, ms, 1, 0), 0, 1)

        ms = tuple(jnp.full((1, 2 * tb), NEG_BIG, F32) for _ in chains)
        ms = lax.cond(i >= 1, lambda v: step(0, v, 2, 1), lambda v: v, ms)
        ms = lax.fori_loop(0, lax.shift_right_logical(jnp.maximum(i - 1, 0), 1), two_steps, ms)
        ms = lax.cond((i >= 2) & (i % 2 == 0), lambda v: step(i - 1, v, 1, 0), lambda v: v, ms)

        last = jnp.where(i >= 1, i % 2, 2)
        diag = [jnp.where(causal, s_refs[c][last], NEG_BIG) for c in range(len(chains))]
        first_logits(jnp.minimum(i + 1, nb - 1))
        for c, (bb, pp) in enumerate(chains):
            update(diag[c], ms[c], c, i)
            heads = []
            for hh in range(2):
                acc_t = acc_refs[c][hh]
                inv = 1.0 / acc_t[ATT_HEAD_DIM:ATT_HEAD_DIM + 1, :]
                heads.append(acc_t[:ATT_HEAD_DIM, :] * inv)
            out_t = jnp.concatenate(heads, axis=0)
            o_ref[pl.ds(bb * seq + q0, tb), pp * LANES:(pp + 1) * LANES] = out_t.T.astype(o_ref.dtype)
        return 0

    first_logits(0)
    lax.fori_loop(0, nb, q_block, 0)


def _attention(q_t, k, v_t, qf_t, kf, batch, seq):
    t = k.shape[0]
    tb = min(ATT_BLOCK, seq)
    nb = seq // tb
    pairs = ATT_PAIRS_PER_STEP
    batches = ATT_BATCHES_PER_STEP if batch % ATT_BATCHES_PER_STEP == 0 else 1
    width = pairs * LANES
    return pl.pallas_call(
        functools.partial(_attn_kernel, tb=tb, pairs=pairs, batches=batches),
        grid=(batch // batches, ATT_HEADS // (2 * pairs)),
        in_specs=[
            pl.BlockSpec((batches * nb, width, tb), lambda b, p: (b, p, 0)),
            pl.BlockSpec((batches * seq, width), lambda b, p: (b, p)),
            pl.BlockSpec((batches * nb, width, tb), lambda b, p: (b, p, 0)),
            pl.BlockSpec((batches * nb, LANES, tb), lambda b, p: (b, 0, 0)),
            pl.BlockSpec((batches * seq, LANES), lambda b, p: (b, 0)),
        ],
        out_specs=pl.BlockSpec((batches * seq, width), lambda b, p: (b, p)),
        out_shape=jax.ShapeDtypeStruct((t, ATT_INNER), BF16),
        scratch_shapes=([pltpu.VMEM((3, tb, 2 * tb), F32) for _ in range(batches * pairs)]
                        + [pltpu.VMEM((2, ATT_HEAD_DIM + 2 * SUBLANES, tb), F32)
                           for _ in range(batches * pairs)]),
        compiler_params=pltpu.CompilerParams(
            dimension_semantics=("arbitrary", "arbitrary"), vmem_limit_bytes=VMEM_LIMIT_BYTES),
        name="fox_attention",
    )(q_t, k, v_t, qf_t, kf)


def _layer_norm(u, g, b):
    mu = jnp.mean(u, axis=1, keepdims=True)
    d = u - mu
    var = jnp.mean(d * d, axis=1, keepdims=True)
    return d * lax.rsqrt(var + LN_EPS) * g + b


def _route(lg):
    rows = lg.shape[0]
    lt = lg.T
    gl = lt[0:N_EXPERT_GROUPS, :]
    g_row = lax.broadcasted_iota(jnp.int32, (N_EXPERT_GROUPS, 1), 0).astype(F32)
    gmax = jnp.max(gl, axis=0, keepdims=True)
    gsum = jnp.sum(jnp.exp(gl - gmax), axis=0, keepdims=True)
    g_val = 1.0 / gsum
    g_idx = jnp.min(jnp.where(gl == gmax, g_row, float(N_EXPERT_GROUPS)), axis=0, keepdims=True)
    el = lt[ROUTE_E_LANE:ROUTE_E_LANE + N_EXPERTS, :]
    e_row = lax.broadcasted_iota(jnp.int32, (N_EXPERTS, 1), 0)
    e_grp = (e_row // EXPERTS_PER_GROUP).astype(F32)
    e_row = e_row.astype(F32)
    in_grp = e_grp == g_idx
    m1 = jnp.max(jnp.where(in_grp, el, NEG_BIG), axis=0, keepdims=True)
    i1 = jnp.min(jnp.where(in_grp & (el == m1), e_row, float(N_EXPERTS)), axis=0, keepdims=True)
    rest = in_grp & (e_row != i1)
    m2 = jnp.max(jnp.where(rest, el, NEG_BIG), axis=0, keepdims=True)
    i2 = jnp.min(jnp.where(rest & (el == m2), e_row, float(N_EXPERTS)), axis=0, keepdims=True)
    r = jnp.exp(m2 - m1)
    w1 = 1.0 / (1.0 + r)
    w2 = r * w1
    comb_e = jnp.where(e_row == i1, g_val * w1, jnp.where(e_row == i2, g_val * w2, 0.0))
    sel_e = jnp.where((e_row == i1) | (e_row == i2), 1.0, 0.0)
    top = jnp.zeros((ROUTE_E_LANE, rows), F32)
    bottom = jnp.zeros((LANES - ROUTE_E_LANE - N_EXPERTS, rows), F32)
    comb = jnp.concatenate([top, comb_e, bottom], axis=0).T
    sel = jnp.concatenate([top, sel_e, bottom], axis=0).T
    return comb, sel


def _out_proj_kernel(ys_ref, ya_ref, x_ref, wa_ref, wb_ref, g_ref, b_ref, wr_ref, br_ref,
                     h_ref, hb_ref, comb_ref, sel_ref, cnt_ref):
    mix = jnp.dot(ys_ref[...], wa_ref[...], preferred_element_type=F32)
    mix = mix + jnp.dot(ya_ref[...], wb_ref[...], preferred_element_type=F32)
    h = _layer_norm(DEEPNORM_ALPHA * x_ref[...] + mix, g_ref[...], b_ref[...])
    h_ref[...] = h
    h_hi = h.astype(BF16)
    hb_ref[...] = h_hi
    h_lo = (h - h_hi.astype(F32)).astype(BF16)
    w_hi = wr_ref[:, :LANES]
    lg = (jnp.dot(h_hi, w_hi, preferred_element_type=F32)
          + jnp.dot(h_lo, w_hi, preferred_element_type=F32)
          + jnp.dot(h_hi, wr_ref[:, LANES:], preferred_element_type=F32)) + br_ref[...]
    comb, sel = _route(lg)
    comb_ref[...] = comb
    sel_ref[...] = sel
    ts = h.shape[0] // cnt_ref.shape[0]
    for c in range(cnt_ref.shape[0]):
        cnt = jnp.sum(sel[c * ts:(c + 1) * ts, :], axis=0, keepdims=True)
        cnt_ref[c] = jnp.broadcast_to(cnt, (SUBLANES, LANES))


def _out_proj(ys, ya, x2d, wo_a, wo_b, ln_g, ln_b, wr, br, ts):
    t = x2d.shape[0]
    tm = min(OUT_PROJ_ROWS, t)
    row = lambda i: (i, 0)
    const = lambda i: (0, 0)
    return pl.pallas_call(
        _out_proj_kernel,
        grid=(t // tm,),
        in_specs=[
            pl.BlockSpec((tm, SSD_INNER), row),
            pl.BlockSpec((tm, ATT_INNER), row),
            pl.BlockSpec((tm, D_MODEL), row),
            pl.BlockSpec((SSD_INNER, D_MODEL), const),
            pl.BlockSpec((ATT_INNER, D_MODEL), const),
            pl.BlockSpec((1, D_MODEL), const),
            pl.BlockSpec((1, D_MODEL), const),
            pl.BlockSpec((D_MODEL, 2 * LANES), const),
            pl.BlockSpec((1, LANES), const),
        ],
        out_specs=[
            pl.BlockSpec((tm, D_MODEL), row),
            pl.BlockSpec((tm, D_MODEL), row),
            pl.BlockSpec((tm, LANES), row),
            pl.BlockSpec((tm, LANES), row),
            pl.BlockSpec((tm // ts, SUBLANES, LANES), lambda i: (i, 0, 0)),
        ],
        out_shape=(
            jax.ShapeDtypeStruct((t, D_MODEL), F32),
            jax.ShapeDtypeStruct((t, D_MODEL), BF16),
            jax.ShapeDtypeStruct((t, LANES), F32),
            jax.ShapeDtypeStruct((t, LANES), F32),
            jax.ShapeDtypeStruct((t // ts, SUBLANES, LANES), F32),
        ),
        compiler_params=pltpu.CompilerParams(
            dimension_semantics=("arbitrary",), vmem_limit_bytes=VMEM_LIMIT_BYTES),
        name="out_proj_ln_router",
    )(ys, ya, x2d, wo_a, wo_b, ln_g, ln_b, wr, br)


def _dispatch_plan(cnt, ts):
    nt = cnt.shape[0]
    c16 = (cnt + (MOE_GRANULE - 1)) // MOE_GRANULE * MOE_GRANULE
    loc_off = jnp.cumsum(c16, axis=1) - c16
    tot = jnp.sum(c16, axis=0)
    tiles = (tot + (MOE_ROW_TILE - 1)) // MOE_ROW_TILE
    cum_tiles = jnp.cumsum(tiles)
    start = (cum_tiles - tiles) * MOE_ROW_TILE
    base = start[None, :] + jnp.cumsum(c16, axis=0) - c16
    max_tiles = _moe_rows(nt * ts, ts) // MOE_ROW_TILE
    tile_expert = jnp.minimum(
        jnp.sum(jnp.arange(max_tiles, dtype=jnp.int32)[:, None] >= cum_tiles[None, :], axis=1),
        N_EXPERTS - 1).astype(jnp.int32)
    loc_vec = jnp.zeros((nt, SUBLANES, LANES), F32).at[:, :, ROUTE_E_LANE:ROUTE_E_LANE + N_EXPERTS].set(
        jnp.broadcast_to(loc_off.astype(F32)[:, None, :], (nt, SUBLANES, N_EXPERTS)))
    return dict(
        chunks=(c16 // MOE_GRANULE).reshape(-1).astype(jnp.int32),
        loc_off=loc_off.reshape(-1).astype(jnp.int32),
        base=base.reshape(-1).astype(jnp.int32),
        step_chunks=(jnp.sum(c16, axis=1).reshape(-1, _moe_group(nt * ts, ts)).sum(axis=1)
                     // MOE_GRANULE).astype(jnp.int32),
        pad_start=(start + tot).astype(jnp.int32),
        pad_chunks=((tiles * MOE_ROW_TILE - tot) // MOE_GRANULE).astype(jnp.int32),
        n_tiles=cum_tiles[-1:].astype(jnp.int32),
        tile_expert=tile_expert,
        loc_vec=loc_vec,
    )


def _moe_group(t, ts):
    return MOE_TILES_PER_STEP if t % (MOE_TILES_PER_STEP * ts) == 0 else 1


def _moe_local_rows(ts):
    rows = 2 * ts + N_EXPERTS * (MOE_GRANULE - 1)
    return (rows + LANES - 1) // LANES * LANES


def _moe_rows(t, ts):
    rows = 2 * t + (t // ts) * N_EXPERTS * (MOE_GRANULE - 1) + N_EXPERTS * MOE_ROW_TILE
    return (rows + MOE_ROW_TILE - 1) // MOE_ROW_TILE * MOE_ROW_TILE


def _chunk_copy(src, dst, sem):
    return pltpu.make_async_copy(src, dst, sem)


def _wait_rows(copy_of_rows, n_chunks):
    def wait(rows):
        def one(_, c):
            copy_of_rows(rows).wait()
            return c
        return one
    shift = MOE_WAIT_BATCH.bit_length() - 1
    lax.fori_loop(0, lax.shift_right_logical(n_chunks, shift), wait(MOE_WAIT_BATCH * MOE_GRANULE), 0)
    lax.fori_loop(0, n_chunks & (MOE_WAIT_BATCH - 1), wait(MOE_GRANULE), 0)


def _dispatch_kernel(chunks_ref, loc_ref, base_ref, schunks_ref, pstart_ref, pchunks_ref,
                     hb_ref, sel_ref, comb_ref, locv_ref, lstrict_ref,
                     xs_ref, posw_ref, buf_ref, zero_ref, sem_ref, zsem_ref):
    step = pl.program_id(0)
    n_steps = pl.num_programs(0)
    par = step % 2
    group = locv_ref.shape[0]
    ts = hb_ref.shape[0] // group
    nl = buf_ref.shape[1]
    g = MOE_GRANULE

    def wait_chunks(n, pp):
        _wait_rows(lambda r: _chunk_copy(buf_ref.at[0, pl.ds(0, r)], xs_ref.at[pl.ds(0, r)],
                                         sem_ref.at[pp]), n)

    @pl.when(step >= 2)
    def _():
        wait_chunks(schunks_ref[step - 2], par)

    lane = lax.broadcasted_iota(jnp.int32, (1, LANES), 1)
    d = lax.broadcasted_iota(jnp.int32, (nl, ts), 0).astype(F32)
    for u in range(group):
        rows = slice(u * ts, (u + 1) * ts)
        sel = sel_ref[rows, :]
        comb = comb_ref[rows, :]
        picked = sel > 0.0
        rank = jnp.dot(lstrict_ref[...], sel.astype(BF16), preferred_element_type=F32)
        dest = rank + locv_ref[u][0:1, :]
        pos_a = jnp.min(jnp.where(picked, dest, 1e9), axis=1, keepdims=True)
        pos_b = jnp.max(jnp.where(picked, dest, -1.0), axis=1, keepdims=True)
        w_a = jnp.sum(jnp.where(picked & (dest == pos_a), comb, 0.0), axis=1, keepdims=True)
        w_b = jnp.sum(jnp.where(picked & (dest == pos_b), comb, 0.0), axis=1, keepdims=True)
        posw = jnp.where(lane == 0, pos_a, jnp.where(lane == 1, pos_b,
                         jnp.where(lane == 2, w_a, jnp.where(lane == 3, w_b, 0.0))))
        posw_ref[rows, :] = posw
        posw_t = posw.T
        perm_a = jnp.where(d == posw_t[0:1, :], 1.0, 0.0).astype(BF16)
        perm_b = jnp.where(d == posw_t[1:2, :], 1.0, 0.0).astype(BF16)
        slot = par * group + u
        buf_ref[slot, :, :D_MODEL] = jnp.dot(perm_a + perm_b, hb_ref[rows, :],
                                             preferred_element_type=F32).astype(BF16)
        wl = [jnp.where(lane == 0, hi.astype(F32), jnp.where(
                  lane == 1, mid.astype(F32), jnp.where(lane == 2, lo.astype(F32), 0.0))).astype(BF16)
              for hi, mid, lo in (_split3(w_a), _split3(w_b))]
        buf_ref[slot, :, D_MODEL:] = (
            jnp.dot(perm_a, wl[0], preferred_element_type=F32)
            + jnp.dot(perm_b, wl[1], preferred_element_type=F32)).astype(BF16)

    for u in range(group):
        for e in range(N_EXPERTS):
            idx = (step * group + u) * N_EXPERTS + e
            lo = loc_ref[idx]
            bs = base_ref[idx]

            def send(k, c, lo=lo, bs=bs, u=u):
                src = buf_ref.at[par * group + u, pl.ds(pl.multiple_of(lo + k * g, g), g)]
                dst = xs_ref.at[pl.ds(pl.multiple_of(bs + k * g, g), g)]
                _chunk_copy(src, dst, sem_ref.at[par]).start()
                return c
            lax.fori_loop(0, chunks_ref[idx], send, 0)

    @pl.when(step == n_steps - 1)
    def _():
        zero_ref[...] = jnp.zeros_like(zero_ref)
        for e in range(N_EXPERTS):
            ps = pstart_ref[e]

            def send_zero(k, c, ps=ps):
                dst = xs_ref.at[pl.ds(pl.multiple_of(ps + k * g, g), g)]
                _chunk_copy(zero_ref, dst, zsem_ref.at[0]).start()
                return c
            lax.fori_loop(0, pchunks_ref[e], send_zero, 0)
        for e in range(N_EXPERTS):
            def wait_zero(_, c):
                _chunk_copy(zero_ref, xs_ref.at[pl.ds(0, g)], zsem_ref.at[0]).wait()
                return c
            lax.fori_loop(0, pchunks_ref[e], wait_zero, 0)
        wait_chunks(schunks_ref[step], par)

        @pl.when(step >= 1)
        def _():
            wait_chunks(schunks_ref[step - 1], 1 - par)


def _dispatch(plan, hb, sel, comb, ts):
    t = hb.shape[0]
    group = _moe_group(t, ts)
    nl = _moe_local_rows(ts)
    rows = _moe_rows(t, ts)
    lstrict = jnp.tri(ts, k=-1, dtype=BF16)
    row = lambda i, *_: (i, 0)
    grid_spec = pltpu.PrefetchScalarGridSpec(
        num_scalar_prefetch=6,
        grid=(t // (group * ts),),
        in_specs=[
            pl.BlockSpec((group * ts, D_MODEL), row),
            pl.BlockSpec((group * ts, LANES), row),
            pl.BlockSpec((group * ts, LANES), row),
            pl.BlockSpec((group, SUBLANES, LANES), lambda i, *_: (i, 0, 0)),
            pl.BlockSpec((ts, ts), lambda i, *_: (0, 0)),
        ],
        out_specs=[
            pl.BlockSpec(memory_space=pl.ANY),
            pl.BlockSpec((group * ts, LANES), row),
        ],
        scratch_shapes=[
            pltpu.VMEM((2 * group, nl, MOE_ROW_LANES), BF16),
            pltpu.VMEM((MOE_GRANULE, MOE_ROW_LANES), BF16),
            pltpu.SemaphoreType.DMA((2,)),
            pltpu.SemaphoreType.DMA((1,)),
        ],
    )
    return pl.pallas_call(
        _dispatch_kernel,
        grid_spec=grid_spec,
        out_shape=(
            jax.ShapeDtypeStruct((rows, MOE_ROW_LANES), BF16),
            jax.ShapeDtypeStruct((t, LANES), F32),
        ),
        compiler_params=pltpu.CompilerParams(
            dimension_semantics=("arbitrary",), vmem_limit_bytes=VMEM_LIMIT_BYTES),
        name="moe_dispatch",
    )(plan["chunks"], plan["loc_off"], plan["base"], plan["step_chunks"], plan["pad_start"],
      plan["pad_chunks"], hb, sel, comb, plan["loc_vec"], lstrict)


def _expert_kernel(te_ref, nt_ref, x_ref, wg_ref, wu_ref, wd_ref, y_ref, wgb_ref, wub_ref, wdb_ref):
    j = pl.program_id(0)

    @pl.when(j < nt_ref[0])
    def _():
        @pl.when((j == 0) | (te_ref[j] != te_ref[jnp.maximum(j - 1, 0)]))
        def _():
            wgb_ref[...] = wg_ref[0].astype(BF16)
            wub_ref[...] = wu_ref[0].astype(BF16)
            wdb_ref[...] = wd_ref[0].astype(BF16)

        x = x_ref[:, :D_MODEL]
        wl = x_ref[:, D_MODEL:].astype(F32)
        w = wl[:, 0:1] + wl[:, 1:2] + wl[:, 2:3]
        gate = jnp.dot(x, wgb_ref[...], preferred_element_type=F32)
        up = jnp.dot(x, wub_ref[...], preferred_element_type=F32)
        act = (_silu(gate) * up).astype(BF16)
        y_ref[...] = (w * jnp.dot(act, wdb_ref[...], preferred_element_type=F32)).astype(y_ref.dtype)


def _experts(plan, xs, w_gate, w_up, w_down):
    rows = xs.shape[0]
    live = lambda j, te, nt: jnp.minimum(j, nt[0] - 1)
    expert = lambda j, te, nt: (te[live(j, te, nt)], 0, 0)
    grid_spec = pltpu.PrefetchScalarGridSpec(
        num_scalar_prefetch=2,
        grid=(rows // MOE_ROW_TILE,),
        in_specs=[
            pl.BlockSpec((MOE_ROW_TILE, MOE_ROW_LANES), lambda j, te, nt: (live(j, te, nt), 0)),
            pl.BlockSpec((1, D_MODEL, D_FF_EXPERT), expert),
            pl.BlockSpec((1, D_MODEL, D_FF_EXPERT), expert),
            pl.BlockSpec((1, D_FF_EXPERT, D_MODEL), expert),
        ],
        out_specs=pl.BlockSpec((MOE_ROW_TILE, D_MODEL), lambda j, te, nt: (live(j, te, nt), 0)),
        scratch_shapes=[
            pltpu.VMEM((D_MODEL, D_FF_EXPERT), BF16),
            pltpu.VMEM((D_MODEL, D_FF_EXPERT), BF16),
            pltpu.VMEM((D_FF_EXPERT, D_MODEL), BF16),
        ],
    )
    return pl.pallas_call(
        _expert_kernel,
        grid_spec=grid_spec,
        out_shape=jax.ShapeDtypeStruct((rows, D_MODEL), BF16),
        compiler_params=pltpu.CompilerParams(
            dimension_semantics=("arbitrary",), vmem_limit_bytes=VMEM_LIMIT_BYTES),
        name="moe_experts",
    )(plan["tile_expert"], plan["n_tiles"], xs, w_gate, w_up, w_down)


def _combine_kernel(chunks_ref, loc_ref, base_ref, schunks_ref,
                    y_ref, posw_ref, h_ref, g_ref, b_ref, o_ref, buf_ref, sem_ref):
    step = pl.program_id(0)
    n_steps = pl.num_programs(0)
    par = step % 2
    nl = buf_ref.shape[1]
    group = buf_ref.shape[0] // 2
    ts = h_ref.shape[0] // group
    g = MOE_GRANULE

    def fetch(ss, pp):
        for u in range(group):
            for e in range(N_EXPERTS):
                idx = (ss * group + u) * N_EXPERTS + e
                lo = loc_ref[idx]
                bs = base_ref[idx]

                def get(k, c, lo=lo, bs=bs, u=u):
                    src = y_ref.at[pl.ds(pl.multiple_of(bs + k * g, g), g)]
                    dst = buf_ref.at[pp * group + u, pl.ds(pl.multiple_of(lo + k * g, g), g)]
                    _chunk_copy(src, dst, sem_ref.at[pp]).start()
                    return c
                lax.fori_loop(0, chunks_ref[idx], get, 0)

    @pl.when(step == 0)
    def _():
        buf_ref[...] = jnp.zeros(buf_ref.shape, BF16)
        fetch(0, 0)

    @pl.when(step + 1 < n_steps)
    def _():
        fetch(step + 1, 1 - par)

    _wait_rows(lambda r: _chunk_copy(y_ref.at[pl.ds(0, r)], buf_ref.at[0, pl.ds(0, r)],
                                     sem_ref.at[par]), schunks_ref[step])

    d = lax.broadcasted_iota(jnp.int32, (1, nl), 1).astype(F32)
    for u in range(group):
        rows = slice(u * ts, (u + 1) * ts)
        yb = buf_ref[par * group + u]
        posw = posw_ref[rows, :]
        gather = jnp.where((posw[:, 0:1] == d) | (posw[:, 1:2] == d), 1.0, 0.0).astype(BF16)
        ffn = jnp.dot(gather, yb, preferred_element_type=F32)
        o_ref[rows, :] = _layer_norm(DEEPNORM_ALPHA * h_ref[rows, :] + ffn, g_ref[...], b_ref[...])


def _combine(plan, y, posw, h, ln_g, ln_b, ts):
    t = h.shape[0]
    group = _moe_group(t, ts)
    nl = _moe_local_rows(ts)
    row = lambda i, *_: (i, 0)
    const = lambda i, *_: (0, 0)
    grid_spec = pltpu.PrefetchScalarGridSpec(
        num_scalar_prefetch=4,
        grid=(t // (group * ts),),
        in_specs=[
            pl.BlockSpec(memory_space=pl.ANY),
            pl.BlockSpec((group * ts, LANES), row),
            pl.BlockSpec((group * ts, D_MODEL), row),
            pl.BlockSpec((1, D_MODEL), const),
            pl.BlockSpec((1, D_MODEL), const),
        ],
        out_specs=pl.BlockSpec((group * ts, D_MODEL), row),
        scratch_shapes=[
            pltpu.VMEM((2 * group, nl, D_MODEL), BF16),
            pltpu.SemaphoreType.DMA((2,)),
        ],
    )
    return pl.pallas_call(
        _combine_kernel,
        grid_spec=grid_spec,
        out_shape=jax.ShapeDtypeStruct((t, D_MODEL), F32),
        compiler_params=pltpu.CompilerParams(
            dimension_semantics=("arbitrary",), vmem_limit_bytes=VMEM_LIMIT_BYTES),
        name="moe_combine_ln",
    )(plan["chunks"], plan["loc_off"], plan["base"], plan["step_chunks"], y, posw, h, ln_g, ln_b)


def _pad_lanes(a, n=LANES):
    return jnp.pad(a, ((0, 0), (0, n - a.shape[1])))


def _layer(h2d, batch, seq, w_in, b_in, conv_w, conv_b, a_log, d_skip, ssd_norm_g, w_out,
           ln1_g, ln1_b, rg_w, rg_b, re_w, re_b, w_gate, w_up, w_down, ln2_g, ln2_b):
    o_dt = SSD_INNER + SSD_CONV_DIM
    o_q = o_dt + SSD_HEADS
    o_f = o_q + 3 * ATT_INNER
    w_main = jnp.concatenate([w_in[:, :o_dt], w_in[:, o_q:o_f]], axis=1).astype(BF16)
    b_main = jnp.concatenate([b_in[:o_dt], b_in[o_q:o_f]])[None, :]
    w_dt, w_f = w_in[:, o_dt:o_q], w_in[:, o_f:]
    w_small = _pad_lanes(jnp.concatenate([w_dt, w_f, w_dt], axis=1)).astype(BF16)
    b_small = _pad_lanes(jnp.concatenate([b_in[o_dt:o_q], b_in[o_f:], b_in[o_dt:o_q]])[None, :])
    a_neg = -jnp.exp(a_log.astype(F32))
    a_row = _pad_lanes(a_neg[None, :])
    a_exp = jnp.repeat(a_neg, SSD_HEAD_DIM)[None, :]
    dskip_exp = jnp.repeat(d_skip.astype(F32), SSD_HEAD_DIM)[None, :]

    z, xbc, q_t, k, v_t, gate, qf_t, kf = _in_proj(h2d, w_main, b_main, w_small, b_small, a_row, seq)
    y_ssd = _ssd(xbc, z, gate, conv_w, conv_b[None, :], a_exp, dskip_exp, ssd_norm_g[None, :],
                 batch, seq)
    y_att = _attention(q_t, k, v_t, qf_t, kf, batch, seq)

    gap = ROUTE_E_LANE - N_EXPERT_GROUPS
    w_r = _pad_lanes(jnp.concatenate(
        [rg_w, jnp.zeros((D_MODEL, gap), F32),
         jnp.transpose(re_w, (1, 0, 2)).reshape(D_MODEL, N_EXPERTS)], axis=1))
    b_r = _pad_lanes(jnp.concatenate(
        [rg_b, jnp.zeros((gap,), F32), re_b.reshape(N_EXPERTS)])[None, :])
    wr_hi = w_r.astype(BF16)
    wr_lo = (w_r - wr_hi.astype(F32)).astype(BF16)
    wo = w_out.astype(BF16)
    ts = min(MOE_SORT_TILE, h2d.shape[0])
    h1, h1b, comb, sel, cnt = _out_proj(y_ssd, y_att, h2d, wo[:SSD_INNER], wo[SSD_INNER:],
                                        ln1_g[None, :], ln1_b[None, :],
                                        jnp.concatenate([wr_hi, wr_lo], axis=1), b_r, ts)

    plan = _dispatch_plan(
        cnt[:, 0, ROUTE_E_LANE:ROUTE_E_LANE + N_EXPERTS].astype(jnp.int32), ts)
    xs, posw = _dispatch(plan, h1b, sel, comb, ts)
    y = _experts(plan, xs, w_gate, w_up, w_down)
    return _combine(plan, y, posw, h1, ln2_g[None, :], ln2_b[None, :], ts)


def kernel(x, w_in, b_in, conv_w, conv_b, a_log, d_skip, ssd_norm_g, w_out, ln1_g, ln1_b,
           router_group_w, router_group_b, router_expert_w, router_expert_b, w_gate, w_up,
           w_down, ln2_g, ln2_b):
    batch, seq, d = x.shape
    h = x.reshape(batch * seq, d)
    for l in range(w_in.shape[0]):
        h = _layer(h, batch, seq, w_in[l], b_in[l], conv_w[l], conv_b[l], a_log[l], d_skip[l],
                   ssd_norm_g[l], w_out[l], ln1_g[l], ln1_b[l], router_group_w[l],
                   router_group_b[l], router_expert_w[l], router_expert_b[l], w_gate[l],
                   w_up[l], w_down[l], ln2_g[l], ln2_b[l])
    return h.reshape(batch, seq, d)
```

```python
import functools

import jax
import jax.numpy as jnp
import numpy as np
from jax import lax
from jax.experimental import pallas as pl
from jax.experimental.pallas import tpu as pltpu

F32 = jnp.float32
BF16 = jnp.bfloat16

D_MODEL = 1024
SSD_HEADS = 8
SSD_HEAD_DIM = 64
SSD_INNER = SSD_HEADS * SSD_HEAD_DIM
SSD_GROUPS = 2
SSD_STATE = 128
CONV_WIDTH = 4
SSD_CONV_DIM = SSD_INNER + 2 * SSD_GROUPS * SSD_STATE
ATT_HEADS = 8
ATT_HEAD_DIM = 64
ATT_INNER = ATT_HEADS * ATT_HEAD_DIM
N_EXPERT_GROUPS = 4
EXPERTS_PER_GROUP = 4
N_EXPERTS = N_EXPERT_GROUPS * EXPERTS_PER_GROUP
D_FF_EXPERT = 512
DEPTH = 1
DEEPNORM_ALPHA = (2.0 * DEPTH) ** 0.25
LN_EPS = 1e-5
RMS_EPS = 1e-5

LANES = 128
SUBLANES = 8
VMEM_LIMIT_BYTES = 56 * 1024 * 1024

GATE_A_LANE = 0
GATE_F_LANE = SSD_HEADS
GATE_DT_LANE = 2 * SSD_HEADS
ROUTE_E_LANE = SUBLANES
MOE_SORT_TILE = 256
MOE_GRANULE = 16
MOE_ROW_TILE = 1024
MOE_TILES_PER_STEP = 4
MOE_ROW_LANES = D_MODEL + LANES
MOE_WAIT_BATCH = 8

IN_PROJ_ROWS = 1024
OUT_PROJ_ROWS = 1024
CUMSUM_BLOCK = 256
FB_LANES = 6
ATT_BLOCK = 256
ATT_PAIRS_PER_STEP = 4
ATT_BATCHES_PER_STEP = 1

SSD_ROWS = 256
SSD_STEP_ROWS = 512
SSD_CHUNK = 64
NEG_BIG = -1e30
LOG2E = 1.4426950408889634


def _split3(v):
    hi = v.astype(BF16)
    r1 = v - hi.astype(F32)
    mid = r1.astype(BF16)
    lo = (r1 - mid.astype(F32)).astype(BF16)
    return hi, mid, lo


def _silu(v):
    half = 0.5 * v
    return half * (1.0 + jnp.tanh(half))


def _softplus(v):
    return jnp.maximum(v, 0.0) + jnp.log1p(jnp.exp(-jnp.abs(v)))


def _in_proj_kernel(x_ref, wm_ref, bm_ref, ws_ref, bs_ref, arow_ref, tri_ref, place_ref, ones_ref,
                    z_ref, xbc_ref, qt_ref, k_ref, vt_ref, gate_ref, qft_ref, kf_ref,
                    carry_ref, *, tiles_per_seq):
    i = pl.program_id(0)
    tm = x_ref.shape[0]
    tb = qt_ref.shape[2]
    xb = x_ref[...].astype(BF16)

    def put_transposed(ref, val):
        for c in range(tm // tb):
            ref[c] = val[c * tb:(c + 1) * tb, :].T.astype(ref.dtype)

    s = jnp.dot(xb, ws_ref[...], preferred_element_type=F32) + bs_ref[...]
    lane = lax.broadcasted_iota(jnp.int32, (1, LANES), 1)
    sp = _softplus(s)
    log_f = -_softplus(-s)
    val = jnp.where(lane < GATE_F_LANE, sp * arow_ref[...],
                    jnp.where(lane < GATE_DT_LANE, log_f,
                              jnp.where(lane < GATE_DT_LANE + SSD_HEADS, sp, 0.0)))
    carry = jnp.where(i % tiles_per_seq == 0, 0.0, carry_ref[...])
    tri = tri_ref[...]
    cb = tri.shape[0]
    sums = []
    for c in range(tm // cb):
        hi, mid, lo = _split3(val[c * cb:(c + 1) * cb, :])
        part = (jnp.dot(tri, hi, preferred_element_type=F32)
                + jnp.dot(tri, mid, preferred_element_type=F32)
                + jnp.dot(tri, lo, preferred_element_type=F32)) + carry
        carry = part[cb - 1:cb, :]
        sums.append(part)
    cs = jnp.concatenate(sums, axis=0)
    carry_ref[...] = carry
    out = jnp.where(lane < GATE_DT_LANE, cs, val)
    gate_ref[...] = out
    pieces = jnp.concatenate(_split3(cs * LOG2E), axis=1)
    fb = jnp.dot(pieces, place_ref[...], preferred_element_type=F32) + ones_ref[...]
    put_transposed(qft_ref, fb[:, :LANES])
    kf_ref[...] = fb[:, LANES:].astype(BF16)

    col = 0
    for ref, scale in ((z_ref, None), (xbc_ref, None), (qt_ref, ATT_HEAD_DIM ** -0.5 * LOG2E),
                       (k_ref, None), (vt_ref, None)):
        transposed = ref is qt_ref or ref is vt_ref
        n = ATT_INNER if transposed else ref.shape[1]
        acc = jnp.dot(xb, wm_ref[:, col:col + n], preferred_element_type=F32) + bm_ref[:, col:col + n]
        if scale is not None:
            acc = acc * scale
        if transposed:
            put_transposed(ref, acc)
        else:
            ref[...] = acc.astype(ref.dtype)
        col += n


def _forget_bias_placement():
    place = np.zeros((3 * LANES, 2 * LANES), np.float32)
    ones = np.zeros((1, 2 * LANES), np.float32)
    for h in range(ATT_HEADS):
        for piece in range(3):
            src = piece * LANES + GATE_F_LANE + h
            place[src, FB_LANES * h + piece] = 1.0
            place[src, LANES + FB_LANES * h + 3 + piece] = -1.0
            ones[0, FB_LANES * h + 3 + piece] = 1.0
            ones[0, LANES + FB_LANES * h + piece] = 1.0
    return jnp.asarray(place, BF16), jnp.asarray(ones, F32)


def _in_proj(x2d, w_main, b_main, w_small, b_small, a_row, seq):
    t = x2d.shape[0]
    tm = min(IN_PROJ_ROWS, seq)
    tk = min(ATT_BLOCK, seq)
    cum = min(CUMSUM_BLOCK, tm)
    tri = jnp.tri(cum, dtype=BF16)
    place, ones = _forget_bias_placement()
    n_main = w_main.shape[1]
    row = lambda i: (i, 0)
    const = lambda i: (0, 0)
    outs = (
        jax.ShapeDtypeStruct((t, SSD_INNER), BF16),
        jax.ShapeDtypeStruct((t, SSD_CONV_DIM), BF16),
        jax.ShapeDtypeStruct((t // tk, ATT_INNER, tk), BF16),
        jax.ShapeDtypeStruct((t, ATT_INNER), BF16),
        jax.ShapeDtypeStruct((t // tk, ATT_INNER, tk), BF16),
        jax.ShapeDtypeStruct((t, LANES), F32),
        jax.ShapeDtypeStruct((t // tk, LANES, tk), BF16),
        jax.ShapeDtypeStruct((t, LANES), BF16),
    )
    return pl.pallas_call(
        functools.partial(_in_proj_kernel, tiles_per_seq=seq // tm),
        grid=(t // tm,),
        in_specs=[
            pl.BlockSpec((tm, D_MODEL), row),
            pl.BlockSpec((D_MODEL, n_main), const),
            pl.BlockSpec((1, n_main), const),
            pl.BlockSpec((D_MODEL, LANES), const),
            pl.BlockSpec((1, LANES), const),
            pl.BlockSpec((1, LANES), const),
            pl.BlockSpec((cum, cum), const),
            pl.BlockSpec((3 * LANES, 2 * LANES), const),
            pl.BlockSpec((1, 2 * LANES), const),
        ],
        out_specs=[
            pl.BlockSpec((tm, SSD_INNER), row),
            pl.BlockSpec((tm, SSD_CONV_DIM), row),
            pl.BlockSpec((tm // tk, ATT_INNER, tk), lambda i: (i, 0, 0)),
            pl.BlockSpec((tm, ATT_INNER), row),
            pl.BlockSpec((tm // tk, ATT_INNER, tk), lambda i: (i, 0, 0)),
            pl.BlockSpec((tm, LANES), row),
            pl.BlockSpec((tm // tk, LANES, tk), lambda i: (i, 0, 0)),
            pl.BlockSpec((tm, LANES), row),
        ],
        out_shape=outs,
        scratch_shapes=[pltpu.VMEM((1, LANES), F32)],
        compiler_params=pltpu.CompilerParams(
            dimension_semantics=("arbitrary",), vmem_limit_bytes=VMEM_LIMIT_BYTES),
        name="in_proj",
    )(x2d, w_main, b_main, w_small, b_small, a_row, tri, place, ones)


def _expand_heads(gate, lane0, rows):
    lane = lax.broadcasted_iota(jnp.int32, (1, LANES), 1)
    pieces = []
    for p in range(SSD_HEADS // 2):
        a = jnp.broadcast_to(gate[:, lane0 + 2 * p:lane0 + 2 * p + 1], (rows, LANES))
        b = jnp.broadcast_to(gate[:, lane0 + 2 * p + 1:lane0 + 2 * p + 2], (rows, LANES))
        pieces.append(jnp.where(lane < SSD_HEAD_DIM, a, b))
    return jnp.concatenate(pieces, axis=1)


def _ssd_kernel(xbc_ref, z_ref, gate_ref, cw_ref, cb_ref, aexp_ref, dskip_ref, ng_ref, shift_ref,
                y_ref, ext_ref, state_ref):
    j = pl.program_id(1)
    rows = shift_ref.shape[1]
    q = SSD_CHUNK
    gw = SSD_INNER // SSD_GROUPS
    hpg = SSD_HEADS // SSD_GROUPS
    pad = SUBLANES

    @pl.when(j == 0)
    def _():
        ext_ref[0:pad, :] = jnp.zeros((pad, SSD_CONV_DIM), F32)
        state_ref[...] = jnp.zeros_like(state_ref)

    r_i = lax.broadcasted_iota(jnp.int32, (q, gw), 0)
    c_i = lax.broadcasted_iota(jnp.int32, (q, gw), 1)
    s_i = c_i % q
    eye_t = s_i == r_i
    tril_t = s_i <= r_i
    rb = lax.broadcasted_iota(jnp.int32, (hpg * q, gw), 0) // q
    cbk = lax.broadcasted_iota(jnp.int32, (hpg * q, gw), 1) // SSD_HEAD_DIM
    blk = rb == cbk

    for t in range(xbc_ref.shape[0] // rows):
        sub = slice(t * rows, (t + 1) * rows)
        xb = xbc_ref[sub, :]
        xf = xb.astype(F32)
        conv = cb_ref[...] + xf * cw_ref[CONV_WIDTH - 1:CONV_WIDTH, :]
        for s in range(1, CONV_WIDTH):
            shifted = jnp.dot(shift_ref[s - 1], xb, preferred_element_type=F32)
            conv = conv + shifted * cw_ref[CONV_WIDTH - 1 - s:CONV_WIDTH - s, :]
        ext_ref[pad:2 * pad, :] = xf[0:pad, :]
        head = cb_ref[...]
        for kk in range(CONV_WIDTH):
            off = pad - (CONV_WIDTH - 1) + kk
            head = head + ext_ref[off:off + pad, :] * cw_ref[kk:kk + 1, :]
        ext_ref[0:pad, :] = xf[rows - pad:rows, :]
        u = _silu(jnp.concatenate([head, conv[pad:, :]], axis=0))
        xs = u[:, :SSD_INNER]
        b_all = u[:, SSD_INNER:SSD_INNER + SSD_GROUPS * SSD_STATE].astype(BF16)
        c_all = u[:, SSD_INNER + SSD_GROUPS * SSD_STATE:].astype(BF16)

        gate = gate_ref[sub, :]
        dt_e = _expand_heads(gate, GATE_DT_LANE, rows)
        a_e = _expand_heads(gate, GATE_A_LANE, rows)
        xdt = xs * dt_e
        da_e = dt_e * aexp_ref[...]

        ys = []
        for c in range(rows // q):
            r0 = c * q
            a_c = a_e[r0:r0 + q, :]
            a_prev = a_c[0:1, :] - da_e[r0:r0 + 1, :]
            a_end = a_c[q - 1:q, :]
            y_groups = []
            for g in range(SSD_GROUPS):
                l0 = g * gw
                a_cg = a_c[:, l0:l0 + gw]
                diag = jnp.sum(jnp.where(eye_t, a_cg, 0.0), axis=0, keepdims=True)
                seg = jnp.minimum(a_cg - diag, 0.0)
                lcat = jnp.where(tril_t, jnp.exp(seg), 0.0)
                bg = b_all[r0:r0 + q, g * SSD_STATE:(g + 1) * SSD_STATE]
                cg = c_all[r0:r0 + q, g * SSD_STATE:(g + 1) * SSD_STATE]
                b_tile = jnp.concatenate([bg] * hpg, axis=0)
                gcat = lax.dot_general(cg, b_tile, (((1,), (1,)), ((), ())),
                                       preferred_element_type=F32)
                xg = xdt[r0:r0 + q, l0:l0 + gw]
                xg_b = xg.astype(BF16)
                xbd = jnp.where(blk, jnp.concatenate([xg_b] * hpg, axis=0), jnp.zeros((), BF16))
                y_diag = jnp.dot((gcat * lcat).astype(BF16), xbd, preferred_element_type=F32)
                st = state_ref[g]
                y_off = jnp.dot(cg, st.astype(BF16), preferred_element_type=F32) \
                    * jnp.exp(a_cg - a_prev[:, l0:l0 + gw])
                dte = jnp.exp(a_end[:, l0:l0 + gw] - a_cg)
                upd = lax.dot_general(bg, (xg * dte).astype(BF16), (((0,), (0,)), ((), ())),
                                      preferred_element_type=F32)
                state_ref[g] = st * jnp.exp(a_end[:, l0:l0 + gw] - a_prev[:, l0:l0 + gw]) + upd
                y_groups.append(y_diag + y_off)
            ys.append(jnp.concatenate(y_groups, axis=1))
        y = jnp.concatenate(ys, axis=0) + xs * dskip_ref[...]
        y = y * _silu(z_ref[sub, :].astype(F32))
        outs = []
        for g in range(SSD_GROUPS):
            yg = y[:, g * gw:(g + 1) * gw]
            ms = jnp.mean(yg * yg, axis=1, keepdims=True)
            outs.append(yg * lax.rsqrt(ms + RMS_EPS))
        y_ref[sub, :] = (jnp.concatenate(outs, axis=1) * ng_ref[...]).astype(y_ref.dtype)


def _ssd(xbc, z, gate, conv_w, conv_b, a_exp, dskip_exp, norm_g, batch, seq):
    t = xbc.shape[0]
    rows = min(SSD_ROWS, seq)
    step_rows = min(SSD_STEP_ROWS, seq)
    nj = seq // step_rows
    row = lambda b, j: (b * nj + j, 0)
    const = lambda b, j: (0, 0)
    shifts = jnp.stack([jnp.eye(rows, k=-s, dtype=BF16) for s in range(1, CONV_WIDTH)])
    return pl.pallas_call(
        _ssd_kernel,
        grid=(batch, nj),
        in_specs=[
            pl.BlockSpec((step_rows, SSD_CONV_DIM), row),
            pl.BlockSpec((step_rows, SSD_INNER), row),
            pl.BlockSpec((step_rows, LANES), row),
            pl.BlockSpec((CONV_WIDTH, SSD_CONV_DIM), const),
            pl.BlockSpec((1, SSD_CONV_DIM), const),
            pl.BlockSpec((1, SSD_INNER), const),
            pl.BlockSpec((1, SSD_INNER), const),
            pl.BlockSpec((1, SSD_INNER), const),
            pl.BlockSpec((CONV_WIDTH - 1, rows, rows), lambda b, j: (0, 0, 0)),
        ],
        out_specs=pl.BlockSpec((step_rows, SSD_INNER), row),
        out_shape=jax.ShapeDtypeStruct((t, SSD_INNER), BF16),
        scratch_shapes=[
            pltpu.VMEM((2 * SUBLANES, SSD_CONV_DIM), F32),
            pltpu.VMEM((SSD_GROUPS, SSD_STATE, SSD_INNER // SSD_GROUPS), F32),
        ],
        compiler_params=pltpu.CompilerParams(
            dimension_semantics=("arbitrary", "arbitrary"), vmem_limit_bytes=VMEM_LIMIT_BYTES),
        name="ssd",
    )(xbc, z, gate, conv_w, conv_b, a_exp, dskip_exp, norm_g, shifts)


def _attn_kernel(qt_ref, k_ref, vt_ref, qft_ref, kf_ref, o_ref, *scratch, tb, pairs, batches):
    pair0 = pl.program_id(1) * pairs
    nb = qt_ref.shape[0] // batches
    seq = nb * tb
    chains = [(bb, pp) for bb in range(batches) for pp in range(pairs)]
    s_refs, acc_refs = scratch[:len(chains)], scratch[len(chains):]
    feat = lax.broadcasted_iota(jnp.int32, (LANES, 1), 0)
    in_head = (feat < ATT_HEAD_DIM, feat >= ATT_HEAD_DIM)
    ones = jnp.ones((2 * SUBLANES, tb), BF16)
    key = lax.broadcasted_iota(jnp.int32, (tb, 2 * tb), 0)
    qry = lax.broadcasted_iota(jnp.int32, (tb, 2 * tb), 1)
    causal = jnp.where(qry >= tb, qry - tb, qry) >= key
    zero = jnp.zeros((), BF16)

    def logits_t(qa_t, c, jk):
        bb, pp = chains[c]
        k0 = pl.multiple_of(bb * seq + jk * tb, tb)
        ka = jnp.concatenate([k_ref[pl.ds(k0, tb), pp * LANES:(pp + 1) * LANES],
                              kf_ref[pl.ds(k0, tb), :]], axis=1)
        return jnp.dot(ka, qa_t, preferred_element_type=F32)

    def update(s_t, m, c, jk):
        bb, pp = chains[c]
        m_new = jnp.maximum(m, jnp.max(s_t, axis=0, keepdims=True))
        alpha = jnp.exp2(m - m_new)
        p_t = jnp.exp2(s_t - m_new).astype(BF16)
        for hh in range(2):
            r0 = pp * LANES + hh * ATT_HEAD_DIM
            va_t = jnp.concatenate([vt_ref[bb * nb + jk, r0:r0 + ATT_HEAD_DIM, :], ones], axis=0)
            cols = slice(hh * tb, (hh + 1) * tb)
            acc_refs[c][hh] = alpha[:, cols] * acc_refs[c][hh] + jnp.dot(
                va_t, p_t[:, cols], preferred_element_type=F32)
        return m_new

    def build_queries(i):
        qas = []
        for bb, pp in chains:
            qf_t = qft_ref[bb * nb + i]
            q_t = qt_ref[bb * nb + i, pp * LANES:(pp + 1) * LANES, :]
            cols = []
            for hh in range(2):
                fb0 = FB_LANES * (2 * (pair0 + pp) + hh)
                in_fb = (feat >= fb0) & (feat < fb0 + FB_LANES)
                cols.append(jnp.concatenate(
                    [jnp.where(in_head[hh], q_t, zero), jnp.where(in_fb, qf_t, zero)], axis=0))
            qas.append(jnp.concatenate(cols, axis=1))
        return qas

    def first_logits(i):
        qas = build_queries(i)
        for c in range(len(chains)):
            s_refs[c][2] = logits_t(qas[c], c, 0)

    def q_block(i, _):
        q0 = pl.multiple_of(i * tb, tb)
        qas = build_queries(i)
        for c in range(len(chains)):
            acc_refs[c][...] = jnp.zeros(acc_refs[c].shape, F32)

        def step(jk, ms, rd, wr):
            new = []
            for c in range(len(chains)):
                s_t = s_refs[c][rd]
                s_refs[c][wr] = logits_t(qas[c], c, jk + 1)
                new.append(update(s_t, ms[c], c, jk))
            return tuple(new)

        def two_steps(t2, ms):
            return step(2 * t2 + 2, step(2 * t2 + 1, ms, 1, 0), 0, 1)

        ms = tuple(jnp.full((1, 2 * tb), NEG_BIG, F32) for _ in chains)
        ms = lax.cond(i >= 1, lambda v: step(0, v, 2, 1), lambda v: v, ms)
        ms = lax.fori_loop(0, lax.shift_right_logical(jnp.maximum(i - 1, 0), 1), two_steps, ms)
        ms = lax.cond((i >= 2) & (i % 2 == 0), lambda v: step(i - 1, v, 1, 0), lambda v: v, ms)

        last = jnp.where(i >= 1, i % 2, 2)
        diag = [jnp.where(causal, s_refs[c][last], NEG_BIG) for c in range(len(chains))]
        first_logits(jnp.minimum(i + 1, nb - 1))
        for c, (bb, pp) in enumerate(chains):
            update(diag[c], ms[c], c, i)
            heads = []
            for hh in range(2):
                acc_t = acc_refs[c][hh]
                inv = 1.0 / acc_t[ATT_HEAD_DIM:ATT_HEAD_DIM + 1, :]
                heads.append(acc_t[:ATT_HEAD_DIM, :] * inv)
            out_t = jnp.concatenate(heads, axis=0)
            o_ref[pl.ds(bb * seq + q0, tb), pp * LANES:(pp + 1) * LANES] = out_t.T.astype(o_ref.dtype)
        return 0

    first_logits(0)
    lax.fori_loop(0, nb, q_block, 0)


def _attention(q_t, k, v_t, qf_t, kf, batch, seq):
    t = k.shape[0]
    tb = min(ATT_BLOCK, seq)
    nb = seq // tb
    pairs = ATT_PAIRS_PER_STEP
    batches = ATT_BATCHES_PER_STEP if batch % ATT_BATCHES_PER_STEP == 0 else 1
    width = pairs * LANES
    return pl.pallas_call(
        functools.partial(_attn_kernel, tb=tb, pairs=pairs, batches=batches),
        grid=(batch // batches, ATT_HEADS // (2 * pairs)),
        in_specs=[
            pl.BlockSpec((batches * nb, width, tb), lambda b, p: (b, p, 0)),
            pl.BlockSpec((batches * seq, width), lambda b, p: (b, p)),
            pl.BlockSpec((batches * nb, width, tb), lambda b, p: (b, p, 0)),
            pl.BlockSpec((batches * nb, LANES, tb), lambda b, p: (b, 0, 0)),
            pl.BlockSpec((batches * seq, LANES), lambda b, p: (b, 0)),
        ],
        out_specs=pl.BlockSpec((batches * seq, width), lambda b, p: (b, p)),
        out_shape=jax.ShapeDtypeStruct((t, ATT_INNER), BF16),
        scratch_shapes=([pltpu.VMEM((3, tb, 2 * tb), F32) for _ in range(batches * pairs)]
                        + [pltpu.VMEM((2, ATT_HEAD_DIM + 2 * SUBLANES, tb), F32)
                           for _ in range(batches * pairs)]),
        compiler_params=pltpu.CompilerParams(
            dimension_semantics=("arbitrary", "arbitrary"), vmem_limit_bytes=VMEM_LIMIT_BYTES),
        name="fox_attention",
    )(q_t, k, v_t, qf_t, kf)


def _layer_norm(u, g, b):
    mu = jnp.mean(u, axis=1, keepdims=True)
    d = u - mu
    var = jnp.mean(d * d, axis=1, keepdims=True)
    return d * lax.rsqrt(var + LN_EPS) * g + b


def _route(lg):
    rows = lg.shape[0]
    lt = lg.T
    gl = lt[0:N_EXPERT_GROUPS, :]
    g_row = lax.broadcasted_iota(jnp.int32, (N_EXPERT_GROUPS, 1), 0).astype(F32)
    gmax = jnp.max(gl, axis=0, keepdims=True)
    gsum = jnp.sum(jnp.exp(gl - gmax), axis=0, keepdims=True)
    g_val = 1.0 / gsum
    g_idx = jnp.min(jnp.where(gl == gmax, g_row, float(N_EXPERT_GROUPS)), axis=0, keepdims=True)
    el = lt[ROUTE_E_LANE:ROUTE_E_LANE + N_EXPERTS, :]
    e_row = lax.broadcasted_iota(jnp.int32, (N_EXPERTS, 1), 0)
    e_grp = (e_row // EXPERTS_PER_GROUP).astype(F32)
    e_row = e_row.astype(F32)
    in_grp = e_grp == g_idx
    m1 = jnp.max(jnp.where(in_grp, el, NEG_BIG), axis=0, keepdims=True)
    i1 = jnp.min(jnp.where(in_grp & (el == m1), e_row, float(N_EXPERTS)), axis=0, keepdims=True)
    rest = in_grp & (e_row != i1)
    m2 = jnp.max(jnp.where(rest, el, NEG_BIG), axis=0, keepdims=True)
    i2 = jnp.min(jnp.where(rest & (el == m2), e_row, float(N_EXPERTS)), axis=0, keepdims=True)
    r = jnp.exp(m2 - m1)
    w1 = 1.0 / (1.0 + r)
    w2 = r * w1
    comb_e = jnp.where(e_row == i1, g_val * w1, jnp.where(e_row == i2, g_val * w2, 0.0))
    sel_e = jnp.where((e_row == i1) | (e_row == i2), 1.0, 0.0)
    top = jnp.zeros((ROUTE_E_LANE, rows), F32)
    bottom = jnp.zeros((LANES - ROUTE_E_LANE - N_EXPERTS, rows), F32)
    comb = jnp.concatenate([top, comb_e, bottom], axis=0).T
    sel = jnp.concatenate([top, sel_e, bottom], axis=0).T
    return comb, sel


def _out_proj_kernel(ys_ref, ya_ref, x_ref, wa_ref, wb_ref, g_ref, b_ref, wr_ref, br_ref,
                     h_ref, hb_ref, comb_ref, sel_ref, cnt_ref):
    mix = jnp.dot(ys_ref[...], wa_ref[...], preferred_element_type=F32)
    mix = mix + jnp.dot(ya_ref[...], wb_ref[...], preferred_element_type=F32)
    h = _layer_norm(DEEPNORM_ALPHA * x_ref[...] + mix, g_ref[...], b_ref[...])
    h_ref[...] = h
    h_hi = h.astype(BF16)
    hb_ref[...] = h_hi
    h_lo = (h - h_hi.astype(F32)).astype(BF16)
    w_hi = wr_ref[:, :LANES]
    lg = (jnp.dot(h_hi, w_hi, preferred_element_type=F32)
          + jnp.dot(h_lo, w_hi, preferred_element_type=F32)
          + jnp.dot(h_hi, wr_ref[:, LANES:], preferred_element_type=F32)) + br_ref[...]
    comb, sel = _route(lg)
    comb_ref[...] = comb
    sel_ref[...] = sel
    ts = h.shape[0] // cnt_ref.shape[0]
    for c in range(cnt_ref.shape[0]):
        cnt = jnp.sum(sel[c * ts:(c + 1) * ts, :], axis=0, keepdims=True)
        cnt_ref[c] = jnp.broadcast_to(cnt, (SUBLANES, LANES))


def _out_proj(ys, ya, x2d, wo_a, wo_b, ln_g, ln_b, wr, br, ts):
    t = x2d.shape[0]
    tm = min(OUT_PROJ_ROWS, t)
    row = lambda i: (i, 0)
    const = lambda i: (0, 0)
    return pl.pallas_call(
        _out_proj_kernel,
        grid=(t // tm,),
        in_specs=[
            pl.BlockSpec((tm, SSD_INNER), row),
            pl.BlockSpec((tm, ATT_INNER), row),
            pl.BlockSpec((tm, D_MODEL), row),
            pl.BlockSpec((SSD_INNER, D_MODEL), const),
            pl.BlockSpec((ATT_INNER, D_MODEL), const),
            pl.BlockSpec((1, D_MODEL), const),
            pl.BlockSpec((1, D_MODEL), const),
            pl.BlockSpec((D_MODEL, 2 * LANES), const),
            pl.BlockSpec((1, LANES), const),
        ],
        out_specs=[
            pl.BlockSpec((tm, D_MODEL), row),
            pl.BlockSpec((tm, D_MODEL), row),
            pl.BlockSpec((tm, LANES), row),
            pl.BlockSpec((tm, LANES), row),
            pl.BlockSpec((tm // ts, SUBLANES, LANES), lambda i: (i, 0, 0)),
        ],
        out_shape=(
            jax.ShapeDtypeStruct((t, D_MODEL), F32),
            jax.ShapeDtypeStruct((t, D_MODEL), BF16),
            jax.ShapeDtypeStruct((t, LANES), F32),
            jax.ShapeDtypeStruct((t, LANES), F32),
            jax.ShapeDtypeStruct((t // ts, SUBLANES, LANES), F32),
        ),
        compiler_params=pltpu.CompilerParams(
            dimension_semantics=("arbitrary",), vmem_limit_bytes=VMEM_LIMIT_BYTES),
        name="out_proj_ln_router",
    )(ys, ya, x2d, wo_a, wo_b, ln_g, ln_b, wr, br)


def _dispatch_plan(cnt, ts):
    nt = cnt.shape[0]
    c16 = (cnt + (MOE_GRANULE - 1)) // MOE_GRANULE * MOE_GRANULE
    loc_off = jnp.cumsum(c16, axis=1) - c16
    tot = jnp.sum(c16, axis=0)
    tiles = (tot + (MOE_ROW_TILE - 1)) // MOE_ROW_TILE
    cum_tiles = jnp.cumsum(tiles)
    start = (cum_tiles - tiles) * MOE_ROW_TILE
    base = start[None, :] + jnp.cumsum(c16, axis=0) - c16
    max_tiles = _moe_rows(nt * ts, ts) // MOE_ROW_TILE
    tile_expert = jnp.minimum(
        jnp.sum(jnp.arange(max_tiles, dtype=jnp.int32)[:, None] >= cum_tiles[None, :], axis=1),
        N_EXPERTS - 1).astype(jnp.int32)
    loc_t = jnp.broadcast_to(loc_off.astype(F32)[:, :, None], (nt, N_EXPERTS, LANES))
    return dict(
        chunks=(c16 // MOE_GRANULE).reshape(-1).astype(jnp.int32),
        loc_off=loc_off.reshape(-1).astype(jnp.int32),
        base=base.reshape(-1).astype(jnp.int32),
        step_chunks=(jnp.sum(c16, axis=1).reshape(-1, _moe_group(nt * ts, ts)).sum(axis=1)
                     // MOE_GRANULE).astype(jnp.int32),
        pad_start=(start + tot).astype(jnp.int32),
        pad_chunks=((tiles * MOE_ROW_TILE - tot) // MOE_GRANULE).astype(jnp.int32),
        n_tiles=cum_tiles[-1:].astype(jnp.int32),
        tile_expert=tile_expert,
        loc_t=loc_t,
    )


def _moe_group(t, ts):
    return MOE_TILES_PER_STEP if t % (MOE_TILES_PER_STEP * ts) == 0 else 1


def _moe_local_rows(ts):
    rows = 2 * ts + N_EXPERTS * (MOE_GRANULE - 1)
    return (rows + LANES - 1) // LANES * LANES


def _moe_rows(t, ts):
    rows = 2 * t + (t // ts) * N_EXPERTS * (MOE_GRANULE - 1) + N_EXPERTS * MOE_ROW_TILE
    return (rows + MOE_ROW_TILE - 1) // MOE_ROW_TILE * MOE_ROW_TILE


def _chunk_copy(src, dst, sem):
    return pltpu.make_async_copy(src, dst, sem)


def _wait_rows(copy_of_rows, n_chunks):
    def wait(rows):
        def one(_, c):
            copy_of_rows(rows).wait()
            return c
        return one
    shift = MOE_WAIT_BATCH.bit_length() - 1
    lax.fori_loop(0, lax.shift_right_logical(n_chunks, shift), wait(MOE_WAIT_BATCH * MOE_GRANULE), 0)
    lax.fori_loop(0, n_chunks & (MOE_WAIT_BATCH - 1), wait(MOE_GRANULE), 0)


def _dispatch_kernel(chunks_ref, loc_ref, base_ref, schunks_ref, pstart_ref, pchunks_ref,
                     hb_ref, sel_ref, comb_ref, loct_ref, ustrict_ref,
                     xs_ref, posw_ref, buf_ref, zero_ref, sem_ref, zsem_ref):
    step = pl.program_id(0)
    n_steps = pl.num_programs(0)
    par = step % 2
    group = loct_ref.shape[0]
    ts = hb_ref.shape[0] // group
    nl = buf_ref.shape[1]
    g = MOE_GRANULE

    def wait_chunks(n, pp):
        _wait_rows(lambda r: _chunk_copy(buf_ref.at[0, pl.ds(0, r)], xs_ref.at[pl.ds(0, r)],
                                         sem_ref.at[pp]), n)

    @pl.when(step >= 2)
    def _():
        wait_chunks(schunks_ref[step - 2], par)

    lane = lax.broadcasted_iota(jnp.int32, (1, LANES), 1)
    prow = lax.broadcasted_iota(jnp.int32, (LANES, 1), 0)
    d = lax.broadcasted_iota(jnp.int32, (nl, ts), 0).astype(F32)
    experts = slice(ROUTE_E_LANE, ROUTE_E_LANE + N_EXPERTS)
    for u in range(group):
        rows = slice(u * ts, (u + 1) * ts)
        sel_t = sel_ref[rows, :].T[experts, :]
        comb_t = comb_ref[rows, :].T[experts, :]
        picked = sel_t > 0.0
        rank = jnp.dot(sel_t.astype(BF16), ustrict_ref[...], preferred_element_type=F32)
        dest = rank + jnp.concatenate([loct_ref[u]] * (ts // LANES), axis=1)
        pos_a = jnp.min(jnp.where(picked, dest, 1e9), axis=0, keepdims=True)
        pos_b = jnp.max(jnp.where(picked, dest, -1.0), axis=0, keepdims=True)
        w_a = jnp.sum(jnp.where(picked & (dest == pos_a), comb_t, 0.0), axis=0, keepdims=True)
        w_b = jnp.sum(jnp.where(picked & (dest == pos_b), comb_t, 0.0), axis=0, keepdims=True)
        pieces = [p.astype(F32) for w in (w_a, w_b) for p in _split3(w)]
        posw_t = jnp.zeros((LANES, ts), F32)
        for r, v in enumerate([pos_a, pos_b] + pieces):
            posw_t = jnp.where(prow == r, v, posw_t)
        posw = posw_t.T
        posw_ref[rows, :] = posw
        perm_a = jnp.where(d == pos_a, 1.0, 0.0).astype(BF16)
        perm_b = jnp.where(d == pos_b, 1.0, 0.0).astype(BF16)
        slot = par * group + u
        buf_ref[slot, :, :D_MODEL] = jnp.dot(perm_a + perm_b, hb_ref[rows, :],
                                             preferred_element_type=F32).astype(BF16)
        wl_a = jnp.where((lane >= 2) & (lane < 5), posw, 0.0).astype(BF16)
        wl_b = jnp.where((lane >= 5) & (lane < 8), posw, 0.0).astype(BF16)
        buf_ref[slot, :, D_MODEL:] = (
            jnp.dot(perm_a, wl_a, preferred_element_type=F32)
            + jnp.dot(perm_b, wl_b, preferred_element_type=F32)).astype(BF16)

    for u in range(group):
        for e in range(N_EXPERTS):
            idx = (step * group + u) * N_EXPERTS + e
            lo = loc_ref[idx]
            bs = base_ref[idx]

            def send(k, c, lo=lo, bs=bs, u=u):
                src = buf_ref.at[par * group + u, pl.ds(pl.multiple_of(lo + k * g, g), g)]
                dst = xs_ref.at[pl.ds(pl.multiple_of(bs + k * g, g), g)]
                _chunk_copy(src, dst, sem_ref.at[par]).start()
                return c
            lax.fori_loop(0, chunks_ref[idx], send, 0)

    @pl.when(step == n_steps - 1)
    def _():
        zero_ref[...] = jnp.zeros_like(zero_ref)
        for e in range(N_EXPERTS):
            ps = pstart_ref[e]

            def send_zero(k, c, ps=ps):
                dst = xs_ref.at[pl.ds(pl.multiple_of(ps + k * g, g), g)]
                _chunk_copy(zero_ref, dst, zsem_ref.at[0]).start()
                return c
            lax.fori_loop(0, pchunks_ref[e], send_zero, 0)
        for e in range(N_EXPERTS):
            def wait_zero(_, c):
                _chunk_copy(zero_ref, xs_ref.at[pl.ds(0, g)], zsem_ref.at[0]).wait()
                return c
            lax.fori_loop(0, pchunks_ref[e], wait_zero, 0)
        wait_chunks(schunks_ref[step], par)

        @pl.when(step >= 1)
        def _():
            wait_chunks(schunks_ref[step - 1], 1 - par)


def _dispatch(plan, hb, sel, comb, ts):
    t = hb.shape[0]
    group = _moe_group(t, ts)
    nl = _moe_local_rows(ts)
    rows = _moe_rows(t, ts)
    ustrict = jnp.tri(ts, k=-1, dtype=BF16).T
    row = lambda i, *_: (i, 0)
    grid_spec = pltpu.PrefetchScalarGridSpec(
        num_scalar_prefetch=6,
        grid=(t // (group * ts),),
        in_specs=[
            pl.BlockSpec((group * ts, D_MODEL), row),
            pl.BlockSpec((group * ts, LANES), row),
            pl.BlockSpec((group * ts, LANES), row),
            pl.BlockSpec((group, N_EXPERTS, LANES), lambda i, *_: (i, 0, 0)),
            pl.BlockSpec((ts, ts), lambda i, *_: (0, 0)),
        ],
        out_specs=[
            pl.BlockSpec(memory_space=pl.ANY),
            pl.BlockSpec((group * ts, LANES), row),
        ],
        scratch_shapes=[
            pltpu.VMEM((2 * group, nl, MOE_ROW_LANES), BF16),
            pltpu.VMEM((MOE_GRANULE, MOE_ROW_LANES), BF16),
            pltpu.SemaphoreType.DMA((2,)),
            pltpu.SemaphoreType.DMA((1,)),
        ],
    )
    return pl.pallas_call(
        _dispatch_kernel,
        grid_spec=grid_spec,
        out_shape=(
            jax.ShapeDtypeStruct((rows, MOE_ROW_LANES), BF16),
            jax.ShapeDtypeStruct((t, LANES), F32),
        ),
        compiler_params=pltpu.CompilerParams(
            dimension_semantics=("arbitrary",), vmem_limit_bytes=VMEM_LIMIT_BYTES),
        name="moe_dispatch",
    )(plan["chunks"], plan["loc_off"], plan["base"], plan["step_chunks"], plan["pad_start"],
      plan["pad_chunks"], hb, sel, comb, plan["loc_t"], ustrict)


def _expert_kernel(te_ref, nt_ref, x_ref, wg_ref, wu_ref, wd_ref, y_ref, wgb_ref, wub_ref, wdb_ref):
    j = pl.program_id(0)

    @pl.when(j < nt_ref[0])
    def _():
        @pl.when((j == 0) | (te_ref[j] != te_ref[jnp.maximum(j - 1, 0)]))
        def _():
            wgb_ref[...] = wg_ref[0].astype(BF16)
            wub_ref[...] = wu_ref[0].astype(BF16)
            wdb_ref[...] = wd_ref[0].astype(BF16)

        x = x_ref[:, :D_MODEL]
        wl = x_ref[:, D_MODEL:].astype(F32)
        w = jnp.sum(wl[:, 2:8], axis=1, keepdims=True)
        gate = jnp.dot(x, wgb_ref[...], preferred_element_type=F32)
        up = jnp.dot(x, wub_ref[...], preferred_element_type=F32)
        act = (_silu(gate) * up).astype(BF16)
        y_ref[...] = (w * jnp.dot(act, wdb_ref[...], preferred_element_type=F32)).astype(y_ref.dtype)


def _experts(plan, xs, w_gate, w_up, w_down):
    rows = xs.shape[0]
    live = lambda j, te, nt: jnp.minimum(j, nt[0] - 1)
    expert = lambda j, te, nt: (te[live(j, te, nt)], 0, 0)
    grid_spec = pltpu.PrefetchScalarGridSpec(
        num_scalar_prefetch=2,
        grid=(rows // MOE_ROW_TILE,),
        in_specs=[
            pl.BlockSpec((MOE_ROW_TILE, MOE_ROW_LANES), lambda j, te, nt: (live(j, te, nt), 0)),
            pl.BlockSpec((1, D_MODEL, D_FF_EXPERT), expert),
            pl.BlockSpec((1, D_MODEL, D_FF_EXPERT), expert),
            pl.BlockSpec((1, D_FF_EXPERT, D_MODEL), expert),
        ],
        out_specs=pl.BlockSpec((MOE_ROW_TILE, D_MODEL), lambda j, te, nt: (live(j, te, nt), 0)),
        scratch_shapes=[
            pltpu.VMEM((D_MODEL, D_FF_EXPERT), BF16),
            pltpu.VMEM((D_MODEL, D_FF_EXPERT), BF16),
            pltpu.VMEM((D_FF_EXPERT, D_MODEL), BF16),
        ],
    )
    return pl.pallas_call(
        _expert_kernel,
        grid_spec=grid_spec,
        out_shape=jax.ShapeDtypeStruct((rows, D_MODEL), BF16),
        compiler_params=pltpu.CompilerParams(
            dimension_semantics=("arbitrary",), vmem_limit_bytes=VMEM_LIMIT_BYTES),
        name="moe_experts",
    )(plan["tile_expert"], plan["n_tiles"], xs, w_gate, w_up, w_down)


def _combine_kernel(chunks_ref, loc_ref, base_ref, schunks_ref,
                    y_ref, posw_ref, h_ref, g_ref, b_ref, o_ref, buf_ref, sem_ref):
    step = pl.program_id(0)
    n_steps = pl.num_programs(0)
    par = step % 2
    nl = buf_ref.shape[1]
    group = buf_ref.shape[0] // 2
    ts = h_ref.shape[0] // group
    g = MOE_GRANULE

    def fetch(ss, pp):
        for u in range(group):
            for e in range(N_EXPERTS):
                idx = (ss * group + u) * N_EXPERTS + e
                lo = loc_ref[idx]
                bs = base_ref[idx]

                def get(k, c, lo=lo, bs=bs, u=u):
                    src = y_ref.at[pl.ds(pl.multiple_of(bs + k * g, g), g)]
                    dst = buf_ref.at[pp * group + u, pl.ds(pl.multiple_of(lo + k * g, g), g)]
                    _chunk_copy(src, dst, sem_ref.at[pp]).start()
                    return c
                lax.fori_loop(0, chunks_ref[idx], get, 0)

    @pl.when(step == 0)
    def _():
        buf_ref[...] = jnp.zeros(buf_ref.shape, BF16)
        fetch(0, 0)

    @pl.when(step + 1 < n_steps)
    def _():
        fetch(step + 1, 1 - par)

    _wait_rows(lambda r: _chunk_copy(y_ref.at[pl.ds(0, r)], buf_ref.at[0, pl.ds(0, r)],
                                     sem_ref.at[par]), schunks_ref[step])

    d = lax.broadcasted_iota(jnp.int32, (1, nl), 1).astype(F32)
    for u in range(group):
        rows = slice(u * ts, (u + 1) * ts)
        yb = buf_ref[par * group + u]
        posw = posw_ref[rows, :]
        gather = jnp.where((posw[:, 0:1] == d) | (posw[:, 1:2] == d), 1.0, 0.0).astype(BF16)
        ffn = jnp.dot(gather, yb, preferred_element_type=F32)
        o_ref[rows, :] = _layer_norm(DEEPNORM_ALPHA * h_ref[rows, :] + ffn, g_ref[...], b_ref[...])


def _combine(plan, y, posw, h, ln_g, ln_b, ts):
    t = h.shape[0]
    group = _moe_group(t, ts)
    nl = _moe_local_rows(ts)
    row = lambda i, *_: (i, 0)
    const = lambda i, *_: (0, 0)
    grid_spec = pltpu.PrefetchScalarGridSpec(
        num_scalar_prefetch=4,
        grid=(t // (group * ts),),
        in_specs=[
            pl.BlockSpec(memory_space=pl.ANY),
            pl.BlockSpec((group * ts, LANES), row),
            pl.BlockSpec((group * ts, D_MODEL), row),
            pl.BlockSpec((1, D_MODEL), const),
            pl.BlockSpec((1, D_MODEL), const),
        ],
        out_specs=pl.BlockSpec((group * ts, D_MODEL), row),
        scratch_shapes=[
            pltpu.VMEM((2 * group, nl, D_MODEL), BF16),
            pltpu.SemaphoreType.DMA((2,)),
        ],
    )
    return pl.pallas_call(
        _combine_kernel,
        grid_spec=grid_spec,
        out_shape=jax.ShapeDtypeStruct((t, D_MODEL), F32),
        compiler_params=pltpu.CompilerParams(
            dimension_semantics=("arbitrary",), vmem_limit_bytes=VMEM_LIMIT_BYTES),
        name="moe_combine_ln",
    )(plan["chunks"], plan["loc_off"], plan["base"], plan["step_chunks"], y, posw, h, ln_g, ln_b)


def _pad_lanes(a, n=LANES):
    return jnp.pad(a, ((0, 0), (0, n - a.shape[1])))


def _layer(h2d, batch, seq, w_in, b_in, conv_w, conv_b, a_log, d_skip, ssd_norm_g, w_out,
           ln1_g, ln1_b, rg_w, rg_b, re_w, re_b, w_gate, w_up, w_down, ln2_g, ln2_b):
    o_dt = SSD_INNER + SSD_CONV_DIM
    o_q = o_dt + SSD_HEADS
    o_f = o_q + 3 * ATT_INNER
    w_main = jnp.concatenate([w_in[:, :o_dt], w_in[:, o_q:o_f]], axis=1).astype(BF16)
    b_main = jnp.concatenate([b_in[:o_dt], b_in[o_q:o_f]])[None, :]
    w_dt, w_f = w_in[:, o_dt:o_q], w_in[:, o_f:]
    w_small = _pad_lanes(jnp.concatenate([w_dt, w_f, w_dt], axis=1)).astype(BF16)
    b_small = _pad_lanes(jnp.concatenate([b_in[o_dt:o_q], b_in[o_f:], b_in[o_dt:o_q]])[None, :])
    a_neg = -jnp.exp(a_log.astype(F32))
    a_row = _pad_lanes(a_neg[None, :])
    a_exp = jnp.repeat(a_neg, SSD_HEAD_DIM)[None, :]
    dskip_exp = jnp.repeat(d_skip.astype(F32), SSD_HEAD_DIM)[None, :]

    z, xbc, q_t, k, v_t, gate, qf_t, kf = _in_proj(h2d, w_main, b_main, w_small, b_small, a_row, seq)
    y_ssd = _ssd(xbc, z, gate, conv_w, conv_b[None, :], a_exp, dskip_exp, ssd_norm_g[None, :],
                 batch, seq)
    y_att = _attention(q_t, k, v_t, qf_t, kf, batch, seq)

    gap = ROUTE_E_LANE - N_EXPERT_GROUPS
    w_r = _pad_lanes(jnp.concatenate(
        [rg_w, jnp.zeros((D_MODEL, gap), F32),
         jnp.transpose(re_w, (1, 0, 2)).reshape(D_MODEL, N_EXPERTS)], axis=1))
    b_r = _pad_lanes(jnp.concatenate(
        [rg_b, jnp.zeros((gap,), F32), re_b.reshape(N_EXPERTS)])[None, :])
    wr_hi = w_r.astype(BF16)
    wr_lo = (w_r - wr_hi.astype(F32)).astype(BF16)
    wo = w_out.astype(BF16)
    ts = min(MOE_SORT_TILE, h2d.shape[0])
    h1, h1b, comb, sel, cnt = _out_proj(y_ssd, y_att, h2d, wo[:SSD_INNER], wo[SSD_INNER:],
                                        ln1_g[None, :], ln1_b[None, :],
                                        jnp.concatenate([wr_hi, wr_lo], axis=1), b_r, ts)

    plan = _dispatch_plan(
        cnt[:, 0, ROUTE_E_LANE:ROUTE_E_LANE + N_EXPERTS].astype(jnp.int32), ts)
    xs, posw = _dispatch(plan, h1b, sel, comb, ts)
    y = _experts(plan, xs, w_gate, w_up, w_down)
    return _combine(plan, y, posw, h1, ln2_g[None, :], ln2_b[None, :], ts)


def kernel(x, w_in, b_in, conv_w, conv_b, a_log, d_skip, ssd_norm_g, w_out, ln1_g, ln1_b,
           router_group_w, router_group_b, router_expert_w, router_expert_b, w_gate, w_up,
           w_down, ln2_g, ln2_b):
    batch, seq, d = x.shape
    h = x.reshape(batch * seq, d)
    for l in range(w_in.shape[0]):
        h = _layer(h, batch, seq, w_in[l], b_in[l], conv_w[l], conv_b[l], a_log[l], d_skip[l],
                   ssd_norm_g[l], w_out[l], ln1_g[l], ln1_b[l], router_group_w[l],
                   router_group_b[l], router_expert_w[l], router_expert_b[l], w_gate[l],
                   w_up[l], w_down[l], ln2_g[l], ln2_b[l])
    return h.reshape(batch, seq, d)
```

```python
import functools

import jax
import jax.numpy as jnp
import numpy as np
from jax import lax
from jax.experimental import pallas as pl
from jax.experimental.pallas import tpu as pltpu

F32 = jnp.float32
BF16 = jnp.bfloat16

D_MODEL = 1024
SSD_HEADS = 8
SSD_HEAD_DIM = 64
SSD_INNER = SSD_HEADS * SSD_HEAD_DIM
SSD_GROUPS = 2
SSD_STATE = 128
CONV_WIDTH = 4
SSD_CONV_DIM = SSD_INNER + 2 * SSD_GROUPS * SSD_STATE
ATT_HEADS = 8
ATT_HEAD_DIM = 64
ATT_INNER = ATT_HEADS * ATT_HEAD_DIM
N_EXPERT_GROUPS = 4
EXPERTS_PER_GROUP = 4
N_EXPERTS = N_EXPERT_GROUPS * EXPERTS_PER_GROUP
D_FF_EXPERT = 512
DEPTH = 1
DEEPNORM_ALPHA = (2.0 * DEPTH) ** 0.25
LN_EPS = 1e-5
RMS_EPS = 1e-5

LANES = 128
SUBLANES = 8
VMEM_LIMIT_BYTES = 56 * 1024 * 1024

GATE_A_LANE = 0
GATE_F_LANE = SSD_HEADS
GATE_DT_LANE = 2 * SSD_HEADS
ROUTE_E_LANE = SUBLANES
MOE_SORT_TILE = 256
MOE_GRANULE = 16
MOE_ROW_TILE = 1024
MOE_TILES_PER_STEP = 4
MOE_ROW_LANES = D_MODEL + LANES
MOE_WAIT_BATCH = 8

IN_PROJ_ROWS = 1024
OUT_PROJ_ROWS = 1024
CUMSUM_BLOCK = 256
FB_LANES = 6
ATT_BLOCK = 256
ATT_PAIRS_PER_STEP = 4
ATT_BATCHES_PER_STEP = 1

SSD_ROWS = 256
SSD_STEP_ROWS = 512
SSD_CHUNK = 64
NEG_BIG = -1e30
LOG2E = 1.4426950408889634


def _split3(v):
    hi = v.astype(BF16)
    r1 = v - hi.astype(F32)
    mid = r1.astype(BF16)
    lo = (r1 - mid.astype(F32)).astype(BF16)
    return hi, mid, lo


def _silu(v):
    half = 0.5 * v
    return half * (1.0 + jnp.tanh(half))


def _softplus(v):
    return jnp.maximum(v, 0.0) + jnp.log1p(jnp.exp(-jnp.abs(v)))


def _in_proj_kernel(x_ref, wm_ref, bm_ref, ws_ref, bs_ref, arow_ref, tri_ref, place_ref, ones_ref,
                    z_ref, xbc_ref, qt_ref, k_ref, vt_ref, gate_ref, qft_ref, kf_ref,
                    carry_ref, *, tiles_per_seq):
    i = pl.program_id(0)
    tm = x_ref.shape[0]
    tb = qt_ref.shape[2]
    xb = x_ref[...].astype(BF16)

    def put_transposed(ref, val):
        for c in range(tm // tb):
            ref[c] = val[c * tb:(c + 1) * tb, :].T.astype(ref.dtype)

    s = jnp.dot(xb, ws_ref[...], preferred_element_type=F32) + bs_ref[...]
    lane = lax.broadcasted_iota(jnp.int32, (1, LANES), 1)
    sp = _softplus(s)
    log_f = -_softplus(-s)
    val = jnp.where(lane < GATE_F_LANE, sp * arow_ref[...],
                    jnp.where(lane < GATE_DT_LANE, log_f,
                              jnp.where(lane < GATE_DT_LANE + SSD_HEADS, sp, 0.0)))
    carry = jnp.where(i % tiles_per_seq == 0, 0.0, carry_ref[...])
    tri = tri_ref[...]
    cb = tri.shape[0]
    sums = []
    for c in range(tm // cb):
        hi, mid, lo = _split3(val[c * cb:(c + 1) * cb, :])
        part = (jnp.dot(tri, hi, preferred_element_type=F32)
                + jnp.dot(tri, mid, preferred_element_type=F32)
                + jnp.dot(tri, lo, preferred_element_type=F32)) + carry
        carry = part[cb - 1:cb, :]
        sums.append(part)
    cs = jnp.concatenate(sums, axis=0)
    carry_ref[...] = carry
    out = jnp.where(lane < GATE_DT_LANE, cs, val)
    gate_ref[...] = out
    pieces = jnp.concatenate(_split3(cs * LOG2E), axis=1)
    fb = jnp.dot(pieces, place_ref[...], preferred_element_type=F32) + ones_ref[...]
    put_transposed(qft_ref, fb[:, :LANES])
    kf_ref[...] = fb[:, LANES:].astype(BF16)

    col = 0
    for ref, scale in ((z_ref, None), (xbc_ref, None), (qt_ref, ATT_HEAD_DIM ** -0.5 * LOG2E),
                       (k_ref, None), (vt_ref, None)):
        transposed = ref is qt_ref or ref is vt_ref
        n = ATT_INNER if transposed else ref.shape[1]
        acc = jnp.dot(xb, wm_ref[:, col:col + n], preferred_element_type=F32) + bm_ref[:, col:col + n]
        if scale is not None:
            acc = acc * scale
        if transposed:
            put_transposed(ref, acc)
        else:
            ref[...] = acc.astype(ref.dtype)
        col += n


def _forget_bias_placement():
    place = np.zeros((3 * LANES, 2 * LANES), np.float32)
    ones = np.zeros((1, 2 * LANES), np.float32)
    for h in range(ATT_HEADS):
        for piece in range(3):
            src = piece * LANES + GATE_F_LANE + h
            place[src, FB_LANES * h + piece] = 1.0
            place[src, LANES + FB_LANES * h + 3 + piece] = -1.0
            ones[0, FB_LANES * h + 3 + piece] = 1.0
            ones[0, LANES + FB_LANES * h + piece] = 1.0
    return jnp.asarray(place, BF16), jnp.asarray(ones, F32)


def _in_proj(x2d, w_main, b_main, w_small, b_small, a_row, seq):
    t = x2d.shape[0]
    tm = min(IN_PROJ_ROWS, seq)
    tk = min(ATT_BLOCK, seq)
    cum = min(CUMSUM_BLOCK, tm)
    tri = jnp.tri(cum, dtype=BF16)
    place, ones = _forget_bias_placement()
    n_main = w_main.shape[1]
    row = lambda i: (i, 0)
    const = lambda i: (0, 0)
    outs = (
        jax.ShapeDtypeStruct((t, SSD_INNER), BF16),
        jax.ShapeDtypeStruct((t, SSD_CONV_DIM), BF16),
        jax.ShapeDtypeStruct((t // tk, ATT_INNER, tk), BF16),
        jax.ShapeDtypeStruct((t, ATT_INNER), BF16),
        jax.ShapeDtypeStruct((t // tk, ATT_INNER, tk), BF16),
        jax.ShapeDtypeStruct((t, LANES), F32),
        jax.ShapeDtypeStruct((t // tk, LANES, tk), BF16),
        jax.ShapeDtypeStruct((t, LANES), BF16),
    )
    return pl.pallas_call(
        functools.partial(_in_proj_kernel, tiles_per_seq=seq // tm),
        grid=(t // tm,),
        in_specs=[
            pl.BlockSpec((tm, D_MODEL), row),
            pl.BlockSpec((D_MODEL, n_main), const),
            pl.BlockSpec((1, n_main), const),
            pl.BlockSpec((D_MODEL, LANES), const),
            pl.BlockSpec((1, LANES), const),
            pl.BlockSpec((1, LANES), const),
            pl.BlockSpec((cum, cum), const),
            pl.BlockSpec((3 * LANES, 2 * LANES), const),
            pl.BlockSpec((1, 2 * LANES), const),
        ],
        out_specs=[
            pl.BlockSpec((tm, SSD_INNER), row),
            pl.BlockSpec((tm, SSD_CONV_DIM), row),
            pl.BlockSpec((tm // tk, ATT_INNER, tk), lambda i: (i, 0, 0)),
            pl.BlockSpec((tm, ATT_INNER), row),
            pl.BlockSpec((tm // tk, ATT_INNER, tk), lambda i: (i, 0, 0)),
            pl.BlockSpec((tm, LANES), row),
            pl.BlockSpec((tm // tk, LANES, tk), lambda i: (i, 0, 0)),
            pl.BlockSpec((tm, LANES), row),
        ],
        out_shape=outs,
        scratch_shapes=[pltpu.VMEM((1, LANES), F32)],
        compiler_params=pltpu.CompilerParams(
            dimension_semantics=("arbitrary",), vmem_limit_bytes=VMEM_LIMIT_BYTES),
        name="in_proj",
    )(x2d, w_main, b_main, w_small, b_small, a_row, tri, place, ones)


def _expand_heads(gate, lane0, rows):
    lane = lax.broadcasted_iota(jnp.int32, (1, LANES), 1)
    pieces = []
    for p in range(SSD_HEADS // 2):
        a = jnp.broadcast_to(gate[:, lane0 + 2 * p:lane0 + 2 * p + 1], (rows, LANES))
        b = jnp.broadcast_to(gate[:, lane0 + 2 * p + 1:lane0 + 2 * p + 2], (rows, LANES))
        pieces.append(jnp.where(lane < SSD_HEAD_DIM, a, b))
    return jnp.concatenate(pieces, axis=1)


def _ssd_kernel(xbc_ref, z_ref, gate_ref, cw_ref, cb_ref, aexp_ref, dskip_ref, ng_ref, shift_ref,
                y_ref, ext_ref, state_ref):
    j = pl.program_id(1)
    rows = shift_ref.shape[1]
    q = SSD_CHUNK
    gw = SSD_INNER // SSD_GROUPS
    hpg = SSD_HEADS // SSD_GROUPS
    pad = SUBLANES

    @pl.when(j == 0)
    def _():
        ext_ref[0:pad, :] = jnp.zeros((pad, SSD_CONV_DIM), F32)
        state_ref[...] = jnp.zeros_like(state_ref)

    r_i = lax.broadcasted_iota(jnp.int32, (q, gw), 0)
    c_i = lax.broadcasted_iota(jnp.int32, (q, gw), 1)
    s_i = c_i % q
    eye_t = s_i == r_i
    tril_t = s_i <= r_i
    rb = lax.broadcasted_iota(jnp.int32, (hpg * q, gw), 0) // q
    cbk = lax.broadcasted_iota(jnp.int32, (hpg * q, gw), 1) // SSD_HEAD_DIM
    blk = rb == cbk

    for t in range(xbc_ref.shape[0] // rows):
        sub = slice(t * rows, (t + 1) * rows)
        xb = xbc_ref[sub, :]
        xf = xb.astype(F32)
        conv = cb_ref[...] + xf * cw_ref[CONV_WIDTH - 1:CONV_WIDTH, :]
        for s in range(1, CONV_WIDTH):
            shifted = jnp.dot(shift_ref[s - 1], xb, preferred_element_type=F32)
            conv = conv + shifted * cw_ref[CONV_WIDTH - 1 - s:CONV_WIDTH - s, :]
        ext_ref[pad:2 * pad, :] = xf[0:pad, :]
        head = cb_ref[...]
        for kk in range(CONV_WIDTH):
            off = pad - (CONV_WIDTH - 1) + kk
            head = head + ext_ref[off:off + pad, :] * cw_ref[kk:kk + 1, :]
        ext_ref[0:pad, :] = xf[rows - pad:rows, :]
        u = _silu(jnp.concatenate([head, conv[pad:, :]], axis=0))
        xs = u[:, :SSD_INNER]
        b_all = u[:, SSD_INNER:SSD_INNER + SSD_GROUPS * SSD_STATE].astype(BF16)
        c_all = u[:, SSD_INNER + SSD_GROUPS * SSD_STATE:].astype(BF16)

        gate = gate_ref[sub, :]
        dt_e = _expand_heads(gate, GATE_DT_LANE, rows)
        a_e = _expand_heads(gate, GATE_A_LANE, rows)
        xdt = xs * dt_e
        da_e = dt_e * aexp_ref[...]

        ys = []
        for c in range(rows // q):
            r0 = c * q
            a_c = a_e[r0:r0 + q, :]
            a_prev = a_c[0:1, :] - da_e[r0:r0 + 1, :]
            a_end = a_c[q - 1:q, :]
            y_groups = []
            for g in range(SSD_GROUPS):
                l0 = g * gw
                a_cg = a_c[:, l0:l0 + gw]
                diag = jnp.sum(jnp.where(eye_t, a_cg, 0.0), axis=0, keepdims=True)
                seg = jnp.minimum(a_cg - diag, 0.0)
                lcat = jnp.where(tril_t, jnp.exp(seg), 0.0)
                bg = b_all[r0:r0 + q, g * SSD_STATE:(g + 1) * SSD_STATE]
                cg = c_all[r0:r0 + q, g * SSD_STATE:(g + 1) * SSD_STATE]
                b_tile = jnp.concatenate([bg] * hpg, axis=0)
                gcat = lax.dot_general(cg, b_tile, (((1,), (1,)), ((), ())),
                                       preferred_element_type=F32)
                xg = xdt[r0:r0 + q, l0:l0 + gw]
                xg_b = xg.astype(BF16)
                xbd = jnp.where(blk, jnp.concatenate([xg_b] * hpg, axis=0), jnp.zeros((), BF16))
                y_diag = jnp.dot((gcat * lcat).astype(BF16), xbd, preferred_element_type=F32)
                st = state_ref[g]
                y_off = jnp.dot(cg, st.astype(BF16), preferred_element_type=F32) \
                    * jnp.exp(a_cg - a_prev[:, l0:l0 + gw])
                dte = jnp.exp(a_end[:, l0:l0 + gw] - a_cg)
                upd = lax.dot_general(bg, (xg * dte).astype(BF16), (((0,), (0,)), ((), ())),
                                      preferred_element_type=F32)
                state_ref[g] = st * jnp.exp(a_end[:, l0:l0 + gw] - a_prev[:, l0:l0 + gw]) + upd
                y_groups.append(y_diag + y_off)
            ys.append(jnp.concatenate(y_groups, axis=1))
        y = jnp.concatenate(ys, axis=0) + xs * dskip_ref[...]
        y = y * _silu(z_ref[sub, :].astype(F32))
        outs = []
        for g in range(SSD_GROUPS):
            yg = y[:, g * gw:(g + 1) * gw]
            ms = jnp.mean(yg * yg, axis=1, keepdims=True)
            outs.append(yg * lax.rsqrt(ms + RMS_EPS))
        y_ref[sub, :] = (jnp.concatenate(outs, axis=1) * ng_ref[...]).astype(y_ref.dtype)


def _ssd(xbc, z, gate, conv_w, conv_b, a_exp, dskip_exp, norm_g, batch, seq):
    t = xbc.shape[0]
    rows = min(SSD_ROWS, seq)
    step_rows = min(SSD_STEP_ROWS, seq)
    nj = seq // step_rows
    row = lambda b, j: (b * nj + j, 0)
    const = lambda b, j: (0, 0)
    shifts = jnp.stack([jnp.eye(rows, k=-s, dtype=BF16) for s in range(1, CONV_WIDTH)])
    return pl.pallas_call(
        _ssd_kernel,
        grid=(batch, nj),
        in_specs=[
            pl.BlockSpec((step_rows, SSD_CONV_DIM), row),
            pl.BlockSpec((step_rows, SSD_INNER), row),
            pl.BlockSpec((step_rows, LANES), row),
            pl.BlockSpec((CONV_WIDTH, SSD_CONV_DIM), const),
            pl.BlockSpec((1, SSD_CONV_DIM), const),
            pl.BlockSpec((1, SSD_INNER), const),
            pl.BlockSpec((1, SSD_INNER), const),
            pl.BlockSpec((1, SSD_INNER), const),
            pl.BlockSpec((CONV_WIDTH - 1, rows, rows), lambda b, j: (0, 0, 0)),
        ],
        out_specs=pl.BlockSpec((step_rows, SSD_INNER), row),
        out_shape=jax.ShapeDtypeStruct((t, SSD_INNER), BF16),
        scratch_shapes=[
            pltpu.VMEM((2 * SUBLANES, SSD_CONV_DIM), F32),
            pltpu.VMEM((SSD_GROUPS, SSD_STATE, SSD_INNER // SSD_GROUPS), F32),
        ],
        compiler_params=pltpu.CompilerParams(
            dimension_semantics=("arbitrary", "arbitrary"), vmem_limit_bytes=VMEM_LIMIT_BYTES),
        name="ssd",
    )(xbc, z, gate, conv_w, conv_b, a_exp, dskip_exp, norm_g, shifts)


def _attn_kernel(qt_ref, k_ref, vt_ref, qft_ref, kf_ref, o_ref, *scratch, tb, pairs, batches):
    pair0 = pl.program_id(1) * pairs
    nb = qt_ref.shape[0] // batches
    seq = nb * tb
    chains = [(bb, pp) for bb in range(batches) for pp in range(pairs)]
    s_refs, acc_refs = scratch[:len(chains)], scratch[len(chains):]
    feat = lax.broadcasted_iota(jnp.int32, (LANES, 1), 0)
    in_head = (feat < ATT_HEAD_DIM, feat >= ATT_HEAD_DIM)
    ones = jnp.ones((2 * SUBLANES, tb), BF16)
    key = lax.broadcasted_iota(jnp.int32, (tb, 2 * tb), 0)
    qry = lax.broadcasted_iota(jnp.int32, (tb, 2 * tb), 1)
    causal = jnp.where(qry >= tb, qry - tb, qry) >= key
    zero = jnp.zeros((), BF16)

    def logits_t(qa_t, c, jk):
        bb, pp = chains[c]
        k0 = pl.multiple_of(bb * seq + jk * tb, tb)
        ka = jnp.concatenate([k_ref[pl.ds(k0, tb), pp * LANES:(pp + 1) * LANES],
                              kf_ref[pl.ds(k0, tb), :]], axis=1)
        return jnp.dot(ka, qa_t, preferred_element_type=F32)

    def update(s_t, m, c, jk):
        bb, pp = chains[c]
        m_new = jnp.maximum(m, jnp.max(s_t, axis=0, keepdims=True))
        alpha = jnp.exp2(m - m_new)
        p_t = jnp.exp2(s_t - m_new).astype(BF16)
        for hh in range(2):
            r0 = pp * LANES + hh * ATT_HEAD_DIM
            va_t = jnp.concatenate([vt_ref[bb * nb + jk, r0:r0 + ATT_HEAD_DIM, :], ones], axis=0)
            cols = slice(hh * tb, (hh + 1) * tb)
            acc_refs[c][hh] = alpha[:, cols] * acc_refs[c][hh] + jnp.dot(
                va_t, p_t[:, cols], preferred_element_type=F32)
        return m_new

    def build_queries(i):
        qas = []
        for bb, pp in chains:
            qf_t = qft_ref[bb * nb + i]
            q_t = qt_ref[bb * nb + i, pp * LANES:(pp + 1) * LANES, :]
            cols = []
            for hh in range(2):
                fb0 = FB_LANES * (2 * (pair0 + pp) + hh)
                in_fb = (feat >= fb0) & (feat < fb0 + FB_LANES)
                cols.append(jnp.concatenate(
                    [jnp.where(in_head[hh], q_t, zero), jnp.where(in_fb, qf_t, zero)], axis=0))
            qas.append(jnp.concatenate(cols, axis=1))
        return qas

    def first_logits(i):
        qas = build_queries(i)
        for c in range(len(chains)):
            s_refs[c][2] = logits_t(qas[c], c, 0)

    def q_block(i, _):
        q0 = pl.multiple_of(i * tb, tb)
        qas = build_queries(i)
        for c in range(len(chains)):
            acc_refs[c][...] = jnp.zeros(acc_refs[c].shape, F32)

        def step(jk, ms, rd, wr):
            new = []
            for c in range(len(chains)):
                s_t = s_refs[c][rd]
                s_refs[c][wr] = logits_t(qas[c], c, jk + 1)
                new.append(update(s_t, ms[c], c, jk))
            return tuple(new)

        def two_steps(t2, ms):
            return step(2 * t2 + 2, step(2 * t2 + 1, ms, 1, 0), 0, 1)

        ms = tuple(jnp.full((1, 2 * tb), NEG_BIG, F32) for _ in chains)
        ms = lax.cond(i >= 1, lambda v: step(0, v, 2, 1), lambda v: v, ms)
        ms = lax.fori_loop(0, lax.shift_right_logical(jnp.maximum(i - 1, 0), 1), two_steps, ms)
        ms = lax.cond((i >= 2) & (i % 2 == 0), lambda v: step(i - 1, v, 1, 0), lambda v: v, ms)

        last = jnp.where(i >= 1, i % 2, 2)
        diag = [jnp.where(causal, s_refs[c][last], NEG_BIG) for c in range(len(chains))]
        first_logits(jnp.minimum(i + 1, nb - 1))
        for c, (bb, pp) in enumerate(chains):
            update(diag[c], ms[c], c, i)
            heads = []
            for hh in range(2):
                acc_t = acc_refs[c][hh]
                inv = 1.0 / acc_t[ATT_HEAD_DIM:ATT_HEAD_DIM + 1, :]
                heads.append(acc_t[:ATT_HEAD_DIM, :] * inv)
            out_t = jnp.concatenate(heads, axis=0)
            o_ref[pl.ds(bb * seq + q0, tb), pp * LANES:(pp + 1) * LANES] = out_t.T.astype(o_ref.dtype)
        return 0

    first_logits(0)
    lax.fori_loop(0, nb, q_block, 0)


def _attention(q_t, k, v_t, qf_t, kf, batch, seq):
    t = k.shape[0]
    tb = min(ATT_BLOCK, seq)
    nb = seq // tb
    pairs = ATT_PAIRS_PER_STEP
    batches = ATT_BATCHES_PER_STEP if batch % ATT_BATCHES_PER_STEP == 0 else 1
    width = pairs * LANES
    return pl.pallas_call(
        functools.partial(_attn_kernel, tb=tb, pairs=pairs, batches=batches),
        grid=(batch // batches, ATT_HEADS // (2 * pairs)),
        in_specs=[
            pl.BlockSpec((batches * nb, width, tb), lambda b, p: (b, p, 0)),
            pl.BlockSpec((batches * seq, width), lambda b, p: (b, p)),
            pl.BlockSpec((batches * nb, width, tb), lambda b, p: (b, p, 0)),
            pl.BlockSpec((batches * nb, LANES, tb), lambda b, p: (b, 0, 0)),
            pl.BlockSpec((batches * seq, LANES), lambda b, p: (b, 0)),
        ],
        out_specs=pl.BlockSpec((batches * seq, width), lambda b, p: (b, p)),
        out_shape=jax.ShapeDtypeStruct((t, ATT_INNER), BF16),
        scratch_shapes=([pltpu.VMEM((3, tb, 2 * tb), F32) for _ in range(batches * pairs)]
                        + [pltpu.VMEM((2, ATT_HEAD_DIM + 2 * SUBLANES, tb), F32)
                           for _ in range(batches * pairs)]),
        compiler_params=pltpu.CompilerParams(
            dimension_semantics=("arbitrary", "arbitrary"), vmem_limit_bytes=VMEM_LIMIT_BYTES),
        name="fox_attention",
    )(q_t, k, v_t, qf_t, kf)


def _layer_norm(u, g, b):
    mu = jnp.mean(u, axis=1, keepdims=True)
    d = u - mu
    var = jnp.mean(d * d, axis=1, keepdims=True)
    return d * lax.rsqrt(var + LN_EPS) * g + b


def _route(lg):
    rows = lg.shape[0]
    lt = lg.T
    gl = lt[0:N_EXPERT_GROUPS, :]
    g_row = lax.broadcasted_iota(jnp.int32, (N_EXPERT_GROUPS, 1), 0).astype(F32)
    gmax = jnp.max(gl, axis=0, keepdims=True)
    gsum = jnp.sum(jnp.exp(gl - gmax), axis=0, keepdims=True)
    g_val = 1.0 / gsum
    g_idx = jnp.min(jnp.where(gl == gmax, g_row, float(N_EXPERT_GROUPS)), axis=0, keepdims=True)
    el = lt[ROUTE_E_LANE:ROUTE_E_LANE + N_EXPERTS, :]
    e_row = lax.broadcasted_iota(jnp.int32, (N_EXPERTS, 1), 0)
    e_grp = (e_row // EXPERTS_PER_GROUP).astype(F32)
    e_row = e_row.astype(F32)
    in_grp = e_grp == g_idx
    m1 = jnp.max(jnp.where(in_grp, el, NEG_BIG), axis=0, keepdims=True)
    i1 = jnp.min(jnp.where(in_grp & (el == m1), e_row, float(N_EXPERTS)), axis=0, keepdims=True)
    rest = in_grp & (e_row != i1)
    m2 = jnp.max(jnp.where(rest, el, NEG_BIG), axis=0, keepdims=True)
    i2 = jnp.min(jnp.where(rest & (el == m2), e_row, float(N_EXPERTS)), axis=0, keepdims=True)
    r = jnp.exp(m2 - m1)
    w1 = 1.0 / (1.0 + r)
    w2 = r * w1
    comb_e = jnp.where(e_row == i1, g_val * w1, jnp.where(e_row == i2, g_val * w2, 0.0))
    sel_e = jnp.where((e_row == i1) | (e_row == i2), 1.0, 0.0)
    top = jnp.zeros((ROUTE_E_LANE, rows), F32)
    bottom = jnp.zeros((LANES - ROUTE_E_LANE - N_EXPERTS, rows), F32)
    comb = jnp.concatenate([top, comb_e, bottom], axis=0).T
    sel = jnp.concatenate([top, sel_e, bottom], axis=0).T
    return comb, sel


def _out_proj_kernel(ys_ref, ya_ref, x_ref, wa_ref, wb_ref, g_ref, b_ref, wr_ref, br_ref,
                     h_ref, hb_ref, comb_ref, sel_ref, cnt_ref):
    mix = jnp.dot(ys_ref[...], wa_ref[...], preferred_element_type=F32)
    mix = mix + jnp.dot(ya_ref[...], wb_ref[...], preferred_element_type=F32)
    h = _layer_norm(DEEPNORM_ALPHA * x_ref[...] + mix, g_ref[...], b_ref[...])
    h_ref[...] = h
    h_hi = h.astype(BF16)
    hb_ref[...] = h_hi
    h_lo = (h - h_hi.astype(F32)).astype(BF16)
    both = jnp.dot(h_hi, wr_ref[...], preferred_element_type=F32)
    lg = (both[:, :LANES] + both[:, LANES:]
          + jnp.dot(h_lo, wr_ref[:, :LANES], preferred_element_type=F32)) + br_ref[...]
    comb, sel = _route(lg)
    comb_ref[...] = comb
    sel_ref[...] = sel
    ts = h.shape[0] // cnt_ref.shape[0]
    for c in range(cnt_ref.shape[0]):
        cnt = jnp.sum(sel[c * ts:(c + 1) * ts, :], axis=0, keepdims=True)
        cnt_ref[c] = jnp.broadcast_to(cnt, (SUBLANES, LANES))


def _out_proj(ys, ya, x2d, wo_a, wo_b, ln_g, ln_b, wr, br, ts):
    t = x2d.shape[0]
    tm = min(OUT_PROJ_ROWS, t)
    row = lambda i: (i, 0)
    const = lambda i: (0, 0)
    return pl.pallas_call(
        _out_proj_kernel,
        grid=(t // tm,),
        in_specs=[
            pl.BlockSpec((tm, SSD_INNER), row),
            pl.BlockSpec((tm, ATT_INNER), row),
            pl.BlockSpec((tm, D_MODEL), row),
            pl.BlockSpec((SSD_INNER, D_MODEL), const),
            pl.BlockSpec((ATT_INNER, D_MODEL), const),
            pl.BlockSpec((1, D_MODEL), const),
            pl.BlockSpec((1, D_MODEL), const),
            pl.BlockSpec((D_MODEL, 2 * LANES), const),
            pl.BlockSpec((1, LANES), const),
        ],
        out_specs=[
            pl.BlockSpec((tm, D_MODEL), row),
            pl.BlockSpec((tm, D_MODEL), row),
            pl.BlockSpec((tm, LANES), row),
            pl.BlockSpec((tm, LANES), row),
            pl.BlockSpec((tm // ts, SUBLANES, LANES), lambda i: (i, 0, 0)),
        ],
        out_shape=(
            jax.ShapeDtypeStruct((t, D_MODEL), F32),
            jax.ShapeDtypeStruct((t, D_MODEL), BF16),
            jax.ShapeDtypeStruct((t, LANES), F32),
            jax.ShapeDtypeStruct((t, LANES), F32),
            jax.ShapeDtypeStruct((t // ts, SUBLANES, LANES), F32),
        ),
        compiler_params=pltpu.CompilerParams(
            dimension_semantics=("arbitrary",), vmem_limit_bytes=VMEM_LIMIT_BYTES),
        name="out_proj_ln_router",
    )(ys, ya, x2d, wo_a, wo_b, ln_g, ln_b, wr, br)


def _dispatch_plan(cnt, ts):
    nt = cnt.shape[0]
    c16 = (cnt + (MOE_GRANULE - 1)) // MOE_GRANULE * MOE_GRANULE
    loc_off = jnp.cumsum(c16, axis=1) - c16
    tot = jnp.sum(c16, axis=0)
    tiles = (tot + (MOE_ROW_TILE - 1)) // MOE_ROW_TILE
    cum_tiles = jnp.cumsum(tiles)
    start = (cum_tiles - tiles) * MOE_ROW_TILE
    base = start[None, :] + jnp.cumsum(c16, axis=0) - c16
    max_tiles = _moe_rows(nt * ts, ts) // MOE_ROW_TILE
    tile_expert = jnp.minimum(
        jnp.sum(jnp.arange(max_tiles, dtype=jnp.int32)[:, None] >= cum_tiles[None, :], axis=1),
        N_EXPERTS - 1).astype(jnp.int32)
    loc_t = jnp.broadcast_to(loc_off.astype(F32)[:, :, None], (nt, N_EXPERTS, LANES))
    return dict(
        chunks=(c16 // MOE_GRANULE).reshape(-1).astype(jnp.int32),
        loc_off=loc_off.reshape(-1).astype(jnp.int32),
        base=base.reshape(-1).astype(jnp.int32),
        step_chunks=(jnp.sum(c16, axis=1).reshape(-1, _moe_group(nt * ts, ts)).sum(axis=1)
                     // MOE_GRANULE).astype(jnp.int32),
        pad_start=(start + tot).astype(jnp.int32),
        pad_chunks=((tiles * MOE_ROW_TILE - tot) // MOE_GRANULE).astype(jnp.int32),
        n_tiles=cum_tiles[-1:].astype(jnp.int32),
        tile_expert=tile_expert,
        loc_t=loc_t,
    )


def _moe_group(t, ts):
    return MOE_TILES_PER_STEP if t % (MOE_TILES_PER_STEP * ts) == 0 else 1


def _moe_local_rows(ts):
    rows = 2 * ts + N_EXPERTS * (MOE_GRANULE - 1)
    return (rows + LANES - 1) // LANES * LANES


def _moe_rows(t, ts):
    rows = 2 * t + (t // ts) * N_EXPERTS * (MOE_GRANULE - 1) + N_EXPERTS * MOE_ROW_TILE
    return (rows + MOE_ROW_TILE - 1) // MOE_ROW_TILE * MOE_ROW_TILE


def _chunk_copy(src, dst, sem):
    return pltpu.make_async_copy(src, dst, sem)


def _wait_rows(copy_of_rows, n_chunks):
    def wait(rows):
        def one(_, c):
            copy_of_rows(rows).wait()
            return c
        return one
    shift = MOE_WAIT_BATCH.bit_length() - 1
    lax.fori_loop(0, lax.shift_right_logical(n_chunks, shift), wait(MOE_WAIT_BATCH * MOE_GRANULE), 0)
    lax.fori_loop(0, n_chunks & (MOE_WAIT_BATCH - 1), wait(MOE_GRANULE), 0)


def _dispatch_kernel(chunks_ref, loc_ref, base_ref, schunks_ref, pstart_ref, pchunks_ref,
                     hb_ref, sel_ref, comb_ref, loct_ref, ustrict_ref,
                     xs_ref, posw_ref, buf_ref, zero_ref, sem_ref, zsem_ref):
    step = pl.program_id(0)
    n_steps = pl.num_programs(0)
    par = step % 2
    group = loct_ref.shape[0]
    ts = hb_ref.shape[0] // group
    nl = buf_ref.shape[1]
    g = MOE_GRANULE

    def wait_chunks(n, pp):
        _wait_rows(lambda r: _chunk_copy(buf_ref.at[0, pl.ds(0, r)], xs_ref.at[pl.ds(0, r)],
                                         sem_ref.at[pp]), n)

    @pl.when(step >= 2)
    def _():
        wait_chunks(schunks_ref[step - 2], par)

    lane = lax.broadcasted_iota(jnp.int32, (1, LANES), 1)
    prow = lax.broadcasted_iota(jnp.int32, (LANES, 1), 0)
    d = lax.broadcasted_iota(jnp.int32, (nl, ts), 0).astype(F32)
    experts = slice(ROUTE_E_LANE, ROUTE_E_LANE + N_EXPERTS)
    for u in range(group):
        rows = slice(u * ts, (u + 1) * ts)
        sel_t = sel_ref[rows, :].T[experts, :]
        comb_t = comb_ref[rows, :].T[experts, :]
        picked = sel_t > 0.0
        rank = jnp.dot(sel_t.astype(BF16), ustrict_ref[...], preferred_element_type=F32)
        dest = rank + jnp.concatenate([loct_ref[u]] * (ts // LANES), axis=1)
        pos_a = jnp.min(jnp.where(picked, dest, 1e9), axis=0, keepdims=True)
        pos_b = jnp.max(jnp.where(picked, dest, -1.0), axis=0, keepdims=True)
        w_a = jnp.sum(jnp.where(picked & (dest == pos_a), comb_t, 0.0), axis=0, keepdims=True)
        w_b = jnp.sum(jnp.where(picked & (dest == pos_b), comb_t, 0.0), axis=0, keepdims=True)
        pieces = [p.astype(F32) for w in (w_a, w_b) for p in _split3(w)]
        posw_t = jnp.zeros((LANES, ts), F32)
        for r, v in enumerate([pos_a, pos_b] + pieces):
            posw_t = jnp.where(prow == r, v, posw_t)
        posw = posw_t.T
        posw_ref[rows, :] = posw
        perm_a = jnp.where(d == pos_a, 1.0, 0.0).astype(BF16)
        perm_b = jnp.where(d == pos_b, 1.0, 0.0).astype(BF16)
        slot = par * group + u
        buf_ref[slot, :, :D_MODEL] = jnp.dot(perm_a + perm_b, hb_ref[rows, :],
                                             preferred_element_type=F32).astype(BF16)
        wl_a = jnp.where((lane >= 2) & (lane < 5), posw, 0.0).astype(BF16)
        wl_b = jnp.where((lane >= 5) & (lane < 8), posw, 0.0).astype(BF16)
        buf_ref[slot, :, D_MODEL:] = (
            jnp.dot(perm_a, wl_a, preferred_element_type=F32)
            + jnp.dot(perm_b, wl_b, preferred_element_type=F32)).astype(BF16)

    for u in range(group):
        for e in range(N_EXPERTS):
            idx = (step * group + u) * N_EXPERTS + e
            lo = loc_ref[idx]
            bs = base_ref[idx]

            def send(k, c, lo=lo, bs=bs, u=u):
                src = buf_ref.at[par * group + u, pl.ds(pl.multiple_of(lo + k * g, g), g)]
                dst = xs_ref.at[pl.ds(pl.multiple_of(bs + k * g, g), g)]
                _chunk_copy(src, dst, sem_ref.at[par]).start()
                return c
            lax.fori_loop(0, chunks_ref[idx], send, 0)

    @pl.when(step == n_steps - 1)
    def _():
        zero_ref[...] = jnp.zeros_like(zero_ref)
        for e in range(N_EXPERTS):
            ps = pstart_ref[e]

            def send_zero(k, c, ps=ps):
                dst = xs_ref.at[pl.ds(pl.multiple_of(ps + k * g, g), g)]
                _chunk_copy(zero_ref, dst, zsem_ref.at[0]).start()
                return c
            lax.fori_loop(0, pchunks_ref[e], send_zero, 0)
        for e in range(N_EXPERTS):
            def wait_zero(_, c):
                _chunk_copy(zero_ref, xs_ref.at[pl.ds(0, g)], zsem_ref.at[0]).wait()
                return c
            lax.fori_loop(0, pchunks_ref[e], wait_zero, 0)
        wait_chunks(schunks_ref[step], par)

        @pl.when(step >= 1)
        def _():
            wait_chunks(schunks_ref[step - 1], 1 - par)


def _dispatch(plan, hb, sel, comb, ts):
    t = hb.shape[0]
    group = _moe_group(t, ts)
    nl = _moe_local_rows(ts)
    rows = _moe_rows(t, ts)
    ustrict = jnp.tri(ts, k=-1, dtype=BF16).T
    row = lambda i, *_: (i, 0)
    grid_spec = pltpu.PrefetchScalarGridSpec(
        num_scalar_prefetch=6,
        grid=(t // (group * ts),),
        in_specs=[
            pl.BlockSpec((group * ts, D_MODEL), row),
            pl.BlockSpec((group * ts, LANES), row),
            pl.BlockSpec((group * ts, LANES), row),
            pl.BlockSpec((group, N_EXPERTS, LANES), lambda i, *_: (i, 0, 0)),
            pl.BlockSpec((ts, ts), lambda i, *_: (0, 0)),
        ],
        out_specs=[
            pl.BlockSpec(memory_space=pl.ANY),
            pl.BlockSpec((group * ts, LANES), row),
        ],
        scratch_shapes=[
            pltpu.VMEM((2 * group, nl, MOE_ROW_LANES), BF16),
            pltpu.VMEM((MOE_GRANULE, MOE_ROW_LANES), BF16),
            pltpu.SemaphoreType.DMA((2,)),
            pltpu.SemaphoreType.DMA((1,)),
        ],
    )
    return pl.pallas_call(
        _dispatch_kernel,
        grid_spec=grid_spec,
        out_shape=(
            jax.ShapeDtypeStruct((rows, MOE_ROW_LANES), BF16),
            jax.ShapeDtypeStruct((t, LANES), F32),
        ),
        compiler_params=pltpu.CompilerParams(
            dimension_semantics=("arbitrary",), vmem_limit_bytes=VMEM_LIMIT_BYTES),
        name="moe_dispatch",
    )(plan["chunks"], plan["loc_off"], plan["base"], plan["step_chunks"], plan["pad_start"],
      plan["pad_chunks"], hb, sel, comb, plan["loc_t"], ustrict)


def _expert_kernel(te_ref, nt_ref, x_ref, wg_ref, wu_ref, wd_ref, y_ref, wgb_ref, wub_ref, wdb_ref):
    j = pl.program_id(0)

    @pl.when(j < nt_ref[0])
    def _():
        @pl.when((j == 0) | (te_ref[j] != te_ref[jnp.maximum(j - 1, 0)]))
        def _():
            wgb_ref[...] = wg_ref[0].astype(BF16)
            wub_ref[...] = wu_ref[0].astype(BF16)
            wdb_ref[...] = wd_ref[0].astype(BF16)

        x = x_ref[:, :D_MODEL]
        wl = x_ref[:, D_MODEL:].astype(F32)
        w = jnp.sum(wl[:, 2:8], axis=1, keepdims=True)
        gate = jnp.dot(x, wgb_ref[...], preferred_element_type=F32)
        up = jnp.dot(x, wub_ref[...], preferred_element_type=F32)
        act = (_silu(gate) * up).astype(BF16)
        y_ref[...] = (w * jnp.dot(act, wdb_ref[...], preferred_element_type=F32)).astype(y_ref.dtype)


def _experts(plan, xs, w_gate, w_up, w_down):
    rows = xs.shape[0]
    live = lambda j, te, nt: jnp.minimum(j, nt[0] - 1)
    expert = lambda j, te, nt: (te[live(j, te, nt)], 0, 0)
    grid_spec = pltpu.PrefetchScalarGridSpec(
        num_scalar_prefetch=2,
        grid=(rows // MOE_ROW_TILE,),
        in_specs=[
            pl.BlockSpec((MOE_ROW_TILE, MOE_ROW_LANES), lambda j, te, nt: (live(j, te, nt), 0)),
            pl.BlockSpec((1, D_MODEL, D_FF_EXPERT), expert),
            pl.BlockSpec((1, D_MODEL, D_FF_EXPERT), expert),
            pl.BlockSpec((1, D_FF_EXPERT, D_MODEL), expert),
        ],
        out_specs=pl.BlockSpec((MOE_ROW_TILE, D_MODEL), lambda j, te, nt: (live(j, te, nt), 0)),
        scratch_shapes=[
            pltpu.VMEM((D_MODEL, D_FF_EXPERT), BF16),
            pltpu.VMEM((D_MODEL, D_FF_EXPERT), BF16),
            pltpu.VMEM((D_FF_EXPERT, D_MODEL), BF16),
        ],
    )
    return pl.pallas_call(
        _expert_kernel,
        grid_spec=grid_spec,
        out_shape=jax.ShapeDtypeStruct((rows, D_MODEL), BF16),
        compiler_params=pltpu.CompilerParams(
            dimension_semantics=("arbitrary",), vmem_limit_bytes=VMEM_LIMIT_BYTES),
        name="moe_experts",
    )(plan["tile_expert"], plan["n_tiles"], xs, w_gate, w_up, w_down)


def _combine_kernel(chunks_ref, loc_ref, base_ref, schunks_ref,
                    y_ref, posw_ref, h_ref, g_ref, b_ref, o_ref, buf_ref, sem_ref):
    step = pl.program_id(0)
    n_steps = pl.num_programs(0)
    par = step % 2
    nl = buf_ref.shape[1]
    group = buf_ref.shape[0] // 2
    ts = h_ref.shape[0] // group
    g = MOE_GRANULE

    def fetch(ss, pp):
        for u in range(group):
            for e in range(N_EXPERTS):
                idx = (ss * group + u) * N_EXPERTS + e
                lo = loc_ref[idx]
                bs = base_ref[idx]

                def get(k, c, lo=lo, bs=bs, u=u):
                    src = y_ref.at[pl.ds(pl.multiple_of(bs + k * g, g), g)]
                    dst = buf_ref.at[pp * group + u, pl.ds(pl.multiple_of(lo + k * g, g), g)]
                    _chunk_copy(src, dst, sem_ref.at[pp]).start()
                    return c
                lax.fori_loop(0, chunks_ref[idx], get, 0)

    @pl.when(step == 0)
    def _():
        buf_ref[...] = jnp.zeros(buf_ref.shape, BF16)
        fetch(0, 0)

    @pl.when(step + 1 < n_steps)
    def _():
        fetch(step + 1, 1 - par)

    _wait_rows(lambda r: _chunk_copy(y_ref.at[pl.ds(0, r)], buf_ref.at[0, pl.ds(0, r)],
                                     sem_ref.at[par]), schunks_ref[step])

    d = lax.broadcasted_iota(jnp.int32, (1, nl), 1).astype(F32)
    for u in range(group):
        rows = slice(u * ts, (u + 1) * ts)
        yb = buf_ref[par * group + u]
        posw = posw_ref[rows, :]
        gather = jnp.where((posw[:, 0:1] == d) | (posw[:, 1:2] == d), 1.0, 0.0).astype(BF16)
        ffn = jnp.dot(gather, yb, preferred_element_type=F32)
        o_ref[rows, :] = _layer_norm(DEEPNORM_ALPHA * h_ref[rows, :] + ffn, g_ref[...], b_ref[...])


def _combine(plan, y, posw, h, ln_g, ln_b, ts):
    t = h.shape[0]
    group = _moe_group(t, ts)
    nl = _moe_local_rows(ts)
    row = lambda i, *_: (i, 0)
    const = lambda i, *_: (0, 0)
    grid_spec = pltpu.PrefetchScalarGridSpec(
        num_scalar_prefetch=4,
        grid=(t // (group * ts),),
        in_specs=[
            pl.BlockSpec(memory_space=pl.ANY),
            pl.BlockSpec((group * ts, LANES), row),
            pl.BlockSpec((group * ts, D_MODEL), row),
            pl.BlockSpec((1, D_MODEL), const),
            pl.BlockSpec((1, D_MODEL), const),
        ],
        out_specs=pl.BlockSpec((group * ts, D_MODEL), row),
        scratch_shapes=[
            pltpu.VMEM((2 * group, nl, D_MODEL), BF16),
            pltpu.SemaphoreType.DMA((2,)),
        ],
    )
    return pl.pallas_call(
        _combine_kernel,
        grid_spec=grid_spec,
        out_shape=jax.ShapeDtypeStruct((t, D_MODEL), F32),
        compiler_params=pltpu.CompilerParams(
            dimension_semantics=("arbitrary",), vmem_limit_bytes=VMEM_LIMIT_BYTES),
        name="moe_combine_ln",
    )(plan["chunks"], plan["loc_off"], plan["base"], plan["step_chunks"], y, posw, h, ln_g, ln_b)


def _pad_lanes(a, n=LANES):
    return jnp.pad(a, ((0, 0), (0, n - a.shape[1])))


def _layer(h2d, batch, seq, w_in, b_in, conv_w, conv_b, a_log, d_skip, ssd_norm_g, w_out,
           ln1_g, ln1_b, rg_w, rg_b, re_w, re_b, w_gate, w_up, w_down, ln2_g, ln2_b):
    o_dt = SSD_INNER + SSD_CONV_DIM
    o_q = o_dt + SSD_HEADS
    o_f = o_q + 3 * ATT_INNER
    w_main = jnp.concatenate([w_in[:, :o_dt], w_in[:, o_q:o_f]], axis=1).astype(BF16)
    b_main = jnp.concatenate([b_in[:o_dt], b_in[o_q:o_f]])[None, :]
    w_dt, w_f = w_in[:, o_dt:o_q], w_in[:, o_f:]
    w_small = _pad_lanes(jnp.concatenate([w_dt, w_f, w_dt], axis=1)).astype(BF16)
    b_small = _pad_lanes(jnp.concatenate([b_in[o_dt:o_q], b_in[o_f:], b_in[o_dt:o_q]])[None, :])
    a_neg = -jnp.exp(a_log.astype(F32))
    a_row = _pad_lanes(a_neg[None, :])
    a_exp = jnp.repeat(a_neg, SSD_HEAD_DIM)[None, :]
    dskip_exp = jnp.repeat(d_skip.astype(F32), SSD_HEAD_DIM)[None, :]

    z, xbc, q_t, k, v_t, gate, qf_t, kf = _in_proj(h2d, w_main, b_main, w_small, b_small, a_row, seq)
    y_ssd = _ssd(xbc, z, gate, conv_w, conv_b[None, :], a_exp, dskip_exp, ssd_norm_g[None, :],
                 batch, seq)
    y_att = _attention(q_t, k, v_t, qf_t, kf, batch, seq)

    gap = ROUTE_E_LANE - N_EXPERT_GROUPS
    w_r = _pad_lanes(jnp.concatenate(
        [rg_w, jnp.zeros((D_MODEL, gap), F32),
         jnp.transpose(re_w, (1, 0, 2)).reshape(D_MODEL, N_EXPERTS)], axis=1))
    b_r = _pad_lanes(jnp.concatenate(
        [rg_b, jnp.zeros((gap,), F32), re_b.reshape(N_EXPERTS)])[None, :])
    wr_hi = w_r.astype(BF16)
    wr_lo = (w_r - wr_hi.astype(F32)).astype(BF16)
    wo = w_out.astype(BF16)
    ts = min(MOE_SORT_TILE, h2d.shape[0])
    h1, h1b, comb, sel, cnt = _out_proj(y_ssd, y_att, h2d, wo[:SSD_INNER], wo[SSD_INNER:],
                                        ln1_g[None, :], ln1_b[None, :],
                                        jnp.concatenate([wr_hi, wr_lo], axis=1), b_r, ts)

    plan = _dispatch_plan(
        cnt[:, 0, ROUTE_E_LANE:ROUTE_E_LANE + N_EXPERTS].astype(jnp.int32), ts)
    xs, posw = _dispatch(plan, h1b, sel, comb, ts)
    y = _experts(plan, xs, w_gate, w_up, w_down)
    return _combine(plan, y, posw, h1, ln2_g[None, :], ln2_b[None, :], ts)


def kernel(x, w_in, b_in, conv_w, conv_b, a_log, d_skip, ssd_norm_g, w_out, ln1_g, ln1_b,
           router_group_w, router_group_b, router_expert_w, router_expert_b, w_gate, w_up,
           w_down, ln2_g, ln2_b):
    batch, seq, d = x.shape
    h = x.reshape(batch * seq, d)
    for l in range(w_in.shape[0]):
        h = _layer(h, batch, seq, w_in[l], b_in[l], conv_w[l], conv_b[l], a_log[l], d_skip[l],
                   ssd_norm_g[l], w_out[l], ln1_g[l], ln1_b[l], router_group_w[l],
                   router_group_b[l], router_expert_w[l], router_expert_b[l], w_gate[l],
                   w_up[l], w_down[l], ln2_g[l], ln2_b[l])
    return h.reshape(batch, seq, d)
```

```python
import functools

import jax
import jax.numpy as jnp
import numpy as np
from jax import lax
from jax.experimental import pallas as pl
from jax.experimental.pallas import tpu as pltpu

F32 = jnp.float32
BF16 = jnp.bfloat16

D_MODEL = 1024
SSD_HEADS = 8
SSD_HEAD_DIM = 64
SSD_INNER = SSD_HEADS * SSD_HEAD_DIM
SSD_GROUPS = 2
SSD_STATE = 128
CONV_WIDTH = 4
SSD_CONV_DIM = SSD_INNER + 2 * SSD_GROUPS * SSD_STATE
ATT_HEADS = 8
ATT_HEAD_DIM = 64
ATT_INNER = ATT_HEADS * ATT_HEAD_DIM
N_EXPERT_GROUPS = 4
EXPERTS_PER_GROUP = 4
N_EXPERTS = N_EXPERT_GROUPS * EXPERTS_PER_GROUP
D_FF_EXPERT = 512
DEPTH = 1
DEEPNORM_ALPHA = (2.0 * DEPTH) ** 0.25
LN_EPS = 1e-5
RMS_EPS = 1e-5

LANES = 128
SUBLANES = 8
VMEM_LIMIT_BYTES = 56 * 1024 * 1024

GATE_A_LANE = 0
GATE_F_LANE = SSD_HEADS
GATE_DT_LANE = 2 * SSD_HEADS
ROUTE_E_LANE = SUBLANES
MOE_SORT_TILE = 256
MOE_GRANULE = 16
MOE_ROW_TILE = 1024
MOE_TILES_PER_STEP = 4
MOE_ROW_LANES = D_MODEL + LANES
MOE_WAIT_BATCH = 8

IN_PROJ_ROWS = 1024
OUT_PROJ_ROWS = 1024
CUMSUM_BLOCK = 256
FB_LANES = 6
ATT_BLOCK = 256
ATT_PAIRS_PER_STEP = 4
ATT_BATCHES_PER_STEP = 1

SSD_ROWS = 256
SSD_STEP_ROWS = 512
SSD_CHUNK = 64
NEG_BIG = -1e30
LOG2E = 1.4426950408889634


def _split3(v):
    hi = v.astype(BF16)
    r1 = v - hi.astype(F32)
    mid = r1.astype(BF16)
    lo = (r1 - mid.astype(F32)).astype(BF16)
    return hi, mid, lo


def _silu(v):
    half = 0.5 * v
    return half * (1.0 + jnp.tanh(half))


def _softplus(v):
    return jnp.maximum(v, 0.0) + jnp.log1p(jnp.exp(-jnp.abs(v)))


def _in_proj_kernel(x_ref, wm_ref, bm_ref, ws_ref, bs_ref, arow_ref, tri_ref, place_ref, ones_ref,
                    z_ref, xbc_ref, qt_ref, k_ref, vt_ref, gate_ref, qft_ref, kf_ref,
                    carry_ref, *, tiles_per_seq):
    i = pl.program_id(0)
    tm = x_ref.shape[0]
    tb = qt_ref.shape[2]
    xb = x_ref[...].astype(BF16)

    def put_transposed(ref, val):
        for c in range(tm // tb):
            ref[c] = val[c * tb:(c + 1) * tb, :].T.astype(ref.dtype)

    s = jnp.dot(xb, ws_ref[...], preferred_element_type=F32) + bs_ref[...]
    lane = lax.broadcasted_iota(jnp.int32, (1, LANES), 1)
    sp = _softplus(s)
    log_f = -_softplus(-s)
    val = jnp.where(lane < GATE_F_LANE, sp * arow_ref[...],
                    jnp.where(lane < GATE_DT_LANE, log_f,
                              jnp.where(lane < GATE_DT_LANE + SSD_HEADS, sp, 0.0)))
    carry = jnp.where(i % tiles_per_seq == 0, 0.0, carry_ref[...])
    tri = tri_ref[...]
    cb = tri.shape[0]
    sums = []
    for c in range(tm // cb):
        hi, mid, lo = _split3(val[c * cb:(c + 1) * cb, :])
        part = (jnp.dot(tri, hi, preferred_element_type=F32)
                + jnp.dot(tri, mid, preferred_element_type=F32)
                + jnp.dot(tri, lo, preferred_element_type=F32)) + carry
        carry = part[cb - 1:cb, :]
        sums.append(part)
    cs = jnp.concatenate(sums, axis=0)
    carry_ref[...] = carry
    out = jnp.where(lane < GATE_DT_LANE, cs, val)
    gate_ref[...] = out
    pieces = jnp.concatenate(_split3(cs * LOG2E), axis=1)
    fb = jnp.dot(pieces, place_ref[...], preferred_element_type=F32) + ones_ref[...]
    put_transposed(qft_ref, fb[:, :LANES])
    kf_ref[...] = fb[:, LANES:].astype(BF16)

    col = 0
    for ref, scale in ((z_ref, None), (xbc_ref, None), (qt_ref, ATT_HEAD_DIM ** -0.5 * LOG2E),
                       (k_ref, None), (vt_ref, None)):
        transposed = ref is qt_ref or ref is vt_ref
        n = ATT_INNER if transposed else ref.shape[1]
        acc = jnp.dot(xb, wm_ref[:, col:col + n], preferred_element_type=F32) + bm_ref[:, col:col + n]
        if scale is not None:
            acc = acc * scale
        if transposed:
            put_transposed(ref, acc)
        else:
            ref[...] = acc.astype(ref.dtype)
        col += n


def _forget_bias_placement():
    place = np.zeros((3 * LANES, 2 * LANES), np.float32)
    ones = np.zeros((1, 2 * LANES), np.float32)
    for h in range(ATT_HEADS):
        for piece in range(3):
            src = piece * LANES + GATE_F_LANE + h
            place[src, FB_LANES * h + piece] = 1.0
            place[src, LANES + FB_LANES * h + 3 + piece] = -1.0
            ones[0, FB_LANES * h + 3 + piece] = 1.0
            ones[0, LANES + FB_LANES * h + piece] = 1.0
    return jnp.asarray(place, BF16), jnp.asarray(ones, F32)


def _in_proj(x2d, w_main, b_main, w_small, b_small, a_row, seq):
    t = x2d.shape[0]
    tm = min(IN_PROJ_ROWS, seq)
    tk = min(ATT_BLOCK, seq)
    cum = min(CUMSUM_BLOCK, tm)
    tri = jnp.tri(cum, dtype=BF16)
    place, ones = _forget_bias_placement()
    n_main = w_main.shape[1]
    row = lambda i: (i, 0)
    const = lambda i: (0, 0)
    outs = (
        jax.ShapeDtypeStruct((t, SSD_INNER), BF16),
        jax.ShapeDtypeStruct((t, SSD_CONV_DIM), BF16),
        jax.ShapeDtypeStruct((t // tk, ATT_INNER, tk), BF16),
        jax.ShapeDtypeStruct((t, ATT_INNER), BF16),
        jax.ShapeDtypeStruct((t // tk, ATT_INNER, tk), BF16),
        jax.ShapeDtypeStruct((t, LANES), F32),
        jax.ShapeDtypeStruct((t // tk, LANES, tk), BF16),
        jax.ShapeDtypeStruct((t, LANES), BF16),
    )
    return pl.pallas_call(
        functools.partial(_in_proj_kernel, tiles_per_seq=seq // tm),
        grid=(t // tm,),
        in_specs=[
            pl.BlockSpec((tm, D_MODEL), row),
            pl.BlockSpec((D_MODEL, n_main), const),
            pl.BlockSpec((1, n_main), const),
            pl.BlockSpec((D_MODEL, LANES), const),
            pl.BlockSpec((1, LANES), const),
            pl.BlockSpec((1, LANES), const),
            pl.BlockSpec((cum, cum), const),
            pl.BlockSpec((3 * LANES, 2 * LANES), const),
            pl.BlockSpec((1, 2 * LANES), const),
        ],
        out_specs=[
            pl.BlockSpec((tm, SSD_INNER), row),
            pl.BlockSpec((tm, SSD_CONV_DIM), row),
            pl.BlockSpec((tm // tk, ATT_INNER, tk), lambda i: (i, 0, 0)),
            pl.BlockSpec((tm, ATT_INNER), row),
            pl.BlockSpec((tm // tk, ATT_INNER, tk), lambda i: (i, 0, 0)),
            pl.BlockSpec((tm, LANES), row),
            pl.BlockSpec((tm // tk, LANES, tk), lambda i: (i, 0, 0)),
            pl.BlockSpec((tm, LANES), row),
        ],
        out_shape=outs,
        scratch_shapes=[pltpu.VMEM((1, LANES), F32)],
        compiler_params=pltpu.CompilerParams(
            dimension_semantics=("arbitrary",), vmem_limit_bytes=VMEM_LIMIT_BYTES),
        name="in_proj",
    )(x2d, w_main, b_main, w_small, b_small, a_row, tri, place, ones)


def _expand_heads(gate, lane0, rows):
    lane = lax.broadcasted_iota(jnp.int32, (1, LANES), 1)
    pieces = []
    for p in range(SSD_HEADS // 2):
        a = jnp.broadcast_to(gate[:, lane0 + 2 * p:lane0 + 2 * p + 1], (rows, LANES))
        b = jnp.broadcast_to(gate[:, lane0 + 2 * p + 1:lane0 + 2 * p + 2], (rows, LANES))
        pieces.append(jnp.where(lane < SSD_HEAD_DIM, a, b))
    return jnp.concatenate(pieces, axis=1)


def _ssd_kernel(xbc_ref, z_ref, gate_ref, cw_ref, cb_ref, aexp_ref, dskip_ref, ng_ref, shift_ref,
                y_ref, ext_ref, state_ref):
    j = pl.program_id(1)
    rows = shift_ref.shape[1]
    q = SSD_CHUNK
    gw = SSD_INNER // SSD_GROUPS
    hpg = SSD_HEADS // SSD_GROUPS
    pad = SUBLANES

    @pl.when(j == 0)
    def _():
        ext_ref[0:pad, :] = jnp.zeros((pad, SSD_CONV_DIM), F32)
        state_ref[...] = jnp.zeros_like(state_ref)

    r_i = lax.broadcasted_iota(jnp.int32, (q, gw), 0)
    c_i = lax.broadcasted_iota(jnp.int32, (q, gw), 1)
    s_i = c_i % q
    eye_t = s_i == r_i
    tril_t = s_i <= r_i
    rb = lax.broadcasted_iota(jnp.int32, (hpg * q, gw), 0) // q
    cbk = lax.broadcasted_iota(jnp.int32, (hpg * q, gw), 1) // SSD_HEAD_DIM
    blk = rb == cbk

    for t in range(xbc_ref.shape[0] // rows):
        sub = slice(t * rows, (t + 1) * rows)
        xb = xbc_ref[sub, :]
        xf = xb.astype(F32)
        conv = cb_ref[...] + xf * cw_ref[CONV_WIDTH - 1:CONV_WIDTH, :]
        for s in range(1, CONV_WIDTH):
            shifted = jnp.dot(shift_ref[s - 1], xb, preferred_element_type=F32)
            conv = conv + shifted * cw_ref[CONV_WIDTH - 1 - s:CONV_WIDTH - s, :]
        ext_ref[pad:2 * pad, :] = xf[0:pad, :]
        head = cb_ref[...]
        for kk in range(CONV_WIDTH):
            off = pad - (CONV_WIDTH - 1) + kk
            head = head + ext_ref[off:off + pad, :] * cw_ref[kk:kk + 1, :]
        ext_ref[0:pad, :] = xf[rows - pad:rows, :]
        u = _silu(jnp.concatenate([head, conv[pad:, :]], axis=0))
        xs = u[:, :SSD_INNER]
        b_all = u[:, SSD_INNER:SSD_INNER + SSD_GROUPS * SSD_STATE].astype(BF16)
        c_all = u[:, SSD_INNER + SSD_GROUPS * SSD_STATE:].astype(BF16)

        gate = gate_ref[sub, :]
        dt_e = _expand_heads(gate, GATE_DT_LANE, rows)
        a_e = _expand_heads(gate, GATE_A_LANE, rows)
        xdt = xs * dt_e
        da_e = dt_e * aexp_ref[...]

        ys = []
        for c in range(rows // q):
            r0 = c * q
            a_c = a_e[r0:r0 + q, :]
            a_prev = a_c[0:1, :] - da_e[r0:r0 + 1, :]
            a_end = a_c[q - 1:q, :]
            y_groups = []
            for g in range(SSD_GROUPS):
                l0 = g * gw
                a_cg = a_c[:, l0:l0 + gw]
                diag = jnp.sum(jnp.where(eye_t, a_cg, 0.0), axis=0, keepdims=True)
                seg = jnp.minimum(a_cg - diag, 0.0)
                lcat = jnp.where(tril_t, jnp.exp(seg), 0.0)
                bg = b_all[r0:r0 + q, g * SSD_STATE:(g + 1) * SSD_STATE]
                cg = c_all[r0:r0 + q, g * SSD_STATE:(g + 1) * SSD_STATE]
                b_tile = jnp.concatenate([bg] * hpg, axis=0)
                gcat = lax.dot_general(cg, b_tile, (((1,), (1,)), ((), ())),
                                       preferred_element_type=F32)
                xg = xdt[r0:r0 + q, l0:l0 + gw]
                xg_b = xg.astype(BF16)
                xbd = jnp.where(blk, jnp.concatenate([xg_b] * hpg, axis=0), jnp.zeros((), BF16))
                y_diag = jnp.dot((gcat * lcat).astype(BF16), xbd, preferred_element_type=F32)
                st = state_ref[g]
                y_off = jnp.dot(cg, st.astype(BF16), preferred_element_type=F32) \
                    * jnp.exp(a_cg - a_prev[:, l0:l0 + gw])
                dte = jnp.exp(a_end[:, l0:l0 + gw] - a_cg)
                upd = lax.dot_general(bg, (xg * dte).astype(BF16), (((0,), (0,)), ((), ())),
                                      preferred_element_type=F32)
                state_ref[g] = st * jnp.exp(a_end[:, l0:l0 + gw] - a_prev[:, l0:l0 + gw]) + upd
                y_groups.append(y_diag + y_off)
            ys.append(jnp.concatenate(y_groups, axis=1))
        y = jnp.concatenate(ys, axis=0) + xs * dskip_ref[...]
        y = y * _silu(z_ref[sub, :].astype(F32))
        outs = []
        for g in range(SSD_GROUPS):
            yg = y[:, g * gw:(g + 1) * gw]
            ms = jnp.mean(yg * yg, axis=1, keepdims=True)
            outs.append(yg * lax.rsqrt(ms + RMS_EPS))
        y_ref[sub, :] = (jnp.concatenate(outs, axis=1) * ng_ref[...]).astype(y_ref.dtype)


def _ssd(xbc, z, gate, conv_w, conv_b, a_exp, dskip_exp, norm_g, batch, seq):
    t = xbc.shape[0]
    rows = min(SSD_ROWS, seq)
    step_rows = min(SSD_STEP_ROWS, seq)
    nj = seq // step_rows
    row = lambda b, j: (b * nj + j, 0)
    const = lambda b, j: (0, 0)
    shifts = jnp.stack([jnp.eye(rows, k=-s, dtype=BF16) for s in range(1, CONV_WIDTH)])
    return pl.pallas_call(
        _ssd_kernel,
        grid=(batch, nj),
        in_specs=[
            pl.BlockSpec((step_rows, SSD_CONV_DIM), row),
            pl.BlockSpec((step_rows, SSD_INNER), row),
            pl.BlockSpec((step_rows, LANES), row),
            pl.BlockSpec((CONV_WIDTH, SSD_CONV_DIM), const),
            pl.BlockSpec((1, SSD_CONV_DIM), const),
            pl.BlockSpec((1, SSD_INNER), const),
            pl.BlockSpec((1, SSD_INNER), const),
            pl.BlockSpec((1, SSD_INNER), const),
            pl.BlockSpec((CONV_WIDTH - 1, rows, rows), lambda b, j: (0, 0, 0)),
        ],
        out_specs=pl.BlockSpec((step_rows, SSD_INNER), row),
        out_shape=jax.ShapeDtypeStruct((t, SSD_INNER), BF16),
        scratch_shapes=[
            pltpu.VMEM((2 * SUBLANES, SSD_CONV_DIM), F32),
            pltpu.VMEM((SSD_GROUPS, SSD_STATE, SSD_INNER // SSD_GROUPS), F32),
        ],
        compiler_params=pltpu.CompilerParams(
            dimension_semantics=("arbitrary", "arbitrary"), vmem_limit_bytes=VMEM_LIMIT_BYTES),
        name="ssd",
    )(xbc, z, gate, conv_w, conv_b, a_exp, dskip_exp, norm_g, shifts)


def _attn_kernel(qt_ref, k_ref, vt_ref, qft_ref, kf_ref, o_ref, *scratch, tb, pairs, batches):
    pair0 = pl.program_id(1) * pairs
    nb = qt_ref.shape[0] // batches
    seq = nb * tb
    chains = [(bb, pp) for bb in range(batches) for pp in range(pairs)]
    s_refs, acc_refs = scratch[:len(chains)], scratch[len(chains):]
    feat = lax.broadcasted_iota(jnp.int32, (LANES, 1), 0)
    in_head = (feat < ATT_HEAD_DIM, feat >= ATT_HEAD_DIM)
    ones = jnp.ones((2 * SUBLANES, tb), BF16)
    key = lax.broadcasted_iota(jnp.int32, (tb, 2 * tb), 0)
    qry = lax.broadcasted_iota(jnp.int32, (tb, 2 * tb), 1)
    causal = jnp.where(qry >= tb, qry - tb, qry) >= key
    zero = jnp.zeros((), BF16)

    def logits_t(qa_t, c, jk):
        bb, pp = chains[c]
        k0 = pl.multiple_of(bb * seq + jk * tb, tb)
        ka = jnp.concatenate([k_ref[pl.ds(k0, tb), pp * LANES:(pp + 1) * LANES],
                              kf_ref[pl.ds(k0, tb), :]], axis=1)
        return jnp.dot(ka, qa_t, preferred_element_type=F32)

    def update(s_t, m, c, jk, par):
        bb, pp = chains[c]
        m_new = jnp.maximum(m, jnp.max(s_t, axis=0, keepdims=True))
        alpha = jnp.exp2(m - m_new)
        p_t = jnp.exp2(s_t - m_new).astype(BF16)
        for hh in range(2):
            r0 = pp * LANES + hh * ATT_HEAD_DIM
            va_t = jnp.concatenate([vt_ref[bb * nb + jk, r0:r0 + ATT_HEAD_DIM, :], ones], axis=0)
            cols = slice(hh * tb, (hh + 1) * tb)
            acc_refs[c][par, hh] = alpha[:, cols] * acc_refs[c][par, hh] + jnp.dot(
                va_t, p_t[:, cols], preferred_element_type=F32)
        return m_new

    def build_queries(i):
        qas = []
        for bb, pp in chains:
            qf_t = qft_ref[bb * nb + i]
            q_t = qt_ref[bb * nb + i, pp * LANES:(pp + 1) * LANES, :]
            cols = []
            for hh in range(2):
                fb0 = FB_LANES * (2 * (pair0 + pp) + hh)
                in_fb = (feat >= fb0) & (feat < fb0 + FB_LANES)
                cols.append(jnp.concatenate(
                    [jnp.where(in_head[hh], q_t, zero), jnp.where(in_fb, qf_t, zero)], axis=0))
            qas.append(jnp.concatenate(cols, axis=1))
        return qas

    def first_logits(i):
        qas = build_queries(i)
        for c in range(len(chains)):
            s_refs[c][2] = logits_t(qas[c], c, 0)

    def q_block(i, _):
        par = i % 2
        qas = build_queries(i)
        for c in range(len(chains)):
            acc_refs[c][par] = jnp.zeros(acc_refs[c].shape[1:], F32)

        def step(jk, ms, rd, wr):
            new = []
            for c in range(len(chains)):
                s_t = s_refs[c][rd]
                s_refs[c][wr] = logits_t(qas[c], c, jk + 1)
                new.append(update(s_t, ms[c], c, jk, par))
            return tuple(new)

        def two_steps(t2, ms):
            return step(2 * t2 + 2, step(2 * t2 + 1, ms, 1, 0), 0, 1)

        ms = tuple(jnp.full((1, 2 * tb), NEG_BIG, F32) for _ in chains)
        ms = lax.cond(i >= 1, lambda v: step(0, v, 2, 1), lambda v: v, ms)
        ms = lax.fori_loop(0, lax.shift_right_logical(jnp.maximum(i - 1, 0), 1), two_steps, ms)
        ms = lax.cond((i >= 2) & (i % 2 == 0), lambda v: step(i - 1, v, 1, 0), lambda v: v, ms)

        last = jnp.where(i >= 1, i % 2, 2)
        diag = [jnp.where(causal, s_refs[c][last], NEG_BIG) for c in range(len(chains))]
        first_logits(jnp.minimum(i + 1, nb - 1))
        finish(jnp.maximum(i - 1, 0), 1 - par)
        for c in range(len(chains)):
            update(diag[c], ms[c], c, i, par)
        return 0

    def finish(iq, par):
        q0 = pl.multiple_of(iq * tb, tb)
        for c, (bb, pp) in enumerate(chains):
            heads = []
            for hh in range(2):
                acc_t = acc_refs[c][par, hh]
                inv = 1.0 / acc_t[ATT_HEAD_DIM:ATT_HEAD_DIM + 1, :]
                heads.append(acc_t[:ATT_HEAD_DIM, :] * inv)
            out_t = jnp.concatenate(heads, axis=0)
            o_ref[pl.ds(bb * seq + q0, tb), pp * LANES:(pp + 1) * LANES] = out_t.T.astype(o_ref.dtype)

    for c in range(len(chains)):
        acc_refs[c][1] = jnp.ones(acc_refs[c].shape[1:], F32)
    first_logits(0)
    lax.fori_loop(0, nb, q_block, 0)
    finish(nb - 1, (nb - 1) % 2)


def _attention(q_t, k, v_t, qf_t, kf, batch, seq):
    t = k.shape[0]
    tb = min(ATT_BLOCK, seq)
    nb = seq // tb
    pairs = ATT_PAIRS_PER_STEP
    batches = ATT_BATCHES_PER_STEP if batch % ATT_BATCHES_PER_STEP == 0 else 1
    width = pairs * LANES
    return pl.pallas_call(
        functools.partial(_attn_kernel, tb=tb, pairs=pairs, batches=batches),
        grid=(batch // batches, ATT_HEADS // (2 * pairs)),
        in_specs=[
            pl.BlockSpec((batches * nb, width, tb), lambda b, p: (b, p, 0)),
            pl.BlockSpec((batches * seq, width), lambda b, p: (b, p)),
            pl.BlockSpec((batches * nb, width, tb), lambda b, p: (b, p, 0)),
            pl.BlockSpec((batches * nb, LANES, tb), lambda b, p: (b, 0, 0)),
            pl.BlockSpec((batches * seq, LANES), lambda b, p: (b, 0)),
        ],
        out_specs=pl.BlockSpec((batches * seq, width), lambda b, p: (b, p)),
        out_shape=jax.ShapeDtypeStruct((t, ATT_INNER), BF16),
        scratch_shapes=([pltpu.VMEM((3, tb, 2 * tb), F32) for _ in range(batches * pairs)]
                        + [pltpu.VMEM((2, 2, ATT_HEAD_DIM + 2 * SUBLANES, tb), F32)
                           for _ in range(batches * pairs)]),
        compiler_params=pltpu.CompilerParams(
            dimension_semantics=("arbitrary", "arbitrary"), vmem_limit_bytes=VMEM_LIMIT_BYTES),
        name="fox_attention",
    )(q_t, k, v_t, qf_t, kf)


def _layer_norm(u, g, b):
    mu = jnp.mean(u, axis=1, keepdims=True)
    d = u - mu
    var = jnp.mean(d * d, axis=1, keepdims=True)
    return d * lax.rsqrt(var + LN_EPS) * g + b


def _route(lg):
    rows = lg.shape[0]
    lt = lg.T
    gl = lt[0:N_EXPERT_GROUPS, :]
    g_row = lax.broadcasted_iota(jnp.int32, (N_EXPERT_GROUPS, 1), 0).astype(F32)
    gmax = jnp.max(gl, axis=0, keepdims=True)
    gsum = jnp.sum(jnp.exp(gl - gmax), axis=0, keepdims=True)
    g_val = 1.0 / gsum
    g_idx = jnp.min(jnp.where(gl == gmax, g_row, float(N_EXPERT_GROUPS)), axis=0, keepdims=True)
    el = lt[ROUTE_E_LANE:ROUTE_E_LANE + N_EXPERTS, :]
    e_row = lax.broadcasted_iota(jnp.int32, (N_EXPERTS, 1), 0)
    e_grp = (e_row // EXPERTS_PER_GROUP).astype(F32)
    e_row = e_row.astype(F32)
    in_grp = e_grp == g_idx
    m1 = jnp.max(jnp.where(in_grp, el, NEG_BIG), axis=0, keepdims=True)
    i1 = jnp.min(jnp.where(in_grp & (el == m1), e_row, float(N_EXPERTS)), axis=0, keepdims=True)
    rest = in_grp & (e_row != i1)
    m2 = jnp.max(jnp.where(rest, el, NEG_BIG), axis=0, keepdims=True)
    i2 = jnp.min(jnp.where(rest & (el == m2), e_row, float(N_EXPERTS)), axis=0, keepdims=True)
    r = jnp.exp(m2 - m1)
    w1 = 1.0 / (1.0 + r)
    w2 = r * w1
    comb_e = jnp.where(e_row == i1, g_val * w1, jnp.where(e_row == i2, g_val * w2, 0.0))
    sel_e = jnp.where((e_row == i1) | (e_row == i2), 1.0, 0.0)
    top = jnp.zeros((ROUTE_E_LANE, rows), F32)
    bottom = jnp.zeros((LANES - ROUTE_E_LANE - N_EXPERTS, rows), F32)
    comb = jnp.concatenate([top, comb_e, bottom], axis=0).T
    sel = jnp.concatenate([top, sel_e, bottom], axis=0).T
    return comb, sel


def _out_proj_kernel(ys_ref, ya_ref, x_ref, wa_ref, wb_ref, g_ref, b_ref, wr_ref, br_ref,
                     h_ref, hb_ref, comb_ref, sel_ref, cnt_ref):
    mix = jnp.dot(ys_ref[...], wa_ref[...], preferred_element_type=F32)
    mix = mix + jnp.dot(ya_ref[...], wb_ref[...], preferred_element_type=F32)
    h = _layer_norm(DEEPNORM_ALPHA * x_ref[...] + mix, g_ref[...], b_ref[...])
    h_ref[...] = h
    h_hi = h.astype(BF16)
    hb_ref[...] = h_hi
    h_lo = (h - h_hi.astype(F32)).astype(BF16)
    both = jnp.dot(h_hi, wr_ref[...], preferred_element_type=F32)
    lg = (both[:, :LANES] + both[:, LANES:]
          + jnp.dot(h_lo, wr_ref[:, :LANES], preferred_element_type=F32)) + br_ref[...]
    comb, sel = _route(lg)
    comb_ref[...] = comb
    sel_ref[...] = sel
    ts = h.shape[0] // cnt_ref.shape[0]
    for c in range(cnt_ref.shape[0]):
        cnt = jnp.sum(sel[c * ts:(c + 1) * ts, :], axis=0, keepdims=True)
        cnt_ref[c] = jnp.broadcast_to(cnt, (SUBLANES, LANES))


def _out_proj(ys, ya, x2d, wo_a, wo_b, ln_g, ln_b, wr, br, ts):
    t = x2d.shape[0]
    tm = min(OUT_PROJ_ROWS, t)
    row = lambda i: (i, 0)
    const = lambda i: (0, 0)
    return pl.pallas_call(
        _out_proj_kernel,
        grid=(t // tm,),
        in_specs=[
            pl.BlockSpec((tm, SSD_INNER), row),
            pl.BlockSpec((tm, ATT_INNER), row),
            pl.BlockSpec((tm, D_MODEL), row),
            pl.BlockSpec((SSD_INNER, D_MODEL), const),
            pl.BlockSpec((ATT_INNER, D_MODEL), const),
            pl.BlockSpec((1, D_MODEL), const),
            pl.BlockSpec((1, D_MODEL), const),
            pl.BlockSpec((D_MODEL, 2 * LANES), const),
            pl.BlockSpec((1, LANES), const),
        ],
        out_specs=[
            pl.BlockSpec((tm, D_MODEL), row),
            pl.BlockSpec((tm, D_MODEL), row),
            pl.BlockSpec((tm, LANES), row),
            pl.BlockSpec((tm, LANES), row),
            pl.BlockSpec((tm // ts, SUBLANES, LANES), lambda i: (i, 0, 0)),
        ],
        out_shape=(
            jax.ShapeDtypeStruct((t, D_MODEL), F32),
            jax.ShapeDtypeStruct((t, D_MODEL), BF16),
            jax.ShapeDtypeStruct((t, LANES), F32),
            jax.ShapeDtypeStruct((t, LANES), F32),
            jax.ShapeDtypeStruct((t // ts, SUBLANES, LANES), F32),
        ),
        compiler_params=pltpu.CompilerParams(
            dimension_semantics=("arbitrary",), vmem_limit_bytes=VMEM_LIMIT_BYTES),
        name="out_proj_ln_router",
    )(ys, ya, x2d, wo_a, wo_b, ln_g, ln_b, wr, br)


def _dispatch_plan(cnt, ts):
    nt = cnt.shape[0]
    c16 = (cnt + (MOE_GRANULE - 1)) // MOE_GRANULE * MOE_GRANULE
    loc_off = jnp.cumsum(c16, axis=1) - c16
    tot = jnp.sum(c16, axis=0)
    tiles = (tot + (MOE_ROW_TILE - 1)) // MOE_ROW_TILE
    cum_tiles = jnp.cumsum(tiles)
    start = (cum_tiles - tiles) * MOE_ROW_TILE
    base = start[None, :] + jnp.cumsum(c16, axis=0) - c16
    max_tiles = _moe_rows(nt * ts, ts) // MOE_ROW_TILE
    tile_expert = jnp.minimum(
        jnp.sum(jnp.arange(max_tiles, dtype=jnp.int32)[:, None] >= cum_tiles[None, :], axis=1),
        N_EXPERTS - 1).astype(jnp.int32)
    loc_t = jnp.broadcast_to(loc_off.astype(F32)[:, :, None], (nt, N_EXPERTS, LANES))
    return dict(
        chunks=(c16 // MOE_GRANULE).reshape(-1).astype(jnp.int32),
        loc_off=loc_off.reshape(-1).astype(jnp.int32),
        base=base.reshape(-1).astype(jnp.int32),
        step_chunks=(jnp.sum(c16, axis=1).reshape(-1, _moe_group(nt * ts, ts)).sum(axis=1)
                     // MOE_GRANULE).astype(jnp.int32),
        pad_start=(start + tot).astype(jnp.int32),
        pad_chunks=((tiles * MOE_ROW_TILE - tot) // MOE_GRANULE).astype(jnp.int32),
        n_tiles=cum_tiles[-1:].astype(jnp.int32),
        tile_expert=tile_expert,
        loc_t=loc_t,
    )


def _moe_group(t, ts):
    return MOE_TILES_PER_STEP if t % (MOE_TILES_PER_STEP * ts) == 0 else 1


def _moe_local_rows(ts):
    rows = 2 * ts + N_EXPERTS * (MOE_GRANULE - 1)
    return (rows + LANES - 1) // LANES * LANES


def _moe_rows(t, ts):
    rows = 2 * t + (t // ts) * N_EXPERTS * (MOE_GRANULE - 1) + N_EXPERTS * MOE_ROW_TILE
    return (rows + MOE_ROW_TILE - 1) // MOE_ROW_TILE * MOE_ROW_TILE


def _chunk_copy(src, dst, sem):
    return pltpu.make_async_copy(src, dst, sem)


def _wait_rows(copy_of_rows, n_chunks):
    def wait(rows):
        def one(_, c):
            copy_of_rows(rows).wait()
            return c
        return one
    shift = MOE_WAIT_BATCH.bit_length() - 1
    lax.fori_loop(0, lax.shift_right_logical(n_chunks, shift), wait(MOE_WAIT_BATCH * MOE_GRANULE), 0)
    lax.fori_loop(0, n_chunks & (MOE_WAIT_BATCH - 1), wait(MOE_GRANULE), 0)


def _dispatch_kernel(chunks_ref, loc_ref, base_ref, schunks_ref, pstart_ref, pchunks_ref,
                     hb_ref, sel_ref, comb_ref, loct_ref, ustrict_ref,
                     xs_ref, posw_ref, buf_ref, zero_ref, sem_ref, zsem_ref):
    step = pl.program_id(0)
    n_steps = pl.num_programs(0)
    par = step % 2
    group = loct_ref.shape[0]
    ts = hb_ref.shape[0] // group
    nl = buf_ref.shape[1]
    g = MOE_GRANULE

    def wait_chunks(n, pp):
        _wait_rows(lambda r: _chunk_copy(buf_ref.at[0, pl.ds(0, r)], xs_ref.at[pl.ds(0, r)],
                                         sem_ref.at[pp]), n)

    @pl.when(step >= 2)
    def _():
        wait_chunks(schunks_ref[step - 2], par)

    lane = lax.broadcasted_iota(jnp.int32, (1, LANES), 1)
    prow = lax.broadcasted_iota(jnp.int32, (LANES, 1), 0)
    d = lax.broadcasted_iota(jnp.int32, (nl, ts), 0).astype(F32)
    experts = slice(ROUTE_E_LANE, ROUTE_E_LANE + N_EXPERTS)
    for u in range(group):
        rows = slice(u * ts, (u + 1) * ts)
        sel_t = sel_ref[rows, :].T[experts, :]
        comb_t = comb_ref[rows, :].T[experts, :]
        picked = sel_t > 0.0
        rank = jnp.dot(sel_t.astype(BF16), ustrict_ref[...], preferred_element_type=F32)
        dest = rank + jnp.concatenate([loct_ref[u]] * (ts // LANES), axis=1)
        pos_a = jnp.min(jnp.where(picked, dest, 1e9), axis=0, keepdims=True)
        pos_b = jnp.max(jnp.where(picked, dest, -1.0), axis=0, keepdims=True)
        w_a = jnp.sum(jnp.where(picked & (dest == pos_a), comb_t, 0.0), axis=0, keepdims=True)
        w_b = jnp.sum(jnp.where(picked & (dest == pos_b), comb_t, 0.0), axis=0, keepdims=True)
        pieces = [p.astype(F32) for w in (w_a, w_b) for p in _split3(w)]
        posw_t = jnp.zeros((LANES, ts), F32)
        for r, v in enumerate([pos_a, pos_b] + pieces):
            posw_t = jnp.where(prow == r, v, posw_t)
        posw = posw_t.T
        posw_ref[rows, :] = posw
        perm_a = jnp.where(d == pos_a, 1.0, 0.0).astype(BF16)
        perm_b = jnp.where(d == pos_b, 1.0, 0.0).astype(BF16)
        slot = par * group + u
        buf_ref[slot, :, :D_MODEL] = jnp.dot(perm_a + perm_b, hb_ref[rows, :],
                                             preferred_element_type=F32).astype(BF16)
        wl_a = jnp.where((lane >= 2) & (lane < 5), posw, 0.0).astype(BF16)
        wl_b = jnp.where((lane >= 5) & (lane < 8), posw, 0.0).astype(BF16)
        buf_ref[slot, :, D_MODEL:] = (
            jnp.dot(perm_a, wl_a, preferred_element_type=F32)
            + jnp.dot(perm_b, wl_b, preferred_element_type=F32)).astype(BF16)

    for u in range(group):
        for e in range(N_EXPERTS):
            idx = (step * group + u) * N_EXPERTS + e
            lo = loc_ref[idx]
            bs = base_ref[idx]

            def send(k, c, lo=lo, bs=bs, u=u):
                src = buf_ref.at[par * group + u, pl.ds(pl.multiple_of(lo + k * g, g), g)]
                dst = xs_ref.at[pl.ds(pl.multiple_of(bs + k * g, g), g)]
                _chunk_copy(src, dst, sem_ref.at[par]).start()
                return c
            lax.fori_loop(0, chunks_ref[idx], send, 0)

    @pl.when(step == n_steps - 1)
    def _():
        zero_ref[...] = jnp.zeros_like(zero_ref)
        for e in range(N_EXPERTS):
            ps = pstart_ref[e]

            def send_zero(k, c, ps=ps):
                dst = xs_ref.at[pl.ds(pl.multiple_of(ps + k * g, g), g)]
                _chunk_copy(zero_ref, dst, zsem_ref.at[0]).start()
                return c
            lax.fori_loop(0, pchunks_ref[e], send_zero, 0)
        for e in range(N_EXPERTS):
            def wait_zero(_, c):
                _chunk_copy(zero_ref, xs_ref.at[pl.ds(0, g)], zsem_ref.at[0]).wait()
                return c
            lax.fori_loop(0, pchunks_ref[e], wait_zero, 0)
        wait_chunks(schunks_ref[step], par)

        @pl.when(step >= 1)
        def _():
            wait_chunks(schunks_ref[step - 1], 1 - par)


def _dispatch(plan, hb, sel, comb, ts):
    t = hb.shape[0]
    group = _moe_group(t, ts)
    nl = _moe_local_rows(ts)
    rows = _moe_rows(t, ts)
    ustrict = jnp.tri(ts, k=-1, dtype=BF16).T
    row = lambda i, *_: (i, 0)
    grid_spec = pltpu.PrefetchScalarGridSpec(
        num_scalar_prefetch=6,
        grid=(t // (group * ts),),
        in_specs=[
            pl.BlockSpec((group * ts, D_MODEL), row),
            pl.BlockSpec((group * ts, LANES), row),
            pl.BlockSpec((group * ts, LANES), row),
            pl.BlockSpec((group, N_EXPERTS, LANES), lambda i, *_: (i, 0, 0)),
            pl.BlockSpec((ts, ts), lambda i, *_: (0, 0)),
        ],
        out_specs=[
            pl.BlockSpec(memory_space=pl.ANY),
            pl.BlockSpec((group * ts, LANES), row),
        ],
        scratch_shapes=[
            pltpu.VMEM((2 * group, nl, MOE_ROW_LANES), BF16),
            pltpu.VMEM((MOE_GRANULE, MOE_ROW_LANES), BF16),
            pltpu.SemaphoreType.DMA((2,)),
            pltpu.SemaphoreType.DMA((1,)),
        ],
    )
    return pl.pallas_call(
        _dispatch_kernel,
        grid_spec=grid_spec,
        out_shape=(
            jax.ShapeDtypeStruct((rows, MOE_ROW_LANES), BF16),
            jax.ShapeDtypeStruct((t, LANES), F32),
        ),
        compiler_params=pltpu.CompilerParams(
            dimension_semantics=("arbitrary",), vmem_limit_bytes=VMEM_LIMIT_BYTES),
        name="moe_dispatch",
    )(plan["chunks"], plan["loc_off"], plan["base"], plan["step_chunks"], plan["pad_start"],
      plan["pad_chunks"], hb, sel, comb, plan["loc_t"], ustrict)


def _expert_kernel(te_ref, nt_ref, x_ref, wg_ref, wu_ref, wd_ref, y_ref, wgb_ref, wub_ref, wdb_ref):
    j = pl.program_id(0)

    @pl.when(j < nt_ref[0])
    def _():
        @pl.when((j == 0) | (te_ref[j] != te_ref[jnp.maximum(j - 1, 0)]))
        def _():
            wgb_ref[...] = wg_ref[0].astype(BF16)
            wub_ref[...] = wu_ref[0].astype(BF16)
            wdb_ref[...] = wd_ref[0].astype(BF16)

        x = x_ref[:, :D_MODEL]
        wl = x_ref[:, D_MODEL:].astype(F32)
        w = jnp.sum(wl[:, 2:8], axis=1, keepdims=True)
        gate = jnp.dot(x, wgb_ref[...], preferred_element_type=F32)
        up = jnp.dot(x, wub_ref[...], preferred_element_type=F32)
        act = (_silu(gate) * up).astype(BF16)
        y_ref[...] = (w * jnp.dot(act, wdb_ref[...], preferred_element_type=F32)).astype(y_ref.dtype)


def _experts(plan, xs, w_gate, w_up, w_down):
    rows = xs.shape[0]
    live = lambda j, te, nt: jnp.minimum(j, nt[0] - 1)
    expert = lambda j, te, nt: (te[live(j, te, nt)], 0, 0)
    grid_spec = pltpu.PrefetchScalarGridSpec(
        num_scalar_prefetch=2,
        grid=(rows // MOE_ROW_TILE,),
        in_specs=[
            pl.BlockSpec((MOE_ROW_TILE, MOE_ROW_LANES), lambda j, te, nt: (live(j, te, nt), 0)),
            pl.BlockSpec((1, D_MODEL, D_FF_EXPERT), expert),
            pl.BlockSpec((1, D_MODEL, D_FF_EXPERT), expert),
            pl.BlockSpec((1, D_FF_EXPERT, D_MODEL), expert),
        ],
        out_specs=pl.BlockSpec((MOE_ROW_TILE, D_MODEL), lambda j, te, nt: (live(j, te, nt), 0)),
        scratch_shapes=[
            pltpu.VMEM((D_MODEL, D_FF_EXPERT), BF16),
            pltpu.VMEM((D_MODEL, D_FF_EXPERT), BF16),
            pltpu.VMEM((D_FF_EXPERT, D_MODEL), BF16),
        ],
    )
    return pl.pallas_call(
        _expert_kernel,
        grid_spec=grid_spec,
        out_shape=jax.ShapeDtypeStruct((rows, D_MODEL), BF16),
        compiler_params=pltpu.CompilerParams(
            dimension_semantics=("arbitrary",), vmem_limit_bytes=VMEM_LIMIT_BYTES),
        name="moe_experts",
    )(plan["tile_expert"], plan["n_tiles"], xs, w_gate, w_up, w_down)


def _combine_kernel(chunks_ref, loc_ref, base_ref, schunks_ref,
                    y_ref, posw_ref, h_ref, g_ref, b_ref, o_ref, buf_ref, sem_ref):
    step = pl.program_id(0)
    n_steps = pl.num_programs(0)
    par = step % 2
    nl = buf_ref.shape[1]
    group = buf_ref.shape[0] // 2
    ts = h_ref.shape[0] // group
    g = MOE_GRANULE

    def fetch(ss, pp):
        for u in range(group):
            for e in range(N_EXPERTS):
                idx = (ss * group + u) * N_EXPERTS + e
                lo = loc_ref[idx]
                bs = base_ref[idx]

                def get(k, c, lo=lo, bs=bs, u=u):
                    src = y_ref.at[pl.ds(pl.multiple_of(bs + k * g, g), g)]
                    dst = buf_ref.at[pp * group + u, pl.ds(pl.multiple_of(lo + k * g, g), g)]
                    _chunk_copy(src, dst, sem_ref.at[pp]).start()
                    return c
                lax.fori_loop(0, chunks_ref[idx], get, 0)

    @pl.when(step == 0)
    def _():
        buf_ref[...] = jnp.zeros(buf_ref.shape, BF16)
        fetch(0, 0)

    @pl.when(step + 1 < n_steps)
    def _():
        fetch(step + 1, 1 - par)

    _wait_rows(lambda r: _chunk_copy(y_ref.at[pl.ds(0, r)], buf_ref.at[0, pl.ds(0, r)],
                                     sem_ref.at[par]), schunks_ref[step])

    d = lax.broadcasted_iota(jnp.int32, (1, nl), 1).astype(F32)
    for u in range(group):
        rows = slice(u * ts, (u + 1) * ts)
        yb = buf_ref[par * group + u]
        posw = posw_ref[rows, :]
        gather = jnp.where((posw[:, 0:1] == d) | (posw[:, 1:2] == d), 1.0, 0.0).astype(BF16)
        ffn = jnp.dot(gather, yb, preferred_element_type=F32)
        o_ref[rows, :] = _layer_norm(DEEPNORM_ALPHA * h_ref[rows, :] + ffn, g_ref[...], b_ref[...])


def _combine(plan, y, posw, h, ln_g, ln_b, ts):
    t = h.shape[0]
    group = _moe_group(t, ts)
    nl = _moe_local_rows(ts)
    row = lambda i, *_: (i, 0)
    const = lambda i, *_: (0, 0)
    grid_spec = pltpu.PrefetchScalarGridSpec(
        num_scalar_prefetch=4,
        grid=(t // (group * ts),),
        in_specs=[
            pl.BlockSpec(memory_space=pl.ANY),
            pl.BlockSpec((group * ts, LANES), row),
            pl.BlockSpec((group * ts, D_MODEL), row),
            pl.BlockSpec((1, D_MODEL), const),
            pl.BlockSpec((1, D_MODEL), const),
        ],
        out_specs=pl.BlockSpec((group * ts, D_MODEL), row),
        scratch_shapes=[
            pltpu.VMEM((2 * group, nl, D_MODEL), BF16),
            pltpu.SemaphoreType.DMA((2,)),
        ],
    )
    return pl.pallas_call(
        _combine_kernel,
        grid_spec=grid_spec,
        out_shape=jax.ShapeDtypeStruct((t, D_MODEL), F32),
        compiler_params=pltpu.CompilerParams(
            dimension_semantics=("arbitrary",), vmem_limit_bytes=VMEM_LIMIT_BYTES),
        name="moe_combine_ln",
    )(plan["chunks"], plan["loc_off"], plan["base"], plan["step_chunks"], y, posw, h, ln_g, ln_b)


def _pad_lanes(a, n=LANES):
    return jnp.pad(a, ((0, 0), (0, n - a.shape[1])))


def _layer(h2d, batch, seq, w_in, b_in, conv_w, conv_b, a_log, d_skip, ssd_norm_g, w_out,
           ln1_g, ln1_b, rg_w, rg_b, re_w, re_b, w_gate, w_up, w_down, ln2_g, ln2_b):
    o_dt = SSD_INNER + SSD_CONV_DIM
    o_q = o_dt + SSD_HEADS
    o_f = o_q + 3 * ATT_INNER
    w_main = jnp.concatenate([w_in[:, :o_dt], w_in[:, o_q:o_f]], axis=1).astype(BF16)
    b_main = jnp.concatenate([b_in[:o_dt], b_in[o_q:o_f]])[None, :]
    w_dt, w_f = w_in[:, o_dt:o_q], w_in[:, o_f:]
    w_small = _pad_lanes(jnp.concatenate([w_dt, w_f, w_dt], axis=1)).astype(BF16)
    b_small = _pad_lanes(jnp.concatenate([b_in[o_dt:o_q], b_in[o_f:], b_in[o_dt:o_q]])[None, :])
    a_neg = -jnp.exp(a_log.astype(F32))
    a_row = _pad_lanes(a_neg[None, :])
    a_exp = jnp.repeat(a_neg, SSD_HEAD_DIM)[None, :]
    dskip_exp = jnp.repeat(d_skip.astype(F32), SSD_HEAD_DIM)[None, :]

    z, xbc, q_t, k, v_t, gate, qf_t, kf = _in_proj(h2d, w_main, b_main, w_small, b_small, a_row, seq)
    y_ssd = _ssd(xbc, z, gate, conv_w, conv_b[None, :], a_exp, dskip_exp, ssd_norm_g[None, :],
                 batch, seq)
    y_att = _attention(q_t, k, v_t, qf_t, kf, batch, seq)

    gap = ROUTE_E_LANE - N_EXPERT_GROUPS
    w_r = _pad_lanes(jnp.concatenate(
        [rg_w, jnp.zeros((D_MODEL, gap), F32),
         jnp.transpose(re_w, (1, 0, 2)).reshape(D_MODEL, N_EXPERTS)], axis=1))
    b_r = _pad_lanes(jnp.concatenate(
        [rg_b, jnp.zeros((gap,), F32), re_b.reshape(N_EXPERTS)])[None, :])
    wr_hi = w_r.astype(BF16)
    wr_lo = (w_r - wr_hi.astype(F32)).astype(BF16)
    wo = w_out.astype(BF16)
    ts = min(MOE_SORT_TILE, h2d.shape[0])
    h1, h1b, comb, sel, cnt = _out_proj(y_ssd, y_att, h2d, wo[:SSD_INNER], wo[SSD_INNER:],
                                        ln1_g[None, :], ln1_b[None, :],
                                        jnp.concatenate([wr_hi, wr_lo], axis=1), b_r, ts)

    plan = _dispatch_plan(
        cnt[:, 0, ROUTE_E_LANE:ROUTE_E_LANE + N_EXPERTS].astype(jnp.int32), ts)
    xs, posw = _dispatch(plan, h1b, sel, comb, ts)
    y = _experts(plan, xs, w_gate, w_up, w_down)
    return _combine(plan, y, posw, h1, ln2_g[None, :], ln2_b[None, :], ts)


def kernel(x, w_in, b_in, conv_w, conv_b, a_log, d_skip, ssd_norm_g, w_out, ln1_g, ln1_b,
           router_group_w, router_group_b, router_expert_w, router_expert_b, w_gate, w_up,
           w_down, ln2_g, ln2_b):
    batch, seq, d = x.shape
    h = x.reshape(batch * seq, d)
    for l in range(w_in.shape[0]):
        h = _layer(h, batch, seq, w_in[l], b_in[l], conv_w[l], conv_b[l], a_log[l], d_skip[l],
                   ssd_norm_g[l], w_out[l], ln1_g[l], ln1_b[l], router_group_w[l],
                   router_group_b[l], router_expert_w[l], router_expert_b[l], w_gate[l],
                   w_up[l], w_down[l], ln2_g[l], ln2_b[l])
    return h.reshape(batch, seq, d)
```

```python
import functools

import jax
import jax.numpy as jnp
import numpy as np
from jax import lax
from jax.experimental import pallas as pl
from jax.experimental.pallas import tpu as pltpu

F32 = jnp.float32
BF16 = jnp.bfloat16

D_MODEL = 1024
SSD_HEADS = 8
SSD_HEAD_DIM = 64
SSD_INNER = SSD_HEADS * SSD_HEAD_DIM
SSD_GROUPS = 2
SSD_STATE = 128
CONV_WIDTH = 4
SSD_CONV_DIM = SSD_INNER + 2 * SSD_GROUPS * SSD_STATE
ATT_HEADS = 8
ATT_HEAD_DIM = 64
ATT_INNER = ATT_HEADS * ATT_HEAD_DIM
N_EXPERT_GROUPS = 4
EXPERTS_PER_GROUP = 4
N_EXPERTS = N_EXPERT_GROUPS * EXPERTS_PER_GROUP
D_FF_EXPERT = 512
DEPTH = 1
DEEPNORM_ALPHA = (2.0 * DEPTH) ** 0.25
LN_EPS = 1e-5
RMS_EPS = 1e-5

LANES = 128
SUBLANES = 8
VMEM_LIMIT_BYTES = 56 * 1024 * 1024

GATE_A_LANE = 0
GATE_F_LANE = SSD_HEADS
GATE_DT_LANE = 2 * SSD_HEADS
ROUTE_E_LANE = SUBLANES
MOE_SORT_TILE = 256
MOE_GRANULE = 16
MOE_ROW_TILE = 1024
MOE_TILES_PER_STEP = 4
MOE_ROW_LANES = D_MODEL + LANES
MOE_WAIT_BATCH = 8

IN_PROJ_ROWS = 1024
OUT_PROJ_ROWS = 1024
CUMSUM_BLOCK = 256
FB_LANES = 6
ATT_BLOCK = 256
ATT_PAIRS_PER_STEP = 4
ATT_BATCHES_PER_STEP = 2

SSD_ROWS = 256
SSD_STEP_ROWS = 512
SSD_CHUNK = 64
NEG_BIG = -1e30
LOG2E = 1.4426950408889634


def _split3(v):
    hi = v.astype(BF16)
    r1 = v - hi.astype(F32)
    mid = r1.astype(BF16)
    lo = (r1 - mid.astype(F32)).astype(BF16)
    return hi, mid, lo


def _silu(v):
    half = 0.5 * v
    return half * (1.0 + jnp.tanh(half))


def _softplus(v):
    return jnp.maximum(v, 0.0) + jnp.log1p(jnp.exp(-jnp.abs(v)))


def _in_proj_kernel(x_ref, wm_ref, bm_ref, ws_ref, bs_ref, arow_ref, place_ref, ones_ref,
                    z_ref, xbc_ref, qt_ref, k_ref, vt_ref, gate_ref, qft_ref, kf_ref,
                    carry_ref, *, tiles_per_seq, cb):
    i = pl.program_id(0)
    tm = x_ref.shape[0]
    tb = qt_ref.shape[2]
    xb = x_ref[...].astype(BF16)

    def put_transposed(ref, val):
        for c in range(tm // tb):
            ref[c] = val[c * tb:(c + 1) * tb, :].T.astype(ref.dtype)

    s = jnp.dot(xb, ws_ref[...], preferred_element_type=F32) + bs_ref[...]
    lane = lax.broadcasted_iota(jnp.int32, (1, LANES), 1)
    sp = _softplus(s)
    log_f = -_softplus(-s)
    val = jnp.where(lane < GATE_F_LANE, sp * arow_ref[...],
                    jnp.where(lane < GATE_DT_LANE, log_f,
                              jnp.where(lane < GATE_DT_LANE + SSD_HEADS, sp, 0.0)))
    carry = jnp.where(i % tiles_per_seq == 0, 0.0, carry_ref[...])
    row = lax.broadcasted_iota(jnp.int32, (cb, 1), 0)
    sums = []
    for c in range(tm // cb):
        part = val[c * cb:(c + 1) * cb, :]
        shift = 1
        while shift < cb:
            part = part + jnp.where(row >= shift, pltpu.roll(part, shift, 0), 0.0)
            shift *= 2
        part = part + carry
        carry = part[cb - 1:cb, :]
        sums.append(part)
    cs = jnp.concatenate(sums, axis=0)
    carry_ref[...] = carry
    out = jnp.where(lane < GATE_DT_LANE, cs, val)
    gate_ref[...] = out
    pieces = jnp.concatenate(_split3(cs * LOG2E), axis=1)
    fb = jnp.dot(pieces, place_ref[...], preferred_element_type=F32) + ones_ref[...]
    put_transposed(qft_ref, fb[:, :LANES])
    kf_ref[...] = fb[:, LANES:].astype(BF16)

    col = 0
    for ref, scale in ((z_ref, None), (xbc_ref, None), (qt_ref, ATT_HEAD_DIM ** -0.5 * LOG2E),
                       (k_ref, None), (vt_ref, None)):
        transposed = ref is qt_ref or ref is vt_ref
        n = ATT_INNER if transposed else ref.shape[1]
        acc = jnp.dot(xb, wm_ref[:, col:col + n], preferred_element_type=F32) + bm_ref[:, col:col + n]
        if scale is not None:
            acc = acc * scale
        if transposed:
            put_transposed(ref, acc)
        else:
            ref[...] = acc.astype(ref.dtype)
        col += n


def _forget_bias_placement():
    place = np.zeros((3 * LANES, 2 * LANES), np.float32)
    ones = np.zeros((1, 2 * LANES), np.float32)
    for h in range(ATT_HEADS):
        for piece in range(3):
            src = piece * LANES + GATE_F_LANE + h
            place[src, FB_LANES * h + piece] = 1.0
            place[src, LANES + FB_LANES * h + 3 + piece] = -1.0
            ones[0, FB_LANES * h + 3 + piece] = 1.0
            ones[0, LANES + FB_LANES * h + piece] = 1.0
    return jnp.asarray(place, BF16), jnp.asarray(ones, F32)


def _in_proj(x2d, w_main, b_main, w_small, b_small, a_row, seq):
    t = x2d.shape[0]
    tm = min(IN_PROJ_ROWS, seq)
    tk = min(ATT_BLOCK, seq)
    place, ones = _forget_bias_placement()
    n_main = w_main.shape[1]
    row = lambda i: (i, 0)
    const = lambda i: (0, 0)
    outs = (
        jax.ShapeDtypeStruct((t, SSD_INNER), BF16),
        jax.ShapeDtypeStruct((t, SSD_CONV_DIM), BF16),
        jax.ShapeDtypeStruct((t // tk, ATT_INNER, tk), BF16),
        jax.ShapeDtypeStruct((t, ATT_INNER), BF16),
        jax.ShapeDtypeStruct((t // tk, ATT_INNER, tk), BF16),
        jax.ShapeDtypeStruct((t, LANES), F32),
        jax.ShapeDtypeStruct((t // tk, LANES, tk), BF16),
        jax.ShapeDtypeStruct((t, LANES), BF16),
    )
    return pl.pallas_call(
        functools.partial(_in_proj_kernel, tiles_per_seq=seq // tm, cb=min(CUMSUM_BLOCK, tm)),
        grid=(t // tm,),
        in_specs=[
            pl.BlockSpec((tm, D_MODEL), row),
            pl.BlockSpec((D_MODEL, n_main), const),
            pl.BlockSpec((1, n_main), const),
            pl.BlockSpec((D_MODEL, LANES), const),
            pl.BlockSpec((1, LANES), const),
            pl.BlockSpec((1, LANES), const),
            pl.BlockSpec((3 * LANES, 2 * LANES), const),
            pl.BlockSpec((1, 2 * LANES), const),
        ],
        out_specs=[
            pl.BlockSpec((tm, SSD_INNER), row),
            pl.BlockSpec((tm, SSD_CONV_DIM), row),
            pl.BlockSpec((tm // tk, ATT_INNER, tk), lambda i: (i, 0, 0)),
            pl.BlockSpec((tm, ATT_INNER), row),
            pl.BlockSpec((tm // tk, ATT_INNER, tk), lambda i: (i, 0, 0)),
            pl.BlockSpec((tm, LANES), row),
            pl.BlockSpec((tm // tk, LANES, tk), lambda i: (i, 0, 0)),
            pl.BlockSpec((tm, LANES), row),
        ],
        out_shape=outs,
        scratch_shapes=[pltpu.VMEM((1, LANES), F32)],
        compiler_params=pltpu.CompilerParams(
            dimension_semantics=("arbitrary",), vmem_limit_bytes=VMEM_LIMIT_BYTES),
        name="in_proj",
    )(x2d, w_main, b_main, w_small, b_small, a_row, place, ones)


def _expand_heads(gate, lane0, rows):
    lane = lax.broadcasted_iota(jnp.int32, (1, LANES), 1)
    pieces = []
    for p in range(SSD_HEADS // 2):
        a = jnp.broadcast_to(gate[:, lane0 + 2 * p:lane0 + 2 * p + 1], (rows, LANES))
        b = jnp.broadcast_to(gate[:, lane0 + 2 * p + 1:lane0 + 2 * p + 2], (rows, LANES))
        pieces.append(jnp.where(lane < SSD_HEAD_DIM, a, b))
    return jnp.concatenate(pieces, axis=1)


def _ssd_kernel(xbc_ref, z_ref, gate_ref, cw_ref, cb_ref, aexp_ref, dskip_ref, ng_ref, shift_ref,
                y_ref, ext_ref, state_ref):
    j = pl.program_id(1)
    rows = shift_ref.shape[1]
    q = SSD_CHUNK
    gw = SSD_INNER // SSD_GROUPS
    hpg = SSD_HEADS // SSD_GROUPS
    pad = SUBLANES

    @pl.when(j == 0)
    def _():
        ext_ref[0:pad, :] = jnp.zeros((pad, SSD_CONV_DIM), F32)
        state_ref[...] = jnp.zeros_like(state_ref)

    r_i = lax.broadcasted_iota(jnp.int32, (q, gw), 0)
    c_i = lax.broadcasted_iota(jnp.int32, (q, gw), 1)
    s_i = c_i % q
    eye_t = s_i == r_i
    tril_t = s_i <= r_i
    rb = lax.broadcasted_iota(jnp.int32, (hpg * q, gw), 0) // q
    cbk = lax.broadcasted_iota(jnp.int32, (hpg * q, gw), 1) // SSD_HEAD_DIM
    blk = rb == cbk

    for t in range(xbc_ref.shape[0] // rows):
        sub = slice(t * rows, (t + 1) * rows)
        xb = xbc_ref[sub, :]
        xf = xb.astype(F32)
        conv = cb_ref[...] + xf * cw_ref[CONV_WIDTH - 1:CONV_WIDTH, :]
        for s in range(1, CONV_WIDTH):
            shifted = jnp.dot(shift_ref[s - 1], xb, preferred_element_type=F32)
            conv = conv + shifted * cw_ref[CONV_WIDTH - 1 - s:CONV_WIDTH - s, :]
        ext_ref[pad:2 * pad, :] = xf[0:pad, :]
        head = cb_ref[...]
        for kk in range(CONV_WIDTH):
            off = pad - (CONV_WIDTH - 1) + kk
            head = head + ext_ref[off:off + pad, :] * cw_ref[kk:kk + 1, :]
        ext_ref[0:pad, :] = xf[rows - pad:rows, :]
        u = _silu(jnp.concatenate([head, conv[pad:, :]], axis=0))
        xs = u[:, :SSD_INNER]
        b_all = u[:, SSD_INNER:SSD_INNER + SSD_GROUPS * SSD_STATE].astype(BF16)
        c_all = u[:, SSD_INNER + SSD_GROUPS * SSD_STATE:].astype(BF16)

        gate = gate_ref[sub, :]
        dt_e = _expand_heads(gate, GATE_DT_LANE, rows)
        a_e = _expand_heads(gate, GATE_A_LANE, rows)
        xdt = xs * dt_e
        da_e = dt_e * aexp_ref[...]

        ys = []
        for c in range(rows // q):
            r0 = c * q
            a_c = a_e[r0:r0 + q, :]
            a_prev = a_c[0:1, :] - da_e[r0:r0 + 1, :]
            a_end = a_c[q - 1:q, :]
            y_groups = []
            for g in range(SSD_GROUPS):
                l0 = g * gw
                a_cg = a_c[:, l0:l0 + gw]
                diag = jnp.sum(jnp.where(eye_t, a_cg, 0.0), axis=0, keepdims=True)
                seg = jnp.minimum(a_cg - diag, 0.0)
                lcat = jnp.where(tril_t, jnp.exp(seg), 0.0)
                bg = b_all[r0:r0 + q, g * SSD_STATE:(g + 1) * SSD_STATE]
                cg = c_all[r0:r0 + q, g * SSD_STATE:(g + 1) * SSD_STATE]
                b_tile = jnp.concatenate([bg] * hpg, axis=0)
                gcat = lax.dot_general(cg, b_tile, (((1,), (1,)), ((), ())),
                                       preferred_element_type=F32)
                xg = xdt[r0:r0 + q, l0:l0 + gw]
                xg_b = xg.astype(BF16)
                xbd = jnp.where(blk, jnp.concatenate([xg_b] * hpg, axis=0), jnp.zeros((), BF16))
                y_diag = jnp.dot((gcat * lcat).astype(BF16), xbd, preferred_element_type=F32)
                st = state_ref[g]
                y_off = jnp.dot(cg, st.astype(BF16), preferred_element_type=F32) \
                    * jnp.exp(a_cg - a_prev[:, l0:l0 + gw])
                dte = jnp.exp(a_end[:, l0:l0 + gw] - a_cg)
                upd = lax.dot_general(bg, (xg * dte).astype(BF16), (((0,), (0,)), ((), ())),
                                      preferred_element_type=F32)
                state_ref[g] = st * jnp.exp(a_end[:, l0:l0 + gw] - a_prev[:, l0:l0 + gw]) + upd
                y_groups.append(y_diag + y_off)
            ys.append(jnp.concatenate(y_groups, axis=1))
        y = jnp.concatenate(ys, axis=0) + xs * dskip_ref[...]
        y = y * _silu(z_ref[sub, :].astype(F32))
        outs = []
        for g in range(SSD_GROUPS):
            yg = y[:, g * gw:(g + 1) * gw]
            ms = jnp.mean(yg * yg, axis=1, keepdims=True)
            outs.append(yg * lax.rsqrt(ms + RMS_EPS))
        y_ref[sub, :] = (jnp.concatenate(outs, axis=1) * ng_ref[...]).astype(y_ref.dtype)


def _ssd(xbc, z, gate, conv_w, conv_b, a_exp, dskip_exp, norm_g, batch, seq):
    t = xbc.shape[0]
    rows = min(SSD_ROWS, seq)
    step_rows = min(SSD_STEP_ROWS, seq)
    nj = seq // step_rows
    row = lambda b, j: (b * nj + j, 0)
    const = lambda b, j: (0, 0)
    shifts = jnp.stack([jnp.eye(rows, k=-s, dtype=BF16) for s in range(1, CONV_WIDTH)])
    return pl.pallas_call(
        _ssd_kernel,
        grid=(batch, nj),
        in_specs=[
            pl.BlockSpec((step_rows, SSD_CONV_DIM), row),
            pl.BlockSpec((step_rows, SSD_INNER), row),
            pl.BlockSpec((step_rows, LANES), row),
            pl.BlockSpec((CONV_WIDTH, SSD_CONV_DIM), const),
            pl.BlockSpec((1, SSD_CONV_DIM), const),
            pl.BlockSpec((1, SSD_INNER), const),
            pl.BlockSpec((1, SSD_INNER), const),
            pl.BlockSpec((1, SSD_INNER), const),
            pl.BlockSpec((CONV_WIDTH - 1, rows, rows), lambda b, j: (0, 0, 0)),
        ],
        out_specs=pl.BlockSpec((step_rows, SSD_INNER), row),
        out_shape=jax.ShapeDtypeStruct((t, SSD_INNER), BF16),
        scratch_shapes=[
            pltpu.VMEM((2 * SUBLANES, SSD_CONV_DIM), F32),
            pltpu.VMEM((SSD_GROUPS, SSD_STATE, SSD_INNER // SSD_GROUPS), F32),
        ],
        compiler_params=pltpu.CompilerParams(
            dimension_semantics=("arbitrary", "arbitrary"), vmem_limit_bytes=VMEM_LIMIT_BYTES),
        name="ssd",
    )(xbc, z, gate, conv_w, conv_b, a_exp, dskip_exp, norm_g, shifts)


def _attn_kernel(qt_ref, k_ref, vt_ref, qft_ref, kf_ref, o_ref, *scratch, tb, pairs, batches):
    pair0 = pl.program_id(1) * pairs
    nb = qt_ref.shape[0] // batches
    seq = nb * tb
    chains = [(bb, pp) for bb in range(batches) for pp in range(pairs)]
    s_refs, acc_refs = scratch[:len(chains)], scratch[len(chains):]
    feat = lax.broadcasted_iota(jnp.int32, (LANES, 1), 0)
    in_head = (feat < ATT_HEAD_DIM, feat >= ATT_HEAD_DIM)
    ones = jnp.ones((2 * SUBLANES, tb), BF16)
    key = lax.broadcasted_iota(jnp.int32, (tb, 2 * tb), 0)
    qry = lax.broadcasted_iota(jnp.int32, (tb, 2 * tb), 1)
    causal = jnp.where(qry >= tb, qry - tb, qry) >= key
    zero = jnp.zeros((), BF16)

    def logits_t(qa_t, c, jk):
        bb, pp = chains[c]
        k0 = pl.multiple_of(bb * seq + jk * tb, tb)
        ka = jnp.concatenate([k_ref[pl.ds(k0, tb), pp * LANES:(pp + 1) * LANES],
                              kf_ref[pl.ds(k0, tb), :]], axis=1)
        return jnp.dot(ka, qa_t, preferred_element_type=F32)

    def update(s_t, m, c, jk, par):
        bb, pp = chains[c]
        m_new = jnp.maximum(m, jnp.max(s_t, axis=0, keepdims=True))
        alpha = jnp.exp2(m - m_new)
        p_t = jnp.exp2(s_t - m_new).astype(BF16)
        for hh in range(2):
            r0 = pp * LANES + hh * ATT_HEAD_DIM
            va_t = jnp.concatenate([vt_ref[bb * nb + jk, r0:r0 + ATT_HEAD_DIM, :], ones], axis=0)
            cols = slice(hh * tb, (hh + 1) * tb)
            acc_refs[c][par, hh] = alpha[:, cols] * acc_refs[c][par, hh] + jnp.dot(
                va_t, p_t[:, cols], preferred_element_type=F32)
        return m_new

    def build_queries(i):
        qas = []
        for bb, pp in chains:
            qf_t = qft_ref[bb * nb + i]
            q_t = qt_ref[bb * nb + i, pp * LANES:(pp + 1) * LANES, :]
            cols = []
            for hh in range(2):
                fb0 = FB_LANES * (2 * (pair0 + pp) + hh)
                in_fb = (feat >= fb0) & (feat < fb0 + FB_LANES)
                cols.append(jnp.concatenate(
                    [jnp.where(in_head[hh], q_t, zero), jnp.where(in_fb, qf_t, zero)], axis=0))
            qas.append(jnp.concatenate(cols, axis=1))
        return qas

    def first_logits(i):
        qas = build_queries(i)
        for c in range(len(chains)):
            s_refs[c][2] = logits_t(qas[c], c, 0)

    def q_block(i, _):
        par = i % 2
        qas = build_queries(i)
        for c in range(len(chains)):
            acc_refs[c][par] = jnp.zeros(acc_refs[c].shape[1:], F32)

        def step(jk, ms, rd, wr):
            new = []
            for c in range(len(chains)):
                s_t = s_refs[c][rd]
                s_refs[c][wr] = logits_t(qas[c], c, jk + 1)
                new.append(update(s_t, ms[c], c, jk, par))
            return tuple(new)

        def two_steps(t2, ms):
            return step(2 * t2 + 2, step(2 * t2 + 1, ms, 1, 0), 0, 1)

        ms = tuple(jnp.full((1, 2 * tb), NEG_BIG, F32) for _ in chains)
        ms = lax.cond(i >= 1, lambda v: step(0, v, 2, 1), lambda v: v, ms)
        ms = lax.fori_loop(0, lax.shift_right_logical(jnp.maximum(i - 1, 0), 1), two_steps, ms)
        ms = lax.cond((i >= 2) & (i % 2 == 0), lambda v: step(i - 1, v, 1, 0), lambda v: v, ms)

        last = jnp.where(i >= 1, i % 2, 2)
        diag = [jnp.where(causal, s_refs[c][last], NEG_BIG) for c in range(len(chains))]
        first_logits(jnp.minimum(i + 1, nb - 1))
        finish(jnp.maximum(i - 1, 0), 1 - par)
        for c in range(len(chains)):
            update(diag[c], ms[c], c, i, par)
        return 0

    def finish(iq, par):
        q0 = pl.multiple_of(iq * tb, tb)
        for c, (bb, pp) in enumerate(chains):
            heads = []
            for hh in range(2):
                acc_t = acc_refs[c][par, hh]
                inv = 1.0 / acc_t[ATT_HEAD_DIM:ATT_HEAD_DIM + 1, :]
                heads.append(acc_t[:ATT_HEAD_DIM, :] * inv)
            out_t = jnp.concatenate(heads, axis=0)
            o_ref[pl.ds(bb * seq + q0, tb), pp * LANES:(pp + 1) * LANES] = out_t.T.astype(o_ref.dtype)

    for c in range(len(chains)):
        acc_refs[c][1] = jnp.ones(acc_refs[c].shape[1:], F32)
    first_logits(0)
    lax.fori_loop(0, nb, q_block, 0)
    finish(nb - 1, (nb - 1) % 2)


def _attention(q_t, k, v_t, qf_t, kf, batch, seq):
    t = k.shape[0]
    tb = min(ATT_BLOCK, seq)
    nb = seq // tb
    pairs = ATT_PAIRS_PER_STEP
    batches = ATT_BATCHES_PER_STEP if batch % ATT_BATCHES_PER_STEP == 0 else 1
    width = pairs * LANES
    return pl.pallas_call(
        functools.partial(_attn_kernel, tb=tb, pairs=pairs, batches=batches),
        grid=(batch // batches, ATT_HEADS // (2 * pairs)),
        in_specs=[
            pl.BlockSpec((batches * nb, width, tb), lambda b, p: (b, p, 0)),
            pl.BlockSpec((batches * seq, width), lambda b, p: (b, p)),
            pl.BlockSpec((batches * nb, width, tb), lambda b, p: (b, p, 0)),
            pl.BlockSpec((batches * nb, LANES, tb), lambda b, p: (b, 0, 0)),
            pl.BlockSpec((batches * seq, LANES), lambda b, p: (b, 0)),
        ],
        out_specs=pl.BlockSpec((batches * seq, width), lambda b, p: (b, p)),
        out_shape=jax.ShapeDtypeStruct((t, ATT_INNER), BF16),
        scratch_shapes=([pltpu.VMEM((3, tb, 2 * tb), F32) for _ in range(batches * pairs)]
                        + [pltpu.VMEM((2, 2, ATT_HEAD_DIM + 2 * SUBLANES, tb), F32)
                           for _ in range(batches * pairs)]),
        compiler_params=pltpu.CompilerParams(
            dimension_semantics=("arbitrary", "arbitrary"), vmem_limit_bytes=VMEM_LIMIT_BYTES),
        name="fox_attention",
    )(q_t, k, v_t, qf_t, kf)


def _layer_norm(u, g, b):
    mu = jnp.mean(u, axis=1, keepdims=True)
    d = u - mu
    var = jnp.mean(d * d, axis=1, keepdims=True)
    return d * lax.rsqrt(var + LN_EPS) * g + b


def _route(lg):
    rows = lg.shape[0]
    lt = lg.T
    gl = lt[0:N_EXPERT_GROUPS, :]
    g_row = lax.broadcasted_iota(jnp.int32, (N_EXPERT_GROUPS, 1), 0).astype(F32)
    gmax = jnp.max(gl, axis=0, keepdims=True)
    gsum = jnp.sum(jnp.exp(gl - gmax), axis=0, keepdims=True)
    g_val = 1.0 / gsum
    g_idx = jnp.min(jnp.where(gl == gmax, g_row, float(N_EXPERT_GROUPS)), axis=0, keepdims=True)
    el = lt[ROUTE_E_LANE:ROUTE_E_LANE + N_EXPERTS, :]
    e_row = lax.broadcasted_iota(jnp.int32, (N_EXPERTS, 1), 0)
    e_grp = (e_row // EXPERTS_PER_GROUP).astype(F32)
    e_row = e_row.astype(F32)
    in_grp = e_grp == g_idx
    m1 = jnp.max(jnp.where(in_grp, el, NEG_BIG), axis=0, keepdims=True)
    i1 = jnp.min(jnp.where(in_grp & (el == m1), e_row, float(N_EXPERTS)), axis=0, keepdims=True)
    rest = in_grp & (e_row != i1)
    m2 = jnp.max(jnp.where(rest, el, NEG_BIG), axis=0, keepdims=True)
    i2 = jnp.min(jnp.where(rest & (el == m2), e_row, float(N_EXPERTS)), axis=0, keepdims=True)
    r = jnp.exp(m2 - m1)
    w1 = 1.0 / (1.0 + r)
    w2 = r * w1
    comb_e = jnp.where(e_row == i1, g_val * w1, jnp.where(e_row == i2, g_val * w2, 0.0))
    sel_e = jnp.where((e_row == i1) | (e_row == i2), 1.0, 0.0)
    top = jnp.zeros((ROUTE_E_LANE, rows), F32)
    bottom = jnp.zeros((LANES - ROUTE_E_LANE - N_EXPERTS, rows), F32)
    comb = jnp.concatenate([top, comb_e, bottom], axis=0).T
    sel = jnp.concatenate([top, sel_e, bottom], axis=0).T
    return comb, sel


def _out_proj_kernel(ys_ref, ya_ref, x_ref, wa_ref, wb_ref, g_ref, b_ref, wr_ref, br_ref,
                     h_ref, hb_ref, comb_ref, sel_ref, cnt_ref):
    mix = jnp.dot(ys_ref[...], wa_ref[...], preferred_element_type=F32)
    mix = mix + jnp.dot(ya_ref[...], wb_ref[...], preferred_element_type=F32)
    h = _layer_norm(DEEPNORM_ALPHA * x_ref[...] + mix, g_ref[...], b_ref[...])
    h_ref[...] = h
    h_hi = h.astype(BF16)
    hb_ref[...] = h_hi
    h_lo = (h - h_hi.astype(F32)).astype(BF16)
    both = jnp.dot(h_hi, wr_ref[...], preferred_element_type=F32)
    lg = (both[:, :LANES] + both[:, LANES:]
          + jnp.dot(h_lo, wr_ref[:, :LANES], preferred_element_type=F32)) + br_ref[...]
    comb, sel = _route(lg)
    comb_ref[...] = comb
    sel_ref[...] = sel
    ts = h.shape[0] // cnt_ref.shape[0]
    for c in range(cnt_ref.shape[0]):
        cnt = jnp.sum(sel[c * ts:(c + 1) * ts, :], axis=0, keepdims=True)
        cnt_ref[c] = jnp.broadcast_to(cnt, (SUBLANES, LANES))


def _out_proj(ys, ya, x2d, wo_a, wo_b, ln_g, ln_b, wr, br, ts):
    t = x2d.shape[0]
    tm = min(OUT_PROJ_ROWS, t)
    row = lambda i: (i, 0)
    const = lambda i: (0, 0)
    return pl.pallas_call(
        _out_proj_kernel,
        grid=(t // tm,),
        in_specs=[
            pl.BlockSpec((tm, SSD_INNER), row),
            pl.BlockSpec((tm, ATT_INNER), row),
            pl.BlockSpec((tm, D_MODEL), row),
            pl.BlockSpec((SSD_INNER, D_MODEL), const),
            pl.BlockSpec((ATT_INNER, D_MODEL), const),
            pl.BlockSpec((1, D_MODEL), const),
            pl.BlockSpec((1, D_MODEL), const),
            pl.BlockSpec((D_MODEL, 2 * LANES), const),
            pl.BlockSpec((1, LANES), const),
        ],
        out_specs=[
            pl.BlockSpec((tm, D_MODEL), row),
            pl.BlockSpec((tm, D_MODEL), row),
            pl.BlockSpec((tm, LANES), row),
            pl.BlockSpec((tm, LANES), row),
            pl.BlockSpec((tm // ts, SUBLANES, LANES), lambda i: (i, 0, 0)),
        ],
        out_shape=(
            jax.ShapeDtypeStruct((t, D_MODEL), F32),
            jax.ShapeDtypeStruct((t, D_MODEL), BF16),
            jax.ShapeDtypeStruct((t, LANES), F32),
            jax.ShapeDtypeStruct((t, LANES), F32),
            jax.ShapeDtypeStruct((t // ts, SUBLANES, LANES), F32),
        ),
        compiler_params=pltpu.CompilerParams(
            dimension_semantics=("arbitrary",), vmem_limit_bytes=VMEM_LIMIT_BYTES),
        name="out_proj_ln_router",
    )(ys, ya, x2d, wo_a, wo_b, ln_g, ln_b, wr, br)


def _dispatch_plan(cnt, ts):
    nt = cnt.shape[0]
    c16 = (cnt + (MOE_GRANULE - 1)) // MOE_GRANULE * MOE_GRANULE
    loc_off = jnp.cumsum(c16, axis=1) - c16
    tot = jnp.sum(c16, axis=0)
    tiles = (tot + (MOE_ROW_TILE - 1)) // MOE_ROW_TILE
    cum_tiles = jnp.cumsum(tiles)
    start = (cum_tiles - tiles) * MOE_ROW_TILE
    base = start[None, :] + jnp.cumsum(c16, axis=0) - c16
    max_tiles = _moe_rows(nt * ts, ts) // MOE_ROW_TILE
    tile_expert = jnp.minimum(
        jnp.sum(jnp.arange(max_tiles, dtype=jnp.int32)[:, None] >= cum_tiles[None, :], axis=1),
        N_EXPERTS - 1).astype(jnp.int32)
    loc_t = jnp.broadcast_to(loc_off.astype(F32)[:, :, None], (nt, N_EXPERTS, LANES))
    return dict(
        chunks=(c16 // MOE_GRANULE).reshape(-1).astype(jnp.int32),
        loc_off=loc_off.reshape(-1).astype(jnp.int32),
        base=base.reshape(-1).astype(jnp.int32),
        step_chunks=(jnp.sum(c16, axis=1).reshape(-1, _moe_group(nt * ts, ts)).sum(axis=1)
                     // MOE_GRANULE).astype(jnp.int32),
        pad_start=(start + tot).astype(jnp.int32),
        pad_chunks=((tiles * MOE_ROW_TILE - tot) // MOE_GRANULE).astype(jnp.int32),
        n_tiles=cum_tiles[-1:].astype(jnp.int32),
        tile_expert=tile_expert,
        loc_t=loc_t,
    )


def _moe_group(t, ts):
    return MOE_TILES_PER_STEP if t % (MOE_TILES_PER_STEP * ts) == 0 else 1


def _moe_local_rows(ts):
    rows = 2 * ts + N_EXPERTS * (MOE_GRANULE - 1)
    return (rows + LANES - 1) // LANES * LANES


def _moe_rows(t, ts):
    rows = 2 * t + (t // ts) * N_EXPERTS * (MOE_GRANULE - 1) + N_EXPERTS * MOE_ROW_TILE
    return (rows + MOE_ROW_TILE - 1) // MOE_ROW_TILE * MOE_ROW_TILE


def _chunk_copy(src, dst, sem):
    return pltpu.make_async_copy(src, dst, sem)


def _wait_rows(copy_of_rows, n_chunks):
    def wait(rows):
        def one(_, c):
            copy_of_rows(rows).wait()
            return c
        return one
    shift = MOE_WAIT_BATCH.bit_length() - 1
    lax.fori_loop(0, lax.shift_right_logical(n_chunks, shift), wait(MOE_WAIT_BATCH * MOE_GRANULE), 0)
    lax.fori_loop(0, n_chunks & (MOE_WAIT_BATCH - 1), wait(MOE_GRANULE), 0)


def _dispatch_kernel(chunks_ref, loc_ref, base_ref, schunks_ref, pstart_ref, pchunks_ref,
                     hb_ref, sel_ref, comb_ref, loct_ref, ustrict_ref,
                     xs_ref, posw_ref, buf_ref, zero_ref, sem_ref, zsem_ref):
    step = pl.program_id(0)
    n_steps = pl.num_programs(0)
    par = step % 2
    group = loct_ref.shape[0]
    ts = hb_ref.shape[0] // group
    nl = buf_ref.shape[1]
    g = MOE_GRANULE

    def wait_chunks(n, pp):
        _wait_rows(lambda r: _chunk_copy(buf_ref.at[0, pl.ds(0, r)], xs_ref.at[pl.ds(0, r)],
                                         sem_ref.at[pp]), n)

    @pl.when(step >= 2)
    def _():
        wait_chunks(schunks_ref[step - 2], par)

    lane = lax.broadcasted_iota(jnp.int32, (1, LANES), 1)
    prow = lax.broadcasted_iota(jnp.int32, (LANES, 1), 0)
    d = lax.broadcasted_iota(jnp.int32, (nl, ts), 0).astype(F32)
    experts = slice(ROUTE_E_LANE, ROUTE_E_LANE + N_EXPERTS)
    for u in range(group):
        rows = slice(u * ts, (u + 1) * ts)
        sel_t = sel_ref[rows, :].T[experts, :]
        comb_t = comb_ref[rows, :].T[experts, :]
        picked = sel_t > 0.0
        rank = jnp.dot(sel_t.astype(BF16), ustrict_ref[...], preferred_element_type=F32)
        dest = rank + jnp.concatenate([loct_ref[u]] * (ts // LANES), axis=1)
        pos_a = jnp.min(jnp.where(picked, dest, 1e9), axis=0, keepdims=True)
        pos_b = jnp.max(jnp.where(picked, dest, -1.0), axis=0, keepdims=True)
        w_a = jnp.sum(jnp.where(picked & (dest == pos_a), comb_t, 0.0), axis=0, keepdims=True)
        w_b = jnp.sum(jnp.where(picked & (dest == pos_b), comb_t, 0.0), axis=0, keepdims=True)
        pieces = [p.astype(F32) for w in (w_a, w_b) for p in _split3(w)]
        posw_t = jnp.zeros((LANES, ts), F32)
        for r, v in enumerate([pos_a, pos_b] + pieces):
            posw_t = jnp.where(prow == r, v, posw_t)
        posw = posw_t.T
        posw_ref[rows, :] = posw
        perm_a = jnp.where(d == pos_a, 1.0, 0.0).astype(BF16)
        perm_b = jnp.where(d == pos_b, 1.0, 0.0).astype(BF16)
        slot = par * group + u
        buf_ref[slot, :, :D_MODEL] = jnp.dot(perm_a + perm_b, hb_ref[rows, :],
                                             preferred_element_type=F32).astype(BF16)
        wl_a = jnp.where((lane >= 2) & (lane < 5), posw, 0.0).astype(BF16)
        wl_b = jnp.where((lane >= 5) & (lane < 8), posw, 0.0).astype(BF16)
        buf_ref[slot, :, D_MODEL:] = (
            jnp.dot(perm_a, wl_a, preferred_element_type=F32)
            + jnp.dot(perm_b, wl_b, preferred_element_type=F32)).astype(BF16)

    for u in range(group):
        for e in range(N_EXPERTS):
            idx = (step * group + u) * N_EXPERTS + e
            lo = loc_ref[idx]
            bs = base_ref[idx]

            def send(k, c, lo=lo, bs=bs, u=u):
                src = buf_ref.at[par * group + u, pl.ds(pl.multiple_of(lo + k * g, g), g)]
                dst = xs_ref.at[pl.ds(pl.multiple_of(bs + k * g, g), g)]
                _chunk_copy(src, dst, sem_ref.at[par]).start()
                return c
            lax.fori_loop(0, chunks_ref[idx], send, 0)

    @pl.when(step == n_steps - 1)
    def _():
        zero_ref[...] = jnp.zeros_like(zero_ref)
        for e in range(N_EXPERTS):
            ps = pstart_ref[e]

            def send_zero(k, c, ps=ps):
                dst = xs_ref.at[pl.ds(pl.multiple_of(ps + k * g, g), g)]
                _chunk_copy(zero_ref, dst, zsem_ref.at[0]).start()
                return c
            lax.fori_loop(0, pchunks_ref[e], send_zero, 0)
        for e in range(N_EXPERTS):
            def wait_zero(_, c):
                _chunk_copy(zero_ref, xs_ref.at[pl.ds(0, g)], zsem_ref.at[0]).wait()
                return c
            lax.fori_loop(0, pchunks_ref[e], wait_zero, 0)
        wait_chunks(schunks_ref[step], par)

        @pl.when(step >= 1)
        def _():
            wait_chunks(schunks_ref[step - 1], 1 - par)


def _dispatch(plan, hb, sel, comb, ts):
    t = hb.shape[0]
    group = _moe_group(t, ts)
    nl = _moe_local_rows(ts)
    rows = _moe_rows(t, ts)
    ustrict = jnp.tri(ts, k=-1, dtype=BF16).T
    row = lambda i, *_: (i, 0)
    grid_spec = pltpu.PrefetchScalarGridSpec(
        num_scalar_prefetch=6,
        grid=(t // (group * ts),),
        in_specs=[
            pl.BlockSpec((group * ts, D_MODEL), row),
            pl.BlockSpec((group * ts, LANES), row),
            pl.BlockSpec((group * ts, LANES), row),
            pl.BlockSpec((group, N_EXPERTS, LANES), lambda i, *_: (i, 0, 0)),
            pl.BlockSpec((ts, ts), lambda i, *_: (0, 0)),
        ],
        out_specs=[
            pl.BlockSpec(memory_space=pl.ANY),
            pl.BlockSpec((group * ts, LANES), row),
        ],
        scratch_shapes=[
            pltpu.VMEM((2 * group, nl, MOE_ROW_LANES), BF16),
            pltpu.VMEM((MOE_GRANULE, MOE_ROW_LANES), BF16),
            pltpu.SemaphoreType.DMA((2,)),
            pltpu.SemaphoreType.DMA((1,)),
        ],
    )
    return pl.pallas_call(
        _dispatch_kernel,
        grid_spec=grid_spec,
        out_shape=(
            jax.ShapeDtypeStruct((rows, MOE_ROW_LANES), BF16),
            jax.ShapeDtypeStruct((t, LANES), F32),
        ),
        compiler_params=pltpu.CompilerParams(
            dimension_semantics=("arbitrary",), vmem_limit_bytes=VMEM_LIMIT_BYTES),
        name="moe_dispatch",
    )(plan["chunks"], plan["loc_off"], plan["base"], plan["step_chunks"], plan["pad_start"],
      plan["pad_chunks"], hb, sel, comb, plan["loc_t"], ustrict)


def _expert_kernel(te_ref, nt_ref, x_ref, wg_ref, wu_ref, wd_ref, y_ref, wgb_ref, wub_ref, wdb_ref):
    j = pl.program_id(0)

    @pl.when(j < nt_ref[0])
    def _():
        @pl.when((j == 0) | (te_ref[j] != te_ref[jnp.maximum(j - 1, 0)]))
        def _():
            wgb_ref[...] = wg_ref[0].astype(BF16)
            wub_ref[...] = wu_ref[0].astype(BF16)
            wdb_ref[...] = wd_ref[0].astype(BF16)

        x = x_ref[:, :D_MODEL]
        wl = x_ref[:, D_MODEL:].astype(F32)
        w = jnp.sum(wl[:, 2:8], axis=1, keepdims=True)
        gate = jnp.dot(x, wgb_ref[...], preferred_element_type=F32)
        up = jnp.dot(x, wub_ref[...], preferred_element_type=F32)
        act = (_silu(gate) * up).astype(BF16)
        y_ref[...] = (w * jnp.dot(act, wdb_ref[...], preferred_element_type=F32)).astype(y_ref.dtype)


def _experts(plan, xs, w_gate, w_up, w_down):
    rows = xs.shape[0]
    live = lambda j, te, nt: jnp.minimum(j, nt[0] - 1)
    expert = lambda j, te, nt: (te[live(j, te, nt)], 0, 0)
    grid_spec = pltpu.PrefetchScalarGridSpec(
        num_scalar_prefetch=2,
        grid=(rows // MOE_ROW_TILE,),
        in_specs=[
            pl.BlockSpec((MOE_ROW_TILE, MOE_ROW_LANES), lambda j, te, nt: (live(j, te, nt), 0)),
            pl.BlockSpec((1, D_MODEL, D_FF_EXPERT), expert),
            pl.BlockSpec((1, D_MODEL, D_FF_EXPERT), expert),
            pl.BlockSpec((1, D_FF_EXPERT, D_MODEL), expert),
        ],
        out_specs=pl.BlockSpec((MOE_ROW_TILE, D_MODEL), lambda j, te, nt: (live(j, te, nt), 0)),
        scratch_shapes=[
            pltpu.VMEM((D_MODEL, D_FF_EXPERT), BF16),
            pltpu.VMEM((D_MODEL, D_FF_EXPERT), BF16),
            pltpu.VMEM((D_FF_EXPERT, D_MODEL), BF16),
        ],
    )
    return pl.pallas_call(
        _expert_kernel,
        grid_spec=grid_spec,
        out_shape=jax.ShapeDtypeStruct((rows, D_MODEL), BF16),
        compiler_params=pltpu.CompilerParams(
            dimension_semantics=("arbitrary",), vmem_limit_bytes=VMEM_LIMIT_BYTES),
        name="moe_experts",
    )(plan["tile_expert"], plan["n_tiles"], xs, w_gate, w_up, w_down)


def _combine_kernel(chunks_ref, loc_ref, base_ref, schunks_ref,
                    y_ref, posw_ref, h_ref, g_ref, b_ref, o_ref, buf_ref, sem_ref):
    step = pl.program_id(0)
    n_steps = pl.num_programs(0)
    par = step % 2
    nl = buf_ref.shape[1]
    group = buf_ref.shape[0] // 2
    ts = h_ref.shape[0] // group
    g = MOE_GRANULE

    def fetch(ss, pp):
        for u in range(group):
            for e in range(N_EXPERTS):
                idx = (ss * group + u) * N_EXPERTS + e
                lo = loc_ref[idx]
                bs = base_ref[idx]

                def get(k, c, lo=lo, bs=bs, u=u):
                    src = y_ref.at[pl.ds(pl.multiple_of(bs + k * g, g), g)]
                    dst = buf_ref.at[pp * group + u, pl.ds(pl.multiple_of(lo + k * g, g), g)]
                    _chunk_copy(src, dst, sem_ref.at[pp]).start()
                    return c
                lax.fori_loop(0, chunks_ref[idx], get, 0)

    @pl.when(step == 0)
    def _():
        buf_ref[...] = jnp.zeros(buf_ref.shape, BF16)
        fetch(0, 0)

    @pl.when(step + 1 < n_steps)
    def _():
        fetch(step + 1, 1 - par)

    _wait_rows(lambda r: _chunk_copy(y_ref.at[pl.ds(0, r)], buf_ref.at[0, pl.ds(0, r)],
                                     sem_ref.at[par]), schunks_ref[step])

    d = lax.broadcasted_iota(jnp.int32, (1, nl), 1).astype(F32)
    for u in range(group):
        rows = slice(u * ts, (u + 1) * ts)
        yb = buf_ref[par * group + u]
        posw = posw_ref[rows, :]
        gather = jnp.where((posw[:, 0:1] == d) | (posw[:, 1:2] == d), 1.0, 0.0).astype(BF16)
        ffn = jnp.dot(gather, yb, preferred_element_type=F32)
        o_ref[rows, :] = _layer_norm(DEEPNORM_ALPHA * h_ref[rows, :] + ffn, g_ref[...], b_ref[...])


def _combine(plan, y, posw, h, ln_g, ln_b, ts):
    t = h.shape[0]
    group = _moe_group(t, ts)
    nl = _moe_local_rows(ts)
    row = lambda i, *_: (i, 0)
    const = lambda i, *_: (0, 0)
    grid_spec = pltpu.PrefetchScalarGridSpec(
        num_scalar_prefetch=4,
        grid=(t // (group * ts),),
        in_specs=[
            pl.BlockSpec(memory_space=pl.ANY),
            pl.BlockSpec((group * ts, LANES), row),
            pl.BlockSpec((group * ts, D_MODEL), row),
            pl.BlockSpec((1, D_MODEL), const),
            pl.BlockSpec((1, D_MODEL), const),
        ],
        out_specs=pl.BlockSpec((group * ts, D_MODEL), row),
        scratch_shapes=[
            pltpu.VMEM((2 * group, nl, D_MODEL), BF16),
            pltpu.SemaphoreType.DMA((2,)),
        ],
    )
    return pl.pallas_call(
        _combine_kernel,
        grid_spec=grid_spec,
        out_shape=jax.ShapeDtypeStruct((t, D_MODEL), F32),
        compiler_params=pltpu.CompilerParams(
            dimension_semantics=("arbitrary",), vmem_limit_bytes=VMEM_LIMIT_BYTES),
        name="moe_combine_ln",
    )(plan["chunks"], plan["loc_off"], plan["base"], plan["step_chunks"], y, posw, h, ln_g, ln_b)


def _pad_lanes(a, n=LANES):
    return jnp.pad(a, ((0, 0), (0, n - a.shape[1])))


def _layer(h2d, batch, seq, w_in, b_in, conv_w, conv_b, a_log, d_skip, ssd_norm_g, w_out,
           ln1_g, ln1_b, rg_w, rg_b, re_w, re_b, w_gate, w_up, w_down, ln2_g, ln2_b):
    o_dt = SSD_INNER + SSD_CONV_DIM
    o_q = o_dt + SSD_HEADS
    o_f = o_q + 3 * ATT_INNER
    w_main = jnp.concatenate([w_in[:, :o_dt], w_in[:, o_q:o_f]], axis=1).astype(BF16)
    b_main = jnp.concatenate([b_in[:o_dt], b_in[o_q:o_f]])[None, :]
    w_dt, w_f = w_in[:, o_dt:o_q], w_in[:, o_f:]
    w_small = _pad_lanes(jnp.concatenate([w_dt, w_f, w_dt], axis=1)).astype(BF16)
    b_small = _pad_lanes(jnp.concatenate([b_in[o_dt:o_q], b_in[o_f:], b_in[o_dt:o_q]])[None, :])
    a_neg = -jnp.exp(a_log.astype(F32))
    a_row = _pad_lanes(a_neg[None, :])
    a_exp = jnp.repeat(a_neg, SSD_HEAD_DIM)[None, :]
    dskip_exp = jnp.repeat(d_skip.astype(F32), SSD_HEAD_DIM)[None, :]

    z, xbc, q_t, k, v_t, gate, qf_t, kf = _in_proj(h2d, w_main, b_main, w_small, b_small, a_row, seq)
    y_ssd = _ssd(xbc, z, gate, conv_w, conv_b[None, :], a_exp, dskip_exp, ssd_norm_g[None, :],
                 batch, seq)
    y_att = _attention(q_t, k, v_t, qf_t, kf, batch, seq)

    gap = ROUTE_E_LANE - N_EXPERT_GROUPS
    w_r = _pad_lanes(jnp.concatenate(
        [rg_w, jnp.zeros((D_MODEL, gap), F32),
         jnp.transpose(re_w, (1, 0, 2)).reshape(D_MODEL, N_EXPERTS)], axis=1))
    b_r = _pad_lanes(jnp.concatenate(
        [rg_b, jnp.zeros((gap,), F32), re_b.reshape(N_EXPERTS)])[None, :])
    wr_hi = w_r.astype(BF16)
    wr_lo = (w_r - wr_hi.astype(F32)).astype(BF16)
    wo = w_out.astype(BF16)
    ts = min(MOE_SORT_TILE, h2d.shape[0])
    h1, h1b, comb, sel, cnt = _out_proj(y_ssd, y_att, h2d, wo[:SSD_INNER], wo[SSD_INNER:],
                                        ln1_g[None, :], ln1_b[None, :],
                                        jnp.concatenate([wr_hi, wr_lo], axis=1), b_r, ts)

    plan = _dispatch_plan(
        cnt[:, 0, ROUTE_E_LANE:ROUTE_E_LANE + N_EXPERTS].astype(jnp.int32), ts)
    xs, posw = _dispatch(plan, h1b, sel, comb, ts)
    y = _experts(plan, xs, w_gate, w_up, w_down)
    return _combine(plan, y, posw, h1, ln2_g[None, :], ln2_b[None, :], ts)


def kernel(x, w_in, b_in, conv_w, conv_b, a_log, d_skip, ssd_norm_g, w_out, ln1_g, ln1_b,
           router_group_w, router_group_b, router_expert_w, router_expert_b, w_gate, w_up,
           w_down, ln2_g, ln2_b):
    batch, seq, d = x.shape
    h = x.reshape(batch * seq, d)
    for l in range(w_in.shape[0]):
        h = _layer(h, batch, seq, w_in[l], b_in[l], conv_w[l], conv_b[l], a_log[l], d_skip[l],
                   ssd_norm_g[l], w_out[l], ln1_g[l], ln1_b[l], router_group_w[l],
                   router_group_b[l], router_expert_w[l], router_expert_b[l], w_gate[l],
                   w_up[l], w_down[l], ln2_g[l], ln2_b[l])
    return h.reshape(batch, seq, d)
```

```python
import functools

import jax
import jax.numpy as jnp
import numpy as np
from jax import lax
from jax.experimental import pallas as pl
from jax.experimental.pallas import tpu as pltpu

F32 = jnp.float32
BF16 = jnp.bfloat16

D_MODEL = 1024
SSD_HEADS = 8
SSD_HEAD_DIM = 64
SSD_INNER = SSD_HEADS * SSD_HEAD_DIM
SSD_GROUPS = 2
SSD_STATE = 128
CONV_WIDTH = 4
SSD_CONV_DIM = SSD_INNER + 2 * SSD_GROUPS * SSD_STATE
ATT_HEADS = 8
ATT_HEAD_DIM = 64
ATT_INNER = ATT_HEADS * ATT_HEAD_DIM
N_EXPERT_GROUPS = 4
EXPERTS_PER_GROUP = 4
N_EXPERTS = N_EXPERT_GROUPS * EXPERTS_PER_GROUP
D_FF_EXPERT = 512
DEPTH = 1
DEEPNORM_ALPHA = (2.0 * DEPTH) ** 0.25
LN_EPS = 1e-5
RMS_EPS = 1e-5

LANES = 128
SUBLANES = 8
VMEM_LIMIT_BYTES = 56 * 1024 * 1024

GATE_A_LANE = 0
GATE_F_LANE = SSD_HEADS
GATE_DT_LANE = 2 * SSD_HEADS
ROUTE_E_LANE = SUBLANES
MOE_SORT_TILE = 256
MOE_GRANULE = 16
MOE_ROW_TILE = 1024
MOE_TILES_PER_STEP = 4
MOE_ROW_LANES = D_MODEL + LANES
MOE_WAIT_BATCH = 8

IN_PROJ_ROWS = 1024
OUT_PROJ_ROWS = 1024
CUMSUM_BLOCK = 256
FB_LANES = 6
ATT_BLOCK = 256
ATT_PAIRS_PER_STEP = 4
ATT_BATCHES_PER_STEP = 2

SSD_ROWS = 256
SSD_STEP_ROWS = 512
SSD_BATCHES_PER_STEP = 2
SSD_CHUNK = 64
NEG_BIG = -1e30
LOG2E = 1.4426950408889634


def _split3(v):
    hi = v.astype(BF16)
    r1 = v - hi.astype(F32)
    mid = r1.astype(BF16)
    lo = (r1 - mid.astype(F32)).astype(BF16)
    return hi, mid, lo


def _silu(v):
    half = 0.5 * v
    return half * (1.0 + jnp.tanh(half))


def _softplus(v):
    return jnp.maximum(v, 0.0) + jnp.log1p(jnp.exp(-jnp.abs(v)))


def _in_proj_kernel(x_ref, wm_ref, bm_ref, ws_ref, bs_ref, arow_ref, place_ref, ones_ref,
                    z_ref, xbc_ref, qt_ref, k_ref, vt_ref, gate_ref, qft_ref, kf_ref,
                    carry_ref, *, tiles_per_seq, cb):
    i = pl.program_id(0)
    tm = x_ref.shape[0]
    tb = qt_ref.shape[2]
    xb = x_ref[...].astype(BF16)

    def put_transposed(ref, val):
        for c in range(tm // tb):
            ref[c] = val[c * tb:(c + 1) * tb, :].T.astype(ref.dtype)

    s = jnp.dot(xb, ws_ref[...], preferred_element_type=F32) + bs_ref[...]
    lane = lax.broadcasted_iota(jnp.int32, (1, LANES), 1)
    sp = _softplus(s)
    log_f = -_softplus(-s)
    val = jnp.where(lane < GATE_F_LANE, sp * arow_ref[...],
                    jnp.where(lane < GATE_DT_LANE, log_f,
                              jnp.where(lane < GATE_DT_LANE + SSD_HEADS, sp, 0.0)))
    carry = jnp.where(i % tiles_per_seq == 0, 0.0, carry_ref[...])
    row = lax.broadcasted_iota(jnp.int32, (cb, 1), 0)
    sums = []
    for c in range(tm // cb):
        part = val[c * cb:(c + 1) * cb, :]
        shift = 1
        while shift < cb:
            part = part + jnp.where(row >= shift, pltpu.roll(part, shift, 0), 0.0)
            shift *= 2
        part = part + carry
        carry = part[cb - 1:cb, :]
        sums.append(part)
    cs = jnp.concatenate(sums, axis=0)
    carry_ref[...] = carry
    out = jnp.where(lane < GATE_DT_LANE, cs, val)
    gate_ref[...] = out
    pieces = jnp.concatenate(_split3(cs * LOG2E), axis=1)
    fb = jnp.dot(pieces, place_ref[...], preferred_element_type=F32) + ones_ref[...]
    put_transposed(qft_ref, fb[:, :LANES])
    kf_ref[...] = fb[:, LANES:].astype(BF16)

    col = 0
    for ref, scale in ((z_ref, None), (xbc_ref, None), (qt_ref, ATT_HEAD_DIM ** -0.5 * LOG2E),
                       (k_ref, None), (vt_ref, None)):
        transposed = ref is qt_ref or ref is vt_ref
        n = ATT_INNER if transposed else ref.shape[1]
        acc = jnp.dot(xb, wm_ref[:, col:col + n], preferred_element_type=F32) + bm_ref[:, col:col + n]
        if scale is not None:
            acc = acc * scale
        if transposed:
            put_transposed(ref, acc)
        else:
            ref[...] = acc.astype(ref.dtype)
        col += n


def _forget_bias_placement():
    place = np.zeros((3 * LANES, 2 * LANES), np.float32)
    ones = np.zeros((1, 2 * LANES), np.float32)
    for h in range(ATT_HEADS):
        for piece in range(3):
            src = piece * LANES + GATE_F_LANE + h
            place[src, FB_LANES * h + piece] = 1.0
            place[src, LANES + FB_LANES * h + 3 + piece] = -1.0
            ones[0, FB_LANES * h + 3 + piece] = 1.0
            ones[0, LANES + FB_LANES * h + piece] = 1.0
    return jnp.asarray(place, BF16), jnp.asarray(ones, F32)


def _in_proj(x2d, w_main, b_main, w_small, b_small, a_row, seq):
    t = x2d.shape[0]
    tm = min(IN_PROJ_ROWS, seq)
    tk = min(ATT_BLOCK, seq)
    place, ones = _forget_bias_placement()
    n_main = w_main.shape[1]
    row = lambda i: (i, 0)
    const = lambda i: (0, 0)
    outs = (
        jax.ShapeDtypeStruct((t, SSD_INNER), BF16),
        jax.ShapeDtypeStruct((t, SSD_CONV_DIM), BF16),
        jax.ShapeDtypeStruct((t // tk, ATT_INNER, tk), BF16),
        jax.ShapeDtypeStruct((t, ATT_INNER), BF16),
        jax.ShapeDtypeStruct((t // tk, ATT_INNER, tk), BF16),
        jax.ShapeDtypeStruct((t, LANES), F32),
        jax.ShapeDtypeStruct((t // tk, LANES, tk), BF16),
        jax.ShapeDtypeStruct((t, LANES), BF16),
    )
    return pl.pallas_call(
        functools.partial(_in_proj_kernel, tiles_per_seq=seq // tm, cb=min(CUMSUM_BLOCK, tm)),
        grid=(t // tm,),
        in_specs=[
            pl.BlockSpec((tm, D_MODEL), row),
            pl.BlockSpec((D_MODEL, n_main), const),
            pl.BlockSpec((1, n_main), const),
            pl.BlockSpec((D_MODEL, LANES), const),
            pl.BlockSpec((1, LANES), const),
            pl.BlockSpec((1, LANES), const),
            pl.BlockSpec((3 * LANES, 2 * LANES), const),
            pl.BlockSpec((1, 2 * LANES), const),
        ],
        out_specs=[
            pl.BlockSpec((tm, SSD_INNER), row),
            pl.BlockSpec((tm, SSD_CONV_DIM), row),
            pl.BlockSpec((tm // tk, ATT_INNER, tk), lambda i: (i, 0, 0)),
            pl.BlockSpec((tm, ATT_INNER), row),
            pl.BlockSpec((tm // tk, ATT_INNER, tk), lambda i: (i, 0, 0)),
            pl.BlockSpec((tm, LANES), row),
            pl.BlockSpec((tm // tk, LANES, tk), lambda i: (i, 0, 0)),
            pl.BlockSpec((tm, LANES), row),
        ],
        out_shape=outs,
        scratch_shapes=[pltpu.VMEM((1, LANES), F32)],
        compiler_params=pltpu.CompilerParams(
            dimension_semantics=("arbitrary",), vmem_limit_bytes=VMEM_LIMIT_BYTES),
        name="in_proj",
    )(x2d, w_main, b_main, w_small, b_small, a_row, place, ones)


def _expand_heads(gate, lane0, rows):
    lane = lax.broadcasted_iota(jnp.int32, (1, LANES), 1)
    pieces = []
    for p in range(SSD_HEADS // 2):
        a = jnp.broadcast_to(gate[:, lane0 + 2 * p:lane0 + 2 * p + 1], (rows, LANES))
        b = jnp.broadcast_to(gate[:, lane0 + 2 * p + 1:lane0 + 2 * p + 2], (rows, LANES))
        pieces.append(jnp.where(lane < SSD_HEAD_DIM, a, b))
    return jnp.concatenate(pieces, axis=1)


def _ssd_kernel(xbc_ref, z_ref, gate_ref, cw_ref, cb_ref, aexp_ref, dskip_ref, ng_ref, shift_ref,
                y_ref, ext_ref, state_ref):
    j = pl.program_id(1)
    rows = shift_ref.shape[1]
    q = SSD_CHUNK
    gw = SSD_INNER // SSD_GROUPS
    hpg = SSD_HEADS // SSD_GROUPS
    pad = SUBLANES

    @pl.when(j == 0)
    def _():
        ext_ref[:, 0:pad, :] = jnp.zeros((ext_ref.shape[0], pad, SSD_CONV_DIM), F32)
        state_ref[...] = jnp.zeros_like(state_ref)

    r_i = lax.broadcasted_iota(jnp.int32, (q, gw), 0)
    c_i = lax.broadcasted_iota(jnp.int32, (q, gw), 1)
    s_i = c_i % q
    eye_t = s_i == r_i
    tril_t = s_i <= r_i
    rb = lax.broadcasted_iota(jnp.int32, (hpg * q, gw), 0) // q
    cbk = lax.broadcasted_iota(jnp.int32, (hpg * q, gw), 1) // SSD_HEAD_DIM
    blk = rb == cbk

    for t, bb in ((t, bb) for t in range(xbc_ref.shape[1] // rows) for bb in range(xbc_ref.shape[0])):
        sub = slice(t * rows, (t + 1) * rows)
        xb = xbc_ref[bb, sub, :]
        xf = xb.astype(F32)
        conv = cb_ref[...] + xf * cw_ref[CONV_WIDTH - 1:CONV_WIDTH, :]
        for s in range(1, CONV_WIDTH):
            shifted = jnp.dot(shift_ref[s - 1], xb, preferred_element_type=F32)
            conv = conv + shifted * cw_ref[CONV_WIDTH - 1 - s:CONV_WIDTH - s, :]
        ext_ref[bb, pad:2 * pad, :] = xf[0:pad, :]
        head = cb_ref[...]
        for kk in range(CONV_WIDTH):
            off = pad - (CONV_WIDTH - 1) + kk
            head = head + ext_ref[bb, off:off + pad, :] * cw_ref[kk:kk + 1, :]
        ext_ref[bb, 0:pad, :] = xf[rows - pad:rows, :]
        u = _silu(jnp.concatenate([head, conv[pad:, :]], axis=0))
        xs = u[:, :SSD_INNER]
        b_all = u[:, SSD_INNER:SSD_INNER + SSD_GROUPS * SSD_STATE].astype(BF16)
        c_all = u[:, SSD_INNER + SSD_GROUPS * SSD_STATE:].astype(BF16)

        gate = gate_ref[bb, sub, :]
        dt_e = _expand_heads(gate, GATE_DT_LANE, rows)
        a_e = _expand_heads(gate, GATE_A_LANE, rows)
        xdt = xs * dt_e
        da_e = dt_e * aexp_ref[...]

        ys = []
        for c in range(rows // q):
            r0 = c * q
            a_c = a_e[r0:r0 + q, :]
            a_prev = a_c[0:1, :] - da_e[r0:r0 + 1, :]
            a_end = a_c[q - 1:q, :]
            y_groups = []
            for g in range(SSD_GROUPS):
                l0 = g * gw
                a_cg = a_c[:, l0:l0 + gw]
                diag = jnp.sum(jnp.where(eye_t, a_cg, 0.0), axis=0, keepdims=True)
                seg = jnp.minimum(a_cg - diag, 0.0)
                lcat = jnp.where(tril_t, jnp.exp(seg), 0.0)
                bg = b_all[r0:r0 + q, g * SSD_STATE:(g + 1) * SSD_STATE]
                cg = c_all[r0:r0 + q, g * SSD_STATE:(g + 1) * SSD_STATE]
                b_tile = jnp.concatenate([bg] * hpg, axis=0)
                gcat = lax.dot_general(cg, b_tile, (((1,), (1,)), ((), ())),
                                       preferred_element_type=F32)
                xg = xdt[r0:r0 + q, l0:l0 + gw]
                xg_b = xg.astype(BF16)
                xbd = jnp.where(blk, jnp.concatenate([xg_b] * hpg, axis=0), jnp.zeros((), BF16))
                y_diag = jnp.dot((gcat * lcat).astype(BF16), xbd, preferred_element_type=F32)
                st = state_ref[bb, g]
                y_off = jnp.dot(cg, st.astype(BF16), preferred_element_type=F32) \
                    * jnp.exp(a_cg - a_prev[:, l0:l0 + gw])
                dte = jnp.exp(a_end[:, l0:l0 + gw] - a_cg)
                upd = lax.dot_general(bg, (xg * dte).astype(BF16), (((0,), (0,)), ((), ())),
                                      preferred_element_type=F32)
                state_ref[bb, g] = st * jnp.exp(a_end[:, l0:l0 + gw] - a_prev[:, l0:l0 + gw]) + upd
                y_groups.append(y_diag + y_off)
            ys.append(jnp.concatenate(y_groups, axis=1))
        y = jnp.concatenate(ys, axis=0) + xs * dskip_ref[...]
        y = y * _silu(z_ref[bb, sub, :].astype(F32))
        outs = []
        for g in range(SSD_GROUPS):
            yg = y[:, g * gw:(g + 1) * gw]
            ms = jnp.mean(yg * yg, axis=1, keepdims=True)
            outs.append(yg * lax.rsqrt(ms + RMS_EPS))
        y_ref[bb, sub, :] = (jnp.concatenate(outs, axis=1) * ng_ref[...]).astype(y_ref.dtype)


def _ssd(xbc, z, gate, conv_w, conv_b, a_exp, dskip_exp, norm_g, batch, seq):
    t = xbc.shape[0]
    rows = min(SSD_ROWS, seq)
    step_rows = min(SSD_STEP_ROWS, seq)
    nj = seq // step_rows
    nbat = SSD_BATCHES_PER_STEP if batch % SSD_BATCHES_PER_STEP == 0 else 1
    row = lambda b, j: (b, j, 0)
    const = lambda b, j: (0, 0)
    shifts = jnp.stack([jnp.eye(rows, k=-s, dtype=BF16) for s in range(1, CONV_WIDTH)])
    per_batch = lambda a: a.reshape(batch, seq, a.shape[1])
    y = pl.pallas_call(
        _ssd_kernel,
        grid=(batch // nbat, nj),
        in_specs=[
            pl.BlockSpec((nbat, step_rows, SSD_CONV_DIM), row),
            pl.BlockSpec((nbat, step_rows, SSD_INNER), row),
            pl.BlockSpec((nbat, step_rows, LANES), row),
            pl.BlockSpec((CONV_WIDTH, SSD_CONV_DIM), const),
            pl.BlockSpec((1, SSD_CONV_DIM), const),
            pl.BlockSpec((1, SSD_INNER), const),
            pl.BlockSpec((1, SSD_INNER), const),
            pl.BlockSpec((1, SSD_INNER), const),
            pl.BlockSpec((CONV_WIDTH - 1, rows, rows), lambda b, j: (0, 0, 0)),
        ],
        out_specs=pl.BlockSpec((nbat, step_rows, SSD_INNER), row),
        out_shape=jax.ShapeDtypeStruct((batch, seq, SSD_INNER), BF16),
        scratch_shapes=[
            pltpu.VMEM((nbat, 2 * SUBLANES, SSD_CONV_DIM), F32),
            pltpu.VMEM((nbat, SSD_GROUPS, SSD_STATE, SSD_INNER // SSD_GROUPS), F32),
        ],
        compiler_params=pltpu.CompilerParams(
            dimension_semantics=("arbitrary", "arbitrary"), vmem_limit_bytes=VMEM_LIMIT_BYTES),
        name="ssd",
    )(per_batch(xbc), per_batch(z), per_batch(gate), conv_w, conv_b, a_exp, dskip_exp, norm_g, shifts)
    return y.reshape(t, SSD_INNER)


def _attn_kernel(qt_ref, k_ref, vt_ref, qft_ref, kf_ref, o_ref, *scratch, tb, pairs, batches):
    pair0 = pl.program_id(1) * pairs
    nb = qt_ref.shape[0] // batches
    seq = nb * tb
    chains = [(bb, pp) for bb in range(batches) for pp in range(pairs)]
    s_refs, acc_refs = scratch[:len(chains)], scratch[len(chains):]
    feat = lax.broadcasted_iota(jnp.int32, (LANES, 1), 0)
    in_head = (feat < ATT_HEAD_DIM, feat >= ATT_HEAD_DIM)
    ones = jnp.ones((2 * SUBLANES, tb), BF16)
    key = lax.broadcasted_iota(jnp.int32, (tb, 2 * tb), 0)
    qry = lax.broadcasted_iota(jnp.int32, (tb, 2 * tb), 1)
    causal = jnp.where(qry >= tb, qry - tb, qry) >= key
    zero = jnp.zeros((), BF16)

    def logits_t(qa_t, c, jk):
        bb, pp = chains[c]
        k0 = pl.multiple_of(bb * seq + jk * tb, tb)
        ka = jnp.concatenate([k_ref[pl.ds(k0, tb), pp * LANES:(pp + 1) * LANES],
                              kf_ref[pl.ds(k0, tb), :]], axis=1)
        return jnp.dot(ka, qa_t, preferred_element_type=F32)

    def update(s_t, m, c, jk, par):
        bb, pp = chains[c]
        m_new = jnp.maximum(m, jnp.max(s_t, axis=0, keepdims=True))
        alpha = jnp.exp2(m - m_new)
        p_t = jnp.exp2(s_t - m_new).astype(BF16)
        for hh in range(2):
            r0 = pp * LANES + hh * ATT_HEAD_DIM
            va_t = jnp.concatenate([vt_ref[bb * nb + jk, r0:r0 + ATT_HEAD_DIM, :], ones], axis=0)
            cols = slice(hh * tb, (hh + 1) * tb)
            acc_refs[c][par, hh] = alpha[:, cols] * acc_refs[c][par, hh] + jnp.dot(
                va_t, p_t[:, cols], preferred_element_type=F32)
        return m_new

    def build_queries(i):
        qas = []
        for bb, pp in chains:
            qf_t = qft_ref[bb * nb + i]
            q_t = qt_ref[bb * nb + i, pp * LANES:(pp + 1) * LANES, :]
            cols = []
            for hh in range(2):
                fb0 = FB_LANES * (2 * (pair0 + pp) + hh)
                in_fb = (feat >= fb0) & (feat < fb0 + FB_LANES)
                cols.append(jnp.concatenate(
                    [jnp.where(in_head[hh], q_t, zero), jnp.where(in_fb, qf_t, zero)], axis=0))
            qas.append(jnp.concatenate(cols, axis=1))
        return qas

    def first_logits(i):
        qas = build_queries(i)
        for c in range(len(chains)):
            s_refs[c][2] = logits_t(qas[c], c, 0)

    def q_block(i, _):
        par = i % 2
        qas = build_queries(i)
        for c in range(len(chains)):
            acc_refs[c][par] = jnp.zeros(acc_refs[c].shape[1:], F32)

        def step(jk, ms, rd, wr):
            new = []
            for c in range(len(chains)):
                s_t = s_refs[c][rd]
                s_refs[c][wr] = logits_t(qas[c], c, jk + 1)
                new.append(update(s_t, ms[c], c, jk, par))
            return tuple(new)

        def two_steps(t2, ms):
            return step(2 * t2 + 2, step(2 * t2 + 1, ms, 1, 0), 0, 1)

        ms = tuple(jnp.full((1, 2 * tb), NEG_BIG, F32) for _ in chains)
        ms = lax.cond(i >= 1, lambda v: step(0, v, 2, 1), lambda v: v, ms)
        ms = lax.fori_loop(0, lax.shift_right_logical(jnp.maximum(i - 1, 0), 1), two_steps, ms)
        ms = lax.cond((i >= 2) & (i % 2 == 0), lambda v: step(i - 1, v, 1, 0), lambda v: v, ms)

        last = jnp.where(i >= 1, i % 2, 2)
        diag = [jnp.where(causal, s_refs[c][last], NEG_BIG) for c in range(len(chains))]
        first_logits(jnp.minimum(i + 1, nb - 1))
        finish(jnp.maximum(i - 1, 0), 1 - par)
        for c in range(len(chains)):
            update(diag[c], ms[c], c, i, par)
        return 0

    def finish(iq, par):
        q0 = pl.multiple_of(iq * tb, tb)
        for c, (bb, pp) in enumerate(chains):
            heads = []
            for hh in range(2):
                acc_t = acc_refs[c][par, hh]
                inv = 1.0 / acc_t[ATT_HEAD_DIM:ATT_HEAD_DIM + 1, :]
                heads.append(acc_t[:ATT_HEAD_DIM, :] * inv)
            out_t = jnp.concatenate(heads, axis=0)
            o_ref[pl.ds(bb * seq + q0, tb), pp * LANES:(pp + 1) * LANES] = out_t.T.astype(o_ref.dtype)

    for c in range(len(chains)):
        acc_refs[c][1] = jnp.ones(acc_refs[c].shape[1:], F32)
    first_logits(0)
    lax.fori_loop(0, nb, q_block, 0)
    finish(nb - 1, (nb - 1) % 2)


def _attention(q_t, k, v_t, qf_t, kf, batch, seq):
    t = k.shape[0]
    tb = min(ATT_BLOCK, seq)
    nb = seq // tb
    pairs = ATT_PAIRS_PER_STEP
    batches = ATT_BATCHES_PER_STEP if batch % ATT_BATCHES_PER_STEP == 0 else 1
    width = pairs * LANES
    return pl.pallas_call(
        functools.partial(_attn_kernel, tb=tb, pairs=pairs, batches=batches),
        grid=(batch // batches, ATT_HEADS // (2 * pairs)),
        in_specs=[
            pl.BlockSpec((batches * nb, width, tb), lambda b, p: (b, p, 0)),
            pl.BlockSpec((batches * seq, width), lambda b, p: (b, p)),
            pl.BlockSpec((batches * nb, width, tb), lambda b, p: (b, p, 0)),
            pl.BlockSpec((batches * nb, LANES, tb), lambda b, p: (b, 0, 0)),
            pl.BlockSpec((batches * seq, LANES), lambda b, p: (b, 0)),
        ],
        out_specs=pl.BlockSpec((batches * seq, width), lambda b, p: (b, p)),
        out_shape=jax.ShapeDtypeStruct((t, ATT_INNER), BF16),
        scratch_shapes=([pltpu.VMEM((3, tb, 2 * tb), F32) for _ in range(batches * pairs)]
                        + [pltpu.VMEM((2, 2, ATT_HEAD_DIM + 2 * SUBLANES, tb), F32)
                           for _ in range(batches * pairs)]),
        compiler_params=pltpu.CompilerParams(
            dimension_semantics=("arbitrary", "arbitrary"), vmem_limit_bytes=VMEM_LIMIT_BYTES),
        name="fox_attention",
    )(q_t, k, v_t, qf_t, kf)


def _layer_norm(u, g, b):
    mu = jnp.mean(u, axis=1, keepdims=True)
    d = u - mu
    var = jnp.mean(d * d, axis=1, keepdims=True)
    return d * lax.rsqrt(var + LN_EPS) * g + b


def _route(lg):
    rows = lg.shape[0]
    lt = lg.T
    gl = lt[0:N_EXPERT_GROUPS, :]
    g_row = lax.broadcasted_iota(jnp.int32, (N_EXPERT_GROUPS, 1), 0).astype(F32)
    gmax = jnp.max(gl, axis=0, keepdims=True)
    gsum = jnp.sum(jnp.exp(gl - gmax), axis=0, keepdims=True)
    g_val = 1.0 / gsum
    g_idx = jnp.min(jnp.where(gl == gmax, g_row, float(N_EXPERT_GROUPS)), axis=0, keepdims=True)
    el = lt[ROUTE_E_LANE:ROUTE_E_LANE + N_EXPERTS, :]
    e_row = lax.broadcasted_iota(jnp.int32, (N_EXPERTS, 1), 0)
    e_grp = (e_row // EXPERTS_PER_GROUP).astype(F32)
    e_row = e_row.astype(F32)
    in_grp = e_grp == g_idx
    m1 = jnp.max(jnp.where(in_grp, el, NEG_BIG), axis=0, keepdims=True)
    i1 = jnp.min(jnp.where(in_grp & (el == m1), e_row, float(N_EXPERTS)), axis=0, keepdims=True)
    rest = in_grp & (e_row != i1)
    m2 = jnp.max(jnp.where(rest, el, NEG_BIG), axis=0, keepdims=True)
    i2 = jnp.min(jnp.where(rest & (el == m2), e_row, float(N_EXPERTS)), axis=0, keepdims=True)
    r = jnp.exp(m2 - m1)
    w1 = 1.0 / (1.0 + r)
    w2 = r * w1
    comb_e = jnp.where(e_row == i1, g_val * w1, jnp.where(e_row == i2, g_val * w2, 0.0))
    sel_e = jnp.where((e_row == i1) | (e_row == i2), 1.0, 0.0)
    top = jnp.zeros((ROUTE_E_LANE, rows), F32)
    bottom = jnp.zeros((LANES - ROUTE_E_LANE - N_EXPERTS, rows), F32)
    comb = jnp.concatenate([top, comb_e, bottom], axis=0).T
    sel = jnp.concatenate([top, sel_e, bottom], axis=0).T
    return comb, sel


def _out_proj_kernel(ys_ref, ya_ref, x_ref, wa_ref, wb_ref, g_ref, b_ref, wr_ref, br_ref,
                     h_ref, hb_ref, comb_ref, sel_ref, cnt_ref):
    mix = jnp.dot(ys_ref[...], wa_ref[...], preferred_element_type=F32)
    mix = mix + jnp.dot(ya_ref[...], wb_ref[...], preferred_element_type=F32)
    h = _layer_norm(DEEPNORM_ALPHA * x_ref[...] + mix, g_ref[...], b_ref[...])
    h_ref[...] = h
    h_hi = h.astype(BF16)
    hb_ref[...] = h_hi
    h_lo = (h - h_hi.astype(F32)).astype(BF16)
    both = jnp.dot(h_hi, wr_ref[...], preferred_element_type=F32)
    lg = (both[:, :LANES] + both[:, LANES:]
          + jnp.dot(h_lo, wr_ref[:, :LANES], preferred_element_type=F32)) + br_ref[...]
    comb, sel = _route(lg)
    comb_ref[...] = comb
    sel_ref[...] = sel
    ts = h.shape[0] // cnt_ref.shape[0]
    for c in range(cnt_ref.shape[0]):
        cnt = jnp.sum(sel[c * ts:(c + 1) * ts, :], axis=0, keepdims=True)
        cnt_ref[c] = jnp.broadcast_to(cnt, (SUBLANES, LANES))


def _out_proj(ys, ya, x2d, wo_a, wo_b, ln_g, ln_b, wr, br, ts):
    t = x2d.shape[0]
    tm = min(OUT_PROJ_ROWS, t)
    row = lambda i: (i, 0)
    const = lambda i: (0, 0)
    return pl.pallas_call(
        _out_proj_kernel,
        grid=(t // tm,),
        in_specs=[
            pl.BlockSpec((tm, SSD_INNER), row),
            pl.BlockSpec((tm, ATT_INNER), row),
            pl.BlockSpec((tm, D_MODEL), row),
            pl.BlockSpec((SSD_INNER, D_MODEL), const),
            pl.BlockSpec((ATT_INNER, D_MODEL), const),
            pl.BlockSpec((1, D_MODEL), const),
            pl.BlockSpec((1, D_MODEL), const),
            pl.BlockSpec((D_MODEL, 2 * LANES), const),
            pl.BlockSpec((1, LANES), const),
        ],
        out_specs=[
            pl.BlockSpec((tm, D_MODEL), row),
            pl.BlockSpec((tm, D_MODEL), row),
            pl.BlockSpec((tm, LANES), row),
            pl.BlockSpec((tm, LANES), row),
            pl.BlockSpec((tm // ts, SUBLANES, LANES), lambda i: (i, 0, 0)),
        ],
        out_shape=(
            jax.ShapeDtypeStruct((t, D_MODEL), F32),
            jax.ShapeDtypeStruct((t, D_MODEL), BF16),
            jax.ShapeDtypeStruct((t, LANES), F32),
            jax.ShapeDtypeStruct((t, LANES), F32),
            jax.ShapeDtypeStruct((t // ts, SUBLANES, LANES), F32),
        ),
        compiler_params=pltpu.CompilerParams(
            dimension_semantics=("arbitrary",), vmem_limit_bytes=VMEM_LIMIT_BYTES),
        name="out_proj_ln_router",
    )(ys, ya, x2d, wo_a, wo_b, ln_g, ln_b, wr, br)


def _dispatch_plan(cnt, ts):
    nt = cnt.shape[0]
    c16 = (cnt + (MOE_GRANULE - 1)) // MOE_GRANULE * MOE_GRANULE
    loc_off = jnp.cumsum(c16, axis=1) - c16
    tot = jnp.sum(c16, axis=0)
    tiles = (tot + (MOE_ROW_TILE - 1)) // MOE_ROW_TILE
    cum_tiles = jnp.cumsum(tiles)
    start = (cum_tiles - tiles) * MOE_ROW_TILE
    base = start[None, :] + jnp.cumsum(c16, axis=0) - c16
    max_tiles = _moe_rows(nt * ts, ts) // MOE_ROW_TILE
    tile_expert = jnp.minimum(
        jnp.sum(jnp.arange(max_tiles, dtype=jnp.int32)[:, None] >= cum_tiles[None, :], axis=1),
        N_EXPERTS - 1).astype(jnp.int32)
    loc_t = jnp.broadcast_to(loc_off.astype(F32)[:, :, None], (nt, N_EXPERTS, LANES))
    return dict(
        chunks=(c16 // MOE_GRANULE).reshape(-1).astype(jnp.int32),
        loc_off=loc_off.reshape(-1).astype(jnp.int32),
        base=base.reshape(-1).astype(jnp.int32),
        step_chunks=(jnp.sum(c16, axis=1).reshape(-1, _moe_group(nt * ts, ts)).sum(axis=1)
                     // MOE_GRANULE).astype(jnp.int32),
        pad_start=(start + tot).astype(jnp.int32),
        pad_chunks=((tiles * MOE_ROW_TILE - tot) // MOE_GRANULE).astype(jnp.int32),
        n_tiles=cum_tiles[-1:].astype(jnp.int32),
        tile_expert=tile_expert,
        loc_t=loc_t,
    )


def _moe_group(t, ts):
    return MOE_TILES_PER_STEP if t % (MOE_TILES_PER_STEP * ts) == 0 else 1


def _moe_local_rows(ts):
    rows = 2 * ts + N_EXPERTS * (MOE_GRANULE - 1)
    return (rows + LANES - 1) // LANES * LANES


def _moe_rows(t, ts):
    rows = 2 * t + (t // ts) * N_EXPERTS * (MOE_GRANULE - 1) + N_EXPERTS * MOE_ROW_TILE
    return (rows + MOE_ROW_TILE - 1) // MOE_ROW_TILE * MOE_ROW_TILE


def _chunk_copy(src, dst, sem):
    return pltpu.make_async_copy(src, dst, sem)


def _wait_rows(copy_of_rows, n_chunks):
    def wait(rows):
        def one(_, c):
            copy_of_rows(rows).wait()
            return c
        return one
    shift = MOE_WAIT_BATCH.bit_length() - 1
    lax.fori_loop(0, lax.shift_right_logical(n_chunks, shift), wait(MOE_WAIT_BATCH * MOE_GRANULE), 0)
    lax.fori_loop(0, n_chunks & (MOE_WAIT_BATCH - 1), wait(MOE_GRANULE), 0)


def _dispatch_kernel(chunks_ref, loc_ref, base_ref, schunks_ref, pstart_ref, pchunks_ref,
                     hb_ref, sel_ref, comb_ref, loct_ref, ustrict_ref,
                     xs_ref, posw_ref, buf_ref, zero_ref, sem_ref, zsem_ref):
    step = pl.program_id(0)
    n_steps = pl.num_programs(0)
    par = step % 2
    group = loct_ref.shape[0]
    ts = hb_ref.shape[0] // group
    nl = buf_ref.shape[1]
    g = MOE_GRANULE

    def wait_chunks(n, pp):
        _wait_rows(lambda r: _chunk_copy(buf_ref.at[0, pl.ds(0, r)], xs_ref.at[pl.ds(0, r)],
                                         sem_ref.at[pp]), n)

    @pl.when(step >= 2)
    def _():
        wait_chunks(schunks_ref[step - 2], par)

    lane = lax.broadcasted_iota(jnp.int32, (1, LANES), 1)
    prow = lax.broadcasted_iota(jnp.int32, (LANES, 1), 0)
    d = lax.broadcasted_iota(jnp.int32, (nl, ts), 0).astype(F32)
    experts = slice(ROUTE_E_LANE, ROUTE_E_LANE + N_EXPERTS)
    for u in range(group):
        rows = slice(u * ts, (u + 1) * ts)
        sel_t = sel_ref[rows, :].T[experts, :]
        comb_t = comb_ref[rows, :].T[experts, :]
        picked = sel_t > 0.0
        rank = jnp.dot(sel_t.astype(BF16), ustrict_ref[...], preferred_element_type=F32)
        dest = rank + jnp.concatenate([loct_ref[u]] * (ts // LANES), axis=1)
        pos_a = jnp.min(jnp.where(picked, dest, 1e9), axis=0, keepdims=True)
        pos_b = jnp.max(jnp.where(picked, dest, -1.0), axis=0, keepdims=True)
        w_a = jnp.sum(jnp.where(picked & (dest == pos_a), comb_t, 0.0), axis=0, keepdims=True)
        w_b = jnp.sum(jnp.where(picked & (dest == pos_b), comb_t, 0.0), axis=0, keepdims=True)
        pieces = [p.astype(F32) for w in (w_a, w_b) for p in _split3(w)]
        posw_t = jnp.zeros((LANES, ts), F32)
        for r, v in enumerate([pos_a, pos_b] + pieces):
            posw_t = jnp.where(prow == r, v, posw_t)
        posw = posw_t.T
        posw_ref[rows, :] = posw
        perm_a = jnp.where(d == pos_a, 1.0, 0.0).astype(BF16)
        perm_b = jnp.where(d == pos_b, 1.0, 0.0).astype(BF16)
        slot = par * group + u
        buf_ref[slot, :, :D_MODEL] = jnp.dot(perm_a + perm_b, hb_ref[rows, :],
                                             preferred_element_type=F32).astype(BF16)
        wl_a = jnp.where((lane >= 2) & (lane < 5), posw, 0.0).astype(BF16)
        wl_b = jnp.where((lane >= 5) & (lane < 8), posw, 0.0).astype(BF16)
        buf_ref[slot, :, D_MODEL:] = (
            jnp.dot(perm_a, wl_a, preferred_element_type=F32)
            + jnp.dot(perm_b, wl_b, preferred_element_type=F32)).astype(BF16)

    for u in range(group):
        for e in range(N_EXPERTS):
            idx = (step * group + u) * N_EXPERTS + e
            lo = loc_ref[idx]
            bs = base_ref[idx]

            def send(k, c, lo=lo, bs=bs, u=u):
                src = buf_ref.at[par * group + u, pl.ds(pl.multiple_of(lo + k * g, g), g)]
                dst = xs_ref.at[pl.ds(pl.multiple_of(bs + k * g, g), g)]
                _chunk_copy(src, dst, sem_ref.at[par]).start()
                return c
            lax.fori_loop(0, chunks_ref[idx], send, 0)

    @pl.when(step == n_steps - 1)
    def _():
        zero_ref[...] = jnp.zeros_like(zero_ref)
        for e in range(N_EXPERTS):
            ps = pstart_ref[e]

            def send_zero(k, c, ps=ps):
                dst = xs_ref.at[pl.ds(pl.multiple_of(ps + k * g, g), g)]
                _chunk_copy(zero_ref, dst, zsem_ref.at[0]).start()
                return c
            lax.fori_loop(0, pchunks_ref[e], send_zero, 0)
        for e in range(N_EXPERTS):
            def wait_zero(_, c):
                _chunk_copy(zero_ref, xs_ref.at[pl.ds(0, g)], zsem_ref.at[0]).wait()
                return c
            lax.fori_loop(0, pchunks_ref[e], wait_zero, 0)
        wait_chunks(schunks_ref[step], par)

        @pl.when(step >= 1)
        def _():
            wait_chunks(schunks_ref[step - 1], 1 - par)


def _dispatch(plan, hb, sel, comb, ts):
    t = hb.shape[0]
    group = _moe_group(t, ts)
    nl = _moe_local_rows(ts)
    rows = _moe_rows(t, ts)
    ustrict = jnp.tri(ts, k=-1, dtype=BF16).T
    row = lambda i, *_: (i, 0)
    grid_spec = pltpu.PrefetchScalarGridSpec(
        num_scalar_prefetch=6,
        grid=(t // (group * ts),),
        in_specs=[
            pl.BlockSpec((group * ts, D_MODEL), row),
            pl.BlockSpec((group * ts, LANES), row),
            pl.BlockSpec((group * ts, LANES), row),
            pl.BlockSpec((group, N_EXPERTS, LANES), lambda i, *_: (i, 0, 0)),
            pl.BlockSpec((ts, ts), lambda i, *_: (0, 0)),
        ],
        out_specs=[
            pl.BlockSpec(memory_space=pl.ANY),
            pl.BlockSpec((group * ts, LANES), row),
        ],
        scratch_shapes=[
            pltpu.VMEM((2 * group, nl, MOE_ROW_LANES), BF16),
            pltpu.VMEM((MOE_GRANULE, MOE_ROW_LANES), BF16),
            pltpu.SemaphoreType.DMA((2,)),
            pltpu.SemaphoreType.DMA((1,)),
        ],
    )
    return pl.pallas_call(
        _dispatch_kernel,
        grid_spec=grid_spec,
        out_shape=(
            jax.ShapeDtypeStruct((rows, MOE_ROW_LANES), BF16),
            jax.ShapeDtypeStruct((t, LANES), F32),
        ),
        compiler_params=pltpu.CompilerParams(
            dimension_semantics=("arbitrary",), vmem_limit_bytes=VMEM_LIMIT_BYTES),
        name="moe_dispatch",
    )(plan["chunks"], plan["loc_off"], plan["base"], plan["step_chunks"], plan["pad_start"],
      plan["pad_chunks"], hb, sel, comb, plan["loc_t"], ustrict)


def _expert_kernel(te_ref, nt_ref, x_ref, wg_ref, wu_ref, wd_ref, y_ref, wgb_ref, wub_ref, wdb_ref):
    j = pl.program_id(0)

    @pl.when(j < nt_ref[0])
    def _():
        @pl.when((j == 0) | (te_ref[j] != te_ref[jnp.maximum(j - 1, 0)]))
        def _():
            wgb_ref[...] = wg_ref[0].astype(BF16)
            wub_ref[...] = wu_ref[0].astype(BF16)
            wdb_ref[...] = wd_ref[0].astype(BF16)

        x = x_ref[:, :D_MODEL]
        wl = x_ref[:, D_MODEL:].astype(F32)
        w = jnp.sum(wl[:, 2:8], axis=1, keepdims=True)
        gate = jnp.dot(x, wgb_ref[...], preferred_element_type=F32)
        up = jnp.dot(x, wub_ref[...], preferred_element_type=F32)
        act = (_silu(gate) * up).astype(BF16)
        y_ref[...] = (w * jnp.dot(act, wdb_ref[...], preferred_element_type=F32)).astype(y_ref.dtype)


def _experts(plan, xs, w_gate, w_up, w_down):
    rows = xs.shape[0]
    live = lambda j, te, nt: jnp.minimum(j, nt[0] - 1)
    expert = lambda j, te, nt: (te[live(j, te, nt)], 0, 0)
    grid_spec = pltpu.PrefetchScalarGridSpec(
        num_scalar_prefetch=2,
        grid=(rows // MOE_ROW_TILE,),
        in_specs=[
            pl.BlockSpec((MOE_ROW_TILE, MOE_ROW_LANES), lambda j, te, nt: (live(j, te, nt), 0)),
            pl.BlockSpec((1, D_MODEL, D_FF_EXPERT), expert),
            pl.BlockSpec((1, D_MODEL, D_FF_EXPERT), expert),
            pl.BlockSpec((1, D_FF_EXPERT, D_MODEL), expert),
        ],
        out_specs=pl.BlockSpec((MOE_ROW_TILE, D_MODEL), lambda j, te, nt: (live(j, te, nt), 0)),
        scratch_shapes=[
            pltpu.VMEM((D_MODEL, D_FF_EXPERT), BF16),
            pltpu.VMEM((D_MODEL, D_FF_EXPERT), BF16),
            pltpu.VMEM((D_FF_EXPERT, D_MODEL), BF16),
        ],
    )
    return pl.pallas_call(
        _expert_kernel,
        grid_spec=grid_spec,
        out_shape=jax.ShapeDtypeStruct((rows, D_MODEL), BF16),
        compiler_params=pltpu.CompilerParams(
            dimension_semantics=("arbitrary",), vmem_limit_bytes=VMEM_LIMIT_BYTES),
        name="moe_experts",
    )(plan["tile_expert"], plan["n_tiles"], xs, w_gate, w_up, w_down)


def _combine_kernel(chunks_ref, loc_ref, base_ref, schunks_ref,
                    y_ref, posw_ref, h_ref, g_ref, b_ref, o_ref, buf_ref, sem_ref):
    step = pl.program_id(0)
    n_steps = pl.num_programs(0)
    par = step % 2
    nl = buf_ref.shape[1]
    group = buf_ref.shape[0] // 2
    ts = h_ref.shape[0] // group
    g = MOE_GRANULE

    def fetch(ss, pp):
        for u in range(group):
            for e in range(N_EXPERTS):
                idx = (ss * group + u) * N_EXPERTS + e
                lo = loc_ref[idx]
                bs = base_ref[idx]

                def get(k, c, lo=lo, bs=bs, u=u):
                    src = y_ref.at[pl.ds(pl.multiple_of(bs + k * g, g), g)]
                    dst = buf_ref.at[pp * group + u, pl.ds(pl.multiple_of(lo + k * g, g), g)]
                    _chunk_copy(src, dst, sem_ref.at[pp]).start()
                    return c
                lax.fori_loop(0, chunks_ref[idx], get, 0)

    @pl.when(step == 0)
    def _():
        buf_ref[...] = jnp.zeros(buf_ref.shape, BF16)
        fetch(0, 0)

    @pl.when(step + 1 < n_steps)
    def _():
        fetch(step + 1, 1 - par)

    _wait_rows(lambda r: _chunk_copy(y_ref.at[pl.ds(0, r)], buf_ref.at[0, pl.ds(0, r)],
                                     sem_ref.at[par]), schunks_ref[step])

    d = lax.broadcasted_iota(jnp.int32, (1, nl), 1).astype(F32)
    for u in range(group):
        rows = slice(u * ts, (u + 1) * ts)
        yb = buf_ref[par * group + u]
        posw = posw_ref[rows, :]
        gather = jnp.where((posw[:, 0:1] == d) | (posw[:, 1:2] == d), 1.0, 0.0).astype(BF16)
        ffn = jnp.dot(gather, yb, preferred_element_type=F32)
        o_ref[rows, :] = _layer_norm(DEEPNORM_ALPHA * h_ref[rows, :] + ffn, g_ref[...], b_ref[...])


def _combine(plan, y, posw, h, ln_g, ln_b, ts):
    t = h.shape[0]
    group = _moe_group(t, ts)
    nl = _moe_local_rows(ts)
    row = lambda i, *_: (i, 0)
    const = lambda i, *_: (0, 0)
    grid_spec = pltpu.PrefetchScalarGridSpec(
        num_scalar_prefetch=4,
        grid=(t // (group * ts),),
        in_specs=[
            pl.BlockSpec(memory_space=pl.ANY),
            pl.BlockSpec((group * ts, LANES), row),
            pl.BlockSpec((group * ts, D_MODEL), row),
            pl.BlockSpec((1, D_MODEL), const),
            pl.BlockSpec((1, D_MODEL), const),
        ],
        out_specs=pl.BlockSpec((group * ts, D_MODEL), row),
        scratch_shapes=[
            pltpu.VMEM((2 * group, nl, D_MODEL), BF16),
            pltpu.SemaphoreType.DMA((2,)),
        ],
    )
    return pl.pallas_call(
        _combine_kernel,
        grid_spec=grid_spec,
        out_shape=jax.ShapeDtypeStruct((t, D_MODEL), F32),
        compiler_params=pltpu.CompilerParams(
            dimension_semantics=("arbitrary",), vmem_limit_bytes=VMEM_LIMIT_BYTES),
        name="moe_combine_ln",
    )(plan["chunks"], plan["loc_off"], plan["base"], plan["step_chunks"], y, posw, h, ln_g, ln_b)


def _pad_lanes(a, n=LANES):
    return jnp.pad(a, ((0, 0), (0, n - a.shape[1])))


def _layer(h2d, batch, seq, w_in, b_in, conv_w, conv_b, a_log, d_skip, ssd_norm_g, w_out,
           ln1_g, ln1_b, rg_w, rg_b, re_w, re_b, w_gate, w_up, w_down, ln2_g, ln2_b):
    o_dt = SSD_INNER + SSD_CONV_DIM
    o_q = o_dt + SSD_HEADS
    o_f = o_q + 3 * ATT_INNER
    w_main = jnp.concatenate([w_in[:, :o_dt], w_in[:, o_q:o_f]], axis=1).astype(BF16)
    b_main = jnp.concatenate([b_in[:o_dt], b_in[o_q:o_f]])[None, :]
    w_dt, w_f = w_in[:, o_dt:o_q], w_in[:, o_f:]
    w_small = _pad_lanes(jnp.concatenate([w_dt, w_f, w_dt], axis=1)).astype(BF16)
    b_small = _pad_lanes(jnp.concatenate([b_in[o_dt:o_q], b_in[o_f:], b_in[o_dt:o_q]])[None, :])
    a_neg = -jnp.exp(a_log.astype(F32))
    a_row = _pad_lanes(a_neg[None, :])
    a_exp = jnp.repeat(a_neg, SSD_HEAD_DIM)[None, :]
    dskip_exp = jnp.repeat(d_skip.astype(F32), SSD_HEAD_DIM)[None, :]

    z, xbc, q_t, k, v_t, gate, qf_t, kf = _in_proj(h2d, w_main, b_main, w_small, b_small, a_row, seq)
    y_ssd = _ssd(xbc, z, gate, conv_w, conv_b[None, :], a_exp, dskip_exp, ssd_norm_g[None, :],
                 batch, seq)
    y_att = _attention(q_t, k, v_t, qf_t, kf, batch, seq)

    gap = ROUTE_E_LANE - N_EXPERT_GROUPS
    w_r = _pad_lanes(jnp.concatenate(
        [rg_w, jnp.zeros((D_MODEL, gap), F32),
         jnp.transpose(re_w, (1, 0, 2)).reshape(D_MODEL, N_EXPERTS)], axis=1))
    b_r = _pad_lanes(jnp.concatenate(
        [rg_b, jnp.zeros((gap,), F32), re_b.reshape(N_EXPERTS)])[None, :])
    wr_hi = w_r.astype(BF16)
    wr_lo = (w_r - wr_hi.astype(F32)).astype(BF16)
    wo = w_out.astype(BF16)
    ts = min(MOE_SORT_TILE, h2d.shape[0])
    h1, h1b, comb, sel, cnt = _out_proj(y_ssd, y_att, h2d, wo[:SSD_INNER], wo[SSD_INNER:],
                                        ln1_g[None, :], ln1_b[None, :],
                                        jnp.concatenate([wr_hi, wr_lo], axis=1), b_r, ts)

    plan = _dispatch_plan(
        cnt[:, 0, ROUTE_E_LANE:ROUTE_E_LANE + N_EXPERTS].astype(jnp.int32), ts)
    xs, posw = _dispatch(plan, h1b, sel, comb, ts)
    y = _experts(plan, xs, w_gate, w_up, w_down)
    return _combine(plan, y, posw, h1, ln2_g[None, :], ln2_b[None, :], ts)


def kernel(x, w_in, b_in, conv_w, conv_b, a_log, d_skip, ssd_norm_g, w_out, ln1_g, ln1_b,
           router_group_w, router_group_b, router_expert_w, router_expert_b, w_gate, w_up,
           w_down, ln2_g, ln2_b):
    batch, seq, d = x.shape
    h = x.reshape(batch * seq, d)
    for l in range(w_in.shape[0]):
        h = _layer(h, batch, seq, w_in[l], b_in[l], conv_w[l], conv_b[l], a_log[l], d_skip[l],
                   ssd_norm_g[l], w_out[l], ln1_g[l], ln1_b[l], router_group_w[l],
                   router_group_b[l], router_expert_w[l], router_expert_b[l], w_gate[l],
                   w_up[l], w_down[l], ln2_g[l], ln2_b[l])
    return h.reshape(batch, seq, d)
```

```python
import functools

import jax
import jax.numpy as jnp
import numpy as np
from jax import lax
from jax.experimental import pallas as pl
from jax.experimental.pallas import tpu as pltpu

F32 = jnp.float32
BF16 = jnp.bfloat16

D_MODEL = 1024
SSD_HEADS = 8
SSD_HEAD_DIM = 64
SSD_INNER = SSD_HEADS * SSD_HEAD_DIM
SSD_GROUPS = 2
SSD_STATE = 128
CONV_WIDTH = 4
SSD_CONV_DIM = SSD_INNER + 2 * SSD_GROUPS * SSD_STATE
ATT_HEADS = 8
ATT_HEAD_DIM = 64
ATT_INNER = ATT_HEADS * ATT_HEAD_DIM
N_EXPERT_GROUPS = 4
EXPERTS_PER_GROUP = 4
N_EXPERTS = N_EXPERT_GROUPS * EXPERTS_PER_GROUP
D_FF_EXPERT = 512
DEPTH = 1
DEEPNORM_ALPHA = (2.0 * DEPTH) ** 0.25
LN_EPS = 1e-5
RMS_EPS = 1e-5

LANES = 128
SUBLANES = 8
VMEM_LIMIT_BYTES = 56 * 1024 * 1024

GATE_A_LANE = 0
GATE_F_LANE = SSD_HEADS
GATE_DT_LANE = 2 * SSD_HEADS
ROUTE_E_LANE = SUBLANES
MOE_SORT_TILE = 256
MOE_GRANULE = 16
MOE_ROW_TILE = 1024
MOE_TILES_PER_STEP = 4
MOE_ROW_LANES = D_MODEL + LANES
MOE_WAIT_BATCH = 8

IN_PROJ_ROWS = 1024
OUT_PROJ_ROWS = 1024
CUMSUM_BLOCK = 256
FB_LANES = 6
ATT_BLOCK = 256
ATT_PAIRS_PER_STEP = 4
ATT_BATCHES_PER_STEP = 2

SSD_ROWS = 256
SSD_STEP_ROWS = 512
SSD_CHUNK = 64
NEG_BIG = -1e30
LOG2E = 1.4426950408889634


def _split3(v):
    hi = v.astype(BF16)
    r1 = v - hi.astype(F32)
    mid = r1.astype(BF16)
    lo = (r1 - mid.astype(F32)).astype(BF16)
    return hi, mid, lo


def _silu(v):
    half = 0.5 * v
    return half * (1.0 + jnp.tanh(half))


def _softplus(v):
    return jnp.maximum(v, 0.0) + jnp.log1p(jnp.exp(-jnp.abs(v)))


def _in_proj_kernel(x_ref, wm_ref, bm_ref, ws_ref, bs_ref, arow_ref, place_ref, ones_ref,
                    z_ref, xbc_ref, qt_ref, k_ref, vt_ref, gate_ref, qft_ref, kf_ref,
                    carry_ref, *, tiles_per_seq, cb):
    i = pl.program_id(0)
    tm = x_ref.shape[0]
    tb = qt_ref.shape[2]
    xb = x_ref[...].astype(BF16)

    def put_transposed(ref, val):
        for c in range(tm // tb):
            ref[c] = val[c * tb:(c + 1) * tb, :].T.astype(ref.dtype)

    s = jnp.dot(xb, ws_ref[...], preferred_element_type=F32) + bs_ref[...]
    lane = lax.broadcasted_iota(jnp.int32, (1, LANES), 1)
    sp = _softplus(s)
    log_f = -_softplus(-s)
    val = jnp.where(lane < GATE_F_LANE, sp * arow_ref[...],
                    jnp.where(lane < GATE_DT_LANE, log_f,
                              jnp.where(lane < GATE_DT_LANE + SSD_HEADS, sp, 0.0)))
    carry = jnp.where(i % tiles_per_seq == 0, 0.0, carry_ref[...])
    row = lax.broadcasted_iota(jnp.int32, (cb, 1), 0)
    sums = []
    for c in range(tm // cb):
        part = val[c * cb:(c + 1) * cb, :]
        shift = 1
        while shift < cb:
            part = part + jnp.where(row >= shift, pltpu.roll(part, shift, 0), 0.0)
            shift *= 2
        part = part + carry
        carry = part[cb - 1:cb, :]
        sums.append(part)
    cs = jnp.concatenate(sums, axis=0)
    carry_ref[...] = carry
    out = jnp.where(lane < GATE_DT_LANE, cs, val)
    gate_ref[...] = out
    pieces = jnp.concatenate(_split3(cs * LOG2E), axis=1)
    fb = jnp.dot(pieces, place_ref[...], preferred_element_type=F32) + ones_ref[...]
    put_transposed(qft_ref, fb[:, :LANES])
    kf_ref[...] = fb[:, LANES:].astype(BF16)

    col = 0
    for ref, scale in ((z_ref, None), (xbc_ref, None), (qt_ref, ATT_HEAD_DIM ** -0.5 * LOG2E),
                       (k_ref, None), (vt_ref, None)):
        transposed = ref is qt_ref or ref is vt_ref
        n = ATT_INNER if transposed else ref.shape[1]
        acc = jnp.dot(xb, wm_ref[:, col:col + n], preferred_element_type=F32) + bm_ref[:, col:col + n]
        if scale is not None:
            acc = acc * scale
        if transposed:
            put_transposed(ref, acc)
        else:
            ref[...] = acc.astype(ref.dtype)
        col += n


def _forget_bias_placement():
    place = np.zeros((3 * LANES, 2 * LANES), np.float32)
    ones = np.zeros((1, 2 * LANES), np.float32)
    for h in range(ATT_HEADS):
        for piece in range(3):
            src = piece * LANES + GATE_F_LANE + h
            place[src, FB_LANES * h + piece] = 1.0
            place[src, LANES + FB_LANES * h + 3 + piece] = -1.0
            ones[0, FB_LANES * h + 3 + piece] = 1.0
            ones[0, LANES + FB_LANES * h + piece] = 1.0
    return jnp.asarray(place, BF16), jnp.asarray(ones, F32)


def _in_proj(x2d, w_main, b_main, w_small, b_small, a_row, seq):
    t = x2d.shape[0]
    tm = min(IN_PROJ_ROWS, seq)
    tk = min(ATT_BLOCK, seq)
    place, ones = _forget_bias_placement()
    n_main = w_main.shape[1]
    row = lambda i: (i, 0)
    const = lambda i: (0, 0)
    outs = (
        jax.ShapeDtypeStruct((t, SSD_INNER), BF16),
        jax.ShapeDtypeStruct((t, SSD_CONV_DIM), BF16),
        jax.ShapeDtypeStruct((t // tk, ATT_INNER, tk), BF16),
        jax.ShapeDtypeStruct((t, ATT_INNER), BF16),
        jax.ShapeDtypeStruct((t // tk, ATT_INNER, tk), BF16),
        jax.ShapeDtypeStruct((t, LANES), F32),
        jax.ShapeDtypeStruct((t // tk, LANES, tk), BF16),
        jax.ShapeDtypeStruct((t, LANES), BF16),
    )
    return pl.pallas_call(
        functools.partial(_in_proj_kernel, tiles_per_seq=seq // tm, cb=min(CUMSUM_BLOCK, tm)),
        grid=(t // tm,),
        in_specs=[
            pl.BlockSpec((tm, D_MODEL), row),
            pl.BlockSpec((D_MODEL, n_main), const),
            pl.BlockSpec((1, n_main), const),
            pl.BlockSpec((D_MODEL, LANES), const),
            pl.BlockSpec((1, LANES), const),
            pl.BlockSpec((1, LANES), const),
            pl.BlockSpec((3 * LANES, 2 * LANES), const),
            pl.BlockSpec((1, 2 * LANES), const),
        ],
        out_specs=[
            pl.BlockSpec((tm, SSD_INNER), row),
            pl.BlockSpec((tm, SSD_CONV_DIM), row),
            pl.BlockSpec((tm // tk, ATT_INNER, tk), lambda i: (i, 0, 0)),
            pl.BlockSpec((tm, ATT_INNER), row),
            pl.BlockSpec((tm // tk, ATT_INNER, tk), lambda i: (i, 0, 0)),
            pl.BlockSpec((tm, LANES), row),
            pl.BlockSpec((tm // tk, LANES, tk), lambda i: (i, 0, 0)),
            pl.BlockSpec((tm, LANES), row),
        ],
        out_shape=outs,
        scratch_shapes=[pltpu.VMEM((1, LANES), F32)],
        compiler_params=pltpu.CompilerParams(
            dimension_semantics=("arbitrary",), vmem_limit_bytes=VMEM_LIMIT_BYTES),
        name="in_proj",
    )(x2d, w_main, b_main, w_small, b_small, a_row, place, ones)


def _expand_heads(gate, lane0, rows):
    lane = lax.broadcasted_iota(jnp.int32, (1, LANES), 1)
    pieces = []
    for p in range(SSD_HEADS // 2):
        a = jnp.broadcast_to(gate[:, lane0 + 2 * p:lane0 + 2 * p + 1], (rows, LANES))
        b = jnp.broadcast_to(gate[:, lane0 + 2 * p + 1:lane0 + 2 * p + 2], (rows, LANES))
        pieces.append(jnp.where(lane < SSD_HEAD_DIM, a, b))
    return jnp.concatenate(pieces, axis=1)


def _ssd_kernel(xbc_ref, z_ref, gate_ref, cw_ref, cb_ref, aexp_ref, dskip_ref, ng_ref, shift_ref,
                y_ref, ext_ref, state_ref):
    j = pl.program_id(1)
    rows = shift_ref.shape[1]
    q = SSD_CHUNK
    gw = SSD_INNER // SSD_GROUPS
    hpg = SSD_HEADS // SSD_GROUPS
    pad = SUBLANES

    @pl.when(j == 0)
    def _():
        ext_ref[0:pad, :] = jnp.zeros((pad, SSD_CONV_DIM), F32)
        state_ref[...] = jnp.zeros_like(state_ref)

    r_i = lax.broadcasted_iota(jnp.int32, (q, gw), 0)
    c_i = lax.broadcasted_iota(jnp.int32, (q, gw), 1)
    s_i = c_i % q
    eye_t = s_i == r_i
    tril_t = s_i <= r_i
    rb = lax.broadcasted_iota(jnp.int32, (hpg * q, gw), 0) // q
    cbk = lax.broadcasted_iota(jnp.int32, (hpg * q, gw), 1) // SSD_HEAD_DIM
    blk = rb == cbk

    for t in range(xbc_ref.shape[0] // rows):
        sub = slice(t * rows, (t + 1) * rows)
        xb = xbc_ref[sub, :]
        xf = xb.astype(F32)
        conv = cb_ref[...] + xf * cw_ref[CONV_WIDTH - 1:CONV_WIDTH, :]
        for s in range(1, CONV_WIDTH):
            shifted = jnp.dot(shift_ref[s - 1], xb, preferred_element_type=F32)
            conv = conv + shifted * cw_ref[CONV_WIDTH - 1 - s:CONV_WIDTH - s, :]
        ext_ref[pad:2 * pad, :] = xf[0:pad, :]
        head = cb_ref[...]
        for kk in range(CONV_WIDTH):
            off = pad - (CONV_WIDTH - 1) + kk
            head = head + ext_ref[off:off + pad, :] * cw_ref[kk:kk + 1, :]
        ext_ref[0:pad, :] = xf[rows - pad:rows, :]
        u = _silu(jnp.concatenate([head, conv[pad:, :]], axis=0))
        xs = u[:, :SSD_INNER]
        b_all = u[:, SSD_INNER:SSD_INNER + SSD_GROUPS * SSD_STATE].astype(BF16)
        c_all = u[:, SSD_INNER + SSD_GROUPS * SSD_STATE:].astype(BF16)

        gate = gate_ref[sub, :]
        dt_e = _expand_heads(gate, GATE_DT_LANE, rows)
        a_e = _expand_heads(gate, GATE_A_LANE, rows)
        xdt = xs * dt_e

        ys = []
        for c in range(rows // q):
            r0 = c * q
            a_c = a_e[r0:r0 + q, :]
            a_prev = a_c[0:1, :] - dt_e[r0:r0 + 1, :] * aexp_ref[...]
            a_end = a_c[q - 1:q, :]
            y_groups = []
            for g in range(SSD_GROUPS):
                l0 = g * gw
                a_cg = a_c[:, l0:l0 + gw]
                diag = jnp.sum(jnp.where(eye_t, a_cg, 0.0), axis=0, keepdims=True)
                seg = jnp.minimum(a_cg - diag, 0.0)
                lcat = jnp.where(tril_t, jnp.exp(seg), 0.0)
                bg = b_all[r0:r0 + q, g * SSD_STATE:(g + 1) * SSD_STATE]
                cg = c_all[r0:r0 + q, g * SSD_STATE:(g + 1) * SSD_STATE]
                b_tile = jnp.concatenate([bg] * hpg, axis=0)
                gcat = lax.dot_general(cg, b_tile, (((1,), (1,)), ((), ())),
                                       preferred_element_type=F32)
                xg = xdt[r0:r0 + q, l0:l0 + gw]
                xg_b = xg.astype(BF16)
                xbd = jnp.where(blk, jnp.concatenate([xg_b] * hpg, axis=0), jnp.zeros((), BF16))
                y_diag = jnp.dot((gcat * lcat).astype(BF16), xbd, preferred_element_type=F32)
                st = state_ref[g]
                y_off = jnp.dot(cg, st.astype(BF16), preferred_element_type=F32) \
                    * jnp.exp(a_cg - a_prev[:, l0:l0 + gw])
                dte = jnp.exp(a_end[:, l0:l0 + gw] - a_cg)
                upd = lax.dot_general(bg, (xg * dte).astype(BF16), (((0,), (0,)), ((), ())),
                                      preferred_element_type=F32)
                state_ref[g] = st * jnp.exp(a_end[:, l0:l0 + gw] - a_prev[:, l0:l0 + gw]) + upd
                y_groups.append(y_diag + y_off)
            ys.append(jnp.concatenate(y_groups, axis=1))
        y = jnp.concatenate(ys, axis=0) + xs * dskip_ref[...]
        y = y * _silu(z_ref[sub, :].astype(F32))
        outs = []
        for g in range(SSD_GROUPS):
            yg = y[:, g * gw:(g + 1) * gw]
            ms = jnp.mean(yg * yg, axis=1, keepdims=True)
            outs.append(yg * lax.rsqrt(ms + RMS_EPS))
        y_ref[sub, :] = (jnp.concatenate(outs, axis=1) * ng_ref[...]).astype(y_ref.dtype)


def _ssd(xbc, z, gate, conv_w, conv_b, a_exp, dskip_exp, norm_g, batch, seq):
    t = xbc.shape[0]
    rows = min(SSD_ROWS, seq)
    step_rows = min(SSD_STEP_ROWS, seq)
    nj = seq // step_rows
    row = lambda b, j: (b * nj + j, 0)
    const = lambda b, j: (0, 0)
    shifts = jnp.stack([jnp.eye(rows, k=-s, dtype=BF16) for s in range(1, CONV_WIDTH)])
    return pl.pallas_call(
        _ssd_kernel,
        grid=(batch, nj),
        in_specs=[
            pl.BlockSpec((step_rows, SSD_CONV_DIM), row),
            pl.BlockSpec((step_rows, SSD_INNER), row),
            pl.BlockSpec((step_rows, LANES), row),
            pl.BlockSpec((CONV_WIDTH, SSD_CONV_DIM), const),
            pl.BlockSpec((1, SSD_CONV_DIM), const),
            pl.BlockSpec((1, SSD_INNER), const),
            pl.BlockSpec((1, SSD_INNER), const),
            pl.BlockSpec((1, SSD_INNER), const),
            pl.BlockSpec((CONV_WIDTH - 1, rows, rows), lambda b, j: (0, 0, 0)),
        ],
        out_specs=pl.BlockSpec((step_rows, SSD_INNER), row),
        out_shape=jax.ShapeDtypeStruct((t, SSD_INNER), BF16),
        scratch_shapes=[
            pltpu.VMEM((2 * SUBLANES, SSD_CONV_DIM), F32),
            pltpu.VMEM((SSD_GROUPS, SSD_STATE, SSD_INNER // SSD_GROUPS), F32),
        ],
        compiler_params=pltpu.CompilerParams(
            dimension_semantics=("arbitrary", "arbitrary"), vmem_limit_bytes=VMEM_LIMIT_BYTES),
        name="ssd",
    )(xbc, z, gate, conv_w, conv_b, a_exp, dskip_exp, norm_g, shifts)


def _attn_kernel(qt_ref, k_ref, vt_ref, qft_ref, kf_ref, o_ref, *scratch, tb, pairs, batches):
    pair0 = pl.program_id(1) * pairs
    nb = qt_ref.shape[0] // batches
    seq = nb * tb
    chains = [(bb, pp) for bb in range(batches) for pp in range(pairs)]
    s_refs, acc_refs = scratch[:len(chains)], scratch[len(chains):]
    feat = lax.broadcasted_iota(jnp.int32, (LANES, 1), 0)
    in_head = (feat < ATT_HEAD_DIM, feat >= ATT_HEAD_DIM)
    ones = jnp.ones((2 * SUBLANES, tb), BF16)
    key = lax.broadcasted_iota(jnp.int32, (tb, 2 * tb), 0)
    qry = lax.broadcasted_iota(jnp.int32, (tb, 2 * tb), 1)
    causal = jnp.where(qry >= tb, qry - tb, qry) >= key
    zero = jnp.zeros((), BF16)

    def logits_t(qa_t, c, jk):
        bb, pp = chains[c]
        k0 = pl.multiple_of(bb * seq + jk * tb, tb)
        ka = jnp.concatenate([k_ref[pl.ds(k0, tb), pp * LANES:(pp + 1) * LANES],
                              kf_ref[pl.ds(k0, tb), :]], axis=1)
        return jnp.dot(ka, qa_t, preferred_element_type=F32)

    def update(s_t, m, c, jk, par):
        bb, pp = chains[c]
        m_new = jnp.maximum(m, jnp.max(s_t, axis=0, keepdims=True))
        alpha = jnp.exp2(m - m_new)
        p_t = jnp.exp2(s_t - m_new).astype(BF16)
        for hh in range(2):
            r0 = pp * LANES + hh * ATT_HEAD_DIM
            va_t = jnp.concatenate([vt_ref[bb * nb + jk, r0:r0 + ATT_HEAD_DIM, :], ones], axis=0)
            cols = slice(hh * tb, (hh + 1) * tb)
            acc_refs[c][par, hh] = alpha[:, cols] * acc_refs[c][par, hh] + jnp.dot(
                va_t, p_t[:, cols], preferred_element_type=F32)
        return m_new

    def build_queries(i):
        qas = []
        for bb, pp in chains:
            qf_t = qft_ref[bb * nb + i]
            q_t = qt_ref[bb * nb + i, pp * LANES:(pp + 1) * LANES, :]
            cols = []
            for hh in range(2):
                fb0 = FB_LANES * (2 * (pair0 + pp) + hh)
                in_fb = (feat >= fb0) & (feat < fb0 + FB_LANES)
                cols.append(jnp.concatenate(
                    [jnp.where(in_head[hh], q_t, zero), jnp.where(in_fb, qf_t, zero)], axis=0))
            qas.append(jnp.concatenate(cols, axis=1))
        return qas

    def first_logits(i):
        qas = build_queries(i)
        for c in range(len(chains)):
            s_refs[c][2] = logits_t(qas[c], c, 0)

    def q_block(i, _):
        par = i % 2
        qas = build_queries(i)
        for c in range(len(chains)):
            acc_refs[c][par] = jnp.zeros(acc_refs[c].shape[1:], F32)

        def step(jk, ms, rd, wr):
            new = []
            for c in range(len(chains)):
                s_t = s_refs[c][rd]
                s_refs[c][wr] = logits_t(qas[c], c, jk + 1)
                new.append(update(s_t, ms[c], c, jk, par))
            return tuple(new)

        def two_steps(t2, ms):
            return step(2 * t2 + 2, step(2 * t2 + 1, ms, 1, 0), 0, 1)

        ms = tuple(jnp.full((1, 2 * tb), NEG_BIG, F32) for _ in chains)
        ms = lax.cond(i >= 1, lambda v: step(0, v, 2, 1), lambda v: v, ms)
        ms = lax.fori_loop(0, lax.shift_right_logical(jnp.maximum(i - 1, 0), 1), two_steps, ms)
        ms = lax.cond((i >= 2) & (i % 2 == 0), lambda v: step(i - 1, v, 1, 0), lambda v: v, ms)

        last = jnp.where(i >= 1, i % 2, 2)
        diag = [jnp.where(causal, s_refs[c][last], NEG_BIG) for c in range(len(chains))]
        first_logits(jnp.minimum(i + 1, nb - 1))
        finish(jnp.maximum(i - 1, 0), 1 - par)
        for c in range(len(chains)):
            update(diag[c], ms[c], c, i, par)
        return 0

    def finish(iq, par):
        q0 = pl.multiple_of(iq * tb, tb)
        for c, (bb, pp) in enumerate(chains):
            heads = []
            for hh in range(2):
                acc_t = acc_refs[c][par, hh]
                inv = 1.0 / acc_t[ATT_HEAD_DIM:ATT_HEAD_DIM + 1, :]
                heads.append(acc_t[:ATT_HEAD_DIM, :] * inv)
            out_t = jnp.concatenate(heads, axis=0)
            o_ref[pl.ds(bb * seq + q0, tb), pp * LANES:(pp + 1) * LANES] = out_t.T.astype(o_ref.dtype)

    for c in range(len(chains)):
        acc_refs[c][1] = jnp.ones(acc_refs[c].shape[1:], F32)
    first_logits(0)
    lax.fori_loop(0, nb, q_block, 0)
    finish(nb - 1, (nb - 1) % 2)


def _attention(q_t, k, v_t, qf_t, kf, batch, seq):
    t = k.shape[0]
    tb = min(ATT_BLOCK, seq)
    nb = seq // tb
    pairs = ATT_PAIRS_PER_STEP
    batches = ATT_BATCHES_PER_STEP if batch % ATT_BATCHES_PER_STEP == 0 else 1
    width = pairs * LANES
    return pl.pallas_call(
        functools.partial(_attn_kernel, tb=tb, pairs=pairs, batches=batches),
        grid=(batch // batches, ATT_HEADS // (2 * pairs)),
        in_specs=[
            pl.BlockSpec((batches * nb, width, tb), lambda b, p: (b, p, 0)),
            pl.BlockSpec((batches * seq, width), lambda b, p: (b, p)),
            pl.BlockSpec((batches * nb, width, tb), lambda b, p: (b, p, 0)),
            pl.BlockSpec((batches * nb, LANES, tb), lambda b, p: (b, 0, 0)),
            pl.BlockSpec((batches * seq, LANES), lambda b, p: (b, 0)),
        ],
        out_specs=pl.BlockSpec((batches * seq, width), lambda b, p: (b, p)),
        out_shape=jax.ShapeDtypeStruct((t, ATT_INNER), BF16),
        scratch_shapes=([pltpu.VMEM((3, tb, 2 * tb), F32) for _ in range(batches * pairs)]
                        + [pltpu.VMEM((2, 2, ATT_HEAD_DIM + 2 * SUBLANES, tb), F32)
                           for _ in range(batches * pairs)]),
        compiler_params=pltpu.CompilerParams(
            dimension_semantics=("arbitrary", "arbitrary"), vmem_limit_bytes=VMEM_LIMIT_BYTES),
        name="fox_attention",
    )(q_t, k, v_t, qf_t, kf)


def _layer_norm(u, g, b):
    mu = jnp.mean(u, axis=1, keepdims=True)
    d = u - mu
    var = jnp.mean(d * d, axis=1, keepdims=True)
    return d * lax.rsqrt(var + LN_EPS) * g + b


def _route(lg):
    rows = lg.shape[0]
    lt = lg.T
    gl = lt[0:N_EXPERT_GROUPS, :]
    g_row = lax.broadcasted_iota(jnp.int32, (N_EXPERT_GROUPS, 1), 0).astype(F32)
    gmax = jnp.max(gl, axis=0, keepdims=True)
    gsum = jnp.sum(jnp.exp(gl - gmax), axis=0, keepdims=True)
    g_val = 1.0 / gsum
    g_idx = jnp.min(jnp.where(gl == gmax, g_row, float(N_EXPERT_GROUPS)), axis=0, keepdims=True)
    el = lt[ROUTE_E_LANE:ROUTE_E_LANE + N_EXPERTS, :]
    e_row = lax.broadcasted_iota(jnp.int32, (N_EXPERTS, 1), 0)
    e_grp = (e_row // EXPERTS_PER_GROUP).astype(F32)
    e_row = e_row.astype(F32)
    in_grp = e_grp == g_idx
    m1 = jnp.max(jnp.where(in_grp, el, NEG_BIG), axis=0, keepdims=True)
    i1 = jnp.min(jnp.where(in_grp & (el == m1), e_row, float(N_EXPERTS)), axis=0, keepdims=True)
    rest = in_grp & (e_row != i1)
    m2 = jnp.max(jnp.where(rest, el, NEG_BIG), axis=0, keepdims=True)
    i2 = jnp.min(jnp.where(rest & (el == m2), e_row, float(N_EXPERTS)), axis=0, keepdims=True)
    r = jnp.exp(m2 - m1)
    w1 = 1.0 / (1.0 + r)
    w2 = r * w1
    comb_e = jnp.where(e_row == i1, g_val * w1, jnp.where(e_row == i2, g_val * w2, 0.0))
    sel_e = jnp.where((e_row == i1) | (e_row == i2), 1.0, 0.0)
    top = jnp.zeros((ROUTE_E_LANE, rows), F32)
    bottom = jnp.zeros((LANES - ROUTE_E_LANE - N_EXPERTS, rows), F32)
    comb = jnp.concatenate([top, comb_e, bottom], axis=0).T
    sel = jnp.concatenate([top, sel_e, bottom], axis=0).T
    return comb, sel


def _out_proj_kernel(ys_ref, ya_ref, x_ref, wa_ref, wb_ref, g_ref, b_ref, wr_ref, br_ref,
                     h_ref, hb_ref, comb_ref, sel_ref, cnt_ref):
    mix = jnp.dot(ys_ref[...], wa_ref[...], preferred_element_type=F32)
    mix = mix + jnp.dot(ya_ref[...], wb_ref[...], preferred_element_type=F32)
    h = _layer_norm(DEEPNORM_ALPHA * x_ref[...] + mix, g_ref[...], b_ref[...])
    h_ref[...] = h
    h_hi = h.astype(BF16)
    hb_ref[...] = h_hi
    h_lo = (h - h_hi.astype(F32)).astype(BF16)
    both = jnp.dot(h_hi, wr_ref[...], preferred_element_type=F32)
    lg = (both[:, :LANES] + both[:, LANES:]
          + jnp.dot(h_lo, wr_ref[:, :LANES], preferred_element_type=F32)) + br_ref[...]
    comb, sel = _route(lg)
    comb_ref[...] = comb
    sel_ref[...] = sel
    ts = h.shape[0] // cnt_ref.shape[0]
    for c in range(cnt_ref.shape[0]):
        cnt = jnp.sum(sel[c * ts:(c + 1) * ts, :], axis=0, keepdims=True)
        cnt_ref[c] = jnp.broadcast_to(cnt, (SUBLANES, LANES))


def _out_proj(ys, ya, x2d, wo_a, wo_b, ln_g, ln_b, wr, br, ts):
    t = x2d.shape[0]
    tm = min(OUT_PROJ_ROWS, t)
    row = lambda i: (i, 0)
    const = lambda i: (0, 0)
    return pl.pallas_call(
        _out_proj_kernel,
        grid=(t // tm,),
        in_specs=[
            pl.BlockSpec((tm, SSD_INNER), row),
            pl.BlockSpec((tm, ATT_INNER), row),
            pl.BlockSpec((tm, D_MODEL), row),
            pl.BlockSpec((SSD_INNER, D_MODEL), const),
            pl.BlockSpec((ATT_INNER, D_MODEL), const),
            pl.BlockSpec((1, D_MODEL), const),
            pl.BlockSpec((1, D_MODEL), const),
            pl.BlockSpec((D_MODEL, 2 * LANES), const),
            pl.BlockSpec((1, LANES), const),
        ],
        out_specs=[
            pl.BlockSpec((tm, D_MODEL), row),
            pl.BlockSpec((tm, D_MODEL), row),
            pl.BlockSpec((tm, LANES), row),
            pl.BlockSpec((tm, LANES), row),
            pl.BlockSpec((tm // ts, SUBLANES, LANES), lambda i: (i, 0, 0)),
        ],
        out_shape=(
            jax.ShapeDtypeStruct((t, D_MODEL), F32),
            jax.ShapeDtypeStruct((t, D_MODEL), BF16),
            jax.ShapeDtypeStruct((t, LANES), F32),
            jax.ShapeDtypeStruct((t, LANES), F32),
            jax.ShapeDtypeStruct((t // ts, SUBLANES, LANES), F32),
        ),
        compiler_params=pltpu.CompilerParams(
            dimension_semantics=("arbitrary",), vmem_limit_bytes=VMEM_LIMIT_BYTES),
        name="out_proj_ln_router",
    )(ys, ya, x2d, wo_a, wo_b, ln_g, ln_b, wr, br)


def _dispatch_plan(cnt, ts):
    nt = cnt.shape[0]
    c16 = (cnt + (MOE_GRANULE - 1)) // MOE_GRANULE * MOE_GRANULE
    loc_off = jnp.cumsum(c16, axis=1) - c16
    tot = jnp.sum(c16, axis=0)
    tiles = (tot + (MOE_ROW_TILE - 1)) // MOE_ROW_TILE
    cum_tiles = jnp.cumsum(tiles)
    start = (cum_tiles - tiles) * MOE_ROW_TILE
    base = start[None, :] + jnp.cumsum(c16, axis=0) - c16
    max_tiles = _moe_rows(nt * ts, ts) // MOE_ROW_TILE
    tile_expert = jnp.minimum(
        jnp.sum(jnp.arange(max_tiles, dtype=jnp.int32)[:, None] >= cum_tiles[None, :], axis=1),
        N_EXPERTS - 1).astype(jnp.int32)
    loc_t = jnp.broadcast_to(loc_off.astype(F32)[:, :, None], (nt, N_EXPERTS, LANES))
    return dict(
        chunks=(c16 // MOE_GRANULE).reshape(-1).astype(jnp.int32),
        loc_off=loc_off.reshape(-1).astype(jnp.int32),
        base=base.reshape(-1).astype(jnp.int32),
        step_chunks=(jnp.sum(c16, axis=1).reshape(-1, _moe_group(nt * ts, ts)).sum(axis=1)
                     // MOE_GRANULE).astype(jnp.int32),
        pad_start=(start + tot).astype(jnp.int32),
        pad_chunks=((tiles * MOE_ROW_TILE - tot) // MOE_GRANULE).astype(jnp.int32),
        n_tiles=cum_tiles[-1:].astype(jnp.int32),
        tile_expert=tile_expert,
        loc_t=loc_t,
    )


def _moe_group(t, ts):
    return MOE_TILES_PER_STEP if t % (MOE_TILES_PER_STEP * ts) == 0 else 1


def _moe_local_rows(ts):
    rows = 2 * ts + N_EXPERTS * (MOE_GRANULE - 1)
    return (rows + LANES - 1) // LANES * LANES


def _moe_rows(t, ts):
    rows = 2 * t + (t // ts) * N_EXPERTS * (MOE_GRANULE - 1) + N_EXPERTS * MOE_ROW_TILE
    return (rows + MOE_ROW_TILE - 1) // MOE_ROW_TILE * MOE_ROW_TILE


def _chunk_copy(src, dst, sem):
    return pltpu.make_async_copy(src, dst, sem)


def _wait_rows(copy_of_rows, n_chunks):
    def wait(rows):
        def one(_, c):
            copy_of_rows(rows).wait()
            return c
        return one
    shift = MOE_WAIT_BATCH.bit_length() - 1
    lax.fori_loop(0, lax.shift_right_logical(n_chunks, shift), wait(MOE_WAIT_BATCH * MOE_GRANULE), 0)
    lax.fori_loop(0, n_chunks & (MOE_WAIT_BATCH - 1), wait(MOE_GRANULE), 0)


def _dispatch_kernel(chunks_ref, loc_ref, base_ref, schunks_ref, pstart_ref, pchunks_ref,
                     hb_ref, sel_ref, comb_ref, loct_ref, ustrict_ref,
                     xs_ref, posw_ref, buf_ref, zero_ref, sem_ref, zsem_ref):
    step = pl.program_id(0)
    n_steps = pl.num_programs(0)
    par = step % 2
    group = loct_ref.shape[0]
    ts = hb_ref.shape[0] // group
    nl = buf_ref.shape[1]
    g = MOE_GRANULE

    def wait_chunks(n, pp):
        _wait_rows(lambda r: _chunk_copy(buf_ref.at[0, pl.ds(0, r)], xs_ref.at[pl.ds(0, r)],
                                         sem_ref.at[pp]), n)

    @pl.when(step >= 2)
    def _():
        wait_chunks(schunks_ref[step - 2], par)

    lane = lax.broadcasted_iota(jnp.int32, (1, LANES), 1)
    prow = lax.broadcasted_iota(jnp.int32, (LANES, 1), 0)
    d = lax.broadcasted_iota(jnp.int32, (nl, ts), 0).astype(F32)
    experts = slice(ROUTE_E_LANE, ROUTE_E_LANE + N_EXPERTS)
    for u in range(group):
        rows = slice(u * ts, (u + 1) * ts)
        sel_t = sel_ref[rows, :].T[experts, :]
        comb_t = comb_ref[rows, :].T[experts, :]
        picked = sel_t > 0.0
        rank = jnp.dot(sel_t.astype(BF16), ustrict_ref[...], preferred_element_type=F32)
        dest = rank + jnp.concatenate([loct_ref[u]] * (ts // LANES), axis=1)
        pos_a = jnp.min(jnp.where(picked, dest, 1e9), axis=0, keepdims=True)
        pos_b = jnp.max(jnp.where(picked, dest, -1.0), axis=0, keepdims=True)
        w_a = jnp.sum(jnp.where(picked & (dest == pos_a), comb_t, 0.0), axis=0, keepdims=True)
        w_b = jnp.sum(jnp.where(picked & (dest == pos_b), comb_t, 0.0), axis=0, keepdims=True)
        pieces = [p.astype(F32) for w in (w_a, w_b) for p in _split3(w)]
        posw_t = jnp.zeros((LANES, ts), F32)
        for r, v in enumerate([pos_a, pos_b] + pieces):
            posw_t = jnp.where(prow == r, v, posw_t)
        posw = posw_t.T
        posw_ref[rows, :] = posw
        perm_a = jnp.where(d == pos_a, 1.0, 0.0).astype(BF16)
        perm_b = jnp.where(d == pos_b, 1.0, 0.0).astype(BF16)
        slot = par * group + u
        buf_ref[slot, :, :D_MODEL] = jnp.dot(perm_a + perm_b, hb_ref[rows, :],
                                             preferred_element_type=F32).astype(BF16)
        wl_a = jnp.where((lane >= 2) & (lane < 5), posw, 0.0).astype(BF16)
        wl_b = jnp.where((lane >= 5) & (lane < 8), posw, 0.0).astype(BF16)
        buf_ref[slot, :, D_MODEL:] = (
            jnp.dot(perm_a, wl_a, preferred_element_type=F32)
            + jnp.dot(perm_b, wl_b, preferred_element_type=F32)).astype(BF16)

    for u in range(group):
        for e in range(N_EXPERTS):
            idx = (step * group + u) * N_EXPERTS + e
            lo = loc_ref[idx]
            bs = base_ref[idx]

            def send(k, c, lo=lo, bs=bs, u=u):
                src = buf_ref.at[par * group + u, pl.ds(pl.multiple_of(lo + k * g, g), g)]
                dst = xs_ref.at[pl.ds(pl.multiple_of(bs + k * g, g), g)]
                _chunk_copy(src, dst, sem_ref.at[par]).start()
                return c
            lax.fori_loop(0, chunks_ref[idx], send, 0)

    @pl.when(step == n_steps - 1)
    def _():
        zero_ref[...] = jnp.zeros_like(zero_ref)
        for e in range(N_EXPERTS):
            ps = pstart_ref[e]

            def send_zero(k, c, ps=ps):
                dst = xs_ref.at[pl.ds(pl.multiple_of(ps + k * g, g), g)]
                _chunk_copy(zero_ref, dst, zsem_ref.at[0]).start()
                return c
            lax.fori_loop(0, pchunks_ref[e], send_zero, 0)
        for e in range(N_EXPERTS):
            def wait_zero(_, c):
                _chunk_copy(zero_ref, xs_ref.at[pl.ds(0, g)], zsem_ref.at[0]).wait()
                return c
            lax.fori_loop(0, pchunks_ref[e], wait_zero, 0)
        wait_chunks(schunks_ref[step], par)

        @pl.when(step >= 1)
        def _():
            wait_chunks(schunks_ref[step - 1], 1 - par)


def _dispatch(plan, hb, sel, comb, ts):
    t = hb.shape[0]
    group = _moe_group(t, ts)
    nl = _moe_local_rows(ts)
    rows = _moe_rows(t, ts)
    ustrict = jnp.tri(ts, k=-1, dtype=BF16).T
    row = lambda i, *_: (i, 0)
    grid_spec = pltpu.PrefetchScalarGridSpec(
        num_scalar_prefetch=6,
        grid=(t // (group * ts),),
        in_specs=[
            pl.BlockSpec((group * ts, D_MODEL), row),
            pl.BlockSpec((group * ts, LANES), row),
            pl.BlockSpec((group * ts, LANES), row),
            pl.BlockSpec((group, N_EXPERTS, LANES), lambda i, *_: (i, 0, 0)),
            pl.BlockSpec((ts, ts), lambda i, *_: (0, 0)),
        ],
        out_specs=[
            pl.BlockSpec(memory_space=pl.ANY),
            pl.BlockSpec((group * ts, LANES), row),
        ],
        scratch_shapes=[
            pltpu.VMEM((2 * group, nl, MOE_ROW_LANES), BF16),
            pltpu.VMEM((MOE_GRANULE, MOE_ROW_LANES), BF16),
            pltpu.SemaphoreType.DMA((2,)),
            pltpu.SemaphoreType.DMA((1,)),
        ],
    )
    return pl.pallas_call(
        _dispatch_kernel,
        grid_spec=grid_spec,
        out_shape=(
            jax.ShapeDtypeStruct((rows, MOE_ROW_LANES), BF16),
            jax.ShapeDtypeStruct((t, LANES), F32),
        ),
        compiler_params=pltpu.CompilerParams(
            dimension_semantics=("arbitrary",), vmem_limit_bytes=VMEM_LIMIT_BYTES),
        name="moe_dispatch",
    )(plan["chunks"], plan["loc_off"], plan["base"], plan["step_chunks"], plan["pad_start"],
      plan["pad_chunks"], hb, sel, comb, plan["loc_t"], ustrict)


def _expert_kernel(te_ref, nt_ref, x_ref, wg_ref, wu_ref, wd_ref, y_ref, wgb_ref, wub_ref, wdb_ref):
    j = pl.program_id(0)

    @pl.when(j < nt_ref[0])
    def _():
        @pl.when((j == 0) | (te_ref[j] != te_ref[jnp.maximum(j - 1, 0)]))
        def _():
            wgb_ref[...] = wg_ref[0].astype(BF16)
            wub_ref[...] = wu_ref[0].astype(BF16)
            wdb_ref[...] = wd_ref[0].astype(BF16)

        x = x_ref[:, :D_MODEL]
        wl = x_ref[:, D_MODEL:].astype(F32)
        w = jnp.sum(wl[:, 2:8], axis=1, keepdims=True)
        gate = jnp.dot(x, wgb_ref[...], preferred_element_type=F32)
        up = jnp.dot(x, wub_ref[...], preferred_element_type=F32)
        act = (_silu(gate) * up).astype(BF16)
        y_ref[...] = (w * jnp.dot(act, wdb_ref[...], preferred_element_type=F32)).astype(y_ref.dtype)


def _experts(plan, xs, w_gate, w_up, w_down):
    rows = xs.shape[0]
    live = lambda j, te, nt: jnp.minimum(j, nt[0] - 1)
    expert = lambda j, te, nt: (te[live(j, te, nt)], 0, 0)
    grid_spec = pltpu.PrefetchScalarGridSpec(
        num_scalar_prefetch=2,
        grid=(rows // MOE_ROW_TILE,),
        in_specs=[
            pl.BlockSpec((MOE_ROW_TILE, MOE_ROW_LANES), lambda j, te, nt: (live(j, te, nt), 0)),
            pl.BlockSpec((1, D_MODEL, D_FF_EXPERT), expert),
            pl.BlockSpec((1, D_MODEL, D_FF_EXPERT), expert),
            pl.BlockSpec((1, D_FF_EXPERT, D_MODEL), expert),
        ],
        out_specs=pl.BlockSpec((MOE_ROW_TILE, D_MODEL), lambda j, te, nt: (live(j, te, nt), 0)),
        scratch_shapes=[
            pltpu.VMEM((D_MODEL, D_FF_EXPERT), BF16),
            pltpu.VMEM((D_MODEL, D_FF_EXPERT), BF16),
            pltpu.VMEM((D_FF_EXPERT, D_MODEL), BF16),
        ],
    )
    return pl.pallas_call(
        _expert_kernel,
        grid_spec=grid_spec,
        out_shape=jax.ShapeDtypeStruct((rows, D_MODEL), BF16),
        compiler_params=pltpu.CompilerParams(
            dimension_semantics=("arbitrary",), vmem_limit_bytes=VMEM_LIMIT_BYTES),
        name="moe_experts",
    )(plan["tile_expert"], plan["n_tiles"], xs, w_gate, w_up, w_down)


def _combine_kernel(chunks_ref, loc_ref, base_ref, schunks_ref,
                    y_ref, posw_ref, h_ref, g_ref, b_ref, o_ref, buf_ref, sem_ref):
    step = pl.program_id(0)
    n_steps = pl.num_programs(0)
    par = step % 2
    nl = buf_ref.shape[1]
    group = buf_ref.shape[0] // 2
    ts = h_ref.shape[0] // group
    g = MOE_GRANULE

    def fetch(ss, pp):
        for u in range(group):
            for e in range(N_EXPERTS):
                idx = (ss * group + u) * N_EXPERTS + e
                lo = loc_ref[idx]
                bs = base_ref[idx]

                def get(k, c, lo=lo, bs=bs, u=u):
                    src = y_ref.at[pl.ds(pl.multiple_of(bs + k * g, g), g)]
                    dst = buf_ref.at[pp * group + u, pl.ds(pl.multiple_of(lo + k * g, g), g)]
                    _chunk_copy(src, dst, sem_ref.at[pp]).start()
                    return c
                lax.fori_loop(0, chunks_ref[idx], get, 0)

    @pl.when(step == 0)
    def _():
        buf_ref[...] = jnp.zeros(buf_ref.shape, BF16)
        fetch(0, 0)

    @pl.when(step + 1 < n_steps)
    def _():
        fetch(step + 1, 1 - par)

    _wait_rows(lambda r: _chunk_copy(y_ref.at[pl.ds(0, r)], buf_ref.at[0, pl.ds(0, r)],
                                     sem_ref.at[par]), schunks_ref[step])

    d = lax.broadcasted_iota(jnp.int32, (1, nl), 1).astype(F32)
    for u in range(group):
        rows = slice(u * ts, (u + 1) * ts)
        yb = buf_ref[par * group + u]
        posw = posw_ref[rows, :]
        gather = jnp.where((posw[:, 0:1] == d) | (posw[:, 1:2] == d), 1.0, 0.0).astype(BF16)
        ffn = jnp.dot(gather, yb, preferred_element_type=F32)
        o_ref[rows, :] = _layer_norm(DEEPNORM_ALPHA * h_ref[rows, :] + ffn, g_ref[...], b_ref[...])


def _combine(plan, y, posw, h, ln_g, ln_b, ts):
    t = h.shape[0]
    group = _moe_group(t, ts)
    nl = _moe_local_rows(ts)
    row = lambda i, *_: (i, 0)
    const = lambda i, *_: (0, 0)
    grid_spec = pltpu.PrefetchScalarGridSpec(
        num_scalar_prefetch=4,
        grid=(t // (group * ts),),
        in_specs=[
            pl.BlockSpec(memory_space=pl.ANY),
            pl.BlockSpec((group * ts, LANES), row),
            pl.BlockSpec((group * ts, D_MODEL), row),
            pl.BlockSpec((1, D_MODEL), const),
            pl.BlockSpec((1, D_MODEL), const),
        ],
        out_specs=pl.BlockSpec((group * ts, D_MODEL), row),
        scratch_shapes=[
            pltpu.VMEM((2 * group, nl, D_MODEL), BF16),
            pltpu.SemaphoreType.DMA((2,)),
        ],
    )
    return pl.pallas_call(
        _combine_kernel,
        grid_spec=grid_spec,
        out_shape=jax.ShapeDtypeStruct((t, D_MODEL), F32),
        compiler_params=pltpu.CompilerParams(
            dimension_semantics=("arbitrary",), vmem_limit_bytes=VMEM_LIMIT_BYTES),
        name="moe_combine_ln",
    )(plan["chunks"], plan["loc_off"], plan["base"], plan["step_chunks"], y, posw, h, ln_g, ln_b)


def _pad_lanes(a, n=LANES):
    return jnp.pad(a, ((0, 0), (0, n - a.shape[1])))


def _layer(h2d, batch, seq, w_in, b_in, conv_w, conv_b, a_log, d_skip, ssd_norm_g, w_out,
           ln1_g, ln1_b, rg_w, rg_b, re_w, re_b, w_gate, w_up, w_down, ln2_g, ln2_b):
    o_dt = SSD_INNER + SSD_CONV_DIM
    o_q = o_dt + SSD_HEADS
    o_f = o_q + 3 * ATT_INNER
    w_main = jnp.concatenate([w_in[:, :o_dt], w_in[:, o_q:o_f]], axis=1).astype(BF16)
    b_main = jnp.concatenate([b_in[:o_dt], b_in[o_q:o_f]])[None, :]
    w_dt, w_f = w_in[:, o_dt:o_q], w_in[:, o_f:]
    w_small = _pad_lanes(jnp.concatenate([w_dt, w_f, w_dt], axis=1)).astype(BF16)
    b_small = _pad_lanes(jnp.concatenate([b_in[o_dt:o_q], b_in[o_f:], b_in[o_dt:o_q]])[None, :])
    a_neg = -jnp.exp(a_log.astype(F32))
    a_row = _pad_lanes(a_neg[None, :])
    a_exp = jnp.repeat(a_neg, SSD_HEAD_DIM)[None, :]
    dskip_exp = jnp.repeat(d_skip.astype(F32), SSD_HEAD_DIM)[None, :]

    z, xbc, q_t, k, v_t, gate, qf_t, kf = _in_proj(h2d, w_main, b_main, w_small, b_small, a_row, seq)
    y_ssd = _ssd(xbc, z, gate, conv_w, conv_b[None, :], a_exp, dskip_exp, ssd_norm_g[None, :],
                 batch, seq)
    y_att = _attention(q_t, k, v_t, qf_t, kf, batch, seq)

    gap = ROUTE_E_LANE - N_EXPERT_GROUPS
    w_r = _pad_lanes(jnp.concatenate(
        [rg_w, jnp.zeros((D_MODEL, gap), F32),
         jnp.transpose(re_w, (1, 0, 2)).reshape(D_MODEL, N_EXPERTS)], axis=1))
    b_r = _pad_lanes(jnp.concatenate(
        [rg_b, jnp.zeros((gap,), F32), re_b.reshape(N_EXPERTS)])[None, :])
    wr_hi = w_r.astype(BF16)
    wr_lo = (w_r - wr_hi.astype(F32)).astype(BF16)
    wo = w_out.astype(BF16)
    ts = min(MOE_SORT_TILE, h2d.shape[0])
    h1, h1b, comb, sel, cnt = _out_proj(y_ssd, y_att, h2d, wo[:SSD_INNER], wo[SSD_INNER:],
                                        ln1_g[None, :], ln1_b[None, :],
                                        jnp.concatenate([wr_hi, wr_lo], axis=1), b_r, ts)

    plan = _dispatch_plan(
        cnt[:, 0, ROUTE_E_LANE:ROUTE_E_LANE + N_EXPERTS].astype(jnp.int32), ts)
    xs, posw = _dispatch(plan, h1b, sel, comb, ts)
    y = _experts(plan, xs, w_gate, w_up, w_down)
    return _combine(plan, y, posw, h1, ln2_g[None, :], ln2_b[None, :], ts)


def kernel(x, w_in, b_in, conv_w, conv_b, a_log, d_skip, ssd_norm_g, w_out, ln1_g, ln1_b,
           router_group_w, router_group_b, router_expert_w, router_expert_b, w_gate, w_up,
           w_down, ln2_g, ln2_b):
    batch, seq, d = x.shape
    h = x.reshape(batch * seq, d)
    for l in range(w_in.shape[0]):
        h = _layer(h, batch, seq, w_in[l], b_in[l], conv_w[l], conv_b[l], a_log[l], d_skip[l],
                   ssd_norm_g[l], w_out[l], ln1_g[l], ln1_b[l], router_group_w[l],
                   router_group_b[l], router_expert_w[l], router_expert_b[l], w_gate[l],
                   w_up[l], w_down[l], ln2_g[l], ln2_b[l])
    return h.reshape(batch, seq, d)
```

```python
import functools

import jax
import jax.numpy as jnp
import numpy as np
from jax import lax
from jax.experimental import pallas as pl
from jax.experimental.pallas import tpu as pltpu

F32 = jnp.float32
BF16 = jnp.bfloat16

D_MODEL = 1024
SSD_HEADS = 8
SSD_HEAD_DIM = 64
SSD_INNER = SSD_HEADS * SSD_HEAD_DIM
SSD_GROUPS = 2
SSD_STATE = 128
CONV_WIDTH = 4
SSD_CONV_DIM = SSD_INNER + 2 * SSD_GROUPS * SSD_STATE
ATT_HEADS = 8
ATT_HEAD_DIM = 64
ATT_INNER = ATT_HEADS * ATT_HEAD_DIM
N_EXPERT_GROUPS = 4
EXPERTS_PER_GROUP = 4
N_EXPERTS = N_EXPERT_GROUPS * EXPERTS_PER_GROUP
D_FF_EXPERT = 512
DEPTH = 1
DEEPNORM_ALPHA = (2.0 * DEPTH) ** 0.25
LN_EPS = 1e-5
RMS_EPS = 1e-5

LANES = 128
SUBLANES = 8
VMEM_LIMIT_BYTES = 56 * 1024 * 1024

GATE_A_LANE = 0
GATE_F_LANE = SSD_HEADS
GATE_DT_LANE = 2 * SSD_HEADS
ROUTE_E_LANE = SUBLANES
MOE_SORT_TILE = 256
MOE_GRANULE = 16
MOE_ROW_TILE = 1024
MOE_DISPATCH_TILES = 8
MOE_COMBINE_TILES = 4
MOE_ROW_LANES = D_MODEL + LANES
MOE_WAIT_BATCH = 8

IN_PROJ_ROWS = 1024
OUT_PROJ_ROWS = 1024
CUMSUM_BLOCK = 256
FB_LANES = 6
ATT_BLOCK = 256
ATT_PAIRS_PER_STEP = 4
ATT_BATCHES_PER_STEP = 2

SSD_ROWS = 256
SSD_STEP_ROWS = 512
SSD_CHUNK = 64
NEG_BIG = -1e30
LOG2E = 1.4426950408889634


def _split3(v):
    hi = v.astype(BF16)
    r1 = v - hi.astype(F32)
    mid = r1.astype(BF16)
    lo = (r1 - mid.astype(F32)).astype(BF16)
    return hi, mid, lo


def _silu(v):
    half = 0.5 * v
    return half * (1.0 + jnp.tanh(half))


def _softplus(v):
    return jnp.maximum(v, 0.0) + jnp.log1p(jnp.exp(-jnp.abs(v)))


def _in_proj_kernel(x_ref, wm_ref, bm_ref, ws_ref, bs_ref, arow_ref, place_ref, ones_ref,
                    z_ref, xbc_ref, qt_ref, k_ref, vt_ref, gate_ref, qft_ref, kf_ref,
                    carry_ref, *, tiles_per_seq, cb):
    i = pl.program_id(0)
    tm = x_ref.shape[0]
    tb = qt_ref.shape[2]
    xb = x_ref[...].astype(BF16)

    def put_transposed(ref, val):
        for c in range(tm // tb):
            ref[c] = val[c * tb:(c + 1) * tb, :].T.astype(ref.dtype)

    s = jnp.dot(xb, ws_ref[...], preferred_element_type=F32) + bs_ref[...]
    lane = lax.broadcasted_iota(jnp.int32, (1, LANES), 1)
    sp = _softplus(s)
    log_f = -_softplus(-s)
    val = jnp.where(lane < GATE_F_LANE, sp * arow_ref[...],
                    jnp.where(lane < GATE_DT_LANE, log_f,
                              jnp.where(lane < GATE_DT_LANE + SSD_HEADS, sp, 0.0)))
    carry = jnp.where(i % tiles_per_seq == 0, 0.0, carry_ref[...])
    row = lax.broadcasted_iota(jnp.int32, (cb, 1), 0)
    sums = []
    for c in range(tm // cb):
        part = val[c * cb:(c + 1) * cb, :]
        shift = 1
        while shift < cb:
            part = part + jnp.where(row >= shift, pltpu.roll(part, shift, 0), 0.0)
            shift *= 2
        part = part + carry
        carry = part[cb - 1:cb, :]
        sums.append(part)
    cs = jnp.concatenate(sums, axis=0)
    carry_ref[...] = carry
    out = jnp.where(lane < GATE_DT_LANE, cs, val)
    gate_ref[...] = out
    pieces = jnp.concatenate(_split3(cs * LOG2E), axis=1)
    fb = jnp.dot(pieces, place_ref[...], preferred_element_type=F32) + ones_ref[...]
    put_transposed(qft_ref, fb[:, :LANES])
    kf_ref[...] = fb[:, LANES:].astype(BF16)

    col = 0
    for ref, scale in ((z_ref, None), (xbc_ref, None), (qt_ref, ATT_HEAD_DIM ** -0.5 * LOG2E),
                       (k_ref, None), (vt_ref, None)):
        transposed = ref is qt_ref or ref is vt_ref
        n = ATT_INNER if transposed else ref.shape[1]
        acc = jnp.dot(xb, wm_ref[:, col:col + n], preferred_element_type=F32) + bm_ref[:, col:col + n]
        if scale is not None:
            acc = acc * scale
        if transposed:
            put_transposed(ref, acc)
        else:
            ref[...] = acc.astype(ref.dtype)
        col += n


def _forget_bias_placement():
    place = np.zeros((3 * LANES, 2 * LANES), np.float32)
    ones = np.zeros((1, 2 * LANES), np.float32)
    for h in range(ATT_HEADS):
        for piece in range(3):
            src = piece * LANES + GATE_F_LANE + h
            place[src, FB_LANES * h + piece] = 1.0
            place[src, LANES + FB_LANES * h + 3 + piece] = -1.0
            ones[0, FB_LANES * h + 3 + piece] = 1.0
            ones[0, LANES + FB_LANES * h + piece] = 1.0
    return jnp.asarray(place, BF16), jnp.asarray(ones, F32)


def _in_proj(x2d, w_main, b_main, w_small, b_small, a_row, seq):
    t = x2d.shape[0]
    tm = min(IN_PROJ_ROWS, seq)
    tk = min(ATT_BLOCK, seq)
    place, ones = _forget_bias_placement()
    n_main = w_main.shape[1]
    row = lambda i: (i, 0)
    const = lambda i: (0, 0)
    outs = (
        jax.ShapeDtypeStruct((t, SSD_INNER), BF16),
        jax.ShapeDtypeStruct((t, SSD_CONV_DIM), BF16),
        jax.ShapeDtypeStruct((t // tk, ATT_INNER, tk), BF16),
        jax.ShapeDtypeStruct((t, ATT_INNER), BF16),
        jax.ShapeDtypeStruct((t // tk, ATT_INNER, tk), BF16),
        jax.ShapeDtypeStruct((t, LANES), F32),
        jax.ShapeDtypeStruct((t // tk, LANES, tk), BF16),
        jax.ShapeDtypeStruct((t, LANES), BF16),
    )
    return pl.pallas_call(
        functools.partial(_in_proj_kernel, tiles_per_seq=seq // tm, cb=min(CUMSUM_BLOCK, tm)),
        grid=(t // tm,),
        in_specs=[
            pl.BlockSpec((tm, D_MODEL), row),
            pl.BlockSpec((D_MODEL, n_main), const),
            pl.BlockSpec((1, n_main), const),
            pl.BlockSpec((D_MODEL, LANES), const),
            pl.BlockSpec((1, LANES), const),
            pl.BlockSpec((1, LANES), const),
            pl.BlockSpec((3 * LANES, 2 * LANES), const),
            pl.BlockSpec((1, 2 * LANES), const),
        ],
        out_specs=[
            pl.BlockSpec((tm, SSD_INNER), row),
            pl.BlockSpec((tm, SSD_CONV_DIM), row),
            pl.BlockSpec((tm // tk, ATT_INNER, tk), lambda i: (i, 0, 0)),
            pl.BlockSpec((tm, ATT_INNER), row),
            pl.BlockSpec((tm // tk, ATT_INNER, tk), lambda i: (i, 0, 0)),
            pl.BlockSpec((tm, LANES), row),
            pl.BlockSpec((tm // tk, LANES, tk), lambda i: (i, 0, 0)),
            pl.BlockSpec((tm, LANES), row),
        ],
        out_shape=outs,
        scratch_shapes=[pltpu.VMEM((1, LANES), F32)],
        compiler_params=pltpu.CompilerParams(
            dimension_semantics=("arbitrary",), vmem_limit_bytes=VMEM_LIMIT_BYTES),
        name="in_proj",
    )(x2d, w_main, b_main, w_small, b_small, a_row, place, ones)


def _expand_heads(gate, lane0, rows):
    lane = lax.broadcasted_iota(jnp.int32, (1, LANES), 1)
    pieces = []
    for p in range(SSD_HEADS // 2):
        a = jnp.broadcast_to(gate[:, lane0 + 2 * p:lane0 + 2 * p + 1], (rows, LANES))
        b = jnp.broadcast_to(gate[:, lane0 + 2 * p + 1:lane0 + 2 * p + 2], (rows, LANES))
        pieces.append(jnp.where(lane < SSD_HEAD_DIM, a, b))
    return jnp.concatenate(pieces, axis=1)


def _ssd_kernel(xbc_ref, z_ref, gate_ref, cw_ref, cb_ref, aexp_ref, dskip_ref, ng_ref, shift_ref,
                y_ref, ext_ref, state_ref):
    j = pl.program_id(1)
    rows = shift_ref.shape[1]
    q = SSD_CHUNK
    gw = SSD_INNER // SSD_GROUPS
    hpg = SSD_HEADS // SSD_GROUPS
    pad = SUBLANES

    @pl.when(j == 0)
    def _():
        ext_ref[0:pad, :] = jnp.zeros((pad, SSD_CONV_DIM), F32)
        state_ref[...] = jnp.zeros_like(state_ref)

    r_i = lax.broadcasted_iota(jnp.int32, (q, gw), 0)
    c_i = lax.broadcasted_iota(jnp.int32, (q, gw), 1)
    s_i = c_i % q
    eye_t = s_i == r_i
    tril_t = s_i <= r_i
    rb = lax.broadcasted_iota(jnp.int32, (hpg * q, gw), 0) // q
    cbk = lax.broadcasted_iota(jnp.int32, (hpg * q, gw), 1) // SSD_HEAD_DIM
    blk = rb == cbk

    for t in range(xbc_ref.shape[0] // rows):
        sub = slice(t * rows, (t + 1) * rows)
        xb = xbc_ref[sub, :]
        xf = xb.astype(F32)
        conv = cb_ref[...] + xf * cw_ref[CONV_WIDTH - 1:CONV_WIDTH, :]
        for s in range(1, CONV_WIDTH):
            shifted = jnp.dot(shift_ref[s - 1], xb, preferred_element_type=F32)
            conv = conv + shifted * cw_ref[CONV_WIDTH - 1 - s:CONV_WIDTH - s, :]
        ext_ref[pad:2 * pad, :] = xf[0:pad, :]
        head = cb_ref[...]
        for kk in range(CONV_WIDTH):
            off = pad - (CONV_WIDTH - 1) + kk
            head = head + ext_ref[off:off + pad, :] * cw_ref[kk:kk + 1, :]
        ext_ref[0:pad, :] = xf[rows - pad:rows, :]
        u = _silu(jnp.concatenate([head, conv[pad:, :]], axis=0))
        xs = u[:, :SSD_INNER]
        b_all = u[:, SSD_INNER:SSD_INNER + SSD_GROUPS * SSD_STATE].astype(BF16)
        c_all = u[:, SSD_INNER + SSD_GROUPS * SSD_STATE:].astype(BF16)

        gate = gate_ref[sub, :]
        dt_e = _expand_heads(gate, GATE_DT_LANE, rows)
        a_e = _expand_heads(gate, GATE_A_LANE, rows)
        xdt = xs * dt_e

        ys = []
        for c in range(rows // q):
            r0 = c * q
            a_c = a_e[r0:r0 + q, :]
            a_prev = a_c[0:1, :] - dt_e[r0:r0 + 1, :] * aexp_ref[...]
            a_end = a_c[q - 1:q, :]
            y_groups = []
            for g in range(SSD_GROUPS):
                l0 = g * gw
                a_cg = a_c[:, l0:l0 + gw]
                diag = jnp.sum(jnp.where(eye_t, a_cg, 0.0), axis=0, keepdims=True)
                seg = jnp.minimum(a_cg - diag, 0.0)
                lcat = jnp.where(tril_t, jnp.exp(seg), 0.0)
                bg = b_all[r0:r0 + q, g * SSD_STATE:(g + 1) * SSD_STATE]
                cg = c_all[r0:r0 + q, g * SSD_STATE:(g + 1) * SSD_STATE]
                b_tile = jnp.concatenate([bg] * hpg, axis=0)
                gcat = lax.dot_general(cg, b_tile, (((1,), (1,)), ((), ())),
                                       preferred_element_type=F32)
                xg = xdt[r0:r0 + q, l0:l0 + gw]
                xg_b = xg.astype(BF16)
                xbd = jnp.where(blk, jnp.concatenate([xg_b] * hpg, axis=0), jnp.zeros((), BF16))
                y_diag = jnp.dot((gcat * lcat).astype(BF16), xbd, preferred_element_type=F32)
                st = state_ref[g]
                y_off = jnp.dot(cg, st.astype(BF16), preferred_element_type=F32) \
                    * jnp.exp(a_cg - a_prev[:, l0:l0 + gw])
                dte = jnp.exp(a_end[:, l0:l0 + gw] - a_cg)
                upd = lax.dot_general(bg, (xg * dte).astype(BF16), (((0,), (0,)), ((), ())),
                                      preferred_element_type=F32)
                state_ref[g] = st * jnp.exp(a_end[:, l0:l0 + gw] - a_prev[:, l0:l0 + gw]) + upd
                y_groups.append(y_diag + y_off)
            ys.append(jnp.concatenate(y_groups, axis=1))
        y = jnp.concatenate(ys, axis=0) + xs * dskip_ref[...]
        y = y * _silu(z_ref[sub, :].astype(F32))
        outs = []
        for g in range(SSD_GROUPS):
            yg = y[:, g * gw:(g + 1) * gw]
            ms = jnp.mean(yg * yg, axis=1, keepdims=True)
            outs.append(yg * lax.rsqrt(ms + RMS_EPS))
        y_ref[sub, :] = (jnp.concatenate(outs, axis=1) * ng_ref[...]).astype(y_ref.dtype)


def _ssd(xbc, z, gate, conv_w, conv_b, a_exp, dskip_exp, norm_g, batch, seq):
    t = xbc.shape[0]
    rows = min(SSD_ROWS, seq)
    step_rows = min(SSD_STEP_ROWS, seq)
    nj = seq // step_rows
    row = lambda b, j: (b * nj + j, 0)
    const = lambda b, j: (0, 0)
    shifts = jnp.stack([jnp.eye(rows, k=-s, dtype=BF16) for s in range(1, CONV_WIDTH)])
    return pl.pallas_call(
        _ssd_kernel,
        grid=(batch, nj),
        in_specs=[
            pl.BlockSpec((step_rows, SSD_CONV_DIM), row),
            pl.BlockSpec((step_rows, SSD_INNER), row),
            pl.BlockSpec((step_rows, LANES), row),
            pl.BlockSpec((CONV_WIDTH, SSD_CONV_DIM), const),
            pl.BlockSpec((1, SSD_CONV_DIM), const),
            pl.BlockSpec((1, SSD_INNER), const),
            pl.BlockSpec((1, SSD_INNER), const),
            pl.BlockSpec((1, SSD_INNER), const),
            pl.BlockSpec((CONV_WIDTH - 1, rows, rows), lambda b, j: (0, 0, 0)),
        ],
        out_specs=pl.BlockSpec((step_rows, SSD_INNER), row),
        out_shape=jax.ShapeDtypeStruct((t, SSD_INNER), BF16),
        scratch_shapes=[
            pltpu.VMEM((2 * SUBLANES, SSD_CONV_DIM), F32),
            pltpu.VMEM((SSD_GROUPS, SSD_STATE, SSD_INNER // SSD_GROUPS), F32),
        ],
        compiler_params=pltpu.CompilerParams(
            dimension_semantics=("arbitrary", "arbitrary"), vmem_limit_bytes=VMEM_LIMIT_BYTES),
        name="ssd",
    )(xbc, z, gate, conv_w, conv_b, a_exp, dskip_exp, norm_g, shifts)


def _attn_kernel(qt_ref, k_ref, vt_ref, qft_ref, kf_ref, o_ref, *scratch, tb, pairs, batches):
    pair0 = pl.program_id(1) * pairs
    nb = qt_ref.shape[0] // batches
    seq = nb * tb
    chains = [(bb, pp) for bb in range(batches) for pp in range(pairs)]
    s_refs, acc_refs = scratch[:len(chains)], scratch[len(chains):]
    feat = lax.broadcasted_iota(jnp.int32, (LANES, 1), 0)
    in_head = (feat < ATT_HEAD_DIM, feat >= ATT_HEAD_DIM)
    ones = jnp.ones((2 * SUBLANES, tb), BF16)
    key = lax.broadcasted_iota(jnp.int32, (tb, 2 * tb), 0)
    qry = lax.broadcasted_iota(jnp.int32, (tb, 2 * tb), 1)
    causal = jnp.where(qry >= tb, qry - tb, qry) >= key
    zero = jnp.zeros((), BF16)

    def logits_t(qa_t, c, jk):
        bb, pp = chains[c]
        k0 = pl.multiple_of(bb * seq + jk * tb, tb)
        ka = jnp.concatenate([k_ref[pl.ds(k0, tb), pp * LANES:(pp + 1) * LANES],
                              kf_ref[pl.ds(k0, tb), :]], axis=1)
        return jnp.dot(ka, qa_t, preferred_element_type=F32)

    def update(s_t, m, c, jk, par):
        bb, pp = chains[c]
        m_new = jnp.maximum(m, jnp.max(s_t, axis=0, keepdims=True))
        alpha = jnp.exp2(m - m_new)
        p_t = jnp.exp2(s_t - m_new).astype(BF16)
        for hh in range(2):
            r0 = pp * LANES + hh * ATT_HEAD_DIM
            va_t = jnp.concatenate([vt_ref[bb * nb + jk, r0:r0 + ATT_HEAD_DIM, :], ones], axis=0)
            cols = slice(hh * tb, (hh + 1) * tb)
            acc_refs[c][par, hh] = alpha[:, cols] * acc_refs[c][par, hh] + jnp.dot(
                va_t, p_t[:, cols], preferred_element_type=F32)
        return m_new

    def build_queries(i):
        qas = []
        for bb, pp in chains:
            qf_t = qft_ref[bb * nb + i]
            q_t = qt_ref[bb * nb + i, pp * LANES:(pp + 1) * LANES, :]
            cols = []
            for hh in range(2):
                fb0 = FB_LANES * (2 * (pair0 + pp) + hh)
                in_fb = (feat >= fb0) & (feat < fb0 + FB_LANES)
                cols.append(jnp.concatenate(
                    [jnp.where(in_head[hh], q_t, zero), jnp.where(in_fb, qf_t, zero)], axis=0))
            qas.append(jnp.concatenate(cols, axis=1))
        return qas

    def first_logits(i):
        qas = build_queries(i)
        for c in range(len(chains)):
            s_refs[c][2] = logits_t(qas[c], c, 0)

    def q_block(i, _):
        par = i % 2
        qas = build_queries(i)
        for c in range(len(chains)):
            acc_refs[c][par] = jnp.zeros(acc_refs[c].shape[1:], F32)

        def step(jk, ms, rd, wr):
            new = []
            for c in range(len(chains)):
                s_t = s_refs[c][rd]
                s_refs[c][wr] = logits_t(qas[c], c, jk + 1)
                new.append(update(s_t, ms[c], c, jk, par))
            return tuple(new)

        def two_steps(t2, ms):
            return step(2 * t2 + 2, step(2 * t2 + 1, ms, 1, 0), 0, 1)

        ms = tuple(jnp.full((1, 2 * tb), NEG_BIG, F32) for _ in chains)
        ms = lax.cond(i >= 1, lambda v: step(0, v, 2, 1), lambda v: v, ms)
        ms = lax.fori_loop(0, lax.shift_right_logical(jnp.maximum(i - 1, 0), 1), two_steps, ms)
        ms = lax.cond((i >= 2) & (i % 2 == 0), lambda v: step(i - 1, v, 1, 0), lambda v: v, ms)

        last = jnp.where(i >= 1, i % 2, 2)
        diag = [jnp.where(causal, s_refs[c][last], NEG_BIG) for c in range(len(chains))]
        first_logits(jnp.minimum(i + 1, nb - 1))
        finish(jnp.maximum(i - 1, 0), 1 - par)
        for c in range(len(chains)):
            update(diag[c], ms[c], c, i, par)
        return 0

    def finish(iq, par):
        q0 = pl.multiple_of(iq * tb, tb)
        for c, (bb, pp) in enumerate(chains):
            heads = []
            for hh in range(2):
                acc_t = acc_refs[c][par, hh]
                inv = 1.0 / acc_t[ATT_HEAD_DIM:ATT_HEAD_DIM + 1, :]
                heads.append(acc_t[:ATT_HEAD_DIM, :] * inv)
            out_t = jnp.concatenate(heads, axis=0)
            o_ref[pl.ds(bb * seq + q0, tb), pp * LANES:(pp + 1) * LANES] = out_t.T.astype(o_ref.dtype)

    for c in range(len(chains)):
        acc_refs[c][1] = jnp.ones(acc_refs[c].shape[1:], F32)
    first_logits(0)
    lax.fori_loop(0, nb, q_block, 0)
    finish(nb - 1, (nb - 1) % 2)


def _attention(q_t, k, v_t, qf_t, kf, batch, seq):
    t = k.shape[0]
    tb = min(ATT_BLOCK, seq)
    nb = seq // tb
    pairs = ATT_PAIRS_PER_STEP
    batches = ATT_BATCHES_PER_STEP if batch % ATT_BATCHES_PER_STEP == 0 else 1
    width = pairs * LANES
    return pl.pallas_call(
        functools.partial(_attn_kernel, tb=tb, pairs=pairs, batches=batches),
        grid=(batch // batches, ATT_HEADS // (2 * pairs)),
        in_specs=[
            pl.BlockSpec((batches * nb, width, tb), lambda b, p: (b, p, 0)),
            pl.BlockSpec((batches * seq, width), lambda b, p: (b, p)),
            pl.BlockSpec((batches * nb, width, tb), lambda b, p: (b, p, 0)),
            pl.BlockSpec((batches * nb, LANES, tb), lambda b, p: (b, 0, 0)),
            pl.BlockSpec((batches * seq, LANES), lambda b, p: (b, 0)),
        ],
        out_specs=pl.BlockSpec((batches * seq, width), lambda b, p: (b, p)),
        out_shape=jax.ShapeDtypeStruct((t, ATT_INNER), BF16),
        scratch_shapes=([pltpu.VMEM((3, tb, 2 * tb), F32) for _ in range(batches * pairs)]
                        + [pltpu.VMEM((2, 2, ATT_HEAD_DIM + 2 * SUBLANES, tb), F32)
                           for _ in range(batches * pairs)]),
        compiler_params=pltpu.CompilerParams(
            dimension_semantics=("arbitrary", "arbitrary"), vmem_limit_bytes=VMEM_LIMIT_BYTES),
        name="fox_attention",
    )(q_t, k, v_t, qf_t, kf)


def _layer_norm(u, g, b):
    mu = jnp.mean(u, axis=1, keepdims=True)
    d = u - mu
    var = jnp.mean(d * d, axis=1, keepdims=True)
    return d * lax.rsqrt(var + LN_EPS) * g + b


def _route(lg):
    rows = lg.shape[0]
    lt = lg.T
    gl = lt[0:N_EXPERT_GROUPS, :]
    g_row = lax.broadcasted_iota(jnp.int32, (N_EXPERT_GROUPS, 1), 0).astype(F32)
    gmax = jnp.max(gl, axis=0, keepdims=True)
    gsum = jnp.sum(jnp.exp(gl - gmax), axis=0, keepdims=True)
    g_val = 1.0 / gsum
    g_idx = jnp.min(jnp.where(gl == gmax, g_row, float(N_EXPERT_GROUPS)), axis=0, keepdims=True)
    el = lt[ROUTE_E_LANE:ROUTE_E_LANE + N_EXPERTS, :]
    e_row = lax.broadcasted_iota(jnp.int32, (N_EXPERTS, 1), 0)
    e_grp = (e_row // EXPERTS_PER_GROUP).astype(F32)
    e_row = e_row.astype(F32)
    in_grp = e_grp == g_idx
    m1 = jnp.max(jnp.where(in_grp, el, NEG_BIG), axis=0, keepdims=True)
    i1 = jnp.min(jnp.where(in_grp & (el == m1), e_row, float(N_EXPERTS)), axis=0, keepdims=True)
    rest = in_grp & (e_row != i1)
    m2 = jnp.max(jnp.where(rest, el, NEG_BIG), axis=0, keepdims=True)
    i2 = jnp.min(jnp.where(rest & (el == m2), e_row, float(N_EXPERTS)), axis=0, keepdims=True)
    r = jnp.exp(m2 - m1)
    w1 = 1.0 / (1.0 + r)
    w2 = r * w1
    comb_e = jnp.where(e_row == i1, g_val * w1, jnp.where(e_row == i2, g_val * w2, 0.0))
    sel_e = jnp.where((e_row == i1) | (e_row == i2), 1.0, 0.0)
    top = jnp.zeros((ROUTE_E_LANE, rows), F32)
    bottom = jnp.zeros((LANES - ROUTE_E_LANE - N_EXPERTS, rows), F32)
    comb = jnp.concatenate([top, comb_e, bottom], axis=0).T
    sel = jnp.concatenate([top, sel_e, bottom], axis=0).T
    return comb, sel


def _out_proj_kernel(ys_ref, ya_ref, x_ref, wa_ref, wb_ref, g_ref, b_ref, wr_ref, br_ref,
                     h_ref, hb_ref, comb_ref, sel_ref, cnt_ref):
    mix = jnp.dot(ys_ref[...], wa_ref[...], preferred_element_type=F32)
    mix = mix + jnp.dot(ya_ref[...], wb_ref[...], preferred_element_type=F32)
    h = _layer_norm(DEEPNORM_ALPHA * x_ref[...] + mix, g_ref[...], b_ref[...])
    h_ref[...] = h
    h_hi = h.astype(BF16)
    hb_ref[...] = h_hi
    h_lo = (h - h_hi.astype(F32)).astype(BF16)
    both = jnp.dot(h_hi, wr_ref[...], preferred_element_type=F32)
    lg = (both[:, :LANES] + both[:, LANES:]
          + jnp.dot(h_lo, wr_ref[:, :LANES], preferred_element_type=F32)) + br_ref[...]
    comb, sel = _route(lg)
    comb_ref[...] = comb
    sel_ref[...] = sel
    ts = h.shape[0] // cnt_ref.shape[0]
    for c in range(cnt_ref.shape[0]):
        cnt = jnp.sum(sel[c * ts:(c + 1) * ts, :], axis=0, keepdims=True)
        cnt_ref[c] = jnp.broadcast_to(cnt, (SUBLANES, LANES))


def _out_proj(ys, ya, x2d, wo_a, wo_b, ln_g, ln_b, wr, br, ts):
    t = x2d.shape[0]
    tm = min(OUT_PROJ_ROWS, t)
    row = lambda i: (i, 0)
    const = lambda i: (0, 0)
    return pl.pallas_call(
        _out_proj_kernel,
        grid=(t // tm,),
        in_specs=[
            pl.BlockSpec((tm, SSD_INNER), row),
            pl.BlockSpec((tm, ATT_INNER), row),
            pl.BlockSpec((tm, D_MODEL), row),
            pl.BlockSpec((SSD_INNER, D_MODEL), const),
            pl.BlockSpec((ATT_INNER, D_MODEL), const),
            pl.BlockSpec((1, D_MODEL), const),
            pl.BlockSpec((1, D_MODEL), const),
            pl.BlockSpec((D_MODEL, 2 * LANES), const),
            pl.BlockSpec((1, LANES), const),
        ],
        out_specs=[
            pl.BlockSpec((tm, D_MODEL), row),
            pl.BlockSpec((tm, D_MODEL), row),
            pl.BlockSpec((tm, LANES), row),
            pl.BlockSpec((tm, LANES), row),
            pl.BlockSpec((tm // ts, SUBLANES, LANES), lambda i: (i, 0, 0)),
        ],
        out_shape=(
            jax.ShapeDtypeStruct((t, D_MODEL), F32),
            jax.ShapeDtypeStruct((t, D_MODEL), BF16),
            jax.ShapeDtypeStruct((t, LANES), F32),
            jax.ShapeDtypeStruct((t, LANES), F32),
            jax.ShapeDtypeStruct((t // ts, SUBLANES, LANES), F32),
        ),
        compiler_params=pltpu.CompilerParams(
            dimension_semantics=("arbitrary",), vmem_limit_bytes=VMEM_LIMIT_BYTES),
        name="out_proj_ln_router",
    )(ys, ya, x2d, wo_a, wo_b, ln_g, ln_b, wr, br)


def _dispatch_plan(cnt, ts):
    nt = cnt.shape[0]
    c16 = (cnt + (MOE_GRANULE - 1)) // MOE_GRANULE * MOE_GRANULE
    loc_off = jnp.cumsum(c16, axis=1) - c16
    tot = jnp.sum(c16, axis=0)
    tiles = (tot + (MOE_ROW_TILE - 1)) // MOE_ROW_TILE
    cum_tiles = jnp.cumsum(tiles)
    start = (cum_tiles - tiles) * MOE_ROW_TILE
    base = start[None, :] + jnp.cumsum(c16, axis=0) - c16
    max_tiles = _moe_rows(nt * ts, ts) // MOE_ROW_TILE
    tile_expert = jnp.minimum(
        jnp.sum(jnp.arange(max_tiles, dtype=jnp.int32)[:, None] >= cum_tiles[None, :], axis=1),
        N_EXPERTS - 1).astype(jnp.int32)
    loc_t = jnp.broadcast_to(loc_off.astype(F32)[:, :, None], (nt, N_EXPERTS, LANES))
    return dict(
        chunks=(c16 // MOE_GRANULE).reshape(-1).astype(jnp.int32),
        loc_off=loc_off.reshape(-1).astype(jnp.int32),
        base=base.reshape(-1).astype(jnp.int32),
        tile_chunks=(jnp.sum(c16, axis=1) // MOE_GRANULE).astype(jnp.int32),
        pad_start=(start + tot).astype(jnp.int32),
        pad_chunks=((tiles * MOE_ROW_TILE - tot) // MOE_GRANULE).astype(jnp.int32),
        n_tiles=cum_tiles[-1:].astype(jnp.int32),
        tile_expert=tile_expert,
        loc_t=loc_t,
    )


def _moe_group(t, ts, tiles):
    return tiles if t % (tiles * ts) == 0 else 1


def _step_chunks(plan, group):
    return plan["tile_chunks"].reshape(-1, group).sum(axis=1)


def _moe_local_rows(ts):
    rows = 2 * ts + N_EXPERTS * (MOE_GRANULE - 1)
    return (rows + LANES - 1) // LANES * LANES


def _moe_rows(t, ts):
    rows = 2 * t + (t // ts) * N_EXPERTS * (MOE_GRANULE - 1) + N_EXPERTS * MOE_ROW_TILE
    return (rows + MOE_ROW_TILE - 1) // MOE_ROW_TILE * MOE_ROW_TILE


def _chunk_copy(src, dst, sem):
    return pltpu.make_async_copy(src, dst, sem)


def _wait_rows(copy_of_rows, n_chunks):
    def wait(rows):
        def one(_, c):
            copy_of_rows(rows).wait()
            return c
        return one
    shift = MOE_WAIT_BATCH.bit_length() - 1
    lax.fori_loop(0, lax.shift_right_logical(n_chunks, shift), wait(MOE_WAIT_BATCH * MOE_GRANULE), 0)
    lax.fori_loop(0, n_chunks & (MOE_WAIT_BATCH - 1), wait(MOE_GRANULE), 0)


def _dispatch_kernel(chunks_ref, loc_ref, base_ref, schunks_ref, pstart_ref, pchunks_ref,
                     hb_ref, sel_ref, comb_ref, loct_ref, ustrict_ref,
                     xs_ref, posw_ref, buf_ref, zero_ref, sem_ref, zsem_ref):
    step = pl.program_id(0)
    n_steps = pl.num_programs(0)
    par = step % 2
    group = loct_ref.shape[0]
    ts = hb_ref.shape[0] // group
    nl = buf_ref.shape[1]
    g = MOE_GRANULE

    def wait_chunks(n, pp):
        _wait_rows(lambda r: _chunk_copy(buf_ref.at[0, pl.ds(0, r)], xs_ref.at[pl.ds(0, r)],
                                         sem_ref.at[pp]), n)

    @pl.when(step >= 2)
    def _():
        wait_chunks(schunks_ref[step - 2], par)

    lane = lax.broadcasted_iota(jnp.int32, (1, LANES), 1)
    prow = lax.broadcasted_iota(jnp.int32, (LANES, 1), 0)
    d = lax.broadcasted_iota(jnp.int32, (nl, ts), 0).astype(F32)
    experts = slice(ROUTE_E_LANE, ROUTE_E_LANE + N_EXPERTS)
    for u in range(group):
        rows = slice(u * ts, (u + 1) * ts)
        sel_t = sel_ref[rows, :].T[experts, :]
        comb_t = comb_ref[rows, :].T[experts, :]
        picked = sel_t > 0.0
        rank = jnp.dot(sel_t.astype(BF16), ustrict_ref[...], preferred_element_type=F32)
        dest = rank + jnp.concatenate([loct_ref[u]] * (ts // LANES), axis=1)
        pos_a = jnp.min(jnp.where(picked, dest, 1e9), axis=0, keepdims=True)
        pos_b = jnp.max(jnp.where(picked, dest, -1.0), axis=0, keepdims=True)
        w_a = jnp.sum(jnp.where(picked & (dest == pos_a), comb_t, 0.0), axis=0, keepdims=True)
        w_b = jnp.sum(jnp.where(picked & (dest == pos_b), comb_t, 0.0), axis=0, keepdims=True)
        pieces = [p.astype(F32) for w in (w_a, w_b) for p in _split3(w)]
        posw_t = jnp.zeros((LANES, ts), F32)
        for r, v in enumerate([pos_a, pos_b] + pieces):
            posw_t = jnp.where(prow == r, v, posw_t)
        posw = posw_t.T
        posw_ref[rows, :] = posw
        perm_a = jnp.where(d == pos_a, 1.0, 0.0).astype(BF16)
        perm_b = jnp.where(d == pos_b, 1.0, 0.0).astype(BF16)
        slot = par * group + u
        buf_ref[slot, :, :D_MODEL] = jnp.dot(perm_a + perm_b, hb_ref[rows, :],
                                             preferred_element_type=F32).astype(BF16)
        wl_a = jnp.where((lane >= 2) & (lane < 5), posw, 0.0).astype(BF16)
        wl_b = jnp.where((lane >= 5) & (lane < 8), posw, 0.0).astype(BF16)
        buf_ref[slot, :, D_MODEL:] = (
            jnp.dot(perm_a, wl_a, preferred_element_type=F32)
            + jnp.dot(perm_b, wl_b, preferred_element_type=F32)).astype(BF16)

    for u in range(group):
        for e in range(N_EXPERTS):
            idx = (step * group + u) * N_EXPERTS + e
            lo = loc_ref[idx]
            bs = base_ref[idx]

            def send(k, c, lo=lo, bs=bs, u=u):
                src = buf_ref.at[par * group + u, pl.ds(pl.multiple_of(lo + k * g, g), g)]
                dst = xs_ref.at[pl.ds(pl.multiple_of(bs + k * g, g), g)]
                _chunk_copy(src, dst, sem_ref.at[par]).start()
                return c
            lax.fori_loop(0, chunks_ref[idx], send, 0)

    @pl.when(step == n_steps - 1)
    def _():
        zero_ref[...] = jnp.zeros_like(zero_ref)
        for e in range(N_EXPERTS):
            ps = pstart_ref[e]

            def send_zero(k, c, ps=ps):
                dst = xs_ref.at[pl.ds(pl.multiple_of(ps + k * g, g), g)]
                _chunk_copy(zero_ref, dst, zsem_ref.at[0]).start()
                return c
            lax.fori_loop(0, pchunks_ref[e], send_zero, 0)
        for e in range(N_EXPERTS):
            def wait_zero(_, c):
                _chunk_copy(zero_ref, xs_ref.at[pl.ds(0, g)], zsem_ref.at[0]).wait()
                return c
            lax.fori_loop(0, pchunks_ref[e], wait_zero, 0)
        wait_chunks(schunks_ref[step], par)

        @pl.when(step >= 1)
        def _():
            wait_chunks(schunks_ref[step - 1], 1 - par)


def _dispatch(plan, hb, sel, comb, ts):
    t = hb.shape[0]
    group = _moe_group(t, ts, MOE_DISPATCH_TILES)
    nl = _moe_local_rows(ts)
    rows = _moe_rows(t, ts)
    ustrict = jnp.tri(ts, k=-1, dtype=BF16).T
    row = lambda i, *_: (i, 0)
    grid_spec = pltpu.PrefetchScalarGridSpec(
        num_scalar_prefetch=6,
        grid=(t // (group * ts),),
        in_specs=[
            pl.BlockSpec((group * ts, D_MODEL), row),
            pl.BlockSpec((group * ts, LANES), row),
            pl.BlockSpec((group * ts, LANES), row),
            pl.BlockSpec((group, N_EXPERTS, LANES), lambda i, *_: (i, 0, 0)),
            pl.BlockSpec((ts, ts), lambda i, *_: (0, 0)),
        ],
        out_specs=[
            pl.BlockSpec(memory_space=pl.ANY),
            pl.BlockSpec((group * ts, LANES), row),
        ],
        scratch_shapes=[
            pltpu.VMEM((2 * group, nl, MOE_ROW_LANES), BF16),
            pltpu.VMEM((MOE_GRANULE, MOE_ROW_LANES), BF16),
            pltpu.SemaphoreType.DMA((2,)),
            pltpu.SemaphoreType.DMA((1,)),
        ],
    )
    return pl.pallas_call(
        _dispatch_kernel,
        grid_spec=grid_spec,
        out_shape=(
            jax.ShapeDtypeStruct((rows, MOE_ROW_LANES), BF16),
            jax.ShapeDtypeStruct((t, LANES), F32),
        ),
        compiler_params=pltpu.CompilerParams(
            dimension_semantics=("arbitrary",), vmem_limit_bytes=VMEM_LIMIT_BYTES),
        name="moe_dispatch",
    )(plan["chunks"], plan["loc_off"], plan["base"], _step_chunks(plan, group), plan["pad_start"],
      plan["pad_chunks"], hb, sel, comb, plan["loc_t"], ustrict)


def _expert_kernel(te_ref, nt_ref, x_ref, wg_ref, wu_ref, wd_ref, y_ref, wgb_ref, wub_ref, wdb_ref):
    j = pl.program_id(0)

    @pl.when(j < nt_ref[0])
    def _():
        @pl.when((j == 0) | (te_ref[j] != te_ref[jnp.maximum(j - 1, 0)]))
        def _():
            wgb_ref[...] = wg_ref[0].astype(BF16)
            wub_ref[...] = wu_ref[0].astype(BF16)
            wdb_ref[...] = wd_ref[0].astype(BF16)

        x = x_ref[:, :D_MODEL]
        wl = x_ref[:, D_MODEL:].astype(F32)
        w = jnp.sum(wl[:, 2:8], axis=1, keepdims=True)
        gate = jnp.dot(x, wgb_ref[...], preferred_element_type=F32)
        up = jnp.dot(x, wub_ref[...], preferred_element_type=F32)
        act = (_silu(gate) * up).astype(BF16)
        y_ref[...] = (w * jnp.dot(act, wdb_ref[...], preferred_element_type=F32)).astype(y_ref.dtype)


def _experts(plan, xs, w_gate, w_up, w_down):
    rows = xs.shape[0]
    live = lambda j, te, nt: jnp.minimum(j, nt[0] - 1)
    expert = lambda j, te, nt: (te[live(j, te, nt)], 0, 0)
    grid_spec = pltpu.PrefetchScalarGridSpec(
        num_scalar_prefetch=2,
        grid=(rows // MOE_ROW_TILE,),
        in_specs=[
            pl.BlockSpec((MOE_ROW_TILE, MOE_ROW_LANES), lambda j, te, nt: (live(j, te, nt), 0)),
            pl.BlockSpec((1, D_MODEL, D_FF_EXPERT), expert),
            pl.BlockSpec((1, D_MODEL, D_FF_EXPERT), expert),
            pl.BlockSpec((1, D_FF_EXPERT, D_MODEL), expert),
        ],
        out_specs=pl.BlockSpec((MOE_ROW_TILE, D_MODEL), lambda j, te, nt: (live(j, te, nt), 0)),
        scratch_shapes=[
            pltpu.VMEM((D_MODEL, D_FF_EXPERT), BF16),
            pltpu.VMEM((D_MODEL, D_FF_EXPERT), BF16),
            pltpu.VMEM((D_FF_EXPERT, D_MODEL), BF16),
        ],
    )
    return pl.pallas_call(
        _expert_kernel,
        grid_spec=grid_spec,
        out_shape=jax.ShapeDtypeStruct((rows, D_MODEL), BF16),
        compiler_params=pltpu.CompilerParams(
            dimension_semantics=("arbitrary",), vmem_limit_bytes=VMEM_LIMIT_BYTES),
        name="moe_experts",
    )(plan["tile_expert"], plan["n_tiles"], xs, w_gate, w_up, w_down)


def _combine_kernel(chunks_ref, loc_ref, base_ref, schunks_ref,
                    y_ref, posw_ref, h_ref, g_ref, b_ref, o_ref, buf_ref, sem_ref):
    step = pl.program_id(0)
    n_steps = pl.num_programs(0)
    par = step % 2
    nl = buf_ref.shape[1]
    group = buf_ref.shape[0] // 2
    ts = h_ref.shape[0] // group
    g = MOE_GRANULE

    def fetch(ss, pp):
        for u in range(group):
            for e in range(N_EXPERTS):
                idx = (ss * group + u) * N_EXPERTS + e
                lo = loc_ref[idx]
                bs = base_ref[idx]

                def get(k, c, lo=lo, bs=bs, u=u):
                    src = y_ref.at[pl.ds(pl.multiple_of(bs + k * g, g), g)]
                    dst = buf_ref.at[pp * group + u, pl.ds(pl.multiple_of(lo + k * g, g), g)]
                    _chunk_copy(src, dst, sem_ref.at[pp]).start()
                    return c
                lax.fori_loop(0, chunks_ref[idx], get, 0)

    @pl.when(step == 0)
    def _():
        buf_ref[...] = jnp.zeros(buf_ref.shape, BF16)
        fetch(0, 0)

    @pl.when(step + 1 < n_steps)
    def _():
        fetch(step + 1, 1 - par)

    _wait_rows(lambda r: _chunk_copy(y_ref.at[pl.ds(0, r)], buf_ref.at[0, pl.ds(0, r)],
                                     sem_ref.at[par]), schunks_ref[step])

    d = lax.broadcasted_iota(jnp.int32, (1, nl), 1).astype(F32)
    for u in range(group):
        rows = slice(u * ts, (u + 1) * ts)
        yb = buf_ref[par * group + u]
        posw = posw_ref[rows, :]
        gather = jnp.where((posw[:, 0:1] == d) | (posw[:, 1:2] == d), 1.0, 0.0).astype(BF16)
        ffn = jnp.dot(gather, yb, preferred_element_type=F32)
        o_ref[rows, :] = _layer_norm(DEEPNORM_ALPHA * h_ref[rows, :] + ffn, g_ref[...], b_ref[...])


def _combine(plan, y, posw, h, ln_g, ln_b, ts):
    t = h.shape[0]
    group = _moe_group(t, ts, MOE_COMBINE_TILES)
    nl = _moe_local_rows(ts)
    row = lambda i, *_: (i, 0)
    const = lambda i, *_: (0, 0)
    grid_spec = pltpu.PrefetchScalarGridSpec(
        num_scalar_prefetch=4,
        grid=(t // (group * ts),),
        in_specs=[
            pl.BlockSpec(memory_space=pl.ANY),
            pl.BlockSpec((group * ts, LANES), row),
            pl.BlockSpec((group * ts, D_MODEL), row),
            pl.BlockSpec((1, D_MODEL), const),
            pl.BlockSpec((1, D_MODEL), const),
        ],
        out_specs=pl.BlockSpec((group * ts, D_MODEL), row),
        scratch_shapes=[
            pltpu.VMEM((2 * group, nl, D_MODEL), BF16),
            pltpu.SemaphoreType.DMA((2,)),
        ],
    )
    return pl.pallas_call(
        _combine_kernel,
        grid_spec=grid_spec,
        out_shape=jax.ShapeDtypeStruct((t, D_MODEL), F32),
        compiler_params=pltpu.CompilerParams(
            dimension_semantics=("arbitrary",), vmem_limit_bytes=VMEM_LIMIT_BYTES),
        name="moe_combine_ln",
    )(plan["chunks"], plan["loc_off"], plan["base"], _step_chunks(plan, group), y, posw, h, ln_g, ln_b)


def _pad_lanes(a, n=LANES):
    return jnp.pad(a, ((0, 0), (0, n - a.shape[1])))


def _layer(h2d, batch, seq, w_in, b_in, conv_w, conv_b, a_log, d_skip, ssd_norm_g, w_out,
           ln1_g, ln1_b, rg_w, rg_b, re_w, re_b, w_gate, w_up, w_down, ln2_g, ln2_b):
    o_dt = SSD_INNER + SSD_CONV_DIM
    o_q = o_dt + SSD_HEADS
    o_f = o_q + 3 * ATT_INNER
    w_main = jnp.concatenate([w_in[:, :o_dt], w_in[:, o_q:o_f]], axis=1).astype(BF16)
    b_main = jnp.concatenate([b_in[:o_dt], b_in[o_q:o_f]])[None, :]
    w_dt, w_f = w_in[:, o_dt:o_q], w_in[:, o_f:]
    w_small = _pad_lanes(jnp.concatenate([w_dt, w_f, w_dt], axis=1)).astype(BF16)
    b_small = _pad_lanes(jnp.concatenate([b_in[o_dt:o_q], b_in[o_f:], b_in[o_dt:o_q]])[None, :])
    a_neg = -jnp.exp(a_log.astype(F32))
    a_row = _pad_lanes(a_neg[None, :])
    a_exp = jnp.repeat(a_neg, SSD_HEAD_DIM)[None, :]
    dskip_exp = jnp.repeat(d_skip.astype(F32), SSD_HEAD_DIM)[None, :]

    z, xbc, q_t, k, v_t, gate, qf_t, kf = _in_proj(h2d, w_main, b_main, w_small, b_small, a_row, seq)
    y_ssd = _ssd(xbc, z, gate, conv_w, conv_b[None, :], a_exp, dskip_exp, ssd_norm_g[None, :],
                 batch, seq)
    y_att = _attention(q_t, k, v_t, qf_t, kf, batch, seq)

    gap = ROUTE_E_LANE - N_EXPERT_GROUPS
    w_r = _pad_lanes(jnp.concatenate(
        [rg_w, jnp.zeros((D_MODEL, gap), F32),
         jnp.transpose(re_w, (1, 0, 2)).reshape(D_MODEL, N_EXPERTS)], axis=1))
    b_r = _pad_lanes(jnp.concatenate(
        [rg_b, jnp.zeros((gap,), F32), re_b.reshape(N_EXPERTS)])[None, :])
    wr_hi = w_r.astype(BF16)
    wr_lo = (w_r - wr_hi.astype(F32)).astype(BF16)
    wo = w_out.astype(BF16)
    ts = min(MOE_SORT_TILE, h2d.shape[0])
    h1, h1b, comb, sel, cnt = _out_proj(y_ssd, y_att, h2d, wo[:SSD_INNER], wo[SSD_INNER:],
                                        ln1_g[None, :], ln1_b[None, :],
                                        jnp.concatenate([wr_hi, wr_lo], axis=1), b_r, ts)

    plan = _dispatch_plan(
        cnt[:, 0, ROUTE_E_LANE:ROUTE_E_LANE + N_EXPERTS].astype(jnp.int32), ts)
    xs, posw = _dispatch(plan, h1b, sel, comb, ts)
    y = _experts(plan, xs, w_gate, w_up, w_down)
    return _combine(plan, y, posw, h1, ln2_g[None, :], ln2_b[None, :], ts)


def kernel(x, w_in, b_in, conv_w, conv_b, a_log, d_skip, ssd_norm_g, w_out, ln1_g, ln1_b,
           router_group_w, router_group_b, router_expert_w, router_expert_b, w_gate, w_up,
           w_down, ln2_g, ln2_b):
    batch, seq, d = x.shape
    h = x.reshape(batch * seq, d)
    for l in range(w_in.shape[0]):
        h = _layer(h, batch, seq, w_in[l], b_in[l], conv_w[l], conv_b[l], a_log[l], d_skip[l],
                   ssd_norm_g[l], w_out[l], ln1_g[l], ln1_b[l], router_group_w[l],
                   router_group_b[l], router_expert_w[l], router_expert_b[l], w_gate[l],
                   w_up[l], w_down[l], ln2_g[l], ln2_b[l])
    return h.reshape(batch, seq, d)
```

```python
import functools

import jax
import jax.numpy as jnp
import numpy as np
from jax import lax
from jax.experimental import pallas as pl
from jax.experimental.pallas import tpu as pltpu

F32 = jnp.float32
BF16 = jnp.bfloat16

D_MODEL = 1024
SSD_HEADS = 8
SSD_HEAD_DIM = 64
SSD_INNER = SSD_HEADS * SSD_HEAD_DIM
SSD_GROUPS = 2
SSD_STATE = 128
CONV_WIDTH = 4
SSD_CONV_DIM = SSD_INNER + 2 * SSD_GROUPS * SSD_STATE
ATT_HEADS = 8
ATT_HEAD_DIM = 64
ATT_INNER = ATT_HEADS * ATT_HEAD_DIM
N_EXPERT_GROUPS = 4
EXPERTS_PER_GROUP = 4
N_EXPERTS = N_EXPERT_GROUPS * EXPERTS_PER_GROUP
D_FF_EXPERT = 512
DEPTH = 1
DEEPNORM_ALPHA = (2.0 * DEPTH) ** 0.25
LN_EPS = 1e-5
RMS_EPS = 1e-5

LANES = 128
SUBLANES = 8
VMEM_LIMIT_BYTES = 56 * 1024 * 1024

GATE_A_LANE = 0
GATE_F_LANE = SSD_HEADS
GATE_DT_LANE = 2 * SSD_HEADS
ROUTE_E_LANE = SUBLANES
MOE_SORT_TILE = 256
MOE_GRANULE = 16
MOE_ROW_TILE = 1024
MOE_TILES_PER_STEP = 4
MOE_ROW_LANES = D_MODEL + LANES
MOE_WAIT_BATCH = 8

IN_PROJ_ROWS = 1024
OUT_PROJ_ROWS = 1024
CUMSUM_BLOCK = 256
FB_LANES = 6
ATT_BLOCK = 256
ATT_PAIRS_PER_STEP = 4
ATT_BATCHES_PER_STEP = 2

SSD_ROWS = 256
SSD_STEP_ROWS = 512
SSD_CHUNK = 64
NEG_BIG = -1e30
LOG2E = 1.4426950408889634


def _split3(v):
    hi = v.astype(BF16)
    r1 = v - hi.astype(F32)
    mid = r1.astype(BF16)
    lo = (r1 - mid.astype(F32)).astype(BF16)
    return hi, mid, lo


def _silu(v):
    half = 0.5 * v
    return half * (1.0 + jnp.tanh(half))


def _softplus(v):
    return jnp.maximum(v, 0.0) + jnp.log1p(jnp.exp(-jnp.abs(v)))


def _in_proj_kernel(x_ref, wm_ref, bm_ref, ws_ref, bs_ref, arow_ref, place_ref, ones_ref,
                    z_ref, xbc_ref, qt_ref, k_ref, vt_ref, gate_ref, qft_ref, kf_ref,
                    carry_ref, *, tiles_per_seq, cb):
    i = pl.program_id(0)
    tm = x_ref.shape[0]
    tb = qt_ref.shape[2]
    xb = x_ref[...].astype(BF16)

    def put_transposed(ref, val):
        for c in range(tm // tb):
            ref[c] = val[c * tb:(c + 1) * tb, :].T.astype(ref.dtype)

    s = jnp.dot(xb, ws_ref[...], preferred_element_type=F32) + bs_ref[...]
    lane = lax.broadcasted_iota(jnp.int32, (1, LANES), 1)
    sp = _softplus(s)
    log_f = -_softplus(-s)
    val = jnp.where(lane < GATE_F_LANE, sp * arow_ref[...],
                    jnp.where(lane < GATE_DT_LANE, log_f,
                              jnp.where(lane < GATE_DT_LANE + SSD_HEADS, sp, 0.0)))
    carry = jnp.where(i % tiles_per_seq == 0, 0.0, carry_ref[...])
    row = lax.broadcasted_iota(jnp.int32, (cb, 1), 0)
    sums = []
    for c in range(tm // cb):
        part = val[c * cb:(c + 1) * cb, :]
        shift = 1
        while shift < cb:
            part = part + jnp.where(row >= shift, pltpu.roll(part, shift, 0), 0.0)
            shift *= 2
        part = part + carry
        carry = part[cb - 1:cb, :]
        sums.append(part)
    cs = jnp.concatenate(sums, axis=0)
    carry_ref[...] = carry
    out = jnp.where(lane < GATE_DT_LANE, cs, val)
    gate_ref[...] = out
    pieces = jnp.concatenate(_split3(cs * LOG2E), axis=1)
    fb = jnp.dot(pieces, place_ref[...], preferred_element_type=F32) + ones_ref[...]
    put_transposed(qft_ref, fb[:, :LANES])
    kf_ref[...] = fb[:, LANES:].astype(BF16)

    col = 0
    for ref, scale in ((z_ref, None), (xbc_ref, None), (qt_ref, ATT_HEAD_DIM ** -0.5 * LOG2E),
                       (k_ref, None), (vt_ref, None)):
        transposed = ref is qt_ref or ref is vt_ref
        n = ATT_INNER if transposed else ref.shape[1]
        acc = jnp.dot(xb, wm_ref[:, col:col + n], preferred_element_type=F32) + bm_ref[:, col:col + n]
        if scale is not None:
            acc = acc * scale
        if transposed:
            put_transposed(ref, acc)
        else:
            ref[...] = acc.astype(ref.dtype)
        col += n


def _forget_bias_placement():
    place = np.zeros((3 * LANES, 2 * LANES), np.float32)
    ones = np.zeros((1, 2 * LANES), np.float32)
    for h in range(ATT_HEADS):
        for piece in range(3):
            src = piece * LANES + GATE_F_LANE + h
            place[src, FB_LANES * h + piece] = 1.0
            place[src, LANES + FB_LANES * h + 3 + piece] = -1.0
            ones[0, FB_LANES * h + 3 + piece] = 1.0
            ones[0, LANES + FB_LANES * h + piece] = 1.0
    return jnp.asarray(place, BF16), jnp.asarray(ones, F32)


def _in_proj(x2d, w_main, b_main, w_small, b_small, a_row, seq):
    t = x2d.shape[0]
    tm = min(IN_PROJ_ROWS, seq)
    tk = min(ATT_BLOCK, seq)
    place, ones = _forget_bias_placement()
    n_main = w_main.shape[1]
    row = lambda i: (i, 0)
    const = lambda i: (0, 0)
    outs = (
        jax.ShapeDtypeStruct((t, SSD_INNER), BF16),
        jax.ShapeDtypeStruct((t, SSD_CONV_DIM), BF16),
        jax.ShapeDtypeStruct((t // tk, ATT_INNER, tk), BF16),
        jax.ShapeDtypeStruct((t, ATT_INNER), BF16),
        jax.ShapeDtypeStruct((t // tk, ATT_INNER, tk), BF16),
        jax.ShapeDtypeStruct((t, LANES), F32),
        jax.ShapeDtypeStruct((t // tk, LANES, tk), BF16),
        jax.ShapeDtypeStruct((t, LANES), BF16),
    )
    return pl.pallas_call(
        functools.partial(_in_proj_kernel, tiles_per_seq=seq // tm, cb=min(CUMSUM_BLOCK, tm)),
        grid=(t // tm,),
        in_specs=[
            pl.BlockSpec((tm, D_MODEL), row),
            pl.BlockSpec((D_MODEL, n_main), const),
            pl.BlockSpec((1, n_main), const),
            pl.BlockSpec((D_MODEL, LANES), const),
            pl.BlockSpec((1, LANES), const),
            pl.BlockSpec((1, LANES), const),
            pl.BlockSpec((3 * LANES, 2 * LANES), const),
            pl.BlockSpec((1, 2 * LANES), const),
        ],
        out_specs=[
            pl.BlockSpec((tm, SSD_INNER), row),
            pl.BlockSpec((tm, SSD_CONV_DIM), row),
            pl.BlockSpec((tm // tk, ATT_INNER, tk), lambda i: (i, 0, 0)),
            pl.BlockSpec((tm, ATT_INNER), row),
            pl.BlockSpec((tm // tk, ATT_INNER, tk), lambda i: (i, 0, 0)),
            pl.BlockSpec((tm, LANES), row),
            pl.BlockSpec((tm // tk, LANES, tk), lambda i: (i, 0, 0)),
            pl.BlockSpec((tm, LANES), row),
        ],
        out_shape=outs,
        scratch_shapes=[pltpu.VMEM((1, LANES), F32)],
        compiler_params=pltpu.CompilerParams(
            dimension_semantics=("arbitrary",), vmem_limit_bytes=VMEM_LIMIT_BYTES),
        name="in_proj",
    )(x2d, w_main, b_main, w_small, b_small, a_row, place, ones)


def _expand_heads(gate, lane0, rows):
    lane = lax.broadcasted_iota(jnp.int32, (1, LANES), 1)
    pieces = []
    for p in range(SSD_HEADS // 2):
        a = jnp.broadcast_to(gate[:, lane0 + 2 * p:lane0 + 2 * p + 1], (rows, LANES))
        b = jnp.broadcast_to(gate[:, lane0 + 2 * p + 1:lane0 + 2 * p + 2], (rows, LANES))
        pieces.append(jnp.where(lane < SSD_HEAD_DIM, a, b))
    return jnp.concatenate(pieces, axis=1)


def _ssd_kernel(xbc_ref, z_ref, gate_ref, cw_ref, cb_ref, aexp_ref, dskip_ref, ng_ref, shift_ref,
                y_ref, ext_ref, state_ref):
    j = pl.program_id(1)
    rows = shift_ref.shape[1]
    q = SSD_CHUNK
    gw = SSD_INNER // SSD_GROUPS
    hpg = SSD_HEADS // SSD_GROUPS
    pad = SUBLANES

    @pl.when(j == 0)
    def _():
        ext_ref[0:pad, :] = jnp.zeros((pad, SSD_CONV_DIM), F32)
        state_ref[...] = jnp.zeros_like(state_ref)

    r_i = lax.broadcasted_iota(jnp.int32, (q, gw), 0)
    c_i = lax.broadcasted_iota(jnp.int32, (q, gw), 1)
    s_i = c_i % q
    eye_t = s_i == r_i
    tril_t = s_i <= r_i
    rb = lax.broadcasted_iota(jnp.int32, (hpg * q, gw), 0) // q
    cbk = lax.broadcasted_iota(jnp.int32, (hpg * q, gw), 1) // SSD_HEAD_DIM
    blk = rb == cbk

    for t in range(xbc_ref.shape[0] // rows):
        sub = slice(t * rows, (t + 1) * rows)
        xb = xbc_ref[sub, :]
        xf = xb.astype(F32)
        conv = cb_ref[...] + xf * cw_ref[CONV_WIDTH - 1:CONV_WIDTH, :]
        for s in range(1, CONV_WIDTH):
            shifted = jnp.dot(shift_ref[s - 1], xb, preferred_element_type=F32)
            conv = conv + shifted * cw_ref[CONV_WIDTH - 1 - s:CONV_WIDTH - s, :]
        ext_ref[pad:2 * pad, :] = xf[0:pad, :]
        head = cb_ref[...]
        for kk in range(CONV_WIDTH):
            off = pad - (CONV_WIDTH - 1) + kk
            head = head + ext_ref[off:off + pad, :] * cw_ref[kk:kk + 1, :]
        ext_ref[0:pad, :] = xf[rows - pad:rows, :]
        u = _silu(jnp.concatenate([head, conv[pad:, :]], axis=0))
        xs = u[:, :SSD_INNER]
        b_all = u[:, SSD_INNER:SSD_INNER + SSD_GROUPS * SSD_STATE].astype(BF16)
        c_all = u[:, SSD_INNER + SSD_GROUPS * SSD_STATE:].astype(BF16)

        gate = gate_ref[sub, :]
        dt_e = _expand_heads(gate, GATE_DT_LANE, rows)
        a_e = _expand_heads(gate, GATE_A_LANE, rows)
        xdt = xs * dt_e

        ys = []
        for c in range(rows // q):
            r0 = c * q
            a_c = a_e[r0:r0 + q, :]
            a_prev = a_c[0:1, :] - dt_e[r0:r0 + 1, :] * aexp_ref[...]
            a_end = a_c[q - 1:q, :]
            y_groups = []
            for g in range(SSD_GROUPS):
                l0 = g * gw
                a_cg = a_c[:, l0:l0 + gw]
                diag = jnp.sum(jnp.where(eye_t, a_cg, 0.0), axis=0, keepdims=True)
                seg = jnp.minimum(a_cg - diag, 0.0)
                lcat = jnp.where(tril_t, jnp.exp(seg), 0.0)
                bg = b_all[r0:r0 + q, g * SSD_STATE:(g + 1) * SSD_STATE]
                cg = c_all[r0:r0 + q, g * SSD_STATE:(g + 1) * SSD_STATE]
                b_tile = jnp.concatenate([bg] * hpg, axis=0)
                gcat = lax.dot_general(cg, b_tile, (((1,), (1,)), ((), ())),
                                       preferred_element_type=F32)
                xg = xdt[r0:r0 + q, l0:l0 + gw]
                xg_b = xg.astype(BF16)
                xbd = jnp.where(blk, jnp.concatenate([xg_b] * hpg, axis=0), jnp.zeros((), BF16))
                y_diag = jnp.dot((gcat * lcat).astype(BF16), xbd, preferred_element_type=F32)
                st = state_ref[g]
                y_off = jnp.dot(cg, st.astype(BF16), preferred_element_type=F32) \
                    * jnp.exp(a_cg - a_prev[:, l0:l0 + gw])
                dte = jnp.exp(a_end[:, l0:l0 + gw] - a_cg)
                upd = lax.dot_general(bg, (xg * dte).astype(BF16), (((0,), (0,)), ((), ())),
                                      preferred_element_type=F32)
                state_ref[g] = st * jnp.exp(a_end[:, l0:l0 + gw] - a_prev[:, l0:l0 + gw]) + upd
                y_groups.append(y_diag + y_off)
            ys.append(jnp.concatenate(y_groups, axis=1))
        y = jnp.concatenate(ys, axis=0) + xs * dskip_ref[...]
        y = y * _silu(z_ref[sub, :].astype(F32))
        outs = []
        for g in range(SSD_GROUPS):
            yg = y[:, g * gw:(g + 1) * gw]
            ms = jnp.mean(yg * yg, axis=1, keepdims=True)
            outs.append(yg * lax.rsqrt(ms + RMS_EPS))
        y_ref[sub, :] = (jnp.concatenate(outs, axis=1) * ng_ref[...]).astype(y_ref.dtype)


def _ssd(xbc, z, gate, conv_w, conv_b, a_exp, dskip_exp, norm_g, batch, seq):
    t = xbc.shape[0]
    rows = min(SSD_ROWS, seq)
    step_rows = min(SSD_STEP_ROWS, seq)
    nj = seq // step_rows
    row = lambda b, j: (b * nj + j, 0)
    const = lambda b, j: (0, 0)
    shifts = jnp.stack([jnp.eye(rows, k=-s, dtype=BF16) for s in range(1, CONV_WIDTH)])
    return pl.pallas_call(
        _ssd_kernel,
        grid=(batch, nj),
        in_specs=[
            pl.BlockSpec((step_rows, SSD_CONV_DIM), row),
            pl.BlockSpec((step_rows, SSD_INNER), row),
            pl.BlockSpec((step_rows, LANES), row),
            pl.BlockSpec((CONV_WIDTH, SSD_CONV_DIM), const),
            pl.BlockSpec((1, SSD_CONV_DIM), const),
            pl.BlockSpec((1, SSD_INNER), const),
            pl.BlockSpec((1, SSD_INNER), const),
            pl.BlockSpec((1, SSD_INNER), const),
            pl.BlockSpec((CONV_WIDTH - 1, rows, rows), lambda b, j: (0, 0, 0)),
        ],
        out_specs=pl.BlockSpec((step_rows, SSD_INNER), row),
        out_shape=jax.ShapeDtypeStruct((t, SSD_INNER), BF16),
        scratch_shapes=[
            pltpu.VMEM((2 * SUBLANES, SSD_CONV_DIM), F32),
            pltpu.VMEM((SSD_GROUPS, SSD_STATE, SSD_INNER // SSD_GROUPS), F32),
        ],
        compiler_params=pltpu.CompilerParams(
            dimension_semantics=("arbitrary", "arbitrary"), vmem_limit_bytes=VMEM_LIMIT_BYTES),
        name="ssd",
    )(xbc, z, gate, conv_w, conv_b, a_exp, dskip_exp, norm_g, shifts)


def _attn_kernel(qt_ref, k_ref, vt_ref, qft_ref, kf_ref, o_ref, *scratch, tb, pairs, batches):
    pair0 = pl.program_id(1) * pairs
    nb = qt_ref.shape[0] // batches
    seq = nb * tb
    chains = [(bb, pp) for bb in range(batches) for pp in range(pairs)]
    s_refs, acc_refs = scratch[:len(chains)], scratch[len(chains):]
    feat = lax.broadcasted_iota(jnp.int32, (LANES, 1), 0)
    in_head = (feat < ATT_HEAD_DIM, feat >= ATT_HEAD_DIM)
    ones = jnp.ones((2 * SUBLANES, tb), BF16)
    key = lax.broadcasted_iota(jnp.int32, (tb, 2 * tb), 0)
    qry = lax.broadcasted_iota(jnp.int32, (tb, 2 * tb), 1)
    causal = jnp.where(qry >= tb, qry - tb, qry) >= key
    zero = jnp.zeros((), BF16)

    def logits_t(qa_t, c, jk):
        bb, pp = chains[c]
        k0 = pl.multiple_of(bb * seq + jk * tb, tb)
        ka = jnp.concatenate([k_ref[pl.ds(k0, tb), pp * LANES:(pp + 1) * LANES],
                              kf_ref[pl.ds(k0, tb), :]], axis=1)
        return jnp.dot(ka, qa_t, preferred_element_type=F32)

    def update(s_t, m, c, jk, par):
        bb, pp = chains[c]
        m_new = jnp.maximum(m, jnp.max(s_t, axis=0, keepdims=True))
        alpha = jnp.exp2(m - m_new)
        p_t = jnp.exp2(s_t - m_new).astype(BF16)
        for hh in range(2):
            r0 = pp * LANES + hh * ATT_HEAD_DIM
            va_t = jnp.concatenate([vt_ref[bb * nb + jk, r0:r0 + ATT_HEAD_DIM, :], ones], axis=0)
            cols = slice(hh * tb, (hh + 1) * tb)
            acc_refs[c][par, hh] = alpha[:, cols] * acc_refs[c][par, hh] + jnp.dot(
                va_t, p_t[:, cols], preferred_element_type=F32)
        return m_new

    def build_queries(i):
        qas = []
        for bb, pp in chains:
            qf_t = qft_ref[bb * nb + i]
            q_t = qt_ref[bb * nb + i, pp * LANES:(pp + 1) * LANES, :]
            cols = []
            for hh in range(2):
                fb0 = FB_LANES * (2 * (pair0 + pp) + hh)
                in_fb = (feat >= fb0) & (feat < fb0 + FB_LANES)
                cols.append(jnp.concatenate(
                    [jnp.where(in_head[hh], q_t, zero), jnp.where(in_fb, qf_t, zero)], axis=0))
            qas.append(jnp.concatenate(cols, axis=1))
        return qas

    def first_logits(i):
        qas = build_queries(i)
        for c in range(len(chains)):
            s_refs[c][2] = logits_t(qas[c], c, 0)

    def q_block(i, _):
        par = i % 2
        qas = build_queries(i)
        for c in range(len(chains)):
            acc_refs[c][par] = jnp.zeros(acc_refs[c].shape[1:], F32)

        def step(jk, ms, rd, wr):
            new = []
            for c in range(len(chains)):
                s_t = s_refs[c][rd]
                s_refs[c][wr] = logits_t(qas[c], c, jk + 1)
                new.append(update(s_t, ms[c], c, jk, par))
            return tuple(new)

        def two_steps(t2, ms):
            return step(2 * t2 + 2, step(2 * t2 + 1, ms, 1, 0), 0, 1)

        ms = tuple(jnp.full((1, 2 * tb), NEG_BIG, F32) for _ in chains)
        ms = lax.cond(i >= 1, lambda v: step(0, v, 2, 1), lambda v: v, ms)
        ms = lax.fori_loop(0, lax.shift_right_logical(jnp.maximum(i - 1, 0), 1), two_steps, ms)
        ms = lax.cond((i >= 2) & (i % 2 == 0), lambda v: step(i - 1, v, 1, 0), lambda v: v, ms)

        last = jnp.where(i >= 1, i % 2, 2)
        diag = [jnp.where(causal, s_refs[c][last], NEG_BIG) for c in range(len(chains))]
        first_logits(jnp.minimum(i + 1, nb - 1))
        finish(jnp.maximum(i - 1, 0), 1 - par)
        for c in range(len(chains)):
            update(diag[c], ms[c], c, i, par)
        return 0

    def finish(iq, par):
        q0 = pl.multiple_of(iq * tb, tb)
        for c, (bb, pp) in enumerate(chains):
            heads = []
            for hh in range(2):
                acc_t = acc_refs[c][par, hh]
                inv = 1.0 / acc_t[ATT_HEAD_DIM:ATT_HEAD_DIM + 1, :]
                heads.append(acc_t[:ATT_HEAD_DIM, :] * inv)
            out_t = jnp.concatenate(heads, axis=0)
            o_ref[pl.ds(bb * seq + q0, tb), pp * LANES:(pp + 1) * LANES] = out_t.T.astype(o_ref.dtype)

    for c in range(len(chains)):
        acc_refs[c][1] = jnp.ones(acc_refs[c].shape[1:], F32)
    first_logits(0)
    lax.fori_loop(0, nb, q_block, 0)
    finish(nb - 1, (nb - 1) % 2)


def _attention(q_t, k, v_t, qf_t, kf, batch, seq):
    t = k.shape[0]
    tb = min(ATT_BLOCK, seq)
    nb = seq // tb
    pairs = ATT_PAIRS_PER_STEP
    batches = ATT_BATCHES_PER_STEP if batch % ATT_BATCHES_PER_STEP == 0 else 1
    width = pairs * LANES
    return pl.pallas_call(
        functools.partial(_attn_kernel, tb=tb, pairs=pairs, batches=batches),
        grid=(batch // batches, ATT_HEADS // (2 * pairs)),
        in_specs=[
            pl.BlockSpec((batches * nb, width, tb), lambda b, p: (b, p, 0)),
            pl.BlockSpec((batches * seq, width), lambda b, p: (b, p)),
            pl.BlockSpec((batches * nb, width, tb), lambda b, p: (b, p, 0)),
            pl.BlockSpec((batches * nb, LANES, tb), lambda b, p: (b, 0, 0)),
            pl.BlockSpec((batches * seq, LANES), lambda b, p: (b, 0)),
        ],
        out_specs=pl.BlockSpec((batches * seq, width), lambda b, p: (b, p)),
        out_shape=jax.ShapeDtypeStruct((t, ATT_INNER), BF16),
        scratch_shapes=([pltpu.VMEM((3, tb, 2 * tb), F32) for _ in range(batches * pairs)]
                        + [pltpu.VMEM((2, 2, ATT_HEAD_DIM + 2 * SUBLANES, tb), F32)
                           for _ in range(batches * pairs)]),
        compiler_params=pltpu.CompilerParams(
            dimension_semantics=("arbitrary", "arbitrary"), vmem_limit_bytes=VMEM_LIMIT_BYTES),
        name="fox_attention",
    )(q_t, k, v_t, qf_t, kf)


def _layer_norm(u, g, b):
    mu = jnp.mean(u, axis=1, keepdims=True)
    d = u - mu
    var = jnp.mean(d * d, axis=1, keepdims=True)
    return d * lax.rsqrt(var + LN_EPS) * g + b


def _route(lg):
    rows = lg.shape[0]
    lt = lg.T
    gl = lt[0:N_EXPERT_GROUPS, :]
    g_row = lax.broadcasted_iota(jnp.int32, (N_EXPERT_GROUPS, 1), 0).astype(F32)
    gmax = jnp.max(gl, axis=0, keepdims=True)
    gsum = jnp.sum(jnp.exp(gl - gmax), axis=0, keepdims=True)
    g_val = 1.0 / gsum
    g_idx = jnp.min(jnp.where(gl == gmax, g_row, float(N_EXPERT_GROUPS)), axis=0, keepdims=True)
    el = lt[ROUTE_E_LANE:ROUTE_E_LANE + N_EXPERTS, :]
    e_row = lax.broadcasted_iota(jnp.int32, (N_EXPERTS, 1), 0)
    e_grp = (e_row // EXPERTS_PER_GROUP).astype(F32)
    e_row = e_row.astype(F32)
    in_grp = e_grp == g_idx
    m1 = jnp.max(jnp.where(in_grp, el, NEG_BIG), axis=0, keepdims=True)
    i1 = jnp.min(jnp.where(in_grp & (el == m1), e_row, float(N_EXPERTS)), axis=0, keepdims=True)
    rest = in_grp & (e_row != i1)
    m2 = jnp.max(jnp.where(rest, el, NEG_BIG), axis=0, keepdims=True)
    i2 = jnp.min(jnp.where(rest & (el == m2), e_row, float(N_EXPERTS)), axis=0, keepdims=True)
    r = jnp.exp(m2 - m1)
    w1 = 1.0 / (1.0 + r)
    w2 = r * w1
    comb_e = jnp.where(e_row == i1, g_val * w1, jnp.where(e_row == i2, g_val * w2, 0.0))
    sel_e = jnp.where((e_row == i1) | (e_row == i2), 1.0, 0.0)
    top = jnp.zeros((ROUTE_E_LANE, rows), F32)
    bottom = jnp.zeros((LANES - ROUTE_E_LANE - N_EXPERTS, rows), F32)
    comb = jnp.concatenate([top, comb_e, bottom], axis=0).T
    sel = jnp.concatenate([top, sel_e, bottom], axis=0).T
    return comb, sel


def _out_proj_kernel(ys_ref, ya_ref, x_ref, wa_ref, wb_ref, g_ref, b_ref, wr_ref, br_ref,
                     h_ref, hb_ref, comb_ref, sel_ref, cnt_ref):
    mix = jnp.dot(ys_ref[...], wa_ref[...], preferred_element_type=F32)
    mix = mix + jnp.dot(ya_ref[...], wb_ref[...], preferred_element_type=F32)
    h = _layer_norm(DEEPNORM_ALPHA * x_ref[...] + mix, g_ref[...], b_ref[...])
    h_ref[...] = h
    h_hi = h.astype(BF16)
    hb_ref[...] = h_hi
    h_lo = (h - h_hi.astype(F32)).astype(BF16)
    both = jnp.dot(h_hi, wr_ref[...], preferred_element_type=F32)
    lg = (both[:, :LANES] + both[:, LANES:]
          + jnp.dot(h_lo, wr_ref[:, :LANES], preferred_element_type=F32)) + br_ref[...]
    comb, sel = _route(lg)
    comb_ref[...] = comb
    sel_ref[...] = sel
    ts = h.shape[0] // cnt_ref.shape[0]
    for c in range(cnt_ref.shape[0]):
        cnt = jnp.sum(sel[c * ts:(c + 1) * ts, :], axis=0, keepdims=True)
        cnt_ref[c] = jnp.broadcast_to(cnt, (SUBLANES, LANES))


def _out_proj(ys, ya, x2d, wo_a, wo_b, ln_g, ln_b, wr, br, ts):
    t = x2d.shape[0]
    tm = min(OUT_PROJ_ROWS, t)
    row = lambda i: (i, 0)
    const = lambda i: (0, 0)
    return pl.pallas_call(
        _out_proj_kernel,
        grid=(t // tm,),
        in_specs=[
            pl.BlockSpec((tm, SSD_INNER), row),
            pl.BlockSpec((tm, ATT_INNER), row),
            pl.BlockSpec((tm, D_MODEL), row),
            pl.BlockSpec((SSD_INNER, D_MODEL), const),
            pl.BlockSpec((ATT_INNER, D_MODEL), const),
            pl.BlockSpec((1, D_MODEL), const),
            pl.BlockSpec((1, D_MODEL), const),
            pl.BlockSpec((D_MODEL, 2 * LANES), const),
            pl.BlockSpec((1, LANES), const),
        ],
        out_specs=[
            pl.BlockSpec((tm, D_MODEL), row),
            pl.BlockSpec((tm, D_MODEL), row),
            pl.BlockSpec((tm, LANES), row),
            pl.BlockSpec((tm, LANES), row),
            pl.BlockSpec((tm // ts, SUBLANES, LANES), lambda i: (i, 0, 0)),
        ],
        out_shape=(
            jax.ShapeDtypeStruct((t, D_MODEL), F32),
            jax.ShapeDtypeStruct((t, D_MODEL), BF16),
            jax.ShapeDtypeStruct((t, LANES), F32),
            jax.ShapeDtypeStruct((t, LANES), F32),
            jax.ShapeDtypeStruct((t // ts, SUBLANES, LANES), F32),
        ),
        compiler_params=pltpu.CompilerParams(
            dimension_semantics=("arbitrary",), vmem_limit_bytes=VMEM_LIMIT_BYTES),
        name="out_proj_ln_router",
    )(ys, ya, x2d, wo_a, wo_b, ln_g, ln_b, wr, br)


def _dispatch_plan(cnt, ts):
    nt = cnt.shape[0]
    c16 = (cnt + (MOE_GRANULE - 1)) // MOE_GRANULE * MOE_GRANULE
    loc_off = jnp.cumsum(c16, axis=1) - c16
    tot = jnp.sum(c16, axis=0)
    tiles = (tot + (MOE_ROW_TILE - 1)) // MOE_ROW_TILE
    cum_tiles = jnp.cumsum(tiles)
    start = (cum_tiles - tiles) * MOE_ROW_TILE
    base = start[None, :] + jnp.cumsum(c16, axis=0) - c16
    max_tiles = _moe_rows(nt * ts, ts) // MOE_ROW_TILE
    tile_expert = jnp.minimum(
        jnp.sum(jnp.arange(max_tiles, dtype=jnp.int32)[:, None] >= cum_tiles[None, :], axis=1),
        N_EXPERTS - 1).astype(jnp.int32)
    loc_t = jnp.broadcast_to(loc_off.astype(F32)[:, :, None], (nt, N_EXPERTS, LANES))
    chunks = c16 // MOE_GRANULE
    cum_chunks = jnp.cumsum(chunks, axis=1)
    p = jnp.arange(_moe_local_rows(ts) // MOE_GRANULE, dtype=jnp.int32)
    owner = jnp.minimum(jnp.sum(p[None, :, None] >= cum_chunks[:, None, :], axis=2), N_EXPERTS - 1)
    first = jnp.take_along_axis(cum_chunks - chunks, owner, axis=1)
    chunk_row = jnp.take_along_axis(base, owner, axis=1) + (p[None, :] - first) * MOE_GRANULE
    return dict(
        chunk_row=chunk_row.reshape(-1).astype(jnp.int32),
        tile_chunks=cum_chunks[:, -1].astype(jnp.int32),
        step_chunks=(jnp.sum(c16, axis=1).reshape(-1, _moe_group(nt * ts, ts)).sum(axis=1)
                     // MOE_GRANULE).astype(jnp.int32),
        pad_start=(start + tot).astype(jnp.int32),
        pad_chunks=((tiles * MOE_ROW_TILE - tot) // MOE_GRANULE).astype(jnp.int32),
        n_tiles=cum_tiles[-1:].astype(jnp.int32),
        tile_expert=tile_expert,
        loc_t=loc_t,
    )


def _moe_group(t, ts):
    return MOE_TILES_PER_STEP if t % (MOE_TILES_PER_STEP * ts) == 0 else 1


def _moe_local_rows(ts):
    rows = 2 * ts + N_EXPERTS * (MOE_GRANULE - 1)
    return (rows + LANES - 1) // LANES * LANES


def _moe_rows(t, ts):
    rows = 2 * t + (t // ts) * N_EXPERTS * (MOE_GRANULE - 1) + N_EXPERTS * MOE_ROW_TILE
    return (rows + MOE_ROW_TILE - 1) // MOE_ROW_TILE * MOE_ROW_TILE


def _chunk_copy(src, dst, sem):
    return pltpu.make_async_copy(src, dst, sem)


def _wait_rows(copy_of_rows, n_chunks):
    def wait(rows):
        def one(_, c):
            copy_of_rows(rows).wait()
            return c
        return one
    shift = MOE_WAIT_BATCH.bit_length() - 1
    lax.fori_loop(0, lax.shift_right_logical(n_chunks, shift), wait(MOE_WAIT_BATCH * MOE_GRANULE), 0)
    lax.fori_loop(0, n_chunks & (MOE_WAIT_BATCH - 1), wait(MOE_GRANULE), 0)


def _dispatch_kernel(crow_ref, tchunks_ref, schunks_ref, pstart_ref, pchunks_ref,
                     hb_ref, sel_ref, comb_ref, loct_ref, ustrict_ref,
                     xs_ref, posw_ref, buf_ref, zero_ref, sem_ref, zsem_ref):
    step = pl.program_id(0)
    n_steps = pl.num_programs(0)
    par = step % 2
    group = loct_ref.shape[0]
    ts = hb_ref.shape[0] // group
    nl = buf_ref.shape[1]
    g = MOE_GRANULE

    def wait_chunks(n, pp):
        _wait_rows(lambda r: _chunk_copy(buf_ref.at[0, pl.ds(0, r)], xs_ref.at[pl.ds(0, r)],
                                         sem_ref.at[pp]), n)

    @pl.when(step >= 2)
    def _():
        wait_chunks(schunks_ref[step - 2], par)

    lane = lax.broadcasted_iota(jnp.int32, (1, LANES), 1)
    prow = lax.broadcasted_iota(jnp.int32, (LANES, 1), 0)
    d = lax.broadcasted_iota(jnp.int32, (nl, ts), 0).astype(F32)
    experts = slice(ROUTE_E_LANE, ROUTE_E_LANE + N_EXPERTS)
    for u in range(group):
        rows = slice(u * ts, (u + 1) * ts)
        sel_t = sel_ref[rows, :].T[experts, :]
        comb_t = comb_ref[rows, :].T[experts, :]
        picked = sel_t > 0.0
        rank = jnp.dot(sel_t.astype(BF16), ustrict_ref[...], preferred_element_type=F32)
        dest = rank + jnp.concatenate([loct_ref[u]] * (ts // LANES), axis=1)
        pos_a = jnp.min(jnp.where(picked, dest, 1e9), axis=0, keepdims=True)
        pos_b = jnp.max(jnp.where(picked, dest, -1.0), axis=0, keepdims=True)
        w_a = jnp.sum(jnp.where(picked & (dest == pos_a), comb_t, 0.0), axis=0, keepdims=True)
        w_b = jnp.sum(jnp.where(picked & (dest == pos_b), comb_t, 0.0), axis=0, keepdims=True)
        pieces = [p.astype(F32) for w in (w_a, w_b) for p in _split3(w)]
        posw_t = jnp.zeros((LANES, ts), F32)
        for r, v in enumerate([pos_a, pos_b] + pieces):
            posw_t = jnp.where(prow == r, v, posw_t)
        posw = posw_t.T
        posw_ref[rows, :] = posw
        perm_a = jnp.where(d == pos_a, 1.0, 0.0).astype(BF16)
        perm_b = jnp.where(d == pos_b, 1.0, 0.0).astype(BF16)
        slot = par * group + u
        buf_ref[slot, :, :D_MODEL] = jnp.dot(perm_a + perm_b, hb_ref[rows, :],
                                             preferred_element_type=F32).astype(BF16)
        wl_a = jnp.where((lane >= 2) & (lane < 5), posw, 0.0).astype(BF16)
        wl_b = jnp.where((lane >= 5) & (lane < 8), posw, 0.0).astype(BF16)
        buf_ref[slot, :, D_MODEL:] = (
            jnp.dot(perm_a, wl_a, preferred_element_type=F32)
            + jnp.dot(perm_b, wl_b, preferred_element_type=F32)).astype(BF16)

    per_tile = nl // g
    for u in range(group):
        tile = step * group + u

        def send(p, c, tile=tile, u=u):
            src = buf_ref.at[par * group + u, pl.ds(pl.multiple_of(p * g, g), g)]
            dst = xs_ref.at[pl.ds(pl.multiple_of(crow_ref[tile * per_tile + p], g), g)]
            _chunk_copy(src, dst, sem_ref.at[par]).start()
            return c
        lax.fori_loop(0, tchunks_ref[tile], send, 0)

    @pl.when(step == n_steps - 1)
    def _():
        zero_ref[...] = jnp.zeros_like(zero_ref)
        for e in range(N_EXPERTS):
            ps = pstart_ref[e]

            def send_zero(k, c, ps=ps):
                dst = xs_ref.at[pl.ds(pl.multiple_of(ps + k * g, g), g)]
                _chunk_copy(zero_ref, dst, zsem_ref.at[0]).start()
                return c
            lax.fori_loop(0, pchunks_ref[e], send_zero, 0)
        for e in range(N_EXPERTS):
            def wait_zero(_, c):
                _chunk_copy(zero_ref, xs_ref.at[pl.ds(0, g)], zsem_ref.at[0]).wait()
                return c
            lax.fori_loop(0, pchunks_ref[e], wait_zero, 0)
        wait_chunks(schunks_ref[step], par)

        @pl.when(step >= 1)
        def _():
            wait_chunks(schunks_ref[step - 1], 1 - par)


def _dispatch(plan, hb, sel, comb, ts):
    t = hb.shape[0]
    group = _moe_group(t, ts)
    nl = _moe_local_rows(ts)
    rows = _moe_rows(t, ts)
    ustrict = jnp.tri(ts, k=-1, dtype=BF16).T
    row = lambda i, *_: (i, 0)
    grid_spec = pltpu.PrefetchScalarGridSpec(
        num_scalar_prefetch=5,
        grid=(t // (group * ts),),
        in_specs=[
            pl.BlockSpec((group * ts, D_MODEL), row),
            pl.BlockSpec((group * ts, LANES), row),
            pl.BlockSpec((group * ts, LANES), row),
            pl.BlockSpec((group, N_EXPERTS, LANES), lambda i, *_: (i, 0, 0)),
            pl.BlockSpec((ts, ts), lambda i, *_: (0, 0)),
        ],
        out_specs=[
            pl.BlockSpec(memory_space=pl.ANY),
            pl.BlockSpec((group * ts, LANES), row),
        ],
        scratch_shapes=[
            pltpu.VMEM((2 * group, nl, MOE_ROW_LANES), BF16),
            pltpu.VMEM((MOE_GRANULE, MOE_ROW_LANES), BF16),
            pltpu.SemaphoreType.DMA((2,)),
            pltpu.SemaphoreType.DMA((1,)),
        ],
    )
    return pl.pallas_call(
        _dispatch_kernel,
        grid_spec=grid_spec,
        out_shape=(
            jax.ShapeDtypeStruct((rows, MOE_ROW_LANES), BF16),
            jax.ShapeDtypeStruct((t, LANES), F32),
        ),
        compiler_params=pltpu.CompilerParams(
            dimension_semantics=("arbitrary",), vmem_limit_bytes=VMEM_LIMIT_BYTES),
        name="moe_dispatch",
    )(plan["chunk_row"], plan["tile_chunks"], plan["step_chunks"], plan["pad_start"],
      plan["pad_chunks"], hb, sel, comb, plan["loc_t"], ustrict)


def _expert_kernel(te_ref, nt_ref, x_ref, wg_ref, wu_ref, wd_ref, y_ref, wgb_ref, wub_ref, wdb_ref):
    j = pl.program_id(0)

    @pl.when(j < nt_ref[0])
    def _():
        @pl.when((j == 0) | (te_ref[j] != te_ref[jnp.maximum(j - 1, 0)]))
        def _():
            wgb_ref[...] = wg_ref[0].astype(BF16)
            wub_ref[...] = wu_ref[0].astype(BF16)
            wdb_ref[...] = wd_ref[0].astype(BF16)

        x = x_ref[:, :D_MODEL]
        wl = x_ref[:, D_MODEL:].astype(F32)
        w = jnp.sum(wl[:, 2:8], axis=1, keepdims=True)
        gate = jnp.dot(x, wgb_ref[...], preferred_element_type=F32)
        up = jnp.dot(x, wub_ref[...], preferred_element_type=F32)
        act = (_silu(gate) * up).astype(BF16)
        y_ref[...] = (w * jnp.dot(act, wdb_ref[...], preferred_element_type=F32)).astype(y_ref.dtype)


def _experts(plan, xs, w_gate, w_up, w_down):
    rows = xs.shape[0]
    live = lambda j, te, nt: jnp.minimum(j, nt[0] - 1)
    expert = lambda j, te, nt: (te[live(j, te, nt)], 0, 0)
    grid_spec = pltpu.PrefetchScalarGridSpec(
        num_scalar_prefetch=2,
        grid=(rows // MOE_ROW_TILE,),
        in_specs=[
            pl.BlockSpec((MOE_ROW_TILE, MOE_ROW_LANES), lambda j, te, nt: (live(j, te, nt), 0)),
            pl.BlockSpec((1, D_MODEL, D_FF_EXPERT), expert),
            pl.BlockSpec((1, D_MODEL, D_FF_EXPERT), expert),
            pl.BlockSpec((1, D_FF_EXPERT, D_MODEL), expert),
        ],
        out_specs=pl.BlockSpec((MOE_ROW_TILE, D_MODEL), lambda j, te, nt: (live(j, te, nt), 0)),
        scratch_shapes=[
            pltpu.VMEM((D_MODEL, D_FF_EXPERT), BF16),
            pltpu.VMEM((D_MODEL, D_FF_EXPERT), BF16),
            pltpu.VMEM((D_FF_EXPERT, D_MODEL), BF16),
        ],
    )
    return pl.pallas_call(
        _expert_kernel,
        grid_spec=grid_spec,
        out_shape=jax.ShapeDtypeStruct((rows, D_MODEL), BF16),
        compiler_params=pltpu.CompilerParams(
            dimension_semantics=("arbitrary",), vmem_limit_bytes=VMEM_LIMIT_BYTES),
        name="moe_experts",
    )(plan["tile_expert"], plan["n_tiles"], xs, w_gate, w_up, w_down)


def _combine_kernel(crow_ref, tchunks_ref, schunks_ref,
                    y_ref, posw_ref, h_ref, g_ref, b_ref, o_ref, buf_ref, sem_ref):
    step = pl.program_id(0)
    n_steps = pl.num_programs(0)
    par = step % 2
    nl = buf_ref.shape[1]
    group = buf_ref.shape[0] // 2
    ts = h_ref.shape[0] // group
    g = MOE_GRANULE

    def fetch(ss, pp):
        per_tile = nl // g
        for u in range(group):
            tile = ss * group + u

            def get(p, c, tile=tile, u=u):
                src = y_ref.at[pl.ds(pl.multiple_of(crow_ref[tile * per_tile + p], g), g)]
                dst = buf_ref.at[pp * group + u, pl.ds(pl.multiple_of(p * g, g), g)]
                _chunk_copy(src, dst, sem_ref.at[pp]).start()
                return c
            lax.fori_loop(0, tchunks_ref[tile], get, 0)

    @pl.when(step == 0)
    def _():
        buf_ref[...] = jnp.zeros(buf_ref.shape, BF16)
        fetch(0, 0)

    @pl.when(step + 1 < n_steps)
    def _():
        fetch(step + 1, 1 - par)

    _wait_rows(lambda r: _chunk_copy(y_ref.at[pl.ds(0, r)], buf_ref.at[0, pl.ds(0, r)],
                                     sem_ref.at[par]), schunks_ref[step])

    d = lax.broadcasted_iota(jnp.int32, (1, nl), 1).astype(F32)
    for u in range(group):
        rows = slice(u * ts, (u + 1) * ts)
        yb = buf_ref[par * group + u]
        posw = posw_ref[rows, :]
        gather = jnp.where((posw[:, 0:1] == d) | (posw[:, 1:2] == d), 1.0, 0.0).astype(BF16)
        ffn = jnp.dot(gather, yb, preferred_element_type=F32)
        o_ref[rows, :] = _layer_norm(DEEPNORM_ALPHA * h_ref[rows, :] + ffn, g_ref[...], b_ref[...])


def _combine(plan, y, posw, h, ln_g, ln_b, ts):
    t = h.shape[0]
    group = _moe_group(t, ts)
    nl = _moe_local_rows(ts)
    row = lambda i, *_: (i, 0)
    const = lambda i, *_: (0, 0)
    grid_spec = pltpu.PrefetchScalarGridSpec(
        num_scalar_prefetch=3,
        grid=(t // (group * ts),),
        in_specs=[
            pl.BlockSpec(memory_space=pl.ANY),
            pl.BlockSpec((group * ts, LANES), row),
            pl.BlockSpec((group * ts, D_MODEL), row),
            pl.BlockSpec((1, D_MODEL), const),
            pl.BlockSpec((1, D_MODEL), const),
        ],
        out_specs=pl.BlockSpec((group * ts, D_MODEL), row),
        scratch_shapes=[
            pltpu.VMEM((2 * group, nl, D_MODEL), BF16),
            pltpu.SemaphoreType.DMA((2,)),
        ],
    )
    return pl.pallas_call(
        _combine_kernel,
        grid_spec=grid_spec,
        out_shape=jax.ShapeDtypeStruct((t, D_MODEL), F32),
        compiler_params=pltpu.CompilerParams(
            dimension_semantics=("arbitrary",), vmem_limit_bytes=VMEM_LIMIT_BYTES),
        name="moe_combine_ln",
    )(plan["chunk_row"], plan["tile_chunks"], plan["step_chunks"], y, posw, h, ln_g, ln_b)


def _pad_lanes(a, n=LANES):
    return jnp.pad(a, ((0, 0), (0, n - a.shape[1])))


def _layer(h2d, batch, seq, w_in, b_in, conv_w, conv_b, a_log, d_skip, ssd_norm_g, w_out,
           ln1_g, ln1_b, rg_w, rg_b, re_w, re_b, w_gate, w_up, w_down, ln2_g, ln2_b):
    o_dt = SSD_INNER + SSD_CONV_DIM
    o_q = o_dt + SSD_HEADS
    o_f = o_q + 3 * ATT_INNER
    w_main = jnp.concatenate([w_in[:, :o_dt], w_in[:, o_q:o_f]], axis=1).astype(BF16)
    b_main = jnp.concatenate([b_in[:o_dt], b_in[o_q:o_f]])[None, :]
    w_dt, w_f = w_in[:, o_dt:o_q], w_in[:, o_f:]
    w_small = _pad_lanes(jnp.concatenate([w_dt, w_f, w_dt], axis=1)).astype(BF16)
    b_small = _pad_lanes(jnp.concatenate([b_in[o_dt:o_q], b_in[o_f:], b_in[o_dt:o_q]])[None, :])
    a_neg = -jnp.exp(a_log.astype(F32))
    a_row = _pad_lanes(a_neg[None, :])
    a_exp = jnp.repeat(a_neg, SSD_HEAD_DIM)[None, :]
    dskip_exp = jnp.repeat(d_skip.astype(F32), SSD_HEAD_DIM)[None, :]

    z, xbc, q_t, k, v_t, gate, qf_t, kf = _in_proj(h2d, w_main, b_main, w_small, b_small, a_row, seq)
    y_ssd = _ssd(xbc, z, gate, conv_w, conv_b[None, :], a_exp, dskip_exp, ssd_norm_g[None, :],
                 batch, seq)
    y_att = _attention(q_t, k, v_t, qf_t, kf, batch, seq)

    gap = ROUTE_E_LANE - N_EXPERT_GROUPS
    w_r = _pad_lanes(jnp.concatenate(
        [rg_w, jnp.zeros((D_MODEL, gap), F32),
         jnp.transpose(re_w, (1, 0, 2)).reshape(D_MODEL, N_EXPERTS)], axis=1))
    b_r = _pad_lanes(jnp.concatenate(
        [rg_b, jnp.zeros((gap,), F32), re_b.reshape(N_EXPERTS)])[None, :])
    wr_hi = w_r.astype(BF16)
    wr_lo = (w_r - wr_hi.astype(F32)).astype(BF16)
    wo = w_out.astype(BF16)
    ts = min(MOE_SORT_TILE, h2d.shape[0])
    h1, h1b, comb, sel, cnt = _out_proj(y_ssd, y_att, h2d, wo[:SSD_INNER], wo[SSD_INNER:],
                                        ln1_g[None, :], ln1_b[None, :],
                                        jnp.concatenate([wr_hi, wr_lo], axis=1), b_r, ts)

    plan = _dispatch_plan(
        cnt[:, 0, ROUTE_E_LANE:ROUTE_E_LANE + N_EXPERTS].astype(jnp.int32), ts)
    xs, posw = _dispatch(plan, h1b, sel, comb, ts)
    y = _experts(plan, xs, w_gate, w_up, w_down)
    return _combine(plan, y, posw, h1, ln2_g[None, :], ln2_b[None, :], ts)


def kernel(x, w_in, b_in, conv_w, conv_b, a_log, d_skip, ssd_norm_g, w_out, ln1_g, ln1_b,
           router_group_w, router_group_b, router_expert_w, router_expert_b, w_gate, w_up,
           w_down, ln2_g, ln2_b):
    batch, seq, d = x.shape
    h = x.reshape(batch * seq, d)
    for l in range(w_in.shape[0]):
        h = _layer(h, batch, seq, w_in[l], b_in[l], conv_w[l], conv_b[l], a_log[l], d_skip[l],
                   ssd_norm_g[l], w_out[l], ln1_g[l], ln1_b[l], router_group_w[l],
                   router_group_b[l], router_expert_w[l], router_expert_b[l], w_gate[l],
                   w_up[l], w_down[l], ln2_g[l], ln2_b[l])
    return h.reshape(batch, seq, d)
```
